```python
import math
import jax
import jax.numpy as jnp
from jax import lax
import numpy as np

D_MODEL = 1024
BATCH = 4
SEQ = 8192
DEPTH = 1

GRID_W = 64
CTX_LEN = 256
RG_WIDTH = 1024
RG_HEADS = 16
RG_HEAD_DIM = RG_WIDTH // RG_HEADS
RG_CONV_W = 4
RG_CONV_LEFT = 2
RG_C = 8.0
HY_WIDTH = 1024
HY_CONV_W = 3
HY_CONV_LEFT = 1
HY_SEQ_BANDS = 16
HY_COL_BANDS = 8
HY_EMB_DIM = 1 + 2 * HY_SEQ_BANDS + 1 + 2 * HY_COL_BANDS
HY_FILTER_HIDDEN = 64
HY_DECAY_TARGET = 1e-2
HY_FAST_DECAY = 0.3
HY_SLOW_DECAY = 1.5
N_GROUPS = 4
EXPERTS_PER_GROUP = 8
N_EXPERTS = N_GROUPS * EXPERTS_PER_GROUP
TOP_K = 2
D_EXPERT = 512
MOE_BLOCK = 256
N_MOD = 6
EPS = 1e-6
IN_COLS = 2 * RG_WIDTH + 3 * HY_WIDTH + 2 * D_MODEL
IN_SPLITS = (RG_WIDTH, 2 * RG_WIDTH, 2 * RG_WIDTH + 3 * HY_WIDTH,
             2 * RG_WIDTH + 3 * HY_WIDTH + D_MODEL)

kernel_name = 'hybrid_rglru_hyena_hmoe_dit_block'


def rmsnorm(x, g):
    xf = x.astype(jnp.float32)
    y = xf * lax.rsqrt(jnp.mean(xf * xf, axis=-1, keepdims=True) + EPS)
    return (y * g.astype(jnp.float32)).astype(x.dtype)


def modulate(h, shift, scale):
    return h * (1.0 + scale[..., None, :]) + shift[..., None, :]


def dwconv_centred(x, w, b, left):
    k, L = w.shape[0], x.shape[1]
    xp = jnp.pad(x, ((0, 0), (left, k - 1 - left), (0, 0)))
    y = b + xp[:, 0:L] * w[0]
    for j in range(1, k):
        y = y + xp[:, j:j + L] * w[j]
    return y


def rglru_coeffs(xc, wa, ba, wx, bx, lam):
    B, L, W = xc.shape
    xh = xc.reshape(B, L, RG_HEADS, RG_HEAD_DIM)
    r = jax.nn.sigmoid(jnp.einsum('blhd,hde->blhe', xh, wa).reshape(B, L, W) + ba)
    i = jax.nn.sigmoid(jnp.einsum('blhd,hde->blhe', xh, wx).reshape(B, L, W) + bx)
    log_a = -RG_C * r.astype(jnp.float32) * jax.nn.softplus(-lam.astype(jnp.float32))
    a = jnp.exp(log_a)
    b = jnp.sqrt(-jnp.expm1(2.0 * log_a)) * (i * xc).astype(jnp.float32)
    return a, b


def _scan_op(left, right):
    a_l, b_l = left
    a_r, b_r = right
    return a_r * a_l, a_r * b_l + b_r


def linear_scan(a, b, h0, reverse):
    idx = -1 if reverse else 0
    b = b.at[:, idx].add(a[:, idx] * h0)
    _, h = lax.associative_scan(_scan_op, (a, b), reverse=reverse, axis=1)
    return h


def bidir_rglru(xc, h0_f, h0_b, rg_f, rg_b):
    a_f, b_f = rglru_coeffs(xc, *rg_f)
    a_b, b_b = rglru_coeffs(xc, *rg_b)
    h_f = linear_scan(a_f, b_f, h0_f, False)
    h_b = linear_scan(a_b, b_b, h0_b, True)
    return h_f, h_b


def hyena_filter(L, rows, w1, b1, w2, b2, freq, w3):
    f32 = jnp.float32
    s = jnp.arange(L, dtype=jnp.int32)
    sf = s.astype(f32)
    t_norm = sf / max(L - 1, 1)
    seq_bands = jnp.linspace(1e-4, HY_SEQ_BANDS - 1, HY_SEQ_BANDS, dtype=f32)
    ang = (2.0 * math.pi / L) * sf[:, None] * seq_bands[None, :]
    if rows is None:
        grid = jnp.zeros((L, 1 + 2 * HY_COL_BANDS), f32)
    else:
        row_lag = (s // GRID_W).astype(f32) / rows
        col_bands = jnp.arange(1, HY_COL_BANDS + 1, dtype=f32)
        col_ang = (2.0 * math.pi / GRID_W) * (s % GRID_W).astype(f32)[:, None] * col_bands[None, :]
        grid = jnp.concatenate([row_lag[:, None], jnp.cos(col_ang), jnp.sin(col_ang)], axis=-1)
    feats = jnp.concatenate([t_norm[:, None], jnp.cos(ang), jnp.sin(ang), grid], axis=-1)
    fr = freq.astype(f32)
    z = jnp.sin(fr * (feats @ w1.astype(f32) + b1.astype(f32)))
    z = jnp.sin(fr * (z @ w2.astype(f32) + b2.astype(f32)))
    k = z @ w3.astype(f32)
    max_decay = math.log(HY_DECAY_TARGET) / HY_FAST_DECAY
    min_decay = math.log(HY_DECAY_TARGET) / HY_SLOW_DECAY
    deltas = jnp.abs(jnp.linspace(min_decay, max_decay, HY_WIDTH, dtype=f32))
    decay = jnp.exp(-t_norm[:, None] * deltas[None, :])
    k_fwd = k[:, :HY_WIDTH] * decay
    k_bwd = k[:, HY_WIDTH:] * decay
    return jnp.concatenate([k_fwd, jnp.zeros((1, HY_WIDTH), f32), k_bwd[:0:-1]], axis=0)


def bidir_fftconv(u, k2, skip):
    L = u.shape[1]
    uf = u.astype(jnp.float32)
    U = jnp.fft.rfft(uf, n=2 * L, axis=1)
    K = jnp.fft.rfft(k2, n=2 * L, axis=0)
    y = jnp.fft.irfft(U * K[None], n=2 * L, axis=1)[:, :L]
    return (y + uf * skip.astype(jnp.float32)).astype(u.dtype)


def hyena_mix(p_hy, conv_w, conv_b, k2, skip):
    z = dwconv_centred(p_hy, conv_w, conv_b, HY_CONV_LEFT)
    x0, x1, v = jnp.split(z, 3, axis=-1)
    return bidir_fftconv(v * x1, k2, skip) * x0


def token_mixer(h, rows, h0_f, h0_b, w_in, rg_conv_w, rg_conv_b, rg_f, rg_b, rg_proj,
                hy_conv_w, hy_conv_b, hy_filt, hy_skip, hy_proj, w_out):
    L = h.shape[1]
    p = h @ w_in
    p_rx, p_rg, p_hy, p_ga, p_gb = jnp.split(p, IN_SPLITS, axis=-1)
    xc = dwconv_centred(p_rx, rg_conv_w, rg_conv_b, RG_CONV_LEFT)
    h_f, h_b = bidir_rglru(xc, h0_f, h0_b, rg_f, rg_b)
    y_rg = (h_f + h_b).astype(h.dtype) * jax.nn.gelu(p_rg)
    k2 = hyena_filter(L, rows, *hy_filt)
    y_hy = hyena_mix(p_hy, hy_conv_w, hy_conv_b, k2, hy_skip)
    merged = jax.nn.sigmoid(p_ga) * (y_rg @ rg_proj) + jax.nn.sigmoid(p_gb) * (y_hy @ hy_proj)
    return merged @ w_out, h_f, h_b


def context_scan_states(hc, w_in_rx, rg_conv_w, rg_conv_b, rg_f, rg_b):
    xc = dwconv_centred(hc @ w_in_rx, rg_conv_w, rg_conv_b, RG_CONV_LEFT)
    zero = jnp.zeros((hc.shape[0], RG_WIDTH), jnp.float32)
    h_f, h_b = bidir_rglru(xc, zero, zero, rg_f, rg_b)
    return h_f[:, -1], h_b[:, 0]


def hier_moe(h, wg, bg, we, be, w1, w3, w2):
    B, L, D = h.shape
    N = B * L
    xs = h.reshape(N, D)
    p_group = jax.nn.softmax((xs @ wg + bg).astype(jnp.float32), axis=-1)
    grp = jnp.argmax(p_group, axis=-1).astype(jnp.int32)
    p_g = jnp.max(p_group, axis=-1)
    e_logits = (xs @ we + be).astype(jnp.float32).reshape(N, N_GROUPS, EXPERTS_PER_GROUP)
    e_in = e_logits[jnp.arange(N), grp]
    top_val, top_idx = lax.top_k(e_in, TOP_K)
    p_k = jax.nn.softmax(top_val, axis=-1)
    eid = (grp[:, None] * EXPERTS_PER_GROUP + top_idx).reshape(-1)
    wt = (p_g[:, None] * p_k).reshape(-1)
    tok = jnp.repeat(jnp.arange(N, dtype=jnp.int32), TOP_K)
    A = N * TOP_K
    order = jnp.argsort(eid)
    e_s, t_s, w_s = eid[order], tok[order], wt[order]
    counts = jax.ops.segment_sum(jnp.ones((A,), jnp.int32), eid, num_segments=N_EXPERTS)
    padded = ((counts + MOE_BLOCK - 1) // MOE_BLOCK) * MOE_BLOCK
    start = jnp.cumsum(counts) - counts
    pend = jnp.cumsum(padded)
    pstart = pend - padded
    dest = pstart[e_s] + (jnp.arange(A, dtype=jnp.int32) - start[e_s])
    NB = (A + N_EXPERTS * (MOE_BLOCK - 1)) // MOE_BLOCK
    P = NB * MOE_BLOCK
    row_tok = jnp.full((P,), N, jnp.int32).at[dest].set(t_s)
    row_wt = jnp.zeros((P,), jnp.float32).at[dest].set(w_s)
    blk_exp = jnp.minimum(jnp.searchsorted(pend, jnp.arange(NB, dtype=jnp.int32) * MOE_BLOCK,
                                           side='right'), N_EXPERTS - 1)
    xs_pad = jnp.concatenate([xs, jnp.zeros((1, D), xs.dtype)], axis=0)
    xb = xs_pad[row_tok].reshape(NB, MOE_BLOCK, D)

    def expert_block(args):
        xblk, e = args
        hid = jax.nn.silu(xblk @ w1[e]) * (xblk @ w3[e])
        return hid @ w2[e]

    yb = lax.map(expert_block, (xb, blk_exp)).reshape(P, D)
    out = jnp.zeros((N + 1, D), h.dtype).at[row_tok].add(yb * row_wt[:, None].astype(h.dtype))
    return out[:N].reshape(B, L, D)


def setup_inputs(seed: int = 0) -> dict:
    key = jax.random.key(seed)
    ks = iter(jax.random.split(key, 48))

    def nrm(shape, s):
        return jax.random.normal(next(ks), shape, jnp.float32) * s

    def lam_init():
        u = jax.random.uniform(next(ks), (DEPTH, RG_WIDTH), jnp.float32, minval=0.9, maxval=0.999)
        sg = u ** (1.0 / RG_C)
        return jnp.log(sg) - jnp.log1p(-sg)

    D = D_MODEL
    return {
        'x': nrm((BATCH, SEQ, D), 1.0),
        'c': nrm((BATCH, D), 1.0),
        'ctx': nrm((BATCH, CTX_LEN, D), 1.0),
        'c_ctx': nrm((D,), 1.0),
        'ada_w': nrm((DEPTH, D, N_MOD * D), D ** -0.5),
        'ada_b': nrm((DEPTH, N_MOD * D), 0.02),
        'norm1_g': 1.0 + nrm((DEPTH, D), 0.05),
        'norm2_g': 1.0 + nrm((DEPTH, D), 0.05),
        'final_g': 1.0 + nrm((D,), 0.05),
        'w_in': nrm((DEPTH, D, IN_COLS), D ** -0.5),
        'rg_conv_w': nrm((DEPTH, RG_CONV_W, RG_WIDTH), 0.5),
        'rg_conv_b': nrm((DEPTH, RG_WIDTH), 0.02),
        'rg_wa_f': nrm((DEPTH, RG_HEADS, RG_HEAD_DIM, RG_HEAD_DIM), RG_HEAD_DIM ** -0.5),
        'rg_ba_f': nrm((DEPTH, RG_WIDTH), 0.1),
        'rg_wx_f': nrm((DEPTH, RG_HEADS, RG_HEAD_DIM, RG_HEAD_DIM), RG_HEAD_DIM ** -0.5),
        'rg_bx_f': nrm((DEPTH, RG_WIDTH), 0.1),
        'rg_lam_f': lam_init(),
        'rg_wa_b': nrm((DEPTH, RG_HEADS, RG_HEAD_DIM, RG_HEAD_DIM), RG_HEAD_DIM ** -0.5),
        'rg_ba_b': nrm((DEPTH, RG_WIDTH), 0.1),
        'rg_wx_b': nrm((DEPTH, RG_HEADS, RG_HEAD_DIM, RG_HEAD_DIM), RG_HEAD_DIM ** -0.5),
        'rg_bx_b': nrm((DEPTH, RG_WIDTH), 0.1),
        'rg_lam_b': lam_init(),
        'rg_proj': nrm((DEPTH, RG_WIDTH, D), RG_WIDTH ** -0.5),
        'hy_conv_w': nrm((DEPTH, HY_CONV_W, 3 * HY_WIDTH), HY_CONV_W ** -0.5),
        'hy_conv_b': nrm((DEPTH, 3 * HY_WIDTH), 0.02),
        'hy_pos_w1': nrm((DEPTH, HY_EMB_DIM, HY_FILTER_HIDDEN), HY_EMB_DIM ** -0.5),
        'hy_pos_b1': nrm((DEPTH, HY_FILTER_HIDDEN), 0.1),
        'hy_pos_w2': nrm((DEPTH, HY_FILTER_HIDDEN, HY_FILTER_HIDDEN), HY_FILTER_HIDDEN ** -0.5),
        'hy_pos_b2': nrm((DEPTH, HY_FILTER_HIDDEN), 0.1),
        'hy_freq': 1.0 + nrm((DEPTH, HY_FILTER_HIDDEN), 0.1),
        'hy_pos_w3': nrm((DEPTH, HY_FILTER_HIDDEN, 2 * HY_WIDTH), 0.003),
        'hy_skip': nrm((DEPTH, HY_WIDTH), 1.0),
        'hy_proj': nrm((DEPTH, HY_WIDTH, D), HY_WIDTH ** -0.5),
        'w_out': nrm((DEPTH, D, D), D ** -0.5),
        'moe_wg': nrm((DEPTH, D, N_GROUPS), D ** -0.5),
        'moe_bg': nrm((DEPTH, N_GROUPS), 0.01),
        'moe_we': nrm((DEPTH, D, N_EXPERTS), D ** -0.5),
        'moe_be': nrm((DEPTH, N_EXPERTS), 0.01),
        'moe_w1': nrm((DEPTH, N_EXPERTS, D, D_EXPERT), D ** -0.5),
        'moe_w3': nrm((DEPTH, N_EXPERTS, D, D_EXPERT), D ** -0.5),
        'moe_w2': nrm((DEPTH, N_EXPERTS, D_EXPERT, D), D_EXPERT ** -0.5),
    }


def reference(x, c, ctx, c_ctx, ada_w, ada_b, norm1_g, norm2_g, final_g, w_in,
              rg_conv_w, rg_conv_b, rg_wa_f, rg_ba_f, rg_wx_f, rg_bx_f, rg_lam_f,
              rg_wa_b, rg_ba_b, rg_wx_b, rg_bx_b, rg_lam_b, rg_proj,
              hy_conv_w, hy_conv_b, hy_pos_w1, hy_pos_b1, hy_pos_w2, hy_pos_b2, hy_freq,
              hy_pos_w3, hy_skip, hy_proj, w_out,
              moe_wg, moe_bg, moe_we, moe_be, moe_w1, moe_w3, moe_w2):
    B, L, D = x.shape
    rows = L // GRID_W
    for l in range(DEPTH):
        rg_f = (rg_wa_f[l], rg_ba_f[l], rg_wx_f[l], rg_bx_f[l], rg_lam_f[l])
        rg_b = (rg_wa_b[l], rg_ba_b[l], rg_wx_b[l], rg_bx_b[l], rg_lam_b[l])
        hy_filt = (hy_pos_w1[l], hy_pos_b1[l], hy_pos_w2[l], hy_pos_b2[l], hy_freq[l], hy_pos_w3[l])
        sh1, sc1, g1, sh2, sc2, g2 = jnp.split(jax.nn.silu(c) @ ada_w[l] + ada_b[l], N_MOD, axis=-1)
        if l == DEPTH - 1:
            csh1, csc1 = jnp.split(jax.nn.silu(c_ctx) @ ada_w[l][:, :2 * D] + ada_b[l][:2 * D], 2, axis=-1)
            hc = modulate(rmsnorm(ctx, norm1_g[l]), csh1, csc1)
            hcf, hcb = context_scan_states(hc, w_in[l][:, :RG_WIDTH], rg_conv_w[l], rg_conv_b[l], rg_f, rg_b)
        else:
            csh1, csc1, cg1, csh2, csc2, cg2 = jnp.split(
                jax.nn.silu(c_ctx) @ ada_w[l] + ada_b[l], N_MOD, axis=-1)
            hc = modulate(rmsnorm(ctx, norm1_g[l]), csh1, csc1)
            zero = jnp.zeros((B, RG_WIDTH), jnp.float32)
            out_c, hcf_seq, hcb_seq = token_mixer(
                hc, None, zero, zero, w_in[l], rg_conv_w[l], rg_conv_b[l], rg_f, rg_b, rg_proj[l],
                hy_conv_w[l], hy_conv_b[l], hy_filt, hy_skip[l], hy_proj[l], w_out[l])
            hcf, hcb = hcf_seq[:, -1], hcb_seq[:, 0]
            ctx = ctx + cg1[None, None, :] * out_c
            hc2 = modulate(rmsnorm(ctx, norm2_g[l]), csh2, csc2)
            ctx = ctx + cg2[None, None, :] * hier_moe(hc2, moe_wg[l], moe_bg[l], moe_we[l], moe_be[l],
                                                      moe_w1[l], moe_w3[l], moe_w2[l])
        hx = modulate(rmsnorm(x, norm1_g[l]), sh1, sc1)
        out_x, _, _ = token_mixer(
            hx, rows, hcf, hcb, w_in[l], rg_conv_w[l], rg_conv_b[l], rg_f, rg_b, rg_proj[l],
            hy_conv_w[l], hy_conv_b[l], hy_filt, hy_skip[l], hy_proj[l], w_out[l])
        x = x + g1[:, None, :] * out_x
        hx2 = modulate(rmsnorm(x, norm2_g[l]), sh2, sc2)
        x = x + g2[:, None, :] * hier_moe(hx2, moe_wg[l], moe_bg[l], moe_we[l], moe_be[l],
                                          moe_w1[l], moe_w3[l], moe_w2[l])
    return rmsnorm(x, final_g)
```

```python
import functools
import math

import jax
import jax.numpy as jnp
from jax import lax
from jax.experimental import pallas as pl
from jax.experimental.pallas import tpu as pltpu

F32 = jnp.float32
BF16 = jnp.bfloat16
I32 = jnp.int32
HIGHEST = lax.Precision.HIGHEST

LANES = 128
SUBLANES = 8
EPS = 1e-6
RG_C = 8.0
RG_HEAD_DIM = 64
GRID_W = 64
HY_SEQ_BANDS = 16
HY_COL_BANDS = 8
HY_DECAY_TARGET = 1e-2
HY_FAST_DECAY = 0.3
HY_SLOW_DECAY = 1.5
N_GROUPS = 4
EXPERTS_PER_GROUP = 8
MOE_BLOCK = 256
VMEM_LIMIT = 56 * 1024 * 1024


def _cparams(*sem):
    return pltpu.CompilerParams(dimension_semantics=sem, vmem_limit_bytes=VMEM_LIMIT)


def _sigmoid(x):
    return 0.5 * (jnp.tanh(0.5 * x) + 1.0)


def _gelu_tanh(x):
    c = math.sqrt(2.0 / math.pi)
    return 0.5 * x * (1.0 + jnp.tanh(c * (x + 0.044715 * (x * x * x))))


def _ada_body(c_ref, w_ref, b_ref, o_ref):
    c = c_ref[...]
    s = c * _sigmoid(c)
    o_ref[...] = jnp.dot(s, w_ref[...], precision=HIGHEST, preferred_element_type=F32) + b_ref[...]


def ada_mods(c8, ada_w, ada_b):
    d, m = ada_w.shape
    tn = 1024 if m % 1024 == 0 else m
    return pl.pallas_call(
        _ada_body,
        grid=(m // tn,),
        in_specs=[pl.BlockSpec((c8.shape[0], d), lambda j: (0, 0)),
                  pl.BlockSpec((d, tn), lambda j: (0, j)),
                  pl.BlockSpec((1, tn), lambda j: (0, j))],
        out_specs=pl.BlockSpec((c8.shape[0], tn), lambda j: (0, j)),
        out_shape=jax.ShapeDtypeStruct((c8.shape[0], m), F32),
        compiler_params=_cparams("parallel"),
        name="ada_mods",
    )(c8, ada_w, ada_b)


def _proj_body(x_ref, g_ref, sh_ref, sc_ref, w_ref, o_ref, hx_ref, *, transposed):
    @pl.when(pl.program_id(2) == 0)
    def _():
        x = x_ref[0]
        ms = jnp.mean(x * x, axis=-1, keepdims=True)
        y = x * lax.rsqrt(ms + EPS) * g_ref[...]
        hx_ref[...] = (y * (1.0 + sc_ref[0]) + sh_ref[0]).astype(BF16)

    hx = hx_ref[...]
    if transposed:
        o_ref[0] = lax.dot_general(w_ref[...], hx, (((1,), (1,)), ((), ())),
                                   preferred_element_type=F32).astype(o_ref.dtype)
    else:
        o_ref[0] = jnp.dot(hx, w_ref[...], preferred_element_type=F32).astype(o_ref.dtype)


def norm_mod_proj(x, g, shift, scale, w, *, transposed, tile_l, tile_n):
    b, l, d = x.shape
    m = w.shape[0] if transposed else w.shape[1]
    tl, tn = min(tile_l, l), min(tile_n, m)
    grid = (b, l // tl, m // tn)
    if transposed:
        w_spec = pl.BlockSpec((tn, d), lambda bi, i, j: (j, 0))
        o_spec = pl.BlockSpec((1, tn, tl), lambda bi, i, j: (bi, j, i))
        o_shape = jax.ShapeDtypeStruct((b, m, l), BF16)
    else:
        w_spec = pl.BlockSpec((d, tn), lambda bi, i, j: (0, j))
        o_spec = pl.BlockSpec((1, tl, tn), lambda bi, i, j: (bi, i, j))
        o_shape = jax.ShapeDtypeStruct((b, l, m), BF16)
    return pl.pallas_call(
        functools.partial(_proj_body, transposed=transposed),
        grid=grid,
        in_specs=[pl.BlockSpec((1, tl, d), lambda bi, i, j: (bi, i, 0)),
                  pl.BlockSpec((1, d), lambda bi, i, j: (0, 0)),
                  pl.BlockSpec((1, 1, d), lambda bi, i, j: (bi, 0, 0)),
                  pl.BlockSpec((1, 1, d), lambda bi, i, j: (bi, 0, 0)),
                  w_spec],
        out_specs=o_spec,
        out_shape=o_shape,
        scratch_shapes=[pltpu.VMEM((tl, d), BF16)],
        compiler_params=_cparams("parallel", "parallel", "arbitrary"),
        name="norm_mod_proj_t" if transposed else "norm_mod_proj",
    )(x, g, shift, scale, w)


HALO = 16


def _scan_body(pc_ref, pp_ref, pn_ref, cw_ref, cb_ref, wg_ref, ba_ref, bx_ref, lam_ref, h0_ref,
               h_ref, hl_ref, xc_s, g_s, a_s, b_s, hloc_s, pcum_s, carry_s,
               *, reverse, n_tiles, t, c, s_len, pitch):
    i = pl.program_id(1)
    ti = (n_tiles - 1 - i) if reverse else i
    n_slab = c // LANES
    n_blk = wg_ref.shape[0]
    blk = c // n_blk

    @pl.when(i == 0)
    def _():
        carry_s[...] = h0_ref[0]

    cur = pc_ref[0].astype(F32)
    has_prev = (ti > 0).astype(F32)
    has_next = (ti < n_tiles - 1).astype(F32)
    prev2 = pp_ref[0, HALO - 2:HALO - 1, :].astype(F32) * has_prev
    prev1 = pp_ref[0, HALO - 1:HALO, :].astype(F32) * has_prev
    next0 = pn_ref[0, 0:1, :].astype(F32) * has_next
    row = lax.broadcasted_iota(I32, (t, c), 0)
    xm1 = jnp.where(row == 0, prev1, pltpu.roll(cur, 1, 0))
    xm2 = jnp.where(row == 0, prev2, jnp.where(row == 1, prev1, pltpu.roll(cur, 2, 0)))
    xp1 = jnp.where(row == t - 1, next0, pltpu.roll(cur, t - 1, 0))
    cw = cw_ref[...]
    xc = cb_ref[...] + cw[0:1] * xm2 + cw[1:2] * xm1 + cw[2:3] * cur + cw[3:4] * xp1
    xc_s[...] = xc

    for k in range(n_blk):
        xb = xc_s[:, k * blk:(k + 1) * blk].astype(BF16)
        g_s[:, k * 2 * blk:(k + 1) * 2 * blk] = jnp.dot(xb, wg_ref[k], preferred_element_type=F32)

    lam = lam_ref[...]
    softplus_neg_lam = jnp.maximum(-lam, 0.0) + jnp.log1p(jnp.exp(-jnp.abs(lam)))
    ca = -RG_C * softplus_neg_lam
    ba, bx = ba_ref[...], bx_ref[...]
    slabs_per_blk = blk // LANES
    for j in range(SUBLANES):
        r0 = j * s_len
        for k in range(n_slab):
            kb, ks = k // slabs_per_blk, k % slabs_per_blk
            ga = g_s[r0:r0 + s_len, kb * 2 * blk + ks * LANES:kb * 2 * blk + (ks + 1) * LANES]
            gx = g_s[r0:r0 + s_len, kb * 2 * blk + blk + ks * LANES:kb * 2 * blk + blk + (ks + 1) * LANES]
            lane = slice(k * LANES, (k + 1) * LANES)
            xck = xc_s[r0:r0 + s_len, lane]
            r = _sigmoid(ga + ba[:, lane])
            ig = _sigmoid(gx + bx[:, lane])
            log_a = ca[:, lane] * r
            a = jnp.exp(log_a)
            a_s[k, j * pitch:j * pitch + s_len, :] = a
            gain2 = -jnp.tanh(log_a) * (a * a + 1.0)
            b_s[k, j * pitch:j * pitch + s_len, :] = jnp.sqrt(gain2) * (ig * xck)

    def step1(s, hp):
        hs, ps = hp
        srow = (s_len - 1 - s) if reverse else s
        hs2, ps2 = [], []
        for k in range(n_slab):
            av = a_s[k, pl.ds(srow, SUBLANES, stride=pitch), :]
            bv = b_s[k, pl.ds(srow, SUBLANES, stride=pitch), :]
            h = av * hs[k] + bv
            p = av * ps[k]
            hloc_s[k, pl.ds(srow, SUBLANES, stride=pitch), :] = h
            pcum_s[k, pl.ds(srow, SUBLANES, stride=pitch), :] = p
            hs2.append(h)
            ps2.append(p)
        return tuple(hs2), tuple(ps2)

    zeros = tuple(jnp.zeros((SUBLANES, LANES), F32) for _ in range(n_slab))
    ones = tuple(jnp.ones((SUBLANES, LANES), F32) for _ in range(n_slab))
    h_end, p_end = lax.fori_loop(0, s_len, step1, (zeros, ones))

    cins = []
    order = range(SUBLANES - 1, -1, -1) if reverse else range(SUBLANES)
    for k in range(n_slab):
        cst = carry_s[:, k * LANES:(k + 1) * LANES]
        rows = [None] * SUBLANES
        for j in order:
            rows[j] = cst
            cst = p_end[k][j:j + 1] * cst + h_end[k][j:j + 1]
        carry_s[:, k * LANES:(k + 1) * LANES] = cst
        cins.append(jnp.concatenate(rows, axis=0))
    hl_ref[0] = carry_s[...]

    def step2(s, _):
        for k in range(n_slab):
            idx = pl.ds(s, SUBLANES, stride=pitch)
            hloc_s[k, idx, :] = hloc_s[k, idx, :] + pcum_s[k, idx, :] * cins[k]
        return 0

    lax.fori_loop(0, s_len, step2, 0)
    for j in range(SUBLANES):
        for k in range(n_slab):
            h_ref[0, j * s_len:(j + 1) * s_len, k * LANES:(k + 1) * LANES] = (
                hloc_s[k, j * pitch:j * pitch + s_len, :].astype(h_ref.dtype))


def rg_scan(p, col_blk, conv_w, conv_b, wg, ba, bx, lam, h0, *, reverse, tile_l):
    b, l, _ = p.shape
    c = conv_w.shape[1]
    t = min(tile_l, l)
    n_tiles = l // t
    s_len = t // SUBLANES
    pitch = s_len + SUBLANES
    hb = t // HALO
    n_hblk = l // HALO

    def nat(i):
        return (n_tiles - 1 - i) if reverse else i

    body = functools.partial(_scan_body, reverse=reverse, n_tiles=n_tiles, t=t, c=c, s_len=s_len, pitch=pitch)
    vec = pl.BlockSpec((1, c), lambda bi, i: (0, 0))
    return pl.pallas_call(
        body,
        grid=(b, n_tiles),
        in_specs=[pl.BlockSpec((1, t, c), lambda bi, i: (bi, nat(i), col_blk)),
                  pl.BlockSpec((1, HALO, c), lambda bi, i: (bi, jnp.maximum(nat(i) * hb - 1, 0), col_blk)),
                  pl.BlockSpec((1, HALO, c), lambda bi, i: (bi, jnp.minimum((nat(i) + 1) * hb, n_hblk - 1), col_blk)),
                  pl.BlockSpec(conv_w.shape, lambda bi, i: (0, 0)),
                  vec,
                  pl.BlockSpec(wg.shape, lambda bi, i: (0, 0, 0)),
                  vec, vec, vec,
                  pl.BlockSpec((1, 1, c), lambda bi, i: (bi, 0, 0))],
        out_specs=[pl.BlockSpec((1, t, c), lambda bi, i: (bi, nat(i), 0)),
                   pl.BlockSpec((1, 1, c), lambda bi, i: (bi, 0, 0))],
        out_shape=[jax.ShapeDtypeStruct((b, l, c), BF16), jax.ShapeDtypeStruct((b, 1, c), F32)],
        scratch_shapes=[pltpu.VMEM((t, c), F32), pltpu.VMEM((t, 2 * c), F32)]
        + [pltpu.VMEM((c // LANES, SUBLANES * pitch, LANES), F32) for _ in range(4)]
        + [pltpu.VMEM((1, c), F32)],
        compiler_params=_cparams("parallel", "arbitrary"),
        name="rg_scan_bwd" if reverse else "rg_scan_fwd",
    )(p, p, p, conv_w, conv_b, wg, ba, bx, lam, h0)


def gate_blocks(wa, wx, n_blk):
    h, d, _ = wa.shape
    hp = h // n_blk
    eye = jnp.eye(hp, dtype=wa.dtype)

    def bd(w):
        w = w.reshape(n_blk, hp, d, d)
        return jnp.einsum('khde,hg->khdge', w, eye).reshape(n_blk, hp * d, hp * d)

    return jnp.concatenate([bd(wa), bd(wx)], axis=-1).astype(BF16)


HY_HID = 64


def _filter_body(w1t_ref, b1_ref, w2t_ref, b2_ref, fr_ref, w3t_ref, o_ref, z_s, *, l, c, ct, rows_grid):
    d = pl.program_id(0)
    j = pl.program_id(1)
    lane = lax.broadcasted_iota(I32, (1, l), 1)
    s_i = jnp.where(d == 0, lane, l - lane)
    sf = s_i.astype(F32)
    t_norm = sf / float(max(l - 1, 1))

    @pl.when(j == 0)
    def _():
        frow = lax.broadcasted_iota(I32, (HY_HID, 1), 0)
        frf = frow.astype(F32)
        band_step = (HY_SEQ_BANDS - 1 - 1e-4) / (HY_SEQ_BANDS - 1)
        n_seq = 1 + 2 * HY_SEQ_BANDS
        is_seq_cos = (frow >= 1) & (frow < 1 + HY_SEQ_BANDS)
        is_seq_sin = (frow >= 1 + HY_SEQ_BANDS) & (frow < n_seq)
        is_col_cos = (frow >= n_seq + 1) & (frow < n_seq + 1 + HY_COL_BANDS)
        is_col_sin = (frow >= n_seq + 1 + HY_COL_BANDS) & (frow < n_seq + 1 + 2 * HY_COL_BANDS)
        seq_band = 1e-4 + band_step * jnp.where(is_seq_cos, frf - 1.0, frf - (1.0 + HY_SEQ_BANDS))
        col_band = jnp.where(is_col_cos, frf - float(n_seq), frf - float(n_seq + HY_COL_BANDS))
        col_pos = (s_i & (GRID_W - 1)).astype(F32)
        row_lag = (s_i >> int(math.log2(GRID_W))).astype(F32) / float(rows_grid)
        ang = jnp.where(is_seq_cos | is_seq_sin,
                        ((2.0 * math.pi / l) * sf) * seq_band,
                        ((2.0 * math.pi / GRID_W) * col_pos) * col_band)
        feats = jnp.where(is_seq_cos | is_col_cos, jnp.cos(ang),
                          jnp.where(is_seq_sin | is_col_sin, jnp.sin(ang), 0.0))
        feats = jnp.where(frow == 0, t_norm, feats)
        feats = jnp.where(frow == n_seq, row_lag, feats)
        fr = fr_ref[...]
        z = jnp.sin(fr * (jnp.dot(w1t_ref[...], feats, precision=HIGHEST, preferred_element_type=F32) + b1_ref[...]))
        z_s[...] = jnp.sin(fr * (jnp.dot(w2t_ref[...], z, precision=HIGHEST, preferred_element_type=F32) + b2_ref[...]))

    k = jnp.dot(w3t_ref[0], z_s[...], precision=HIGHEST, preferred_element_type=F32)
    ch = (lax.broadcasted_iota(I32, (ct, 1), 0) + j * ct).astype(F32)
    max_decay = math.log(HY_DECAY_TARGET) / HY_FAST_DECAY
    min_decay = math.log(HY_DECAY_TARGET) / HY_SLOW_DECAY
    delta = jnp.abs(min_decay + ch * ((max_decay - min_decay) / (c - 1)))
    k = k * jnp.exp(-t_norm * delta)
    k = jnp.where((d == 1) & (lane == 0), 0.0, k)
    o_ref[0] = k.astype(o_ref.dtype)


def hyena_filter_t(w1t, b1, w2t, b2, fr, w3t, l, tile_c):
    assert GRID_W & (GRID_W - 1) == 0
    c = w3t.shape[1]
    ct = min(tile_c, c)
    body = functools.partial(_filter_body, l=l, c=c, ct=ct, rows_grid=l // GRID_W)
    small = lambda shape: pl.BlockSpec(shape, lambda d, j: (0,) * len(shape))
    return pl.pallas_call(
        body,
        grid=(2, c // ct),
        in_specs=[small(w1t.shape), small(b1.shape), small(w2t.shape), small(b2.shape), small(fr.shape),
                  pl.BlockSpec((1, ct, HY_HID), lambda d, j: (d, j, 0))],
        out_specs=pl.BlockSpec((1, ct, l), lambda d, j: (d, j, 0)),
        out_shape=jax.ShapeDtypeStruct((2, c, l), BF16),
        scratch_shapes=[pltpu.VMEM((HY_HID, l), F32)],
        compiler_params=_cparams("arbitrary", "arbitrary"),
        name="hyena_filter",
    )(w1t, b1, w2t, b2, fr, w3t)


def dft_tables(l):
    import numpy as np
    n = 2 * l
    r_in, nk = l // LANES, n // LANES
    ka = np.arange(nk)[:, None].astype(np.float64)
    r = np.arange(r_in)[None, :].astype(np.float64)
    a1 = 2.0 * np.pi * ka * r / nk
    f1 = np.concatenate([np.cos(a1), -np.sin(a1)], axis=0)
    lane = np.arange(LANES)[None, :].astype(np.float64)
    at = 2.0 * np.pi * ka * lane / n
    twr, twi = np.cos(at), -np.sin(at)
    a2 = 2.0 * np.pi * np.arange(LANES)[:, None] * np.arange(LANES)[None, :] / LANES
    cr, ci = np.cos(a2), -np.sin(a2)
    m2 = np.block([[cr, ci], [-ci, cr]])
    m2i = np.block([[cr, -ci], [ci, cr]])
    ai = 2.0 * np.pi * np.arange(r_in)[:, None] * np.arange(nk)[None, :] / nk
    gi = np.concatenate([np.cos(ai), -np.sin(ai)], axis=1) / n
    as_bf = lambda a: jnp.asarray(a, F32).astype(BF16)
    return as_bf(f1), jnp.asarray(twr, F32), jnp.asarray(twi, F32), as_bf(m2), as_bf(m2i), as_bf(gi)


def _fwd_rows_twiddle(x_c, f1, twr, twi, nk):
    a = jnp.dot(f1, x_c, preferred_element_type=F32)
    re, im = a[:nk], a[nk:]
    return re * twr - im * twi, re * twi + im * twr


def _spectrum_body(k_ref, f1_ref, twr_ref, twi_ref, m2_ref, o_ref, a2_s, *, g, nk, r_in):
    f1, twr, twi = f1_ref[...], twr_ref[...], twi_ref[...]
    sign = jnp.where((lax.broadcasted_iota(I32, (nk, 1), 0) & 1) == 0, 1.0, -1.0)
    for ci in range(g):
        fre, fim = _fwd_rows_twiddle(k_ref[0, ci], f1, twr, twi, nk)
        bre, bim = _fwd_rows_twiddle(k_ref[1, ci], f1, twr, twi, nk)
        a2_s[ci * nk:(ci + 1) * nk, 0:LANES] = (fre + sign * bre).astype(BF16)
        a2_s[ci * nk:(ci + 1) * nk, LANES:2 * LANES] = (fim + sign * bim).astype(BF16)
    spec = jnp.dot(a2_s[...], m2_ref[...], preferred_element_type=F32)
    o_ref[...] = spec.reshape(g, nk, 2 * LANES).astype(o_ref.dtype)


def hyena_spectrum(kt4, tables, group):
    _, c, r_in, _ = kt4.shape
    nk = 2 * r_in
    f1, twr, twi, m2, _, _ = tables
    g = min(group, c)
    full = lambda a: pl.BlockSpec(a.shape, lambda j: (0,) * a.ndim)
    return pl.pallas_call(
        functools.partial(_spectrum_body, g=g, nk=nk, r_in=r_in),
        grid=(c // g,),
        in_specs=[pl.BlockSpec((2, g, r_in, LANES), lambda j: (0, j, 0, 0)), full(f1), full(twr), full(twi), full(m2)],
        out_specs=pl.BlockSpec((g, nk, 2 * LANES), lambda j: (j, 0, 0)),
        out_shape=jax.ShapeDtypeStruct((c, nk, 2 * LANES), BF16),
        scratch_shapes=[pltpu.VMEM((g * nk, 2 * LANES), BF16)],
        compiler_params=_cparams("parallel"),
        name="hyena_spectrum",
    )(kt4, f1, twr, twi, m2)


def _time_conv3(x3, taps_ref, ct, r_in):
    rows = ct * r_in
    x2 = x3.reshape(rows, LANES)
    lane = lax.broadcasted_iota(I32, (1, 1, LANES), 2)
    rr = lax.broadcasted_iota(I32, (1, r_in, 1), 1)
    a = pltpu.roll(x2, 1, 1)
    prev = jnp.where(lane == 0, pltpu.roll(a, 1, 0).reshape(ct, r_in, LANES), a.reshape(ct, r_in, LANES))
    prev = jnp.where((lane == 0) & (rr == 0), 0.0, prev)
    a = pltpu.roll(x2, LANES - 1, 1)
    nxt = jnp.where(lane == LANES - 1, pltpu.roll(a, rows - 1, 0).reshape(ct, r_in, LANES), a.reshape(ct, r_in, LANES))
    nxt = jnp.where((lane == LANES - 1) & (rr == r_in - 1), 0.0, nxt)
    return taps_ref[3] + taps_ref[0] * prev + taps_ref[1] * x3 + taps_ref[2] * nxt


def _fftconv_body(x0_ref, x1_ref, v_ref, t0_ref, t1_ref, tv_ref, skip_ref, k_ref,
                  f1_ref, twr_ref, twi_ref, m2_ref, m2i_ref, gi_ref,
                  o_ref, u_s, z0_s, a2_s, *, ct, g, nk, r_in):
    z0 = _time_conv3(x0_ref[0].astype(F32), t0_ref, ct, r_in)
    z1 = _time_conv3(x1_ref[0].astype(F32), t1_ref, ct, r_in)
    zv = _time_conv3(v_ref[0].astype(F32), tv_ref, ct, r_in)
    u_s[...] = zv * z1
    z0_s[...] = z0
    f1, twr, twi = f1_ref[...], twr_ref[...], twi_ref[...]

    def group(gi_, _):
        c0 = gi_ * g
        for ci in range(g):
            tre, tim = _fwd_rows_twiddle(u_s[c0 + ci].astype(BF16), f1, twr, twi, nk)
            a2_s[ci * nk:(ci + 1) * nk, 0:LANES] = tre.astype(BF16)
            a2_s[ci * nk:(ci + 1) * nk, LANES:2 * LANES] = tim.astype(BF16)
        spec = jnp.dot(a2_s[...], m2_ref[...], preferred_element_type=F32)
        kf = k_ref[pl.ds(c0, g)].astype(F32).reshape(g * nk, 2 * LANES)
        sre, sim = spec[:, :LANES], spec[:, LANES:]
        kre, kim = kf[:, :LANES], kf[:, LANES:]
        prod = jnp.concatenate([sre * kre - sim * kim, sre * kim + sim * kre], axis=1).astype(BF16)
        cc = jnp.dot(prod, m2i_ref[...], preferred_element_type=F32)
        for ci in range(g):
            cre, cim = cc[ci * nk:(ci + 1) * nk, :LANES], cc[ci * nk:(ci + 1) * nk, LANES:]
            st = jnp.concatenate([cre * twr + cim * twi, cim * twr - cre * twi], axis=0).astype(BF16)
            y = jnp.dot(gi_ref[...], st, preferred_element_type=F32)
            u_c = u_s[c0 + ci]
            o_ref[0, c0 + ci] = ((y + u_c * skip_ref[c0 + ci]) * z0_s[c0 + ci]).astype(o_ref.dtype)
        return 0

    lax.fori_loop(0, ct // g, group, 0)


def hyena_fftconv(pt4, taps, skip3, spec, tables, *, tile_c, group):
    b, c3, r_in, _ = pt4.shape
    c = c3 // 3
    nk = 2 * r_in
    ct = min(tile_c, c)
    g = min(group, ct)
    nct = c // ct
    f1, twr, twi, m2, m2i, gi = tables
    full = lambda a: pl.BlockSpec(a.shape, lambda j, bi: (0,) * a.ndim)
    xspec = lambda off: pl.BlockSpec((1, ct, r_in, LANES), lambda j, bi: (bi, off * nct + j, 0, 0))
    tspec = lambda off: pl.BlockSpec((4, ct, 1, 1), lambda j, bi: (0, off * nct + j, 0, 0))
    return pl.pallas_call(
        functools.partial(_fftconv_body, ct=ct, g=g, nk=nk, r_in=r_in),
        grid=(nct, b),
        in_specs=[xspec(0), xspec(1), xspec(2), tspec(0), tspec(1), tspec(2),
                  pl.BlockSpec((ct, 1, 1), lambda j, bi: (j, 0, 0)),
                  pl.BlockSpec((ct, nk, 2 * LANES), lambda j, bi: (j, 0, 0)),
                  full(f1), full(twr), full(twi), full(m2), full(m2i), full(gi)],
        out_specs=pl.BlockSpec((1, ct, r_in, LANES), lambda j, bi: (bi, j, 0, 0)),
        out_shape=jax.ShapeDtypeStruct((b, c, r_in, LANES), BF16),
        scratch_shapes=[pltpu.VMEM((ct, r_in, LANES), F32), pltpu.VMEM((ct, r_in, LANES), F32),
                        pltpu.VMEM((g * nk, 2 * LANES), BF16)],
        compiler_params=_cparams("parallel", "arbitrary"),
        name="hyena_fftconv",
    )(pt4, pt4, pt4, taps, taps, taps, skip3, spec, f1, twr, twi, m2, m2i, gi)


ROUTE_LANES = LANES
NEG_BIG = -1e30
HALF_WORD = 16


def _pack_bf16_pairs(v):
    h = v.shape[1] // 2
    bits = pltpu.bitcast(v.astype(BF16).astype(F32), I32)
    return bits[:, :h] | lax.shift_right_logical(bits[:, h:], HALF_WORD)


def _unpack_bf16_pairs(w):
    hi = pltpu.bitcast(w & jnp.int32(-65536), F32)
    lo = pltpu.bitcast(lax.shift_left(w, HALF_WORD), F32)
    return jnp.concatenate([hi, lo], axis=1)


def _mix_route_body(x_ref, hf_ref, hb_ref, prg_ref, pga_ref, pgb_ref, yt_ref, rgp_ref, hyp_ref, wo_ref, g1_ref,
                    n2g_ref, sh2_ref, sc2_ref, wr_ref, br_ref, tri_ref,
                    x1_ref, hxp_ref, route_ref, cnt_ref, carry_s, *, t, n_exp):
    @pl.when((pl.program_id(0) == 0) & (pl.program_id(1) == 0))
    def _():
        carry_s[...] = jnp.zeros_like(carry_s)

    hsum = hf_ref[0].astype(F32) + hb_ref[0].astype(F32)
    y_rg = (hsum * _gelu_tanh(prg_ref[0].astype(F32))).astype(BF16)
    t1 = jnp.dot(y_rg, rgp_ref[...], preferred_element_type=F32)
    t2 = lax.dot_general(yt_ref[0], hyp_ref[...], (((0,), (0,)), ((), ())), preferred_element_type=F32)
    merged = _sigmoid(pga_ref[0].astype(F32)) * t1 + _sigmoid(pgb_ref[0].astype(F32)) * t2
    out = jnp.dot(merged.astype(BF16), wo_ref[...], preferred_element_type=F32)
    x1 = x_ref[0] + g1_ref[0] * out
    x1_ref[0] = x1
    ms = jnp.mean(x1 * x1, axis=-1, keepdims=True)
    hx2 = (x1 * lax.rsqrt(ms + EPS) * n2g_ref[...]) * (1.0 + sc2_ref[0]) + sh2_ref[0]
    hxp_ref[...] = _pack_bf16_pairs(hx2)

    logits = jnp.dot(hx2, wr_ref[...], precision=HIGHEST, preferred_element_type=F32) + br_ref[...]
    lane = lax.broadcasted_iota(I32, (t, ROUTE_LANES), 1)
    is_g = lane < N_GROUPS
    glog = jnp.where(is_g, logits, NEG_BIG)
    gmax = jnp.max(glog, axis=1, keepdims=True)
    gidx = jnp.min(jnp.where(glog == gmax, lane, ROUTE_LANES), axis=1, keepdims=True)
    gsum = jnp.sum(jnp.where(is_g, jnp.exp(glog - gmax), 0.0), axis=1, keepdims=True)
    p_g = 1.0 / gsum
    e_lane = lane - N_GROUPS
    grp_of_lane = lax.shift_right_arithmetic(e_lane, int(math.log2(EXPERTS_PER_GROUP)))
    in_grp = (e_lane >= 0) & (e_lane < n_exp) & (grp_of_lane == gidx)
    elog = jnp.where(in_grp, logits, NEG_BIG)
    m1 = jnp.max(elog, axis=1, keepdims=True)
    i1 = jnp.min(jnp.where(elog == m1, lane, ROUTE_LANES), axis=1, keepdims=True)
    elog2 = jnp.where(lane == i1, NEG_BIG, elog)
    m2 = jnp.max(elog2, axis=1, keepdims=True)
    i2 = jnp.min(jnp.where(elog2 == m2, lane, ROUTE_LANES), axis=1, keepdims=True)
    e21 = jnp.exp(m2 - m1)
    pk1 = 1.0 / (1.0 + e21)
    wt1, wt2 = p_g * pk1, p_g * (e21 * pk1)

    oh1 = (lane == i1 - N_GROUPS).astype(F32)
    oh2 = (lane == i2 - N_GROUPS).astype(F32)
    cnt = oh1 + oh2
    before = jnp.dot(tri_ref[...], cnt.astype(BF16), preferred_element_type=F32) + carry_s[...]
    rank1 = jnp.sum(oh1 * before, axis=1, keepdims=True)
    rank2 = jnp.sum(oh2 * before, axis=1, keepdims=True)
    carry_s[...] = carry_s[...] + jnp.sum(cnt, axis=0, keepdims=True)
    cnt_ref[...] = carry_s[...]
    vals = ((i1 - N_GROUPS).astype(F32), (i2 - N_GROUPS).astype(F32), rank1, rank2, wt1, wt2)
    route = jnp.zeros((t, ROUTE_LANES), F32)
    for k, v in enumerate(vals):
        route = jnp.where(lane == k, v, route)
    route_ref[...] = route


def mix_route(x, h_f, h_b, p_rm, y_hy_t, rg_proj, hy_proj, w_out, g1, n2g, sh2, sc2, wr, br, *, tile_l, n_exp):
    b, l, d = x.shape
    c = h_f.shape[2]
    t = min(tile_l, l)
    nt = l // t
    n = b * l
    tri = (jnp.arange(t)[:, None] > jnp.arange(t)[None, :]).astype(BF16)
    tok = lambda bi, i: (bi, i, 0)
    col = lambda k: (lambda bi, i: (bi, i, k))
    full2 = lambda a: pl.BlockSpec(a.shape, lambda bi, i: (0, 0))
    per_b = pl.BlockSpec((1, 1, d), lambda bi, i: (bi, 0, 0))
    row = lambda bi, i: (bi * nt + i, 0)
    return pl.pallas_call(
        functools.partial(_mix_route_body, t=t, n_exp=n_exp),
        grid=(b, nt),
        in_specs=[pl.BlockSpec((1, t, d), tok), pl.BlockSpec((1, t, c), tok), pl.BlockSpec((1, t, c), tok),
                  pl.BlockSpec((1, t, c), col(1)), pl.BlockSpec((1, t, c), col(2)), pl.BlockSpec((1, t, c), col(3)),
                  pl.BlockSpec((1, c, t), lambda bi, i: (bi, 0, i)),
                  full2(rg_proj), full2(hy_proj), full2(w_out), per_b,
                  full2(n2g), per_b, per_b, full2(wr), full2(br), full2(tri)],
        out_specs=[pl.BlockSpec((1, t, d), tok), pl.BlockSpec((t, d // 2), row),
                   pl.BlockSpec((t, ROUTE_LANES), row), pl.BlockSpec((1, ROUTE_LANES), lambda bi, i: (0, 0))],
        out_shape=[jax.ShapeDtypeStruct((b, l, d), F32), jax.ShapeDtypeStruct((n, d // 2), I32),
                   jax.ShapeDtypeStruct((n, ROUTE_LANES), F32), jax.ShapeDtypeStruct((1, ROUTE_LANES), F32)],
        scratch_shapes=[pltpu.VMEM((1, ROUTE_LANES), F32)],
        compiler_params=_cparams("arbitrary", "arbitrary"),
        name="mix_route",
    )(x, h_f, h_b, p_rm, p_rm, p_rm, y_hy_t, rg_proj, hy_proj, w_out, g1, n2g, sh2, sc2, wr, br, tri)


def _dest_body(route_ref, cnt_ref, ut_ref, dest_ref, blk_ref, *, t, n_exp, nb_pad):
    lane1 = lax.broadcasted_iota(I32, (1, ROUTE_LANES), 1)
    padded = jnp.floor((cnt_ref[...] + (MOE_BLOCK - 1.0)) * (1.0 / MOE_BLOCK)) * MOE_BLOCK
    padded = jnp.where(lane1 < n_exp, padded, 0.0)
    pend = jnp.dot(jnp.broadcast_to(padded, (SUBLANES, ROUTE_LANES)), ut_ref[...], precision=HIGHEST,
                   preferred_element_type=F32)[0:1]
    pstart = pend - padded
    route = route_ref[...]
    lane = lax.broadcasted_iota(I32, (t, ROUTE_LANES), 1)
    lf = lane.astype(F32)
    d1 = jnp.sum(jnp.where(lf == route[:, 0:1], pstart, 0.0), axis=1, keepdims=True) + route[:, 2:3]
    d2 = jnp.sum(jnp.where(lf == route[:, 1:2], pstart, 0.0), axis=1, keepdims=True) + route[:, 3:4]
    dmat = jnp.where(lane == 0, d1, jnp.where(lane == 1, d2, 0.0))
    dest_ref[...] = dmat.T[0:SUBLANES].astype(I32)
    first_row = lax.broadcasted_iota(I32, (nb_pad, ROUTE_LANES), 0).astype(F32) * float(MOE_BLOCK)
    lane_b = lax.broadcasted_iota(I32, (nb_pad, ROUTE_LANES), 1)
    nle = jnp.sum(jnp.where((lane_b < n_exp) & (pend <= first_row), 1.0, 0.0), axis=1, keepdims=True)
    blk_ref[...] = jnp.broadcast_to(jnp.minimum(nle, n_exp - 1.0), (nb_pad, ROUTE_LANES)).astype(I32)


def moe_dest(route, cnt, *, tile, n_exp, n_blocks):
    n = route.shape[0]
    t = min(tile, n)
    nb_pad = -(-n_blocks // SUBLANES) * SUBLANES
    ut = (jnp.arange(ROUTE_LANES)[:, None] <= jnp.arange(ROUTE_LANES)[None, :]).astype(F32)
    return pl.pallas_call(
        functools.partial(_dest_body, t=t, n_exp=n_exp, nb_pad=nb_pad),
        grid=(n // t,),
        in_specs=[pl.BlockSpec((t, ROUTE_LANES), lambda i: (i, 0)),
                  pl.BlockSpec((1, ROUTE_LANES), lambda i: (0, 0)),
                  pl.BlockSpec((ROUTE_LANES, ROUTE_LANES), lambda i: (0, 0))],
        out_specs=[pl.BlockSpec((SUBLANES, t), lambda i: (i, 0)),
                   pl.BlockSpec((nb_pad, ROUTE_LANES), lambda i: (0, 0))],
        out_shape=[jax.ShapeDtypeStruct((n // t * SUBLANES, t), I32),
                   jax.ShapeDtypeStruct((nb_pad, ROUTE_LANES), I32)],
        compiler_params=_cparams("arbitrary"),
        name="moe_dest",
    )(route, cnt, ut)


def _scatter_body(dest_ref, hx_ref, xb_in_ref, xb_ref, sem, *, t):
    del xb_in_ref

    def issue(r, _):
        for k in range(2):
            pltpu.make_async_copy(hx_ref.at[pl.ds(r, 1)], xb_ref.at[pl.ds(dest_ref[k, r], 1)], sem).start()
        return 0

    lax.fori_loop(0, t, issue, 0, unroll=8)
    for k in range(2):
        pltpu.make_async_copy(hx_ref, xb_ref.at[pl.ds(0, t)], sem).wait()


def moe_scatter(dest, hxp, n_rows, *, tile):
    n, hw = hxp.shape
    t = min(tile, n)
    xb0 = jnp.zeros((n_rows, hw), I32)
    return pl.pallas_call(
        functools.partial(_scatter_body, t=t),
        grid=(n // t,),
        in_specs=[pl.BlockSpec((SUBLANES, t), lambda i: (i, 0), memory_space=pltpu.SMEM),
                  pl.BlockSpec((t, hw), lambda i: (i, 0)),
                  pl.BlockSpec(memory_space=pl.ANY)],
        out_specs=pl.BlockSpec(memory_space=pl.ANY),
        out_shape=jax.ShapeDtypeStruct((n_rows, hw), I32),
        scratch_shapes=[pltpu.SemaphoreType.DMA],
        input_output_aliases={2: 0},
        compiler_params=_cparams("arbitrary"),
        name="moe_scatter",
    )(dest, hxp, xb0)


def _expert_body(blk_ref, xb_ref, w1_ref, w3_ref, w2_ref, yb_ref, w1_s, w3_s, w2_s):
    i = pl.program_id(0)
    changed = (i == 0) | (blk_ref[i] != blk_ref[jnp.maximum(i - 1, 0)])

    @pl.when(changed)
    def _():
        w1_s[...] = w1_ref[0].astype(BF16)
        w3_s[...] = w3_ref[0].astype(BF16)
        w2_s[...] = w2_ref[0].astype(BF16)

    xblk = _unpack_bf16_pairs(xb_ref[...]).astype(BF16)
    h1 = jnp.dot(xblk, w1_s[...], preferred_element_type=F32)
    h3 = jnp.dot(xblk, w3_s[...], preferred_element_type=F32)
    hid = (h1 * _sigmoid(h1) * h3).astype(BF16)
    yb_ref[...] = _pack_bf16_pairs(jnp.dot(hid, w2_s[...], preferred_element_type=F32))


def moe_experts(blk_exp, xb, w1, w3, w2):
    p, hw = xb.shape
    _, d, de = w1.shape
    nb = p // MOE_BLOCK
    grid_spec = pltpu.PrefetchScalarGridSpec(
        num_scalar_prefetch=1,
        grid=(nb,),
        in_specs=[pl.BlockSpec((MOE_BLOCK, hw), lambda i, blk: (i, 0)),
                  pl.BlockSpec((1, d, de), lambda i, blk: (blk[i], 0, 0)),
                  pl.BlockSpec((1, d, de), lambda i, blk: (blk[i], 0, 0)),
                  pl.BlockSpec((1, de, d), lambda i, blk: (blk[i], 0, 0))],
        out_specs=pl.BlockSpec((MOE_BLOCK, hw), lambda i, blk: (i, 0)),
        scratch_shapes=[pltpu.VMEM((d, de), BF16), pltpu.VMEM((d, de), BF16), pltpu.VMEM((de, d), BF16)],
    )
    return pl.pallas_call(
        _expert_body,
        grid_spec=grid_spec,
        out_shape=jax.ShapeDtypeStruct((p, hw), I32),
        compiler_params=_cparams("arbitrary"),
        name="moe_experts",
    )(blk_exp, xb, w1, w3, w2)


def _combine_body(dest_ref, x1_ref, route_ref, g2_ref, fg_ref, yb_ref, o_ref, y1_s, y2_s, sem, *, t):
    def issue(r, _):
        pltpu.make_async_copy(yb_ref.at[pl.ds(dest_ref[0, r], 1)], y1_s.at[pl.ds(r, 1)], sem).start()
        pltpu.make_async_copy(yb_ref.at[pl.ds(dest_ref[1, r], 1)], y2_s.at[pl.ds(r, 1)], sem).start()
        return 0

    lax.fori_loop(0, t, issue, 0, unroll=8)
    for y_s in (y1_s, y2_s):
        pltpu.make_async_copy(yb_ref.at[pl.ds(0, t)], y_s, sem).wait()
    route = route_ref[...]
    moe = route[:, 4:5] * _unpack_bf16_pairs(y1_s[...]) + route[:, 5:6] * _unpack_bf16_pairs(y2_s[...])
    x2 = x1_ref[0] + g2_ref[0] * moe
    ms = jnp.mean(x2 * x2, axis=-1, keepdims=True)
    o_ref[0] = x2 * lax.rsqrt(ms + EPS) * fg_ref[...]


def moe_combine(dest, x1, route, g2, final_g, yb, *, tile_l):
    b, l, d = x1.shape
    t = min(tile_l, l)
    nt = l // t
    hw = yb.shape[1]
    return pl.pallas_call(
        functools.partial(_combine_body, t=t),
        grid=(b, nt),
        in_specs=[pl.BlockSpec((SUBLANES, t), lambda bi, i: (bi * nt + i, 0), memory_space=pltpu.SMEM),
                  pl.BlockSpec((1, t, d), lambda bi, i: (bi, i, 0)),
                  pl.BlockSpec((t, ROUTE_LANES), lambda bi, i: (bi * nt + i, 0)),
                  pl.BlockSpec((1, 1, d), lambda bi, i: (bi, 0, 0)),
                  pl.BlockSpec((1, d), lambda bi, i: (0, 0)),
                  pl.BlockSpec(memory_space=pl.ANY)],
        out_specs=pl.BlockSpec((1, t, d), lambda bi, i: (bi, i, 0)),
        out_shape=jax.ShapeDtypeStruct((b, l, d), F32),
        scratch_shapes=[pltpu.VMEM((t, hw), I32), pltpu.VMEM((t, hw), I32), pltpu.SemaphoreType.DMA],
        compiler_params=_cparams("arbitrary", "arbitrary"),
        name="moe_combine",
    )(dest, x1, route, g2, final_g, yb)


def kernel(x, c, ctx, c_ctx, ada_w, ada_b, norm1_g, norm2_g, final_g, w_in, rg_conv_w, rg_conv_b, rg_wa_f, rg_ba_f, rg_wx_f, rg_bx_f, rg_lam_f, rg_wa_b, rg_ba_b, rg_wx_b, rg_bx_b, rg_lam_b, rg_proj, hy_conv_w, hy_conv_b, hy_pos_w1, hy_pos_b1, hy_pos_w2, hy_pos_b2, hy_freq, hy_pos_w3, hy_skip, hy_proj, w_out, moe_wg, moe_bg, moe_we, moe_be, moe_w1, moe_w3, moe_w2):
    B, L, D = x.shape
    C = rg_conv_w.shape[-1]
    LC = ctx.shape[1]
    c8 = jnp.zeros((8, D), F32).at[:B].set(c).at[B].set(c_ctx)
    mods = ada_mods(c8, ada_w[0], ada_b)
    sh1, sc1, g1 = (mods[:B, None, k * D:(k + 1) * D] for k in range(3))
    sh2, sc2, g2 = (mods[:B, None, k * D:(k + 1) * D] for k in range(3, 6))
    csh1 = jnp.broadcast_to(mods[B:B + 1, None, 0:D], (B, 1, D))
    csc1 = jnp.broadcast_to(mods[B:B + 1, None, D:2 * D], (B, 1, D))

    w_in_b = w_in[0].astype(BF16)
    w_rm = jnp.concatenate([w_in_b[:, :2 * C], w_in_b[:, 5 * C:]], axis=1)
    w_hy_t = w_in_b[:, 2 * C:5 * C].T
    wg_f = gate_blocks(rg_wa_f[0], rg_wx_f[0], C // 256)
    wg_b = gate_blocks(rg_wa_b[0], rg_wx_b[0], C // 256)
    rg_f = (rg_conv_w[0], rg_conv_b, wg_f, rg_ba_f, rg_bx_f, rg_lam_f)
    rg_b = (rg_conv_w[0], rg_conv_b, wg_b, rg_ba_b, rg_bx_b, rg_lam_b)

    pc = norm_mod_proj(ctx, norm1_g, csh1, csc1, w_rm[:, :C], transposed=False, tile_l=LC, tile_n=C)
    zero = jnp.zeros((B, 1, C), F32)
    _, hcf = rg_scan(pc, 0, *rg_f, zero, reverse=False, tile_l=256)
    _, hcb = rg_scan(pc, 0, *rg_b, zero, reverse=True, tile_l=256)

    p_rm = norm_mod_proj(x, norm1_g, sh1, sc1, w_rm, transposed=False, tile_l=512, tile_n=1024)
    p_t = norm_mod_proj(x, norm1_g, sh1, sc1, w_hy_t, transposed=True, tile_l=512, tile_n=1024)
    h_f, _ = rg_scan(p_rm, 0, *rg_f, hcf, reverse=False, tile_l=256)
    h_b, _ = rg_scan(p_rm, 0, *rg_b, hcb, reverse=True, tile_l=256)

    tables = dft_tables(L)
    w1t = jnp.zeros((HY_HID, HY_HID), F32).at[:, :hy_pos_w1.shape[1]].set(hy_pos_w1[0].T)
    kt = hyena_filter_t(w1t, hy_pos_b1[0][:, None], hy_pos_w2[0].T, hy_pos_b2[0][:, None], hy_freq[0][:, None],
                        hy_pos_w3[0].T.reshape(2, C, HY_HID), L, 256)
    spec = hyena_spectrum(kt.reshape(2, C, L // LANES, LANES), tables, 8)
    taps = jnp.concatenate([hy_conv_w[0], hy_conv_b], axis=0)[:, :, None, None]
    y_hy_t = hyena_fftconv(p_t.reshape(B, 3 * C, L // LANES, LANES), taps, hy_skip[0][:, None, None], spec, tables,
                           tile_c=64, group=8).reshape(B, C, L)

    n_exp = moe_we.shape[-1]
    n_grp = moe_wg.shape[-1]
    assert n_grp == N_GROUPS and n_exp == N_GROUPS * EXPERTS_PER_GROUP
    wr = jnp.zeros((D, ROUTE_LANES), F32).at[:, :n_grp].set(moe_wg[0]).at[:, n_grp:n_grp + n_exp].set(moe_we[0])
    br = jnp.zeros((1, ROUTE_LANES), F32).at[:, :n_grp].set(moe_bg).at[:, n_grp:n_grp + n_exp].set(moe_be)
    x1, hxp, route, cnt = mix_route(x, h_f, h_b, p_rm, y_hy_t, rg_proj[0].astype(BF16), hy_proj[0].astype(BF16),
                                    w_out[0].astype(BF16), g1, norm2_g, sh2, sc2, wr, br, tile_l=512, n_exp=n_exp)

    n_blocks = (2 * B * L + n_exp * (MOE_BLOCK - 1)) // MOE_BLOCK
    dest, blk = moe_dest(route, cnt, tile=512, n_exp=n_exp, n_blocks=n_blocks)
    xb = moe_scatter(dest, hxp, n_blocks * MOE_BLOCK, tile=512)
    yb = moe_experts(blk[:n_blocks, 0], xb, moe_w1[0], moe_w3[0], moe_w2[0])
    return moe_combine(dest, x1, route, g2, final_g[None], yb, tile_l=512)
```

```python
import functools
import math

import jax
import jax.numpy as jnp
from jax import lax
from jax.experimental import pallas as pl
from jax.experimental.pallas import tpu as pltpu

F32 = jnp.float32
BF16 = jnp.bfloat16
I32 = jnp.int32
HIGHEST = lax.Precision.HIGHEST

LANES = 128
SUBLANES = 8
EPS = 1e-6
RG_C = 8.0
RG_HEAD_DIM = 64
GRID_W = 64
HY_SEQ_BANDS = 16
HY_COL_BANDS = 8
HY_DECAY_TARGET = 1e-2
HY_FAST_DECAY = 0.3
HY_SLOW_DECAY = 1.5
N_GROUPS = 4
EXPERTS_PER_GROUP = 8
MOE_BLOCK = 512
VMEM_LIMIT = 56 * 1024 * 1024


def _cparams(*sem):
    return pltpu.CompilerParams(dimension_semantics=sem, vmem_limit_bytes=VMEM_LIMIT)


def _sigmoid(x):
    return 0.5 * (jnp.tanh(0.5 * x) + 1.0)


def _gelu_tanh(x):
    c = math.sqrt(2.0 / math.pi)
    return 0.5 * x * (1.0 + jnp.tanh(c * (x + 0.044715 * (x * x * x))))


def _ada_body(c_ref, w_ref, b_ref, o_ref):
    c = c_ref[...]
    s = c * _sigmoid(c)
    o_ref[...] = jnp.dot(s, w_ref[...], precision=HIGHEST, preferred_element_type=F32) + b_ref[...]


def ada_mods(c8, ada_w, ada_b):
    d, m = ada_w.shape
    tn = 1024 if m % 1024 == 0 else m
    return pl.pallas_call(
        _ada_body,
        grid=(m // tn,),
        in_specs=[pl.BlockSpec((c8.shape[0], d), lambda j: (0, 0)),
                  pl.BlockSpec((d, tn), lambda j: (0, j)),
                  pl.BlockSpec((1, tn), lambda j: (0, j))],
        out_specs=pl.BlockSpec((c8.shape[0], tn), lambda j: (0, j)),
        out_shape=jax.ShapeDtypeStruct((c8.shape[0], m), F32),
        compiler_params=_cparams("parallel"),
        name="ada_mods",
    )(c8, ada_w, ada_b)


def _proj_body(x_ref, g_ref, sh_ref, sc_ref, w_ref, o_ref, *, transposed, chunk):
    x = x_ref[0]
    ms = jnp.mean(x * x, axis=-1, keepdims=True)
    y = x * lax.rsqrt(ms + EPS) * g_ref[...]
    hx = (y * (1.0 + sc_ref[0]) + sh_ref[0]).astype(BF16)
    m = w_ref.shape[0] if transposed else w_ref.shape[1]
    for j in range(m // chunk):
        cols = slice(j * chunk, (j + 1) * chunk)
        if transposed:
            o_ref[0, cols, :] = lax.dot_general(w_ref[cols, :], hx, (((1,), (1,)), ((), ())),
                                                preferred_element_type=F32).astype(o_ref.dtype)
        else:
            o_ref[0, :, cols] = jnp.dot(hx, w_ref[:, cols], preferred_element_type=F32).astype(o_ref.dtype)


def norm_mod_proj(x, g, shift, scale, w, *, transposed, tile_l, chunk):
    b, l, d = x.shape
    m = w.shape[0] if transposed else w.shape[1]
    tl = min(tile_l, l)
    if transposed:
        o_spec = pl.BlockSpec((1, m, tl), lambda bi, i: (bi, 0, i))
        o_shape = jax.ShapeDtypeStruct((b, m, l), BF16)
    else:
        o_spec = pl.BlockSpec((1, tl, m), lambda bi, i: (bi, i, 0))
        o_shape = jax.ShapeDtypeStruct((b, l, m), BF16)
    return pl.pallas_call(
        functools.partial(_proj_body, transposed=transposed, chunk=min(chunk, m)),
        grid=(b, l // tl),
        in_specs=[pl.BlockSpec((1, tl, d), lambda bi, i: (bi, i, 0)),
                  pl.BlockSpec((1, d), lambda bi, i: (0, 0)),
                  pl.BlockSpec((1, 1, d), lambda bi, i: (bi, 0, 0)),
                  pl.BlockSpec((1, 1, d), lambda bi, i: (bi, 0, 0)),
                  pl.BlockSpec(w.shape, lambda bi, i: (0, 0))],
        out_specs=o_spec,
        out_shape=o_shape,
        compiler_params=_cparams("parallel", "parallel"),
        name="norm_mod_proj_t" if transposed else "norm_mod_proj",
    )(x, g, shift, scale, w)


HALO = 16


def _scan_body(pc_ref, pp_ref, pn_ref, cw_ref, cb_ref, wg_ref, ba_ref, bx_ref, lam_ref, h0_ref,
               h_ref, hl_ref, xc_s, g_s, a_s, b_s, hloc_s, pcum_s, carry_s,
               *, reverse, n_tiles, t, c, s_len, pitch):
    i = pl.program_id(1)
    ti = (n_tiles - 1 - i) if reverse else i
    n_slab = c // LANES
    n_blk = wg_ref.shape[0]
    blk = c // n_blk

    @pl.when(i == 0)
    def _():
        carry_s[...] = h0_ref[0]

    cur = pc_ref[0].astype(F32)
    has_prev = (ti > 0).astype(F32)
    has_next = (ti < n_tiles - 1).astype(F32)
    prev2 = pp_ref[0, HALO - 2:HALO - 1, :].astype(F32) * has_prev
    prev1 = pp_ref[0, HALO - 1:HALO, :].astype(F32) * has_prev
    next0 = pn_ref[0, 0:1, :].astype(F32) * has_next
    row = lax.broadcasted_iota(I32, (t, c), 0)
    xm1 = jnp.where(row == 0, prev1, pltpu.roll(cur, 1, 0))
    xm2 = jnp.where(row == 0, prev2, jnp.where(row == 1, prev1, pltpu.roll(cur, 2, 0)))
    xp1 = jnp.where(row == t - 1, next0, pltpu.roll(cur, t - 1, 0))
    cw = cw_ref[...]
    xc = cb_ref[...] + cw[0:1] * xm2 + cw[1:2] * xm1 + cw[2:3] * cur + cw[3:4] * xp1
    xc_s[...] = xc

    for k in range(n_blk):
        xb = xc_s[:, k * blk:(k + 1) * blk].astype(BF16)
        g_s[:, k * 2 * blk:(k + 1) * 2 * blk] = jnp.dot(xb, wg_ref[k], preferred_element_type=F32)

    lam = lam_ref[...]
    softplus_neg_lam = jnp.maximum(-lam, 0.0) + jnp.log1p(jnp.exp(-jnp.abs(lam)))
    half_ca = (-0.5 * RG_C) * softplus_neg_lam
    half_ba, half_bx = 0.5 * ba_ref[...], 0.5 * bx_ref[...]
    slabs_per_blk = blk // LANES
    for j in range(SUBLANES):
        r0 = j * s_len
        for k in range(n_slab):
            kb, ks = k // slabs_per_blk, k % slabs_per_blk
            ga = g_s[r0:r0 + s_len, kb * 2 * blk + ks * LANES:kb * 2 * blk + (ks + 1) * LANES]
            gx = g_s[r0:r0 + s_len, kb * 2 * blk + blk + ks * LANES:kb * 2 * blk + blk + (ks + 1) * LANES]
            lane = slice(k * LANES, (k + 1) * LANES)
            half_x = 0.5 * xc_s[r0:r0 + s_len, lane]
            hca = half_ca[:, lane]
            log_a = hca * jnp.tanh(ga + half_ba[:, lane]) + hca
            gated_x = half_x * jnp.tanh(gx + half_bx[:, lane]) + half_x
            a = jnp.exp(log_a)
            a_s[k, j * pitch:j * pitch + s_len, :] = a
            gain2 = -jnp.tanh(log_a) * (a * a + 1.0)
            gain = jnp.where(gain2 > 0.0, gain2 * lax.rsqrt(gain2), 0.0)
            b_s[k, j * pitch:j * pitch + s_len, :] = gain * gated_x

    def step1(s, hp):
        hs, ps = hp
        srow = (s_len - 1 - s) if reverse else s
        hs2, ps2 = [], []
        for k in range(n_slab):
            av = a_s[k, pl.ds(srow, SUBLANES, stride=pitch), :]
            bv = b_s[k, pl.ds(srow, SUBLANES, stride=pitch), :]
            h = av * hs[k] + bv
            p = av * ps[k]
            hloc_s[k, pl.ds(srow, SUBLANES, stride=pitch), :] = h
            pcum_s[k, pl.ds(srow, SUBLANES, stride=pitch), :] = p
            hs2.append(h)
            ps2.append(p)
        return tuple(hs2), tuple(ps2)

    zeros = tuple(jnp.zeros((SUBLANES, LANES), F32) for _ in range(n_slab))
    ones = tuple(jnp.ones((SUBLANES, LANES), F32) for _ in range(n_slab))
    h_end, p_end = lax.fori_loop(0, s_len, step1, (zeros, ones))

    order = range(SUBLANES - 1, -1, -1) if reverse else range(SUBLANES)
    for k in range(n_slab):
        cst = carry_s[:, k * LANES:(k + 1) * LANES]
        for j in order:
            rows = slice(j * pitch, j * pitch + s_len)
            h_ref[0, j * s_len:(j + 1) * s_len, k * LANES:(k + 1) * LANES] = (
                hloc_s[k, rows, :] + pcum_s[k, rows, :] * cst).astype(h_ref.dtype)
            cst = p_end[k][j:j + 1] * cst + h_end[k][j:j + 1]
        carry_s[:, k * LANES:(k + 1) * LANES] = cst
    hl_ref[0] = carry_s[...]


def rg_scan(p, col_blk, conv_w, conv_b, wg, ba, bx, lam, h0, *, reverse, tile_l):
    b, l, _ = p.shape
    c = conv_w.shape[1]
    t = min(tile_l, l)
    n_tiles = l // t
    s_len = t // SUBLANES
    pitch = s_len + SUBLANES
    hb = t // HALO
    n_hblk = l // HALO

    def nat(i):
        return (n_tiles - 1 - i) if reverse else i

    body = functools.partial(_scan_body, reverse=reverse, n_tiles=n_tiles, t=t, c=c, s_len=s_len, pitch=pitch)
    vec = pl.BlockSpec((1, c), lambda bi, i: (0, 0))
    return pl.pallas_call(
        body,
        grid=(b, n_tiles),
        in_specs=[pl.BlockSpec((1, t, c), lambda bi, i: (bi, nat(i), col_blk)),
                  pl.BlockSpec((1, HALO, c), lambda bi, i: (bi, jnp.maximum(nat(i) * hb - 1, 0), col_blk)),
                  pl.BlockSpec((1, HALO, c), lambda bi, i: (bi, jnp.minimum((nat(i) + 1) * hb, n_hblk - 1), col_blk)),
                  pl.BlockSpec(conv_w.shape, lambda bi, i: (0, 0)),
                  vec,
                  pl.BlockSpec(wg.shape, lambda bi, i: (0, 0, 0)),
                  vec, vec, vec,
                  pl.BlockSpec((1, 1, c), lambda bi, i: (bi, 0, 0))],
        out_specs=[pl.BlockSpec((1, t, c), lambda bi, i: (bi, nat(i), 0)),
                   pl.BlockSpec((1, 1, c), lambda bi, i: (bi, 0, 0))],
        out_shape=[jax.ShapeDtypeStruct((b, l, c), BF16), jax.ShapeDtypeStruct((b, 1, c), F32)],
        scratch_shapes=[pltpu.VMEM((t, c), F32), pltpu.VMEM((t, 2 * c), F32)]
        + [pltpu.VMEM((c // LANES, SUBLANES * pitch, LANES), F32) for _ in range(4)]
        + [pltpu.VMEM((1, c), F32)],
        compiler_params=_cparams("parallel", "arbitrary"),
        name="rg_scan_bwd" if reverse else "rg_scan_fwd",
    )(p, p, p, conv_w, conv_b, wg, ba, bx, lam, h0)


def gate_blocks(wa, wx, n_blk):
    h, d, _ = wa.shape
    hp = h // n_blk
    eye = jnp.eye(hp, dtype=wa.dtype)

    def bd(w):
        w = w.reshape(n_blk, hp, d, d)
        return jnp.einsum('khde,hg->khdge', w, eye).reshape(n_blk, hp * d, hp * d)

    return (0.5 * jnp.concatenate([bd(wa), bd(wx)], axis=-1)).astype(BF16)


HY_HID = 64


def _filter_body(w1t_ref, b1_ref, w2t_ref, b2_ref, fr_ref, w3t_ref, o_ref, z_s, *, l, c, ct, rows_grid):
    d = pl.program_id(0)
    j = pl.program_id(1)
    lane = lax.broadcasted_iota(I32, (1, l), 1)
    s_i = jnp.where(d == 0, lane, l - lane)
    sf = s_i.astype(F32)
    t_norm = sf / float(max(l - 1, 1))

    @pl.when(j == 0)
    def _():
        frow = lax.broadcasted_iota(I32, (HY_HID, 1), 0)
        frf = frow.astype(F32)
        band_step = (HY_SEQ_BANDS - 1 - 1e-4) / (HY_SEQ_BANDS - 1)
        n_seq = 1 + 2 * HY_SEQ_BANDS
        is_seq_cos = (frow >= 1) & (frow < 1 + HY_SEQ_BANDS)
        is_seq_sin = (frow >= 1 + HY_SEQ_BANDS) & (frow < n_seq)
        is_col_cos = (frow >= n_seq + 1) & (frow < n_seq + 1 + HY_COL_BANDS)
        is_col_sin = (frow >= n_seq + 1 + HY_COL_BANDS) & (frow < n_seq + 1 + 2 * HY_COL_BANDS)
        seq_band = 1e-4 + band_step * jnp.where(is_seq_cos, frf - 1.0, frf - (1.0 + HY_SEQ_BANDS))
        col_band = jnp.where(is_col_cos, frf - float(n_seq), frf - float(n_seq + HY_COL_BANDS))
        col_pos = (s_i & (GRID_W - 1)).astype(F32)
        row_lag = (s_i >> int(math.log2(GRID_W))).astype(F32) / float(rows_grid)
        ang = jnp.where(is_seq_cos | is_seq_sin,
                        ((2.0 * math.pi / l) * sf) * seq_band,
                        ((2.0 * math.pi / GRID_W) * col_pos) * col_band)
        feats = jnp.where(is_seq_cos | is_col_cos, jnp.cos(ang),
                          jnp.where(is_seq_sin | is_col_sin, jnp.sin(ang), 0.0))
        feats = jnp.where(frow == 0, t_norm, feats)
        feats = jnp.where(frow == n_seq, row_lag, feats)
        fr = fr_ref[...]
        z = jnp.sin(fr * (jnp.dot(w1t_ref[...], feats, precision=HIGHEST, preferred_element_type=F32) + b1_ref[...]))
        z_s[...] = jnp.sin(fr * (jnp.dot(w2t_ref[...], z, precision=HIGHEST, preferred_element_type=F32) + b2_ref[...]))

    k = jnp.dot(w3t_ref[0], z_s[...], precision=HIGHEST, preferred_element_type=F32)
    ch = (lax.broadcasted_iota(I32, (ct, 1), 0) + j * ct).astype(F32)
    max_decay = math.log(HY_DECAY_TARGET) / HY_FAST_DECAY
    min_decay = math.log(HY_DECAY_TARGET) / HY_SLOW_DECAY
    delta = jnp.abs(min_decay + ch * ((max_decay - min_decay) / (c - 1)))
    k = k * jnp.exp(-t_norm * delta)
    k = jnp.where((d == 1) & (lane == 0), 0.0, k)
    o_ref[0] = k.astype(o_ref.dtype)


def hyena_filter_t(w1t, b1, w2t, b2, fr, w3t, l, tile_c):
    assert GRID_W & (GRID_W - 1) == 0
    c = w3t.shape[1]
    ct = min(tile_c, c)
    body = functools.partial(_filter_body, l=l, c=c, ct=ct, rows_grid=l // GRID_W)
    small = lambda shape: pl.BlockSpec(shape, lambda d, j: (0,) * len(shape))
    return pl.pallas_call(
        body,
        grid=(2, c // ct),
        in_specs=[small(w1t.shape), small(b1.shape), small(w2t.shape), small(b2.shape), small(fr.shape),
                  pl.BlockSpec((1, ct, HY_HID), lambda d, j: (d, j, 0))],
        out_specs=pl.BlockSpec((1, ct, l), lambda d, j: (d, j, 0)),
        out_shape=jax.ShapeDtypeStruct((2, c, l), BF16),
        scratch_shapes=[pltpu.VMEM((HY_HID, l), F32)],
        compiler_params=_cparams("arbitrary", "arbitrary"),
        name="hyena_filter",
    )(w1t, b1, w2t, b2, fr, w3t)


def dft_tables(l):
    import numpy as np
    n = 2 * l
    r_in, nk = l // LANES, n // LANES
    ka = np.arange(nk)[:, None].astype(np.float64)
    r = np.arange(r_in)[None, :].astype(np.float64)
    a1 = 2.0 * np.pi * ka * r / nk
    f1 = np.concatenate([np.cos(a1), -np.sin(a1)], axis=0)
    lane = np.arange(LANES)[None, :].astype(np.float64)
    at = 2.0 * np.pi * ka * lane / n
    twr, twi = np.cos(at), -np.sin(at)
    a2 = 2.0 * np.pi * np.arange(LANES)[:, None] * np.arange(LANES)[None, :] / LANES
    cr, ci = np.cos(a2), -np.sin(a2)
    m2 = np.block([[cr, ci], [-ci, cr]])
    m2i = np.block([[cr, -ci], [ci, cr]])
    ai = 2.0 * np.pi * np.arange(r_in)[:, None] * np.arange(nk)[None, :] / nk
    gi = np.concatenate([np.cos(ai), -np.sin(ai)], axis=1) / n
    as_bf = lambda a: jnp.asarray(a, F32).astype(BF16)
    return as_bf(f1), jnp.asarray(twr, F32), jnp.asarray(twi, F32), as_bf(m2), as_bf(m2i), as_bf(gi)


def _fwd_rows_twiddle(x_a, x_b, f1, twr, twi, nk):
    a = jnp.dot(f1, jnp.concatenate([x_a, x_b], axis=1), preferred_element_type=F32)
    out = []
    for h in range(2):
        re, im = a[:nk, h * LANES:(h + 1) * LANES], a[nk:, h * LANES:(h + 1) * LANES]
        out.append((re * twr - im * twi, re * twi + im * twr))
    return out


def _spectrum_body(k_ref, f1_ref, twr_ref, twi_ref, m2_ref, o_ref, *, g, nk, r_in):
    f1, twr, twi = f1_ref[...], twr_ref[...], twi_ref[...]
    sign = jnp.where((lax.broadcasted_iota(I32, (nk, 1), 0) & 1) == 0, 1.0, -1.0)
    a2 = []
    for ci in range(g):
        (fre, fim), (bre, bim) = _fwd_rows_twiddle(k_ref[0, ci], k_ref[1, ci], f1, twr, twi, nk)
        a2.append(jnp.concatenate([fre + sign * bre, fim + sign * bim], axis=1).astype(BF16))
    spec = jnp.dot(jnp.concatenate(a2, axis=0), m2_ref[...], preferred_element_type=F32)
    o_ref[...] = spec.reshape(g, nk, 2 * LANES).astype(o_ref.dtype)


def hyena_spectrum(kt4, tables, group):
    _, c, r_in, _ = kt4.shape
    nk = 2 * r_in
    f1, twr, twi, m2, _, _ = tables
    g = min(group, c)
    full = lambda a: pl.BlockSpec(a.shape, lambda j: (0,) * a.ndim)
    return pl.pallas_call(
        functools.partial(_spectrum_body, g=g, nk=nk, r_in=r_in),
        grid=(c // g,),
        in_specs=[pl.BlockSpec((2, g, r_in, LANES), lambda j: (0, j, 0, 0)), full(f1), full(twr), full(twi), full(m2)],
        out_specs=pl.BlockSpec((g, nk, 2 * LANES), lambda j: (j, 0, 0)),
        out_shape=jax.ShapeDtypeStruct((c, nk, 2 * LANES), BF16),
        compiler_params=_cparams("parallel"),
        name="hyena_spectrum",
    )(kt4, f1, twr, twi, m2)


def _time_conv3(x3, taps_ref, ct, r_in):
    rows = ct * r_in
    x2 = x3.reshape(rows, LANES)
    lane = lax.broadcasted_iota(I32, (1, 1, LANES), 2)
    rr = lax.broadcasted_iota(I32, (1, r_in, 1), 1)
    a = pltpu.roll(x2, 1, 1)
    prev = jnp.where(lane == 0, pltpu.roll(a, 1, 0).reshape(ct, r_in, LANES), a.reshape(ct, r_in, LANES))
    prev = jnp.where((lane == 0) & (rr == 0), 0.0, prev)
    a = pltpu.roll(x2, LANES - 1, 1)
    nxt = jnp.where(lane == LANES - 1, pltpu.roll(a, rows - 1, 0).reshape(ct, r_in, LANES), a.reshape(ct, r_in, LANES))
    nxt = jnp.where((lane == LANES - 1) & (rr == r_in - 1), 0.0, nxt)
    return taps_ref[3] + taps_ref[0] * prev + taps_ref[1] * x3 + taps_ref[2] * nxt


def _fftconv_body(x0_ref, x1_ref, v_ref, t0_ref, t1_ref, tv_ref, skip_ref, k_ref,
                  f1_ref, twr_ref, twi_ref, m2_ref, m2i_ref, gi_ref,
                  o_ref, u_s, z0_s, *, ct, g, nk, r_in):
    z0 = _time_conv3(x0_ref[0].astype(F32), t0_ref, ct, r_in)
    z1 = _time_conv3(x1_ref[0].astype(F32), t1_ref, ct, r_in)
    zv = _time_conv3(v_ref[0].astype(F32), tv_ref, ct, r_in)
    u_s[...] = zv * z1
    z0_s[...] = z0
    f1, twr, twi = f1_ref[...], twr_ref[...], twi_ref[...]

    def group(c0):
        a2 = []
        for ci in range(0, g, 2):
            pair = _fwd_rows_twiddle(u_s[c0 + ci].astype(BF16), u_s[c0 + ci + 1].astype(BF16), f1, twr, twi, nk)
            a2 += [jnp.concatenate([tre, tim], axis=1).astype(BF16) for tre, tim in pair]
        spec = jnp.dot(jnp.concatenate(a2, axis=0), m2_ref[...], preferred_element_type=F32)
        kf = k_ref[pl.ds(c0, g)].astype(F32).reshape(g * nk, 2 * LANES)
        sre, sim = spec[:, :LANES], spec[:, LANES:]
        kre, kim = kf[:, :LANES], kf[:, LANES:]
        prod = jnp.concatenate([sre * kre - sim * kim, sre * kim + sim * kre], axis=1).astype(BF16)
        cc = jnp.dot(prod, m2i_ref[...], preferred_element_type=F32)
        for ci in range(0, g, 2):
            st = []
            for h in range(2):
                blk = cc[(ci + h) * nk:(ci + h + 1) * nk]
                cre, cim = blk[:, :LANES], blk[:, LANES:]
                st.append(jnp.concatenate([cre * twr + cim * twi, cim * twr - cre * twi], axis=0).astype(BF16))
            y2 = jnp.dot(gi_ref[...], jnp.concatenate(st, axis=1), preferred_element_type=F32)
            for h in range(2):
                ch = c0 + ci + h
                y = y2[:, h * LANES:(h + 1) * LANES]
                o_ref[0, ch] = ((y + u_s[ch] * skip_ref[ch]) * z0_s[ch]).astype(o_ref.dtype)

    def two_groups(i, _):
        group(2 * g * i)
        group(2 * g * i + g)
        return 0

    lax.fori_loop(0, ct // (2 * g), two_groups, 0)


def hyena_fftconv(pt4, taps, skip3, spec, tables, *, tile_c, group):
    b, c3, r_in, _ = pt4.shape
    c = c3 // 3
    nk = 2 * r_in
    ct = min(tile_c, c)
    g = min(group, ct)
    nct = c // ct
    f1, twr, twi, m2, m2i, gi = tables
    full = lambda a: pl.BlockSpec(a.shape, lambda j, bi: (0,) * a.ndim)
    xspec = lambda off: pl.BlockSpec((1, ct, r_in, LANES), lambda j, bi: (bi, off * nct + j, 0, 0))
    tspec = lambda off: pl.BlockSpec((4, ct, 1, 1), lambda j, bi: (0, off * nct + j, 0, 0))
    return pl.pallas_call(
        functools.partial(_fftconv_body, ct=ct, g=g, nk=nk, r_in=r_in),
        grid=(nct, b),
        in_specs=[xspec(0), xspec(1), xspec(2), tspec(0), tspec(1), tspec(2),
                  pl.BlockSpec((ct, 1, 1), lambda j, bi: (j, 0, 0)),
                  pl.BlockSpec((ct, nk, 2 * LANES), lambda j, bi: (j, 0, 0)),
                  full(f1), full(twr), full(twi), full(m2), full(m2i), full(gi)],
        out_specs=pl.BlockSpec((1, ct, r_in, LANES), lambda j, bi: (bi, j, 0, 0)),
        out_shape=jax.ShapeDtypeStruct((b, c, r_in, LANES), BF16),
        scratch_shapes=[pltpu.VMEM((ct, r_in, LANES), F32), pltpu.VMEM((ct, r_in, LANES), F32)],
        compiler_params=_cparams("parallel", "arbitrary"),
        name="hyena_fftconv",
    )(pt4, pt4, pt4, taps, taps, taps, skip3, spec, f1, twr, twi, m2, m2i, gi)


ROUTE_LANES = LANES
NEG_BIG = -1e30
HALF_WORD = 16


def _pack_bf16_pairs(v):
    h = v.shape[1] // 2
    bits = pltpu.bitcast(v.astype(BF16).astype(F32), I32)
    return bits[:, :h] | lax.shift_right_logical(bits[:, h:], HALF_WORD)


def _unpack_bf16_pairs(w):
    hi = pltpu.bitcast(w & jnp.int32(-65536), F32)
    lo = pltpu.bitcast(lax.shift_left(w, HALF_WORD), F32)
    return jnp.concatenate([hi, lo], axis=1)


def _mix_route_body(x_ref, hf_ref, hb_ref, prg_ref, pga_ref, pgb_ref, yt_ref, rgp_ref, hyp_ref, wo_ref, g1_ref,
                    n2g_ref, sh2_ref, sc2_ref, wr_ref, br_ref, tri_ref,
                    x1_ref, hxp_ref, route_ref, cnt_ref, carry_s, *, t, n_exp):
    @pl.when((pl.program_id(0) == 0) & (pl.program_id(1) == 0))
    def _():
        carry_s[...] = jnp.zeros_like(carry_s)

    hsum = hf_ref[0].astype(F32) + hb_ref[0].astype(F32)
    y_rg = (hsum * _gelu_tanh(prg_ref[0].astype(F32))).astype(BF16)
    t1 = jnp.dot(y_rg, rgp_ref[...], preferred_element_type=F32)
    t2 = lax.dot_general(yt_ref[0], hyp_ref[...], (((0,), (0,)), ((), ())), preferred_element_type=F32)
    merged = _sigmoid(pga_ref[0].astype(F32)) * t1 + _sigmoid(pgb_ref[0].astype(F32)) * t2
    out = jnp.dot(merged.astype(BF16), wo_ref[...], preferred_element_type=F32)
    x1 = x_ref[0] + g1_ref[0] * out
    x1_ref[0] = x1
    ms = jnp.mean(x1 * x1, axis=-1, keepdims=True)
    hx2 = (x1 * lax.rsqrt(ms + EPS) * n2g_ref[...]) * (1.0 + sc2_ref[0]) + sh2_ref[0]
    hxp_ref[...] = _pack_bf16_pairs(hx2)

    hx_hi = hx2.astype(BF16)
    hx_lo = (hx2 - hx_hi.astype(F32)).astype(BF16)
    parts = (jnp.dot(hx_hi, wr_ref[...], preferred_element_type=F32)
             + jnp.dot(hx_lo, wr_ref[...], preferred_element_type=F32))
    logits = parts[:, :ROUTE_LANES] + parts[:, ROUTE_LANES:] + br_ref[...]
    lane = lax.broadcasted_iota(I32, (t, ROUTE_LANES), 1)
    is_g = lane < N_GROUPS
    glog = jnp.where(is_g, logits, NEG_BIG)
    gmax = jnp.max(glog, axis=1, keepdims=True)
    gidx = jnp.min(jnp.where(glog == gmax, lane, ROUTE_LANES), axis=1, keepdims=True)
    gsum = jnp.sum(jnp.where(is_g, jnp.exp(glog - gmax), 0.0), axis=1, keepdims=True)
    p_g = 1.0 / gsum
    e_lane = lane - N_GROUPS
    grp_of_lane = lax.shift_right_arithmetic(e_lane, int(math.log2(EXPERTS_PER_GROUP)))
    in_grp = (e_lane >= 0) & (e_lane < n_exp) & (grp_of_lane == gidx)
    elog = jnp.where(in_grp, logits, NEG_BIG)
    m1 = jnp.max(elog, axis=1, keepdims=True)
    i1 = jnp.min(jnp.where(elog == m1, lane, ROUTE_LANES), axis=1, keepdims=True)
    elog2 = jnp.where(lane == i1, NEG_BIG, elog)
    m2 = jnp.max(elog2, axis=1, keepdims=True)
    i2 = jnp.min(jnp.where(elog2 == m2, lane, ROUTE_LANES), axis=1, keepdims=True)
    e21 = jnp.exp(m2 - m1)
    pk1 = 1.0 / (1.0 + e21)
    wt1, wt2 = p_g * pk1, p_g * (e21 * pk1)

    oh1 = (lane == i1 - N_GROUPS).astype(F32)
    oh2 = (lane == i2 - N_GROUPS).astype(F32)
    cnt = oh1 + oh2
    before = jnp.dot(tri_ref[...], cnt.astype(BF16), preferred_element_type=F32) + carry_s[...]
    rank1 = jnp.sum(oh1 * before, axis=1, keepdims=True)
    rank2 = jnp.sum(oh2 * before, axis=1, keepdims=True)
    carry_s[...] = carry_s[...] + jnp.sum(cnt, axis=0, keepdims=True)
    cnt_ref[...] = carry_s[...]
    vals = ((i1 - N_GROUPS).astype(F32), (i2 - N_GROUPS).astype(F32), rank1, rank2, wt1, wt2)
    route = jnp.zeros((t, ROUTE_LANES), F32)
    for k, v in enumerate(vals):
        route = jnp.where(lane == k, v, route)
    route_ref[...] = route


def mix_route(x, h_f, h_b, p_rm, y_hy_t, rg_proj, hy_proj, w_out, g1, n2g, sh2, sc2, wr, br, *, tile_l, n_exp):
    b, l, d = x.shape
    c = h_f.shape[2]
    t = min(tile_l, l)
    nt = l // t
    n = b * l
    tri = (jnp.arange(t)[:, None] > jnp.arange(t)[None, :]).astype(BF16)
    tok = lambda bi, i: (bi, i, 0)
    col = lambda k: (lambda bi, i: (bi, i, k))
    full2 = lambda a: pl.BlockSpec(a.shape, lambda bi, i: (0, 0))
    per_b = pl.BlockSpec((1, 1, d), lambda bi, i: (bi, 0, 0))
    row = lambda bi, i: (bi * nt + i, 0)
    return pl.pallas_call(
        functools.partial(_mix_route_body, t=t, n_exp=n_exp),
        grid=(b, nt),
        in_specs=[pl.BlockSpec((1, t, d), tok), pl.BlockSpec((1, t, c), tok), pl.BlockSpec((1, t, c), tok),
                  pl.BlockSpec((1, t, c), col(1)), pl.BlockSpec((1, t, c), col(2)), pl.BlockSpec((1, t, c), col(3)),
                  pl.BlockSpec((1, c, t), lambda bi, i: (bi, 0, i)),
                  full2(rg_proj), full2(hy_proj), full2(w_out), per_b,
                  full2(n2g), per_b, per_b, full2(wr), full2(br), full2(tri)],
        out_specs=[pl.BlockSpec((1, t, d), tok), pl.BlockSpec((t, d // 2), row),
                   pl.BlockSpec((t, ROUTE_LANES), row), pl.BlockSpec((1, ROUTE_LANES), lambda bi, i: (0, 0))],
        out_shape=[jax.ShapeDtypeStruct((b, l, d), F32), jax.ShapeDtypeStruct((n, d // 2), I32),
                   jax.ShapeDtypeStruct((n, ROUTE_LANES), F32), jax.ShapeDtypeStruct((1, ROUTE_LANES), F32)],
        scratch_shapes=[pltpu.VMEM((1, ROUTE_LANES), F32)],
        compiler_params=_cparams("arbitrary", "arbitrary"),
        name="mix_route",
    )(x, h_f, h_b, p_rm, p_rm, p_rm, y_hy_t, rg_proj, hy_proj, w_out, g1, n2g, sh2, sc2, wr, br, tri)


def _dest_body(route_ref, cnt_ref, ut_ref, dest_ref, blk_ref, *, t, n_exp, nb_pad):
    lane1 = lax.broadcasted_iota(I32, (1, ROUTE_LANES), 1)
    padded = jnp.floor((cnt_ref[...] + (MOE_BLOCK - 1.0)) * (1.0 / MOE_BLOCK)) * MOE_BLOCK
    padded = jnp.where(lane1 < n_exp, padded, 0.0)
    pend = jnp.dot(jnp.broadcast_to(padded, (SUBLANES, ROUTE_LANES)), ut_ref[...], precision=HIGHEST,
                   preferred_element_type=F32)[0:1]
    pstart = pend - padded
    route = route_ref[...]
    lane = lax.broadcasted_iota(I32, (t, ROUTE_LANES), 1)
    lf = lane.astype(F32)
    d1 = jnp.sum(jnp.where(lf == route[:, 0:1], pstart, 0.0), axis=1, keepdims=True) + route[:, 2:3]
    d2 = jnp.sum(jnp.where(lf == route[:, 1:2], pstart, 0.0), axis=1, keepdims=True) + route[:, 3:4]
    dmat = jnp.where(lane == 0, d1, jnp.where(lane == 1, d2, 0.0))
    dest_ref[...] = dmat.T[0:SUBLANES].astype(I32)
    first_row = lax.broadcasted_iota(I32, (nb_pad, ROUTE_LANES), 0).astype(F32) * float(MOE_BLOCK)
    lane_b = lax.broadcasted_iota(I32, (nb_pad, ROUTE_LANES), 1)
    nle = jnp.sum(jnp.where((lane_b < n_exp) & (pend <= first_row), 1.0, 0.0), axis=1, keepdims=True)
    blk_ref[...] = jnp.broadcast_to(jnp.minimum(nle, n_exp - 1.0), (nb_pad, ROUTE_LANES)).astype(I32)


def moe_dest(route, cnt, *, tile, n_exp, n_blocks):
    n = route.shape[0]
    t = min(tile, n)
    nb_pad = -(-n_blocks // SUBLANES) * SUBLANES
    ut = (jnp.arange(ROUTE_LANES)[:, None] <= jnp.arange(ROUTE_LANES)[None, :]).astype(F32)
    return pl.pallas_call(
        functools.partial(_dest_body, t=t, n_exp=n_exp, nb_pad=nb_pad),
        grid=(n // t,),
        in_specs=[pl.BlockSpec((t, ROUTE_LANES), lambda i: (i, 0)),
                  pl.BlockSpec((1, ROUTE_LANES), lambda i: (0, 0)),
                  pl.BlockSpec((ROUTE_LANES, ROUTE_LANES), lambda i: (0, 0))],
        out_specs=[pl.BlockSpec((SUBLANES, t), lambda i: (i, 0)),
                   pl.BlockSpec((nb_pad, ROUTE_LANES), lambda i: (0, 0))],
        out_shape=[jax.ShapeDtypeStruct((n // t * SUBLANES, t), I32),
                   jax.ShapeDtypeStruct((nb_pad, ROUTE_LANES), I32)],
        compiler_params=_cparams("arbitrary"),
        name="moe_dest",
    )(route, cnt, ut)


def _scatter_body(dest_ref, hx_ref, xb_in_ref, xb_ref, sem, *, t):
    del xb_in_ref

    def issue(r, _):
        for k in range(2):
            pltpu.make_async_copy(hx_ref.at[pl.ds(r, 1)], xb_ref.at[pl.ds(dest_ref[k, r], 1)], sem).start()
        return 0

    lax.fori_loop(0, t, issue, 0, unroll=8)
    for k in range(2):
        pltpu.make_async_copy(hx_ref, xb_ref.at[pl.ds(0, t)], sem).wait()


def moe_scatter(dest, hxp, n_rows, *, tile):
    n, hw = hxp.shape
    t = min(tile, n)
    xb0 = jnp.zeros((n_rows, hw), I32)
    return pl.pallas_call(
        functools.partial(_scatter_body, t=t),
        grid=(n // t,),
        in_specs=[pl.BlockSpec((SUBLANES, t), lambda i: (i, 0), memory_space=pltpu.SMEM),
                  pl.BlockSpec((t, hw), lambda i: (i, 0)),
                  pl.BlockSpec(memory_space=pl.ANY)],
        out_specs=pl.BlockSpec(memory_space=pl.ANY),
        out_shape=jax.ShapeDtypeStruct((n_rows, hw), I32),
        scratch_shapes=[pltpu.SemaphoreType.DMA],
        input_output_aliases={2: 0},
        compiler_params=_cparams("arbitrary"),
        name="moe_scatter",
    )(dest, hxp, xb0)


def _expert_body(blk_ref, xb_ref, w1_ref, w3_ref, w2_ref, yb_ref, w1_s, w3_s, w2_s):
    i = pl.program_id(0)
    changed = (i == 0) | (blk_ref[i] != blk_ref[jnp.maximum(i - 1, 0)])

    @pl.when(changed)
    def _():
        w1_s[...] = w1_ref[0].astype(BF16)
        w3_s[...] = w3_ref[0].astype(BF16)
        w2_s[...] = w2_ref[0].astype(BF16)

    xblk = _unpack_bf16_pairs(xb_ref[...]).astype(BF16)
    h1 = jnp.dot(xblk, w1_s[...], preferred_element_type=F32)
    h3 = jnp.dot(xblk, w3_s[...], preferred_element_type=F32)
    hid = (h1 * _sigmoid(h1) * h3).astype(BF16)
    yb_ref[...] = _pack_bf16_pairs(jnp.dot(hid, w2_s[...], preferred_element_type=F32))


def moe_experts(blk_exp, xb, w1, w3, w2):
    p, hw = xb.shape
    _, d, de = w1.shape
    nb = p // MOE_BLOCK
    grid_spec = pltpu.PrefetchScalarGridSpec(
        num_scalar_prefetch=1,
        grid=(nb,),
        in_specs=[pl.BlockSpec((MOE_BLOCK, hw), lambda i, blk: (i, 0)),
                  pl.BlockSpec((1, d, de), lambda i, blk: (blk[i], 0, 0)),
                  pl.BlockSpec((1, d, de), lambda i, blk: (blk[i], 0, 0)),
                  pl.BlockSpec((1, de, d), lambda i, blk: (blk[i], 0, 0))],
        out_specs=pl.BlockSpec((MOE_BLOCK, hw), lambda i, blk: (i, 0)),
        scratch_shapes=[pltpu.VMEM((d, de), BF16), pltpu.VMEM((d, de), BF16), pltpu.VMEM((de, d), BF16)],
    )
    return pl.pallas_call(
        _expert_body,
        grid_spec=grid_spec,
        out_shape=jax.ShapeDtypeStruct((p, hw), I32),
        compiler_params=_cparams("arbitrary"),
        name="moe_experts",
    )(blk_exp, xb, w1, w3, w2)


def _combine_body(dest_ref, x1_ref, route_ref, g2_ref, fg_ref, yb_ref, o_ref, y1_s, y2_s, sem, *, t):
    def issue(r, _):
        pltpu.make_async_copy(yb_ref.at[pl.ds(dest_ref[0, r], 1)], y1_s.at[pl.ds(r, 1)], sem).start()
        pltpu.make_async_copy(yb_ref.at[pl.ds(dest_ref[1, r], 1)], y2_s.at[pl.ds(r, 1)], sem).start()
        return 0

    lax.fori_loop(0, t, issue, 0, unroll=8)
    for y_s in (y1_s, y2_s):
        pltpu.make_async_copy(yb_ref.at[pl.ds(0, t)], y_s, sem).wait()
    route = route_ref[...]
    moe = route[:, 4:5] * _unpack_bf16_pairs(y1_s[...]) + route[:, 5:6] * _unpack_bf16_pairs(y2_s[...])
    x2 = x1_ref[0] + g2_ref[0] * moe
    ms = jnp.mean(x2 * x2, axis=-1, keepdims=True)
    o_ref[0] = x2 * lax.rsqrt(ms + EPS) * fg_ref[...]


def moe_combine(dest, x1, route, g2, final_g, yb, *, tile_l):
    b, l, d = x1.shape
    t = min(tile_l, l)
    nt = l // t
    hw = yb.shape[1]
    return pl.pallas_call(
        functools.partial(_combine_body, t=t),
        grid=(b, nt),
        in_specs=[pl.BlockSpec((SUBLANES, t), lambda bi, i: (bi * nt + i, 0), memory_space=pltpu.SMEM),
                  pl.BlockSpec((1, t, d), lambda bi, i: (bi, i, 0)),
                  pl.BlockSpec((t, ROUTE_LANES), lambda bi, i: (bi * nt + i, 0)),
                  pl.BlockSpec((1, 1, d), lambda bi, i: (bi, 0, 0)),
                  pl.BlockSpec((1, d), lambda bi, i: (0, 0)),
                  pl.BlockSpec(memory_space=pl.ANY)],
        out_specs=pl.BlockSpec((1, t, d), lambda bi, i: (bi, i, 0)),
        out_shape=jax.ShapeDtypeStruct((b, l, d), F32),
        scratch_shapes=[pltpu.VMEM((t, hw), I32), pltpu.VMEM((t, hw), I32), pltpu.SemaphoreType.DMA],
        compiler_params=_cparams("arbitrary", "arbitrary"),
        name="moe_combine",
    )(dest, x1, route, g2, final_g, yb)


def kernel(x, c, ctx, c_ctx, ada_w, ada_b, norm1_g, norm2_g, final_g, w_in, rg_conv_w, rg_conv_b, rg_wa_f, rg_ba_f, rg_wx_f, rg_bx_f, rg_lam_f, rg_wa_b, rg_ba_b, rg_wx_b, rg_bx_b, rg_lam_b, rg_proj, hy_conv_w, hy_conv_b, hy_pos_w1, hy_pos_b1, hy_pos_w2, hy_pos_b2, hy_freq, hy_pos_w3, hy_skip, hy_proj, w_out, moe_wg, moe_bg, moe_we, moe_be, moe_w1, moe_w3, moe_w2):
    B, L, D = x.shape
    C = rg_conv_w.shape[-1]
    LC = ctx.shape[1]
    c8 = jnp.zeros((8, D), F32).at[:B].set(c).at[B].set(c_ctx)
    mods = ada_mods(c8, ada_w[0], ada_b)
    sh1, sc1, g1 = (mods[:B, None, k * D:(k + 1) * D] for k in range(3))
    sh2, sc2, g2 = (mods[:B, None, k * D:(k + 1) * D] for k in range(3, 6))
    csh1 = jnp.broadcast_to(mods[B:B + 1, None, 0:D], (B, 1, D))
    csc1 = jnp.broadcast_to(mods[B:B + 1, None, D:2 * D], (B, 1, D))

    w_in_b = w_in[0].astype(BF16)
    w_rm = jnp.concatenate([w_in_b[:, :2 * C], w_in_b[:, 5 * C:]], axis=1)
    w_hy_t = w_in_b[:, 2 * C:5 * C].T
    wg_f = gate_blocks(rg_wa_f[0], rg_wx_f[0], C // 256)
    wg_b = gate_blocks(rg_wa_b[0], rg_wx_b[0], C // 256)
    rg_f = (rg_conv_w[0], rg_conv_b, wg_f, rg_ba_f, rg_bx_f, rg_lam_f)
    rg_b = (rg_conv_w[0], rg_conv_b, wg_b, rg_ba_b, rg_bx_b, rg_lam_b)

    pc = norm_mod_proj(ctx, norm1_g, csh1, csc1, w_rm[:, :C], transposed=False, tile_l=LC, chunk=C)
    zero = jnp.zeros((B, 1, C), F32)
    _, hcf = rg_scan(pc, 0, *rg_f, zero, reverse=False, tile_l=256)
    _, hcb = rg_scan(pc, 0, *rg_b, zero, reverse=True, tile_l=256)

    p_rm = norm_mod_proj(x, norm1_g, sh1, sc1, w_rm, transposed=False, tile_l=512, chunk=1024)
    p_t = norm_mod_proj(x, norm1_g, sh1, sc1, w_hy_t, transposed=True, tile_l=512, chunk=1024)
    h_f, _ = rg_scan(p_rm, 0, *rg_f, hcf, reverse=False, tile_l=256)
    h_b, _ = rg_scan(p_rm, 0, *rg_b, hcb, reverse=True, tile_l=256)

    tables = dft_tables(L)
    w1t = jnp.zeros((HY_HID, HY_HID), F32).at[:, :hy_pos_w1.shape[1]].set(hy_pos_w1[0].T)
    kt = hyena_filter_t(w1t, hy_pos_b1[0][:, None], hy_pos_w2[0].T, hy_pos_b2[0][:, None], hy_freq[0][:, None],
                        hy_pos_w3[0].T.reshape(2, C, HY_HID), L, 256)
    spec = hyena_spectrum(kt.reshape(2, C, L // LANES, LANES), tables, 8)
    taps = jnp.concatenate([hy_conv_w[0], hy_conv_b], axis=0)[:, :, None, None]
    y_hy_t = hyena_fftconv(p_t.reshape(B, 3 * C, L // LANES, LANES), taps, hy_skip[0][:, None, None], spec, tables,
                           tile_c=64, group=8).reshape(B, C, L)

    n_exp = moe_we.shape[-1]
    n_grp = moe_wg.shape[-1]
    assert n_grp == N_GROUPS and n_exp == N_GROUPS * EXPERTS_PER_GROUP
    wr = jnp.zeros((D, ROUTE_LANES), F32).at[:, :n_grp].set(moe_wg[0]).at[:, n_grp:n_grp + n_exp].set(moe_we[0])
    br = jnp.zeros((1, ROUTE_LANES), F32).at[:, :n_grp].set(moe_bg).at[:, n_grp:n_grp + n_exp].set(moe_be)
    wr_hi = wr.astype(BF16)
    wr_split = jnp.concatenate([wr_hi, (wr - wr_hi.astype(F32)).astype(BF16)], axis=1)
    x1, hxp, route, cnt = mix_route(x, h_f, h_b, p_rm, y_hy_t, rg_proj[0].astype(BF16), hy_proj[0].astype(BF16),
                                    w_out[0].astype(BF16), g1, norm2_g, sh2, sc2, wr_split, br, tile_l=512,
                                    n_exp=n_exp)

    n_blocks = (2 * B * L + n_exp * (MOE_BLOCK - 1)) // MOE_BLOCK
    dest, blk = moe_dest(route, cnt, tile=512, n_exp=n_exp, n_blocks=n_blocks)
    xb = moe_scatter(dest, hxp, n_blocks * MOE_BLOCK, tile=512)
    yb = moe_experts(blk[:n_blocks, 0], xb, moe_w1[0], moe_w3[0], moe_w2[0])
    return moe_combine(dest, x1, route, g2, final_g[None], yb, tile_l=512)
```

```python
import functools
import math

import jax
import jax.numpy as jnp
from jax import lax
from jax.experimental import pallas as pl
from jax.experimental.pallas import tpu as pltpu

F32 = jnp.float32
BF16 = jnp.bfloat16
I32 = jnp.int32
HIGHEST = lax.Precision.HIGHEST

LANES = 128
SUBLANES = 8
EPS = 1e-6
RG_C = 8.0
RG_HEAD_DIM = 64
GRID_W = 64
HY_SEQ_BANDS = 16
HY_COL_BANDS = 8
HY_DECAY_TARGET = 1e-2
HY_FAST_DECAY = 0.3
HY_SLOW_DECAY = 1.5
N_GROUPS = 4
EXPERTS_PER_GROUP = 8
MOE_BLOCK = 512
VMEM_LIMIT = 56 * 1024 * 1024


def _cparams(*sem):
    return pltpu.CompilerParams(dimension_semantics=sem, vmem_limit_bytes=VMEM_LIMIT)


def _sigmoid(x):
    return 0.5 * (jnp.tanh(0.5 * x) + 1.0)


def _gelu_tanh(x):
    c = math.sqrt(2.0 / math.pi)
    return 0.5 * x * (1.0 + jnp.tanh(c * (x + 0.044715 * (x * x * x))))


def _ada_body(c_ref, w_ref, b_ref, o_ref):
    c = c_ref[...]
    s = c * _sigmoid(c)
    o_ref[...] = jnp.dot(s, w_ref[...], precision=HIGHEST, preferred_element_type=F32) + b_ref[...]


def ada_mods(c8, ada_w, ada_b):
    d, m = ada_w.shape
    tn = 1024 if m % 1024 == 0 else m
    return pl.pallas_call(
        _ada_body,
        grid=(m // tn,),
        in_specs=[pl.BlockSpec((c8.shape[0], d), lambda j: (0, 0)),
                  pl.BlockSpec((d, tn), lambda j: (0, j)),
                  pl.BlockSpec((1, tn), lambda j: (0, j))],
        out_specs=pl.BlockSpec((c8.shape[0], tn), lambda j: (0, j)),
        out_shape=jax.ShapeDtypeStruct((c8.shape[0], m), F32),
        compiler_params=_cparams("parallel"),
        name="ada_mods",
    )(c8, ada_w, ada_b)


def _norm_mod(x_ref, g_ref, sh_ref, sc_ref):
    x = x_ref[0]
    ms = jnp.mean(x * x, axis=-1, keepdims=True)
    y = x * lax.rsqrt(ms + EPS) * g_ref[...]
    return (y * (1.0 + sc_ref[0]) + sh_ref[0]).astype(BF16)


def _proj_body(x_ref, g_ref, sh_ref, sc_ref, w_ref, o_ref, *, chunk):
    hx = _norm_mod(x_ref, g_ref, sh_ref, sc_ref)
    for j in range(w_ref.shape[1] // chunk):
        cols = slice(j * chunk, (j + 1) * chunk)
        o_ref[0, :, cols] = jnp.dot(hx, w_ref[:, cols], preferred_element_type=F32).astype(o_ref.dtype)


def norm_mod_proj(x, g, shift, scale, w, *, tile_l, chunk):
    b, l, d = x.shape
    m = w.shape[1]
    tl = min(tile_l, l)
    return pl.pallas_call(
        functools.partial(_proj_body, chunk=min(chunk, m)),
        grid=(b, l // tl),
        in_specs=[pl.BlockSpec((1, tl, d), lambda bi, i: (bi, i, 0)),
                  pl.BlockSpec((1, d), lambda bi, i: (0, 0)),
                  pl.BlockSpec((1, 1, d), lambda bi, i: (bi, 0, 0)),
                  pl.BlockSpec((1, 1, d), lambda bi, i: (bi, 0, 0)),
                  pl.BlockSpec(w.shape, lambda bi, i: (0, 0))],
        out_specs=pl.BlockSpec((1, tl, m), lambda bi, i: (bi, i, 0)),
        out_shape=jax.ShapeDtypeStruct((b, l, m), BF16),
        compiler_params=_cparams("parallel", "parallel"),
        name="norm_mod_proj",
    )(x, g, shift, scale, w)


HALO = 16


def _scan_body(pc_ref, pp_ref, pn_ref, cw_ref, cb_ref, wg_ref, ba_ref, bx_ref, lam_ref, h0_ref,
               h_ref, hl_ref, xc_s, g_s, a_s, b_s, hloc_s, pcum_s, carry_s,
               *, reverse, n_tiles, t, c, s_len, pitch):
    i = pl.program_id(1)
    ti = (n_tiles - 1 - i) if reverse else i
    n_slab = c // LANES
    n_blk = wg_ref.shape[0]
    blk = c // n_blk

    @pl.when(i == 0)
    def _():
        carry_s[...] = h0_ref[0]

    cur = pc_ref[0].astype(F32)
    has_prev = (ti > 0).astype(F32)
    has_next = (ti < n_tiles - 1).astype(F32)
    prev2 = pp_ref[0, HALO - 2:HALO - 1, :].astype(F32) * has_prev
    prev1 = pp_ref[0, HALO - 1:HALO, :].astype(F32) * has_prev
    next0 = pn_ref[0, 0:1, :].astype(F32) * has_next
    row = lax.broadcasted_iota(I32, (t, c), 0)
    xm1 = jnp.where(row == 0, prev1, pltpu.roll(cur, 1, 0))
    xm2 = jnp.where(row == 0, prev2, jnp.where(row == 1, prev1, pltpu.roll(cur, 2, 0)))
    xp1 = jnp.where(row == t - 1, next0, pltpu.roll(cur, t - 1, 0))
    cw = cw_ref[...]
    xc = cb_ref[...] + cw[0:1] * xm2 + cw[1:2] * xm1 + cw[2:3] * cur + cw[3:4] * xp1
    xc_s[...] = xc

    for k in range(n_blk):
        xb = xc_s[:, k * blk:(k + 1) * blk].astype(BF16)
        g_s[:, k * 2 * blk:(k + 1) * 2 * blk] = jnp.dot(xb, wg_ref[k], preferred_element_type=F32)

    lam = lam_ref[...]
    softplus_neg_lam = jnp.maximum(-lam, 0.0) + jnp.log1p(jnp.exp(-jnp.abs(lam)))
    half_ca = (-0.5 * RG_C) * softplus_neg_lam
    half_ba, half_bx = 0.5 * ba_ref[...], 0.5 * bx_ref[...]
    slabs_per_blk = blk // LANES
    for j in range(SUBLANES):
        r0 = j * s_len
        for k in range(n_slab):
            kb, ks = k // slabs_per_blk, k % slabs_per_blk
            ga = g_s[r0:r0 + s_len, kb * 2 * blk + ks * LANES:kb * 2 * blk + (ks + 1) * LANES]
            gx = g_s[r0:r0 + s_len, kb * 2 * blk + blk + ks * LANES:kb * 2 * blk + blk + (ks + 1) * LANES]
            lane = slice(k * LANES, (k + 1) * LANES)
            half_x = 0.5 * xc_s[r0:r0 + s_len, lane]
            hca = half_ca[:, lane]
            log_a = hca * jnp.tanh(ga + half_ba[:, lane]) + hca
            gated_x = half_x * jnp.tanh(gx + half_bx[:, lane]) + half_x
            a = jnp.exp(log_a)
            a_s[k, j * pitch:j * pitch + s_len, :] = a
            gain2 = -jnp.tanh(log_a) * (a * a + 1.0)
            gain = jnp.where(gain2 > 0.0, gain2 * lax.rsqrt(gain2), 0.0)
            b_s[k, j * pitch:j * pitch + s_len, :] = gain * gated_x

    def step1(s, hp):
        hs, ps = hp
        srow = (s_len - 1 - s) if reverse else s
        hs2, ps2 = [], []
        for k in range(n_slab):
            av = a_s[k, pl.ds(srow, SUBLANES, stride=pitch), :]
            bv = b_s[k, pl.ds(srow, SUBLANES, stride=pitch), :]
            h = av * hs[k] + bv
            p = av * ps[k]
            hloc_s[k, pl.ds(srow, SUBLANES, stride=pitch), :] = h
            pcum_s[k, pl.ds(srow, SUBLANES, stride=pitch), :] = p
            hs2.append(h)
            ps2.append(p)
        return tuple(hs2), tuple(ps2)

    zeros = tuple(jnp.zeros((SUBLANES, LANES), F32) for _ in range(n_slab))
    ones = tuple(jnp.ones((SUBLANES, LANES), F32) for _ in range(n_slab))
    h_end, p_end = lax.fori_loop(0, s_len, step1, (zeros, ones))

    order = range(SUBLANES - 1, -1, -1) if reverse else range(SUBLANES)
    for k in range(n_slab):
        cst = carry_s[:, k * LANES:(k + 1) * LANES]
        for j in order:
            rows = slice(j * pitch, j * pitch + s_len)
            h_ref[0, j * s_len:(j + 1) * s_len, k * LANES:(k + 1) * LANES] = (
                hloc_s[k, rows, :] + pcum_s[k, rows, :] * cst).astype(h_ref.dtype)
            cst = p_end[k][j:j + 1] * cst + h_end[k][j:j + 1]
        carry_s[:, k * LANES:(k + 1) * LANES] = cst
    hl_ref[0] = carry_s[...]


def rg_scan(p, col_blk, conv_w, conv_b, wg, ba, bx, lam, h0, *, reverse, tile_l):
    b, l, _ = p.shape
    c = conv_w.shape[1]
    t = min(tile_l, l)
    n_tiles = l // t
    s_len = t // SUBLANES
    pitch = s_len + SUBLANES
    hb = t // HALO
    n_hblk = l // HALO

    def nat(i):
        return (n_tiles - 1 - i) if reverse else i

    body = functools.partial(_scan_body, reverse=reverse, n_tiles=n_tiles, t=t, c=c, s_len=s_len, pitch=pitch)
    vec = pl.BlockSpec((1, c), lambda bi, i: (0, 0))
    return pl.pallas_call(
        body,
        grid=(b, n_tiles),
        in_specs=[pl.BlockSpec((1, t, c), lambda bi, i: (bi, nat(i), col_blk)),
                  pl.BlockSpec((1, HALO, c), lambda bi, i: (bi, jnp.maximum(nat(i) * hb - 1, 0), col_blk)),
                  pl.BlockSpec((1, HALO, c), lambda bi, i: (bi, jnp.minimum((nat(i) + 1) * hb, n_hblk - 1), col_blk)),
                  pl.BlockSpec(conv_w.shape, lambda bi, i: (0, 0)),
                  vec,
                  pl.BlockSpec(wg.shape, lambda bi, i: (0, 0, 0)),
                  vec, vec, vec,
                  pl.BlockSpec((1, 1, c), lambda bi, i: (bi, 0, 0))],
        out_specs=[pl.BlockSpec((1, t, c), lambda bi, i: (bi, nat(i), 0)),
                   pl.BlockSpec((1, 1, c), lambda bi, i: (bi, 0, 0))],
        out_shape=[jax.ShapeDtypeStruct((b, l, c), BF16), jax.ShapeDtypeStruct((b, 1, c), F32)],
        scratch_shapes=[pltpu.VMEM((t, c), F32), pltpu.VMEM((t, 2 * c), F32)]
        + [pltpu.VMEM((c // LANES, SUBLANES * pitch, LANES), F32) for _ in range(4)]
        + [pltpu.VMEM((1, c), F32)],
        compiler_params=_cparams("parallel", "arbitrary"),
        name="rg_scan_bwd" if reverse else "rg_scan_fwd",
    )(p, p, p, conv_w, conv_b, wg, ba, bx, lam, h0)


def gate_blocks(wa, wx, n_blk):
    h, d, _ = wa.shape
    hp = h // n_blk
    eye = jnp.eye(hp, dtype=wa.dtype)

    def bd(w):
        w = w.reshape(n_blk, hp, d, d)
        return jnp.einsum('khde,hg->khdge', w, eye).reshape(n_blk, hp * d, hp * d)

    return (0.5 * jnp.concatenate([bd(wa), bd(wx)], axis=-1)).astype(BF16)


HY_HID = 64


def _filter_body(w1t_ref, b1_ref, w2t_ref, b2_ref, fr_ref, w3t_ref, o_ref, z_s, *, l, c, ct, rows_grid):
    d = pl.program_id(0)
    j = pl.program_id(1)
    lane = lax.broadcasted_iota(I32, (1, l), 1)
    s_i = jnp.where(d == 0, lane, l - lane)
    sf = s_i.astype(F32)
    t_norm = sf / float(max(l - 1, 1))

    @pl.when(j == 0)
    def _():
        frow = lax.broadcasted_iota(I32, (HY_HID, 1), 0)
        frf = frow.astype(F32)
        band_step = (HY_SEQ_BANDS - 1 - 1e-4) / (HY_SEQ_BANDS - 1)
        n_seq = 1 + 2 * HY_SEQ_BANDS
        is_seq_cos = (frow >= 1) & (frow < 1 + HY_SEQ_BANDS)
        is_seq_sin = (frow >= 1 + HY_SEQ_BANDS) & (frow < n_seq)
        is_col_cos = (frow >= n_seq + 1) & (frow < n_seq + 1 + HY_COL_BANDS)
        is_col_sin = (frow >= n_seq + 1 + HY_COL_BANDS) & (frow < n_seq + 1 + 2 * HY_COL_BANDS)
        seq_band = 1e-4 + band_step * jnp.where(is_seq_cos, frf - 1.0, frf - (1.0 + HY_SEQ_BANDS))
        col_band = jnp.where(is_col_cos, frf - float(n_seq), frf - float(n_seq + HY_COL_BANDS))
        col_pos = (s_i & (GRID_W - 1)).astype(F32)
        row_lag = (s_i >> int(math.log2(GRID_W))).astype(F32) / float(rows_grid)
        ang = jnp.where(is_seq_cos | is_seq_sin,
                        ((2.0 * math.pi / l) * sf) * seq_band,
                        ((2.0 * math.pi / GRID_W) * col_pos) * col_band)
        feats = jnp.where(is_seq_cos | is_col_cos, jnp.cos(ang),
                          jnp.where(is_seq_sin | is_col_sin, jnp.sin(ang), 0.0))
        feats = jnp.where(frow == 0, t_norm, feats)
        feats = jnp.where(frow == n_seq, row_lag, feats)
        fr = fr_ref[...]
        z = jnp.sin(fr * (jnp.dot(w1t_ref[...], feats, precision=HIGHEST, preferred_element_type=F32) + b1_ref[...]))
        z_s[...] = jnp.sin(fr * (jnp.dot(w2t_ref[...], z, precision=HIGHEST, preferred_element_type=F32) + b2_ref[...]))

    k = jnp.dot(w3t_ref[0], z_s[...], precision=HIGHEST, preferred_element_type=F32)
    ch = (lax.broadcasted_iota(I32, (ct, 1), 0) + j * ct).astype(F32)
    max_decay = math.log(HY_DECAY_TARGET) / HY_FAST_DECAY
    min_decay = math.log(HY_DECAY_TARGET) / HY_SLOW_DECAY
    delta = jnp.abs(min_decay + ch * ((max_decay - min_decay) / (c - 1)))
    k = k * jnp.exp(-t_norm * delta)
    k = jnp.where((d == 1) & (lane == 0), 0.0, k)
    o_ref[0] = k.astype(o_ref.dtype)


def hyena_filter_t(w1t, b1, w2t, b2, fr, w3t, l, tile_c):
    assert GRID_W & (GRID_W - 1) == 0
    c = w3t.shape[1]
    ct = min(tile_c, c)
    body = functools.partial(_filter_body, l=l, c=c, ct=ct, rows_grid=l // GRID_W)
    small = lambda shape: pl.BlockSpec(shape, lambda d, j: (0,) * len(shape))
    return pl.pallas_call(
        body,
        grid=(2, c // ct),
        in_specs=[small(w1t.shape), small(b1.shape), small(w2t.shape), small(b2.shape), small(fr.shape),
                  pl.BlockSpec((1, ct, HY_HID), lambda d, j: (d, j, 0))],
        out_specs=pl.BlockSpec((1, ct, l), lambda d, j: (d, j, 0)),
        out_shape=jax.ShapeDtypeStruct((2, c, l), BF16),
        scratch_shapes=[pltpu.VMEM((HY_HID, l), F32)],
        compiler_params=_cparams("arbitrary", "arbitrary"),
        name="hyena_filter",
    )(w1t, b1, w2t, b2, fr, w3t)


def dft_tables(l):
    import numpy as np
    n = 2 * l
    r_in, nk = l // LANES, n // LANES
    ka = np.arange(nk)[:, None].astype(np.float64)
    r = np.arange(r_in)[None, :].astype(np.float64)
    a1 = 2.0 * np.pi * ka * r / nk
    f1 = np.concatenate([np.cos(a1), -np.sin(a1)], axis=0)
    lane = np.arange(LANES)[None, :].astype(np.float64)
    at = 2.0 * np.pi * ka * lane / n
    twr, twi = np.cos(at), -np.sin(at)
    a2 = 2.0 * np.pi * np.arange(LANES)[:, None] * np.arange(LANES)[None, :] / LANES
    cr, ci = np.cos(a2), -np.sin(a2)
    m2 = np.block([[cr, ci], [-ci, cr]])
    m2i = np.block([[cr, -ci], [ci, cr]])
    ai = 2.0 * np.pi * np.arange(r_in)[:, None] * np.arange(nk)[None, :] / nk
    gi = np.concatenate([np.cos(ai), -np.sin(ai)], axis=1) / n
    as_bf = lambda a: jnp.asarray(a, F32).astype(BF16)
    return as_bf(f1), jnp.asarray(twr, F32), jnp.asarray(twi, F32), as_bf(m2), as_bf(m2i), as_bf(gi)


def _fwd_rows_twiddle(x_a, x_b, f1, twr, twi, nk):
    a = jnp.dot(f1, jnp.concatenate([x_a, x_b], axis=1), preferred_element_type=F32)
    out = []
    for h in range(2):
        re, im = a[:nk, h * LANES:(h + 1) * LANES], a[nk:, h * LANES:(h + 1) * LANES]
        out.append((re * twr - im * twi, re * twi + im * twr))
    return out


def _spectrum_body(k_ref, f1_ref, twr_ref, twi_ref, m2_ref, o_ref, *, g, nk, r_in):
    f1, twr, twi = f1_ref[...], twr_ref[...], twi_ref[...]
    sign = jnp.where((lax.broadcasted_iota(I32, (nk, 1), 0) & 1) == 0, 1.0, -1.0)
    a2 = []
    for ci in range(g):
        (fre, fim), (bre, bim) = _fwd_rows_twiddle(k_ref[0, ci], k_ref[1, ci], f1, twr, twi, nk)
        a2.append(jnp.concatenate([fre + sign * bre, fim + sign * bim], axis=1).astype(BF16))
    spec = jnp.dot(jnp.concatenate(a2, axis=0), m2_ref[...], preferred_element_type=F32)
    o_ref[...] = spec.reshape(g, nk, 2 * LANES).astype(o_ref.dtype)


def hyena_spectrum(kt4, tables, group):
    _, c, r_in, _ = kt4.shape
    nk = 2 * r_in
    f1, twr, twi, m2, _, _ = tables
    g = min(group, c)
    full = lambda a: pl.BlockSpec(a.shape, lambda j: (0,) * a.ndim)
    return pl.pallas_call(
        functools.partial(_spectrum_body, g=g, nk=nk, r_in=r_in),
        grid=(c // g,),
        in_specs=[pl.BlockSpec((2, g, r_in, LANES), lambda j: (0, j, 0, 0)), full(f1), full(twr), full(twi), full(m2)],
        out_specs=pl.BlockSpec((g, nk, 2 * LANES), lambda j: (j, 0, 0)),
        out_shape=jax.ShapeDtypeStruct((c, nk, 2 * LANES), BF16),
        compiler_params=_cparams("parallel"),
        name="hyena_spectrum",
    )(kt4, f1, twr, twi, m2)


HY_CHUNK = 256


def _hyena_proj_body(x_ref, g_ref, sh_ref, sc_ref, w_ref, taps_ref, u_ref, z0_ref, pa_s, pb_s, last_s,
                     *, n_tiles, t, c):
    i = pl.program_id(1)

    @pl.when(i == 0)
    def _():
        last_s[...] = jnp.zeros_like(last_s)
        pb_s[...] = jnp.zeros_like(pb_s)

    def step(p_new, p_old):
        hx = _norm_mod(x_ref, g_ref, sh_ref, sc_ref)
        for k in range(3):
            p_new[:, k * c:(k + 1) * c] = jnp.dot(hx, w_ref[:, k * c:(k + 1) * c], preferred_element_type=F32)
        row = lax.broadcasted_iota(I32, (t, 1), 0)
        has_next = (i < n_tiles).astype(F32)
        cw = min(HY_CHUNK, c)
        for j in range(c // cw):
            zs = []
            for k in range(3):
                cols = slice(k * c + j * cw, k * c + (j + 1) * cw)
                p = p_old[:, cols]
                up = jnp.where(row == 0, last_s[:, cols], pltpu.roll(p, 1, 0))
                dn = jnp.where(row == t - 1, p_new[0:1, cols] * has_next, pltpu.roll(p, t - 1, 0))
                tp = taps_ref[:, cols]
                zs.append(tp[3:4] + tp[0:1] * up + tp[1:2] * p + tp[2:3] * dn)
                last_s[:, cols] = p[t - 1:t, :]
            z0, z1, zv = zs
            u_t, z0_t = (zv * z1).T, z0.T
            for q in range(t // LANES):
                u_ref[0, q, j * cw:(j + 1) * cw, :] = u_t[:, q * LANES:(q + 1) * LANES].astype(u_ref.dtype)
                z0_ref[0, q, j * cw:(j + 1) * cw, :] = z0_t[:, q * LANES:(q + 1) * LANES].astype(z0_ref.dtype)

    parity = lax.rem(i, 2)

    @pl.when(parity == 0)
    def _():
        step(pa_s, pb_s)

    @pl.when(parity == 1)
    def _():
        step(pb_s, pa_s)


def hyena_proj(x, g, shift, scale, w, taps, *, tile_l):
    b, l, d = x.shape
    c = w.shape[1] // 3
    t = min(tile_l, l)
    n_tiles = l // t
    rq = t // LANES
    o_spec = pl.BlockSpec((1, rq, c, LANES), lambda bi, i: (bi, jnp.maximum(i - 1, 0), 0, 0))
    o_shape = jax.ShapeDtypeStruct((b, l // LANES, c, LANES), BF16)
    return pl.pallas_call(
        functools.partial(_hyena_proj_body, n_tiles=n_tiles, t=t, c=c),
        grid=(b, n_tiles + 1),
        in_specs=[pl.BlockSpec((1, t, d), lambda bi, i: (bi, jnp.minimum(i, n_tiles - 1), 0)),
                  pl.BlockSpec((1, d), lambda bi, i: (0, 0)),
                  pl.BlockSpec((1, 1, d), lambda bi, i: (bi, 0, 0)),
                  pl.BlockSpec((1, 1, d), lambda bi, i: (bi, 0, 0)),
                  pl.BlockSpec(w.shape, lambda bi, i: (0, 0)),
                  pl.BlockSpec(taps.shape, lambda bi, i: (0, 0))],
        out_specs=[o_spec, o_spec],
        out_shape=[o_shape, o_shape],
        scratch_shapes=[pltpu.VMEM((t, 3 * c), F32), pltpu.VMEM((t, 3 * c), F32), pltpu.VMEM((1, 3 * c), F32)],
        compiler_params=_cparams("parallel", "arbitrary"),
        name="hyena_proj",
    )(x, g, shift, scale, w, taps)


def _fftconv_body(u_ref, z0_ref, skip_ref, k_ref, f1_ref, twr_ref, twi_ref, m2_ref, m2i_ref, gi_ref,
                  o_ref, u_s, z0_s, y_s, *, ct, g, nk, r_in, pitch):
    for r in range(r_in):
        u_s[r * pitch:r * pitch + ct, :] = u_ref[0, r].astype(F32)
        z0_s[r * pitch:r * pitch + ct, :] = z0_ref[0, r].astype(F32)
    f1, twr, twi = f1_ref[...], twr_ref[...], twi_ref[...]

    def chan(ref, ch):
        return ref[pl.ds(ch, r_in, stride=pitch), :]

    def group(c0):
        a2 = []
        for ci in range(0, g, 2):
            pair = _fwd_rows_twiddle(chan(u_s, c0 + ci).astype(BF16), chan(u_s, c0 + ci + 1).astype(BF16),
                                     f1, twr, twi, nk)
            a2 += [jnp.concatenate([tre, tim], axis=1).astype(BF16) for tre, tim in pair]
        spec = jnp.dot(jnp.concatenate(a2, axis=0), m2_ref[...], preferred_element_type=F32)
        kf = k_ref[pl.ds(c0, g)].astype(F32).reshape(g * nk, 2 * LANES)
        sre, sim = spec[:, :LANES], spec[:, LANES:]
        kre, kim = kf[:, :LANES], kf[:, LANES:]
        prod = jnp.concatenate([sre * kre - sim * kim, sre * kim + sim * kre], axis=1).astype(BF16)
        cc = jnp.dot(prod, m2i_ref[...], preferred_element_type=F32)
        for ci in range(0, g, 2):
            st = []
            for h in range(2):
                blk = cc[(ci + h) * nk:(ci + h + 1) * nk]
                cre, cim = blk[:, :LANES], blk[:, LANES:]
                st.append(jnp.concatenate([cre * twr + cim * twi, cim * twr - cre * twi], axis=0).astype(BF16))
            y2 = jnp.dot(gi_ref[...], jnp.concatenate(st, axis=1), preferred_element_type=F32)
            for h in range(2):
                ch = c0 + ci + h
                y = y2[:, h * LANES:(h + 1) * LANES]
                y_s[pl.ds(ch, r_in, stride=pitch), :] = (y + chan(u_s, ch) * skip_ref[ch]) * chan(z0_s, ch)

    def two_groups(i, _):
        group(2 * g * i)
        group(2 * g * i + g)
        return 0

    lax.fori_loop(0, ct // (2 * g), two_groups, 0)
    for r in range(r_in):
        o_ref[0, r] = y_s[r * pitch:r * pitch + ct, :].astype(o_ref.dtype)


def hyena_fftconv(u, z0, skip3, spec, tables, *, tile_c, group):
    b, r_in, c, _ = u.shape
    nk = 2 * r_in
    ct = min(tile_c, c)
    g = min(group, ct // 2)
    assert ct % (2 * g) == 0 and g % 2 == 0
    pitch = ct + SUBLANES
    f1, twr, twi, m2, m2i, gi = tables
    full = lambda a: pl.BlockSpec(a.shape, lambda j, bi: (0,) * a.ndim)
    io_spec = pl.BlockSpec((1, r_in, ct, LANES), lambda j, bi: (bi, 0, j, 0))
    return pl.pallas_call(
        functools.partial(_fftconv_body, ct=ct, g=g, nk=nk, r_in=r_in, pitch=pitch),
        grid=(c // ct, b),
        in_specs=[io_spec, io_spec,
                  pl.BlockSpec((ct, 1, 1), lambda j, bi: (j, 0, 0)),
                  pl.BlockSpec((ct, nk, 2 * LANES), lambda j, bi: (j, 0, 0)),
                  full(f1), full(twr), full(twi), full(m2), full(m2i), full(gi)],
        out_specs=io_spec,
        out_shape=jax.ShapeDtypeStruct((b, r_in, c, LANES), BF16),
        scratch_shapes=[pltpu.VMEM((r_in * pitch, LANES), F32) for _ in range(3)],
        compiler_params=_cparams("parallel", "arbitrary"),
        name="hyena_fftconv",
    )(u, z0, skip3, spec, f1, twr, twi, m2, m2i, gi)


ROUTE_LANES = LANES
NEG_BIG = -1e30
HALF_WORD = 16


def _pack_bf16_pairs(v):
    h = v.shape[1] // 2
    bits = pltpu.bitcast(v.astype(BF16).astype(F32), I32)
    return bits[:, :h] | lax.shift_right_logical(bits[:, h:], HALF_WORD)


def _unpack_bf16_pairs(w):
    hi = pltpu.bitcast(w & jnp.int32(-65536), F32)
    lo = pltpu.bitcast(lax.shift_left(w, HALF_WORD), F32)
    return jnp.concatenate([hi, lo], axis=1)


SLAB = 4


def _store_row_slabs(ref, words):
    rows = words.shape[0]
    for j in range(SLAB):
        ref[pl.ds(j, rows, stride=SLAB), :] = words[:, j * LANES:(j + 1) * LANES]


def _load_row_slabs(ref):
    rows = ref.shape[0] // SLAB
    return jnp.concatenate([ref[pl.ds(j, rows, stride=SLAB), :] for j in range(SLAB)], axis=1)


def _mix_route_body(x_ref, hf_ref, hb_ref, prg_ref, pga_ref, pgb_ref, yt_ref, rgp_ref, hyp_ref, wo_ref, g1_ref,
                    n2g_ref, sh2_ref, sc2_ref, wr_ref, br_ref, tri_ref,
                    x1_ref, hxp_ref, route_ref, cnt_ref, carry_s, *, t, n_exp):
    @pl.when((pl.program_id(0) == 0) & (pl.program_id(1) == 0))
    def _():
        carry_s[...] = jnp.zeros_like(carry_s)

    hsum = hf_ref[0].astype(F32) + hb_ref[0].astype(F32)
    y_rg = (hsum * _gelu_tanh(prg_ref[0].astype(F32))).astype(BF16)
    t1 = jnp.dot(y_rg, rgp_ref[...], preferred_element_type=F32)
    t2 = jnp.concatenate([lax.dot_general(yt_ref[0, q], hyp_ref[...], (((0,), (0,)), ((), ())),
                                          preferred_element_type=F32) for q in range(t // LANES)], axis=0)
    merged = _sigmoid(pga_ref[0].astype(F32)) * t1 + _sigmoid(pgb_ref[0].astype(F32)) * t2
    out = jnp.dot(merged.astype(BF16), wo_ref[...], preferred_element_type=F32)
    x1 = x_ref[0] + g1_ref[0] * out
    x1_ref[0] = x1
    ms = jnp.mean(x1 * x1, axis=-1, keepdims=True)
    hx2 = (x1 * lax.rsqrt(ms + EPS) * n2g_ref[...]) * (1.0 + sc2_ref[0]) + sh2_ref[0]
    _store_row_slabs(hxp_ref, _pack_bf16_pairs(hx2))

    hx_hi = hx2.astype(BF16)
    hx_lo = (hx2 - hx_hi.astype(F32)).astype(BF16)
    parts = (jnp.dot(hx_hi, wr_ref[...], preferred_element_type=F32)
             + jnp.dot(hx_lo, wr_ref[...], preferred_element_type=F32))
    logits = parts[:, :ROUTE_LANES] + parts[:, ROUTE_LANES:] + br_ref[...]
    lane = lax.broadcasted_iota(I32, (t, ROUTE_LANES), 1)
    is_g = lane < N_GROUPS
    glog = jnp.where(is_g, logits, NEG_BIG)
    gmax = jnp.max(glog, axis=1, keepdims=True)
    gidx = jnp.min(jnp.where(glog == gmax, lane, ROUTE_LANES), axis=1, keepdims=True)
    gsum = jnp.sum(jnp.where(is_g, jnp.exp(glog - gmax), 0.0), axis=1, keepdims=True)
    p_g = 1.0 / gsum
    e_lane = lane - N_GROUPS
    grp_of_lane = lax.shift_right_arithmetic(e_lane, int(math.log2(EXPERTS_PER_GROUP)))
    in_grp = (e_lane >= 0) & (e_lane < n_exp) & (grp_of_lane == gidx)
    elog = jnp.where(in_grp, logits, NEG_BIG)
    m1 = jnp.max(elog, axis=1, keepdims=True)
    i1 = jnp.min(jnp.where(elog == m1, lane, ROUTE_LANES), axis=1, keepdims=True)
    elog2 = jnp.where(lane == i1, NEG_BIG, elog)
    m2 = jnp.max(elog2, axis=1, keepdims=True)
    i2 = jnp.min(jnp.where(elog2 == m2, lane, ROUTE_LANES), axis=1, keepdims=True)
    e21 = jnp.exp(m2 - m1)
    pk1 = 1.0 / (1.0 + e21)
    wt1, wt2 = p_g * pk1, p_g * (e21 * pk1)

    oh1 = (lane == i1 - N_GROUPS).astype(F32)
    oh2 = (lane == i2 - N_GROUPS).astype(F32)
    cnt = oh1 + oh2
    before = jnp.dot(tri_ref[...], cnt.astype(BF16), preferred_element_type=F32) + carry_s[...]
    rank1 = jnp.sum(oh1 * before, axis=1, keepdims=True)
    rank2 = jnp.sum(oh2 * before, axis=1, keepdims=True)
    carry_s[...] = carry_s[...] + jnp.sum(cnt, axis=0, keepdims=True)
    cnt_ref[...] = carry_s[...]
    vals = ((i1 - N_GROUPS).astype(F32), (i2 - N_GROUPS).astype(F32), rank1, rank2, wt1, wt2)
    route = jnp.zeros((t, ROUTE_LANES), F32)
    for k, v in enumerate(vals):
        route = jnp.where(lane == k, v, route)
    route_ref[...] = route


def mix_route(x, h_f, h_b, p_rm, y_hy_t, rg_proj, hy_proj, w_out, g1, n2g, sh2, sc2, wr, br, *, tile_l, n_exp):
    b, l, d = x.shape
    c = h_f.shape[2]
    t = min(tile_l, l)
    nt = l // t
    n = b * l
    tri = (jnp.arange(t)[:, None] > jnp.arange(t)[None, :]).astype(BF16)
    tok = lambda bi, i: (bi, i, 0)
    col = lambda k: (lambda bi, i: (bi, i, k))
    full2 = lambda a: pl.BlockSpec(a.shape, lambda bi, i: (0, 0))
    per_b = pl.BlockSpec((1, 1, d), lambda bi, i: (bi, 0, 0))
    row = lambda bi, i: (bi * nt + i, 0)
    return pl.pallas_call(
        functools.partial(_mix_route_body, t=t, n_exp=n_exp),
        grid=(b, nt),
        in_specs=[pl.BlockSpec((1, t, d), tok), pl.BlockSpec((1, t, c), tok), pl.BlockSpec((1, t, c), tok),
                  pl.BlockSpec((1, t, c), col(1)), pl.BlockSpec((1, t, c), col(2)), pl.BlockSpec((1, t, c), col(3)),
                  pl.BlockSpec((1, t // LANES, c, LANES), lambda bi, i: (bi, i, 0, 0)),
                  full2(rg_proj), full2(hy_proj), full2(w_out), per_b,
                  full2(n2g), per_b, per_b, full2(wr), full2(br), full2(tri)],
        out_specs=[pl.BlockSpec((1, t, d), tok),
                   pl.BlockSpec((t * SLAB, LANES), row),
                   pl.BlockSpec((t, ROUTE_LANES), row), pl.BlockSpec((1, ROUTE_LANES), lambda bi, i: (0, 0))],
        out_shape=[jax.ShapeDtypeStruct((b, l, d), F32), jax.ShapeDtypeStruct((n * SLAB, LANES), I32),
                   jax.ShapeDtypeStruct((n, ROUTE_LANES), F32), jax.ShapeDtypeStruct((1, ROUTE_LANES), F32)],
        scratch_shapes=[pltpu.VMEM((1, ROUTE_LANES), F32)],
        compiler_params=_cparams("arbitrary", "arbitrary"),
        name="mix_route",
    )(x, h_f, h_b, p_rm, p_rm, p_rm, y_hy_t, rg_proj, hy_proj, w_out, g1, n2g, sh2, sc2, wr, br, tri)


def _dest_body(route_ref, cnt_ref, ut_ref, dest_ref, blk_ref, *, t, n_exp, nb_pad):
    lane1 = lax.broadcasted_iota(I32, (1, ROUTE_LANES), 1)
    padded = jnp.floor((cnt_ref[...] + (MOE_BLOCK - 1.0)) * (1.0 / MOE_BLOCK)) * MOE_BLOCK
    padded = jnp.where(lane1 < n_exp, padded, 0.0)
    pend = jnp.dot(jnp.broadcast_to(padded, (SUBLANES, ROUTE_LANES)), ut_ref[...], precision=HIGHEST,
                   preferred_element_type=F32)[0:1]
    pstart = pend - padded
    route = route_ref[...]
    lane = lax.broadcasted_iota(I32, (t, ROUTE_LANES), 1)
    lf = lane.astype(F32)
    d1 = jnp.sum(jnp.where(lf == route[:, 0:1], pstart, 0.0), axis=1, keepdims=True) + route[:, 2:3]
    d2 = jnp.sum(jnp.where(lf == route[:, 1:2], pstart, 0.0), axis=1, keepdims=True) + route[:, 3:4]
    dmat = jnp.where(lane == 0, d1, jnp.where(lane == 1, d2, 0.0))
    dest_ref[...] = dmat.T[0:SUBLANES].astype(I32)
    first_row = lax.broadcasted_iota(I32, (nb_pad, ROUTE_LANES), 0).astype(F32) * float(MOE_BLOCK)
    lane_b = lax.broadcasted_iota(I32, (nb_pad, ROUTE_LANES), 1)
    nle = jnp.sum(jnp.where((lane_b < n_exp) & (pend <= first_row), 1.0, 0.0), axis=1, keepdims=True)
    blk_ref[...] = jnp.broadcast_to(jnp.minimum(nle, n_exp - 1.0), (nb_pad, ROUTE_LANES)).astype(I32)


def moe_dest(route, cnt, *, tile, n_exp, n_blocks):
    n = route.shape[0]
    t = min(tile, n)
    nb_pad = -(-n_blocks // SUBLANES) * SUBLANES
    ut = (jnp.arange(ROUTE_LANES)[:, None] <= jnp.arange(ROUTE_LANES)[None, :]).astype(F32)
    return pl.pallas_call(
        functools.partial(_dest_body, t=t, n_exp=n_exp, nb_pad=nb_pad),
        grid=(n // t,),
        in_specs=[pl.BlockSpec((t, ROUTE_LANES), lambda i: (i, 0)),
                  pl.BlockSpec((1, ROUTE_LANES), lambda i: (0, 0)),
                  pl.BlockSpec((ROUTE_LANES, ROUTE_LANES), lambda i: (0, 0))],
        out_specs=[pl.BlockSpec((SUBLANES, t), lambda i: (i, 0)),
                   pl.BlockSpec((nb_pad, ROUTE_LANES), lambda i: (0, 0))],
        out_shape=[jax.ShapeDtypeStruct((n // t * SUBLANES, t), I32),
                   jax.ShapeDtypeStruct((nb_pad, ROUTE_LANES), I32)],
        compiler_params=_cparams("arbitrary"),
        name="moe_dest",
    )(route, cnt, ut)


def _scatter_body(dest_ref, hx_ref, xb_in_ref, xb_ref, sem, *, t):
    del xb_in_ref

    def issue(r, _):
        for k in range(2):
            pltpu.make_async_copy(hx_ref.at[pl.ds(SLAB * r, SLAB)], xb_ref.at[pl.ds(SLAB * dest_ref[k, r], SLAB)],
                                  sem).start(priority=k)
        return 0

    lax.fori_loop(0, t, issue, 0, unroll=8)
    for k in range(2):
        pltpu.make_async_copy(hx_ref, xb_ref.at[pl.ds(0, SLAB * t)], sem).wait()


def moe_scatter(dest, hxp, n_rows, *, tile):
    n = hxp.shape[0] // SLAB
    t = min(tile, n)
    xb0 = jnp.zeros((n_rows * SLAB, LANES), I32)
    return pl.pallas_call(
        functools.partial(_scatter_body, t=t),
        grid=(n // t,),
        in_specs=[pl.BlockSpec((SUBLANES, t), lambda i: (i, 0), memory_space=pltpu.SMEM),
                  pl.BlockSpec((t * SLAB, LANES), lambda i: (i, 0)),
                  pl.BlockSpec(memory_space=pl.ANY)],
        out_specs=pl.BlockSpec(memory_space=pl.ANY),
        out_shape=jax.ShapeDtypeStruct((n_rows * SLAB, LANES), I32),
        scratch_shapes=[pltpu.SemaphoreType.DMA],
        input_output_aliases={2: 0},
        compiler_params=_cparams("arbitrary"),
        name="moe_scatter",
    )(dest, hxp, xb0)


def _expert_body(blk_ref, xb_ref, w1_ref, w3_ref, w2_ref, yb_ref, w1_s, w3_s, w2_s):
    i = pl.program_id(0)
    changed = (i == 0) | (blk_ref[i] != blk_ref[jnp.maximum(i - 1, 0)])

    @pl.when(changed)
    def _():
        w1_s[...] = w1_ref[0].astype(BF16)
        w3_s[...] = w3_ref[0].astype(BF16)
        w2_s[...] = w2_ref[0].astype(BF16)

    xblk = _unpack_bf16_pairs(_load_row_slabs(xb_ref)).astype(BF16)
    h1 = jnp.dot(xblk, w1_s[...], preferred_element_type=F32)
    h3 = jnp.dot(xblk, w3_s[...], preferred_element_type=F32)
    hid = (h1 * _sigmoid(h1) * h3).astype(BF16)
    _store_row_slabs(yb_ref, _pack_bf16_pairs(jnp.dot(hid, w2_s[...], preferred_element_type=F32)))


def moe_experts(blk_exp, xb, w1, w3, w2):
    p = xb.shape[0] // SLAB
    _, d, de = w1.shape
    nb = p // MOE_BLOCK
    grid_spec = pltpu.PrefetchScalarGridSpec(
        num_scalar_prefetch=1,
        grid=(nb,),
        in_specs=[pl.BlockSpec((MOE_BLOCK * SLAB, LANES), lambda i, blk: (i, 0)),
                  pl.BlockSpec((1, d, de), lambda i, blk: (blk[i], 0, 0)),
                  pl.BlockSpec((1, d, de), lambda i, blk: (blk[i], 0, 0)),
                  pl.BlockSpec((1, de, d), lambda i, blk: (blk[i], 0, 0))],
        out_specs=pl.BlockSpec((MOE_BLOCK * SLAB, LANES), lambda i, blk: (i, 0)),
        scratch_shapes=[pltpu.VMEM((d, de), BF16), pltpu.VMEM((d, de), BF16), pltpu.VMEM((de, d), BF16)],
    )
    return pl.pallas_call(
        _expert_body,
        grid_spec=grid_spec,
        out_shape=jax.ShapeDtypeStruct((p * SLAB, LANES), I32),
        compiler_params=_cparams("arbitrary"),
        name="moe_experts",
    )(blk_exp, xb, w1, w3, w2)


def _combine_body(dest_ref, x1_ref, route_ref, g2_ref, fg_ref, yb_ref, o_ref, y1_s, y2_s, sem, *, t):
    def issue(r, _):
        pltpu.make_async_copy(yb_ref.at[pl.ds(SLAB * dest_ref[0, r], SLAB)], y1_s.at[pl.ds(SLAB * r, SLAB)],
                              sem).start(priority=0)
        pltpu.make_async_copy(yb_ref.at[pl.ds(SLAB * dest_ref[1, r], SLAB)], y2_s.at[pl.ds(SLAB * r, SLAB)],
                              sem).start(priority=1)
        return 0

    lax.fori_loop(0, t, issue, 0, unroll=8)
    for y_s in (y1_s, y2_s):
        pltpu.make_async_copy(yb_ref.at[pl.ds(0, SLAB * t)], y_s, sem).wait()
    route = route_ref[...]
    moe = (route[:, 4:5] * _unpack_bf16_pairs(_load_row_slabs(y1_s))
           + route[:, 5:6] * _unpack_bf16_pairs(_load_row_slabs(y2_s)))
    x2 = x1_ref[0] + g2_ref[0] * moe
    ms = jnp.mean(x2 * x2, axis=-1, keepdims=True)
    o_ref[0] = x2 * lax.rsqrt(ms + EPS) * fg_ref[...]


def moe_combine(dest, x1, route, g2, final_g, yb, *, tile_l):
    b, l, d = x1.shape
    t = min(tile_l, l)
    nt = l // t
    slab = (t * SLAB, LANES)
    return pl.pallas_call(
        functools.partial(_combine_body, t=t),
        grid=(b, nt),
        in_specs=[pl.BlockSpec((SUBLANES, t), lambda bi, i: (bi * nt + i, 0), memory_space=pltpu.SMEM),
                  pl.BlockSpec((1, t, d), lambda bi, i: (bi, i, 0)),
                  pl.BlockSpec((t, ROUTE_LANES), lambda bi, i: (bi * nt + i, 0)),
                  pl.BlockSpec((1, 1, d), lambda bi, i: (bi, 0, 0)),
                  pl.BlockSpec((1, d), lambda bi, i: (0, 0)),
                  pl.BlockSpec(memory_space=pl.ANY)],
        out_specs=pl.BlockSpec((1, t, d), lambda bi, i: (bi, i, 0)),
        out_shape=jax.ShapeDtypeStruct((b, l, d), F32),
        scratch_shapes=[pltpu.VMEM(slab, I32), pltpu.VMEM(slab, I32), pltpu.SemaphoreType.DMA],
        compiler_params=_cparams("arbitrary", "arbitrary"),
        name="moe_combine",
    )(dest, x1, route, g2, final_g, yb)


def kernel(x, c, ctx, c_ctx, ada_w, ada_b, norm1_g, norm2_g, final_g, w_in, rg_conv_w, rg_conv_b, rg_wa_f, rg_ba_f, rg_wx_f, rg_bx_f, rg_lam_f, rg_wa_b, rg_ba_b, rg_wx_b, rg_bx_b, rg_lam_b, rg_proj, hy_conv_w, hy_conv_b, hy_pos_w1, hy_pos_b1, hy_pos_w2, hy_pos_b2, hy_freq, hy_pos_w3, hy_skip, hy_proj, w_out, moe_wg, moe_bg, moe_we, moe_be, moe_w1, moe_w3, moe_w2):
    B, L, D = x.shape
    C = rg_conv_w.shape[-1]
    LC = ctx.shape[1]
    c8 = jnp.zeros((8, D), F32).at[:B].set(c).at[B].set(c_ctx)
    mods = ada_mods(c8, ada_w[0], ada_b)
    sh1, sc1, g1 = (mods[:B, None, k * D:(k + 1) * D] for k in range(3))
    sh2, sc2, g2 = (mods[:B, None, k * D:(k + 1) * D] for k in range(3, 6))
    csh1 = jnp.broadcast_to(mods[B:B + 1, None, 0:D], (B, 1, D))
    csc1 = jnp.broadcast_to(mods[B:B + 1, None, D:2 * D], (B, 1, D))

    w_in_b = w_in[0].astype(BF16)
    w_rm = jnp.concatenate([w_in_b[:, :2 * C], w_in_b[:, 5 * C:]], axis=1)
    wg_f = gate_blocks(rg_wa_f[0], rg_wx_f[0], C // 256)
    wg_b = gate_blocks(rg_wa_b[0], rg_wx_b[0], C // 256)
    rg_f = (rg_conv_w[0], rg_conv_b, wg_f, rg_ba_f, rg_bx_f, rg_lam_f)
    rg_b = (rg_conv_w[0], rg_conv_b, wg_b, rg_ba_b, rg_bx_b, rg_lam_b)

    pc = norm_mod_proj(ctx, norm1_g, csh1, csc1, w_rm[:, :C], tile_l=LC, chunk=C)
    zero = jnp.zeros((B, 1, C), F32)
    _, hcf = rg_scan(pc, 0, *rg_f, zero, reverse=False, tile_l=256)
    _, hcb = rg_scan(pc, 0, *rg_b, zero, reverse=True, tile_l=256)

    p_rm = norm_mod_proj(x, norm1_g, sh1, sc1, w_rm, tile_l=512, chunk=1024)
    hy_taps = jnp.concatenate([hy_conv_w[0], hy_conv_b], axis=0)
    u_hy, z0_hy = hyena_proj(x, norm1_g, sh1, sc1, w_in_b[:, 2 * C:5 * C], hy_taps, tile_l=512)
    h_f, _ = rg_scan(p_rm, 0, *rg_f, hcf, reverse=False, tile_l=256)
    h_b, _ = rg_scan(p_rm, 0, *rg_b, hcb, reverse=True, tile_l=256)

    tables = dft_tables(L)
    w1t = jnp.zeros((HY_HID, HY_HID), F32).at[:, :hy_pos_w1.shape[1]].set(hy_pos_w1[0].T)
    kt = hyena_filter_t(w1t, hy_pos_b1[0][:, None], hy_pos_w2[0].T, hy_pos_b2[0][:, None], hy_freq[0][:, None],
                        hy_pos_w3[0].T.reshape(2, C, HY_HID), L, 256)
    spec = hyena_spectrum(kt.reshape(2, C, L // LANES, LANES), tables, 8)
    y_hy_t = hyena_fftconv(u_hy, z0_hy, hy_skip[0][:, None, None], spec, tables, tile_c=64, group=8)

    n_exp = moe_we.shape[-1]
    n_grp = moe_wg.shape[-1]
    assert n_grp == N_GROUPS and n_exp == N_GROUPS * EXPERTS_PER_GROUP
    wr = jnp.zeros((D, ROUTE_LANES), F32).at[:, :n_grp].set(moe_wg[0]).at[:, n_grp:n_grp + n_exp].set(moe_we[0])
    br = jnp.zeros((1, ROUTE_LANES), F32).at[:, :n_grp].set(moe_bg).at[:, n_grp:n_grp + n_exp].set(moe_be)
    wr_hi = wr.astype(BF16)
    wr_split = jnp.concatenate([wr_hi, (wr - wr_hi.astype(F32)).astype(BF16)], axis=1)
    x1, hxp, route, cnt = mix_route(x, h_f, h_b, p_rm, y_hy_t, rg_proj[0].astype(BF16), hy_proj[0].astype(BF16),
                                    w_out[0].astype(BF16), g1, norm2_g, sh2, sc2, wr_split, br, tile_l=512,
                                    n_exp=n_exp)

    n_blocks = (2 * B * L + n_exp * (MOE_BLOCK - 1)) // MOE_BLOCK
    dest, blk = moe_dest(route, cnt, tile=512, n_exp=n_exp, n_blocks=n_blocks)
    xb = moe_scatter(dest, hxp, n_blocks * MOE_BLOCK, tile=512)
    yb = moe_experts(blk[:n_blocks, 0], xb, moe_w1[0], moe_w3[0], moe_w2[0])
    return moe_combine(dest, x1, route, g2, final_g[None], yb, tile_l=512)
```

```python
import functools
import math

import jax
import jax.numpy as jnp
from jax import lax
from jax.experimental import pallas as pl
from jax.experimental.pallas import tpu as pltpu

F32 = jnp.float32
BF16 = jnp.bfloat16
I32 = jnp.int32
HIGHEST = lax.Precision.HIGHEST

LANES = 128
SUBLANES = 8
EPS = 1e-6
RG_C = 8.0
RG_HEAD_DIM = 64
GRID_W = 64
HY_SEQ_BANDS = 16
HY_COL_BANDS = 8
HY_DECAY_TARGET = 1e-2
HY_FAST_DECAY = 0.3
HY_SLOW_DECAY = 1.5
N_GROUPS = 4
EXPERTS_PER_GROUP = 8
MOE_BLOCK = 512
VMEM_LIMIT = 56 * 1024 * 1024


def _cparams(*sem):
    return pltpu.CompilerParams(dimension_semantics=sem, vmem_limit_bytes=VMEM_LIMIT)


def _sigmoid(x):
    return 0.5 * (jnp.tanh(0.5 * x) + 1.0)


def _gelu_tanh(x):
    c = math.sqrt(2.0 / math.pi)
    return 0.5 * x * (1.0 + jnp.tanh(c * (x + 0.044715 * (x * x * x))))


def _ada_body(c_ref, w_ref, b_ref, o_ref):
    c = c_ref[...]
    s = c * _sigmoid(c)
    o_ref[...] = jnp.dot(s, w_ref[...], precision=HIGHEST, preferred_element_type=F32) + b_ref[...]


def ada_mods(c8, ada_w, ada_b):
    d, m = ada_w.shape
    tn = 1024 if m % 1024 == 0 else m
    return pl.pallas_call(
        _ada_body,
        grid=(m // tn,),
        in_specs=[pl.BlockSpec((c8.shape[0], d), lambda j: (0, 0)),
                  pl.BlockSpec((d, tn), lambda j: (0, j)),
                  pl.BlockSpec((1, tn), lambda j: (0, j))],
        out_specs=pl.BlockSpec((c8.shape[0], tn), lambda j: (0, j)),
        out_shape=jax.ShapeDtypeStruct((c8.shape[0], m), F32),
        compiler_params=_cparams("parallel"),
        name="ada_mods",
    )(c8, ada_w, ada_b)


def _norm_mod(x_ref, g_ref, sh_ref, sc_ref):
    x = x_ref[0]
    ms = jnp.mean(x * x, axis=-1, keepdims=True)
    y = x * lax.rsqrt(ms + EPS) * g_ref[...]
    return (y * (1.0 + sc_ref[0]) + sh_ref[0]).astype(BF16)


def _proj_body(x_ref, g_ref, sh_ref, sc_ref, w_ref, o_ref, *, chunk):
    hx = _norm_mod(x_ref, g_ref, sh_ref, sc_ref)
    for j in range(w_ref.shape[1] // chunk):
        cols = slice(j * chunk, (j + 1) * chunk)
        o_ref[0, :, cols] = jnp.dot(hx, w_ref[:, cols], preferred_element_type=F32).astype(o_ref.dtype)


def norm_mod_proj(x, g, shift, scale, w, *, tile_l, chunk):
    b, l, d = x.shape
    m = w.shape[1]
    tl = min(tile_l, l)
    return pl.pallas_call(
        functools.partial(_proj_body, chunk=min(chunk, m)),
        grid=(b, l // tl),
        in_specs=[pl.BlockSpec((1, tl, d), lambda bi, i: (bi, i, 0)),
                  pl.BlockSpec((1, d), lambda bi, i: (0, 0)),
                  pl.BlockSpec((1, 1, d), lambda bi, i: (bi, 0, 0)),
                  pl.BlockSpec((1, 1, d), lambda bi, i: (bi, 0, 0)),
                  pl.BlockSpec(w.shape, lambda bi, i: (0, 0))],
        out_specs=pl.BlockSpec((1, tl, m), lambda bi, i: (bi, i, 0)),
        out_shape=jax.ShapeDtypeStruct((b, l, m), BF16),
        compiler_params=_cparams("parallel", "parallel"),
        name="norm_mod_proj",
    )(x, g, shift, scale, w)


HALO = 16


def _scan_body(pc_ref, pp_ref, pn_ref, cw_ref, cb_ref, wg_ref, ba_ref, bx_ref, lam_ref, h0_ref,
               h_ref, hl_ref, ext_s, xc_s, g_s, a_s, b_s, hloc_s, pcum_s, carry_s,
               *, reverse, n_tiles, t, c, s_len, pitch):
    i = pl.program_id(1)
    ti = (n_tiles - 1 - i) if reverse else i
    n_slab = c // LANES
    n_blk = wg_ref.shape[0]
    blk = c // n_blk

    @pl.when(i == 0)
    def _():
        carry_s[...] = h0_ref[0]

    cur = pc_ref[0].astype(F32)
    has_prev = (ti > 0).astype(F32)
    has_next = (ti < n_tiles - 1).astype(F32)
    ext_s[0:SUBLANES, :] = pp_ref[0, HALO - SUBLANES:HALO, :].astype(F32) * has_prev
    ext_s[SUBLANES:SUBLANES + t, :] = cur
    ext_s[SUBLANES + t:2 * SUBLANES + t, :] = pn_ref[0, 0:SUBLANES, :].astype(F32) * has_next
    cw = cw_ref[...]
    xc = cb_ref[...] + cw[2:3] * cur
    for tap, off in ((0, -2), (1, -1), (3, 1)):
        xc = xc + cw[tap:tap + 1] * ext_s[SUBLANES + off:SUBLANES + off + t, :]
    xc_s[...] = xc

    for k in range(n_blk):
        xb = xc_s[:, k * blk:(k + 1) * blk].astype(BF16)
        g_s[:, k * 2 * blk:(k + 1) * 2 * blk] = jnp.dot(xb, wg_ref[k], preferred_element_type=F32)

    lam = lam_ref[...]
    softplus_neg_lam = jnp.maximum(-lam, 0.0) + jnp.log1p(jnp.exp(-jnp.abs(lam)))
    half_ca = (-0.5 * RG_C) * softplus_neg_lam
    half_ba, half_bx = 0.5 * ba_ref[...], 0.5 * bx_ref[...]
    slabs_per_blk = blk // LANES
    for j in range(SUBLANES):
        r0 = j * s_len
        for k in range(n_slab):
            kb, ks = k // slabs_per_blk, k % slabs_per_blk
            ga = g_s[r0:r0 + s_len, kb * 2 * blk + ks * LANES:kb * 2 * blk + (ks + 1) * LANES]
            gx = g_s[r0:r0 + s_len, kb * 2 * blk + blk + ks * LANES:kb * 2 * blk + blk + (ks + 1) * LANES]
            lane = slice(k * LANES, (k + 1) * LANES)
            half_x = 0.5 * xc_s[r0:r0 + s_len, lane]
            hca = half_ca[:, lane]
            log_a = hca * jnp.tanh(ga + half_ba[:, lane]) + hca
            gated_x = half_x * jnp.tanh(gx + half_bx[:, lane]) + half_x
            a = jnp.exp(log_a)
            a_s[k, j * pitch:j * pitch + s_len, :] = a
            gain2 = -jnp.tanh(log_a) * (a * a + 1.0)
            gain = jnp.where(gain2 > 0.0, gain2 * lax.rsqrt(gain2), 0.0)
            b_s[k, j * pitch:j * pitch + s_len, :] = gain * gated_x

    def step1(s, hp):
        hs, ps = hp
        srow = (s_len - 1 - s) if reverse else s
        hs2, ps2 = [], []
        for k in range(n_slab):
            av = a_s[k, pl.ds(srow, SUBLANES, stride=pitch), :]
            bv = b_s[k, pl.ds(srow, SUBLANES, stride=pitch), :]
            h = av * hs[k] + bv
            p = av * ps[k]
            hloc_s[k, pl.ds(srow, SUBLANES, stride=pitch), :] = h
            pcum_s[k, pl.ds(srow, SUBLANES, stride=pitch), :] = p
            hs2.append(h)
            ps2.append(p)
        return tuple(hs2), tuple(ps2)

    zeros = tuple(jnp.zeros((SUBLANES, LANES), F32) for _ in range(n_slab))
    ones = tuple(jnp.ones((SUBLANES, LANES), F32) for _ in range(n_slab))
    h_end, p_end = lax.fori_loop(0, s_len, step1, (zeros, ones))

    order = range(SUBLANES - 1, -1, -1) if reverse else range(SUBLANES)
    for k in range(n_slab):
        cst = carry_s[:, k * LANES:(k + 1) * LANES]
        for j in order:
            rows = slice(j * pitch, j * pitch + s_len)
            h_ref[0, j * s_len:(j + 1) * s_len, k * LANES:(k + 1) * LANES] = (
                hloc_s[k, rows, :] + pcum_s[k, rows, :] * cst).astype(h_ref.dtype)
            cst = p_end[k][j:j + 1] * cst + h_end[k][j:j + 1]
        carry_s[:, k * LANES:(k + 1) * LANES] = cst
    hl_ref[0] = carry_s[...]


def rg_scan(p, col_blk, conv_w, conv_b, wg, ba, bx, lam, h0, *, reverse, tile_l):
    b, l, _ = p.shape
    c = conv_w.shape[1]
    t = min(tile_l, l)
    n_tiles = l // t
    s_len = t // SUBLANES
    pitch = s_len + SUBLANES
    hb = t // HALO
    n_hblk = l // HALO

    def nat(i):
        return (n_tiles - 1 - i) if reverse else i

    body = functools.partial(_scan_body, reverse=reverse, n_tiles=n_tiles, t=t, c=c, s_len=s_len, pitch=pitch)
    vec = pl.BlockSpec((1, c), lambda bi, i: (0, 0))
    return pl.pallas_call(
        body,
        grid=(b, n_tiles),
        in_specs=[pl.BlockSpec((1, t, c), lambda bi, i: (bi, nat(i), col_blk)),
                  pl.BlockSpec((1, HALO, c), lambda bi, i: (bi, jnp.maximum(nat(i) * hb - 1, 0), col_blk)),
                  pl.BlockSpec((1, HALO, c), lambda bi, i: (bi, jnp.minimum((nat(i) + 1) * hb, n_hblk - 1), col_blk)),
                  pl.BlockSpec(conv_w.shape, lambda bi, i: (0, 0)),
                  vec,
                  pl.BlockSpec(wg.shape, lambda bi, i: (0, 0, 0)),
                  vec, vec, vec,
                  pl.BlockSpec((1, 1, c), lambda bi, i: (bi, 0, 0))],
        out_specs=[pl.BlockSpec((1, t, c), lambda bi, i: (bi, nat(i), 0)),
                   pl.BlockSpec((1, 1, c), lambda bi, i: (bi, 0, 0))],
        out_shape=[jax.ShapeDtypeStruct((b, l, c), BF16), jax.ShapeDtypeStruct((b, 1, c), F32)],
        scratch_shapes=[pltpu.VMEM((t + 2 * SUBLANES, c), F32), pltpu.VMEM((t, c), F32), pltpu.VMEM((t, 2 * c), F32)]
        + [pltpu.VMEM((c // LANES, SUBLANES * pitch, LANES), F32) for _ in range(4)]
        + [pltpu.VMEM((1, c), F32)],
        compiler_params=_cparams("parallel", "arbitrary"),
        name="rg_scan_bwd" if reverse else "rg_scan_fwd",
    )(p, p, p, conv_w, conv_b, wg, ba, bx, lam, h0)


def gate_blocks(wa, wx, n_blk):
    h, d, _ = wa.shape
    hp = h // n_blk
    eye = jnp.eye(hp, dtype=wa.dtype)

    def bd(w):
        w = w.reshape(n_blk, hp, d, d)
        return jnp.einsum('khde,hg->khdge', w, eye).reshape(n_blk, hp * d, hp * d)

    return (0.5 * jnp.concatenate([bd(wa), bd(wx)], axis=-1)).astype(BF16)


HY_HID = 64
HY_FEATURE_ORDER = (list(range(1, 1 + 2 * HY_SEQ_BANDS))
                    + list(range(2 + 2 * HY_SEQ_BANDS, 2 + 2 * HY_SEQ_BANDS + 2 * HY_COL_BANDS))
                    + [0, 1 + 2 * HY_SEQ_BANDS])


def _filter_body(w1t_ref, b1_ref, w2t_ref, b2_ref, fr_ref, w3t_ref, o_ref, z_s, *, l, c, ct, rows_grid):
    d = pl.program_id(0)
    j = pl.program_id(1)
    lane = lax.broadcasted_iota(I32, (1, l), 1)
    s_i = jnp.where(d == 0, lane, l - lane)
    sf = s_i.astype(F32)
    t_norm = sf / float(max(l - 1, 1))

    @pl.when(j == 0)
    def _():
        band_step = (HY_SEQ_BANDS - 1 - 1e-4) / (HY_SEQ_BANDS - 1)
        seq_band = 1e-4 + band_step * lax.broadcasted_iota(I32, (HY_SEQ_BANDS, 1), 0).astype(F32)
        col_band = 1.0 + lax.broadcasted_iota(I32, (HY_COL_BANDS, 1), 0).astype(F32)
        col_pos = (s_i & (GRID_W - 1)).astype(F32)
        row_lag = (s_i >> int(math.log2(GRID_W))).astype(F32) / float(rows_grid)
        ang_seq = ((2.0 * math.pi / l) * sf) * seq_band
        ang_col = ((2.0 * math.pi / GRID_W) * col_pos) * col_band
        n_trig = 2 * HY_SEQ_BANDS + 2 * HY_COL_BANDS
        trow = lax.broadcasted_iota(I32, (HY_HID - n_trig, 1), 0)
        tail = jnp.where(trow == 0, t_norm, jnp.where(trow == 1, row_lag, 0.0))
        feats = jnp.concatenate([jnp.cos(ang_seq), jnp.sin(ang_seq), jnp.cos(ang_col), jnp.sin(ang_col), tail], axis=0)
        fr = fr_ref[...]
        z = jnp.sin(fr * (jnp.dot(w1t_ref[...], feats, precision=HIGHEST, preferred_element_type=F32) + b1_ref[...]))
        z_s[...] = jnp.sin(fr * (jnp.dot(w2t_ref[...], z, precision=HIGHEST, preferred_element_type=F32) + b2_ref[...]))

    k = jnp.dot(w3t_ref[0], z_s[...].astype(BF16), preferred_element_type=F32)
    ch = (lax.broadcasted_iota(I32, (ct, 1), 0) + j * ct).astype(F32)
    max_decay = math.log(HY_DECAY_TARGET) / HY_FAST_DECAY
    min_decay = math.log(HY_DECAY_TARGET) / HY_SLOW_DECAY
    delta = jnp.abs(min_decay + ch * ((max_decay - min_decay) / (c - 1)))
    k = k * jnp.exp(-t_norm * delta)
    k = jnp.where((d == 1) & (lane == 0), 0.0, k)
    o_ref[0] = k.astype(o_ref.dtype)


def hyena_filter_t(w1t, b1, w2t, b2, fr, w3t, l, tile_c):
    assert GRID_W & (GRID_W - 1) == 0
    c = w3t.shape[1]
    ct = min(tile_c, c)
    body = functools.partial(_filter_body, l=l, c=c, ct=ct, rows_grid=l // GRID_W)
    small = lambda shape: pl.BlockSpec(shape, lambda d, j: (0,) * len(shape))
    return pl.pallas_call(
        body,
        grid=(2, c // ct),
        in_specs=[small(w1t.shape), small(b1.shape), small(w2t.shape), small(b2.shape), small(fr.shape),
                  pl.BlockSpec((1, ct, HY_HID), lambda d, j: (d, j, 0))],
        out_specs=pl.BlockSpec((1, ct, l), lambda d, j: (d, j, 0)),
        out_shape=jax.ShapeDtypeStruct((2, c, l), BF16),
        scratch_shapes=[pltpu.VMEM((HY_HID, l), F32)],
        compiler_params=_cparams("arbitrary", "arbitrary"),
        name="hyena_filter",
    )(w1t, b1, w2t, b2, fr, w3t)


def dft_tables(l):
    import numpy as np
    n = 2 * l
    r_in, nk = l // LANES, n // LANES
    ka = np.arange(nk)[:, None].astype(np.float64)
    r = np.arange(r_in)[None, :].astype(np.float64)
    a1 = 2.0 * np.pi * ka * r / nk
    f1 = np.concatenate([np.cos(a1), -np.sin(a1)], axis=0)
    lane = np.arange(LANES)[None, :].astype(np.float64)
    at = 2.0 * np.pi * ka * lane / n
    twr, twi = np.cos(at), -np.sin(at)
    a2 = 2.0 * np.pi * np.arange(LANES)[:, None] * np.arange(LANES)[None, :] / LANES
    cr, ci = np.cos(a2), -np.sin(a2)
    m2 = np.block([[cr, ci], [-ci, cr]])
    m2i = np.block([[cr, -ci], [ci, cr]])
    ai = 2.0 * np.pi * np.arange(r_in)[:, None] * np.arange(nk)[None, :] / nk
    gi = np.concatenate([np.cos(ai), -np.sin(ai)], axis=1) / n
    as_bf = lambda a: jnp.asarray(a, F32).astype(BF16)
    return as_bf(f1), jnp.asarray(twr, F32), jnp.asarray(twi, F32), as_bf(m2), as_bf(m2i), as_bf(gi)


def _fwd_rows_twiddle(x_a, x_b, f1, twr, twi, nk):
    a = jnp.dot(f1, jnp.concatenate([x_a, x_b], axis=1), preferred_element_type=F32)
    out = []
    for h in range(2):
        re, im = a[:nk, h * LANES:(h + 1) * LANES], a[nk:, h * LANES:(h + 1) * LANES]
        out.append((re * twr - im * twi, re * twi + im * twr))
    return out


def _spectrum_body(k_ref, f1_ref, twr_ref, twi_ref, m2_ref, o_ref, *, g, nk, r_in):
    f1, twr, twi = f1_ref[...], twr_ref[...], twi_ref[...]
    sign = jnp.where((lax.broadcasted_iota(I32, (nk, 1), 0) & 1) == 0, 1.0, -1.0)
    a2 = []
    for ci in range(g):
        (fre, fim), (bre, bim) = _fwd_rows_twiddle(k_ref[0, ci], k_ref[1, ci], f1, twr, twi, nk)
        a2.append(jnp.concatenate([fre + sign * bre, fim + sign * bim], axis=1).astype(BF16))
    spec = jnp.dot(jnp.concatenate(a2, axis=0), m2_ref[...], preferred_element_type=F32)
    o_ref[...] = spec.reshape(g, nk, 2 * LANES).astype(o_ref.dtype)


def hyena_spectrum(kt4, tables, group):
    _, c, r_in, _ = kt4.shape
    nk = 2 * r_in
    f1, twr, twi, m2, _, _ = tables
    g = min(group, c)
    full = lambda a: pl.BlockSpec(a.shape, lambda j: (0,) * a.ndim)
    return pl.pallas_call(
        functools.partial(_spectrum_body, g=g, nk=nk, r_in=r_in),
        grid=(c // g,),
        in_specs=[pl.BlockSpec((2, g, r_in, LANES), lambda j: (0, j, 0, 0)), full(f1), full(twr), full(twi), full(m2)],
        out_specs=pl.BlockSpec((g, nk, 2 * LANES), lambda j: (j, 0, 0)),
        out_shape=jax.ShapeDtypeStruct((c, nk, 2 * LANES), BF16),
        compiler_params=_cparams("parallel"),
        name="hyena_spectrum",
    )(kt4, f1, twr, twi, m2)


HY_CHUNK = 256


def _hyena_proj_body(x_ref, g_ref, sh_ref, sc_ref, w_ref, taps_ref, u_ref, z0_ref, pa_s, pb_s, last_s,
                     *, n_tiles, t, c):
    i = pl.program_id(1)

    @pl.when(i == 0)
    def _():
        last_s[...] = jnp.zeros_like(last_s)
        pb_s[...] = jnp.zeros_like(pb_s)

    def step(p_new, p_old):
        hx = _norm_mod(x_ref, g_ref, sh_ref, sc_ref)
        for k in range(3):
            p_new[:, k * c:(k + 1) * c] = jnp.dot(hx, w_ref[:, k * c:(k + 1) * c], preferred_element_type=F32)
        row = lax.broadcasted_iota(I32, (t, 1), 0)
        has_next = (i < n_tiles).astype(F32)
        cw = min(HY_CHUNK, c)
        for j in range(c // cw):
            zs = []
            for k in range(3):
                cols = slice(k * c + j * cw, k * c + (j + 1) * cw)
                p = p_old[:, cols]
                up = jnp.where(row == 0, last_s[:, cols], pltpu.roll(p, 1, 0))
                dn = jnp.where(row == t - 1, p_new[0:1, cols] * has_next, pltpu.roll(p, t - 1, 0))
                tp = taps_ref[:, cols]
                zs.append(tp[3:4] + tp[0:1] * up + tp[1:2] * p + tp[2:3] * dn)
                last_s[:, cols] = p[t - 1:t, :]
            z0, z1, zv = zs
            u_t, z0_t = (zv * z1).T, z0.T
            for q in range(t // LANES):
                u_ref[0, q, j * cw:(j + 1) * cw, :] = u_t[:, q * LANES:(q + 1) * LANES].astype(u_ref.dtype)
                z0_ref[0, q, j * cw:(j + 1) * cw, :] = z0_t[:, q * LANES:(q + 1) * LANES].astype(z0_ref.dtype)

    parity = lax.rem(i, 2)

    @pl.when(parity == 0)
    def _():
        step(pa_s, pb_s)

    @pl.when(parity == 1)
    def _():
        step(pb_s, pa_s)


def hyena_proj(x, g, shift, scale, w, taps, *, tile_l):
    b, l, d = x.shape
    c = w.shape[1] // 3
    t = min(tile_l, l)
    n_tiles = l // t
    rq = t // LANES
    o_spec = pl.BlockSpec((1, rq, c, LANES), lambda bi, i: (bi, jnp.maximum(i - 1, 0), 0, 0))
    o_shape = jax.ShapeDtypeStruct((b, l // LANES, c, LANES), BF16)
    return pl.pallas_call(
        functools.partial(_hyena_proj_body, n_tiles=n_tiles, t=t, c=c),
        grid=(b, n_tiles + 1),
        in_specs=[pl.BlockSpec((1, t, d), lambda bi, i: (bi, jnp.minimum(i, n_tiles - 1), 0)),
                  pl.BlockSpec((1, d), lambda bi, i: (0, 0)),
                  pl.BlockSpec((1, 1, d), lambda bi, i: (bi, 0, 0)),
                  pl.BlockSpec((1, 1, d), lambda bi, i: (bi, 0, 0)),
                  pl.BlockSpec(w.shape, lambda bi, i: (0, 0)),
                  pl.BlockSpec(taps.shape, lambda bi, i: (0, 0))],
        out_specs=[o_spec, o_spec],
        out_shape=[o_shape, o_shape],
        scratch_shapes=[pltpu.VMEM((t, 3 * c), F32), pltpu.VMEM((t, 3 * c), F32), pltpu.VMEM((1, 3 * c), F32)],
        compiler_params=_cparams("parallel", "arbitrary"),
        name="hyena_proj",
    )(x, g, shift, scale, w, taps)


def _fftconv_body(u_ref, z0_ref, skip_ref, k_ref, f1_ref, twr_ref, twi_ref, m2_ref, m2i_ref, gi_ref,
                  o_ref, u_s, z0_s, y_s, *, ct, g, nk, r_in, pitch):
    for r in range(r_in):
        u_s[r * pitch:r * pitch + ct, :] = u_ref[0, r].astype(F32)
        z0_s[r * pitch:r * pitch + ct, :] = z0_ref[0, r].astype(F32)
    f1, twr, twi = f1_ref[...], twr_ref[...], twi_ref[...]

    def chan(ref, ch):
        return ref[pl.ds(ch, r_in, stride=pitch), :]

    def group(c0):
        a2 = []
        for ci in range(0, g, 2):
            pair = _fwd_rows_twiddle(chan(u_s, c0 + ci).astype(BF16), chan(u_s, c0 + ci + 1).astype(BF16),
                                     f1, twr, twi, nk)
            a2 += [jnp.concatenate([tre, tim], axis=1).astype(BF16) for tre, tim in pair]
        spec = jnp.dot(jnp.concatenate(a2, axis=0), m2_ref[...], preferred_element_type=F32)
        kf = k_ref[pl.ds(c0, g)].astype(F32).reshape(g * nk, 2 * LANES)
        sre, sim = spec[:, :LANES], spec[:, LANES:]
        kre, kim = kf[:, :LANES], kf[:, LANES:]
        prod = jnp.concatenate([sre * kre - sim * kim, sre * kim + sim * kre], axis=1).astype(BF16)
        cc = jnp.dot(prod, m2i_ref[...], preferred_element_type=F32)
        for ci in range(0, g, 2):
            st = []
            for h in range(2):
                blk = cc[(ci + h) * nk:(ci + h + 1) * nk]
                cre, cim = blk[:, :LANES], blk[:, LANES:]
                st.append(jnp.concatenate([cre * twr + cim * twi, cim * twr - cre * twi], axis=0).astype(BF16))
            y2 = jnp.dot(gi_ref[...], jnp.concatenate(st, axis=1), preferred_element_type=F32)
            for h in range(2):
                ch = c0 + ci + h
                y = y2[:, h * LANES:(h + 1) * LANES]
                y_s[pl.ds(ch, r_in, stride=pitch), :] = (y + chan(u_s, ch) * skip_ref[ch]) * chan(z0_s, ch)

    def two_groups(i, _):
        group(2 * g * i)
        group(2 * g * i + g)
        return 0

    lax.fori_loop(0, ct // (2 * g), two_groups, 0)
    for r in range(r_in):
        o_ref[0, r] = y_s[r * pitch:r * pitch + ct, :].astype(o_ref.dtype)


def hyena_fftconv(u, z0, skip3, spec, tables, *, tile_c, group):
    b, r_in, c, _ = u.shape
    nk = 2 * r_in
    ct = min(tile_c, c)
    g = min(group, ct // 2)
    assert ct % (2 * g) == 0 and g % 2 == 0
    pitch = ct + SUBLANES
    f1, twr, twi, m2, m2i, gi = tables
    full = lambda a: pl.BlockSpec(a.shape, lambda j, bi: (0,) * a.ndim)
    io_spec = pl.BlockSpec((1, r_in, ct, LANES), lambda j, bi: (bi, 0, j, 0))
    return pl.pallas_call(
        functools.partial(_fftconv_body, ct=ct, g=g, nk=nk, r_in=r_in, pitch=pitch),
        grid=(c // ct, b),
        in_specs=[io_spec, io_spec,
                  pl.BlockSpec((ct, 1, 1), lambda j, bi: (j, 0, 0)),
                  pl.BlockSpec((ct, nk, 2 * LANES), lambda j, bi: (j, 0, 0)),
                  full(f1), full(twr), full(twi), full(m2), full(m2i), full(gi)],
        out_specs=io_spec,
        out_shape=jax.ShapeDtypeStruct((b, r_in, c, LANES), BF16),
        scratch_shapes=[pltpu.VMEM((r_in * pitch, LANES), F32) for _ in range(3)],
        compiler_params=_cparams("parallel", "arbitrary"),
        name="hyena_fftconv",
    )(u, z0, skip3, spec, f1, twr, twi, m2, m2i, gi)


ROUTE_LANES = LANES
NEG_BIG = -1e30
HALF_WORD = 16


def _pack_bf16_pairs(v):
    h = v.shape[1] // 2
    bits = pltpu.bitcast(v.astype(BF16).astype(F32), I32)
    return bits[:, :h] | lax.shift_right_logical(bits[:, h:], HALF_WORD)


def _unpack_bf16_pairs(w):
    hi = pltpu.bitcast(w & jnp.int32(-65536), F32)
    lo = pltpu.bitcast(lax.shift_left(w, HALF_WORD), F32)
    return jnp.concatenate([hi, lo], axis=1)


SLAB = 4


def _store_row_slabs(ref, words, row0=0):
    rows = words.shape[0]
    for j in range(SLAB):
        ref[pl.ds(SLAB * row0 + j, rows, stride=SLAB), :] = words[:, j * LANES:(j + 1) * LANES]


def _load_row_slabs(ref):
    rows = ref.shape[0] // SLAB
    return jnp.concatenate([ref[pl.ds(j, rows, stride=SLAB), :] for j in range(SLAB)], axis=1)


MIX_SUB = 512


def _mix_route_body(x_ref, hf_ref, hb_ref, prg_ref, pga_ref, pgb_ref, yt_ref, rgp_ref, hyp_ref, wo_ref, g1_ref,
                    n2g_ref, sh2_ref, sc2_ref, wr_ref, br_ref, tri_ref,
                    x1_ref, hxp_ref, route_ref, cnt_ref, carry_s, *, t, sub, n_exp):
    @pl.when((pl.program_id(0) == 0) & (pl.program_id(1) == 0))
    def _():
        carry_s[...] = jnp.zeros_like(carry_s)

    running = carry_s[...]
    for r0 in range(0, t, sub):
        rows = slice(r0, r0 + sub)
        hsum = hf_ref[0, rows, :].astype(F32) + hb_ref[0, rows, :].astype(F32)
        y_rg = (hsum * _gelu_tanh(prg_ref[0, rows, :].astype(F32))).astype(BF16)
        t1 = jnp.dot(y_rg, rgp_ref[...], preferred_element_type=F32)
        t2 = jnp.concatenate([lax.dot_general(yt_ref[0, q], hyp_ref[...], (((0,), (0,)), ((), ())),
                                              preferred_element_type=F32)
                              for q in range(r0 // LANES, (r0 + sub) // LANES)], axis=0)
        merged = _sigmoid(pga_ref[0, rows, :].astype(F32)) * t1 + _sigmoid(pgb_ref[0, rows, :].astype(F32)) * t2
        out = jnp.dot(merged.astype(BF16), wo_ref[...], preferred_element_type=F32)
        x1 = x_ref[0, rows, :] + g1_ref[0] * out
        x1_ref[0, rows, :] = x1
        ms = jnp.mean(x1 * x1, axis=-1, keepdims=True)
        hx2 = (x1 * lax.rsqrt(ms + EPS) * n2g_ref[...]) * (1.0 + sc2_ref[0]) + sh2_ref[0]
        _store_row_slabs(hxp_ref, _pack_bf16_pairs(hx2), r0)

        hx_hi = hx2.astype(BF16)
        hx_lo = (hx2 - hx_hi.astype(F32)).astype(BF16)
        parts = (jnp.dot(hx_hi, wr_ref[...], preferred_element_type=F32)
                 + jnp.dot(hx_lo, wr_ref[...], preferred_element_type=F32))
        logits = parts[:, :ROUTE_LANES] + parts[:, ROUTE_LANES:] + br_ref[...]
        lane = lax.broadcasted_iota(I32, (sub, ROUTE_LANES), 1)
        is_g = lane < N_GROUPS
        glog = jnp.where(is_g, logits, NEG_BIG)
        gmax = jnp.max(glog, axis=1, keepdims=True)
        gidx = jnp.min(jnp.where(glog == gmax, lane, ROUTE_LANES), axis=1, keepdims=True)
        gsum = jnp.sum(jnp.where(is_g, jnp.exp(glog - gmax), 0.0), axis=1, keepdims=True)
        p_g = 1.0 / gsum
        e_lane = lane - N_GROUPS
        grp_of_lane = lax.shift_right_arithmetic(e_lane, int(math.log2(EXPERTS_PER_GROUP)))
        in_grp = (e_lane >= 0) & (e_lane < n_exp) & (grp_of_lane == gidx)
        elog = jnp.where(in_grp, logits, NEG_BIG)
        m1 = jnp.max(elog, axis=1, keepdims=True)
        i1 = jnp.min(jnp.where(elog == m1, lane, ROUTE_LANES), axis=1, keepdims=True)
        elog2 = jnp.where(lane == i1, NEG_BIG, elog)
        m2 = jnp.max(elog2, axis=1, keepdims=True)
        i2 = jnp.min(jnp.where(elog2 == m2, lane, ROUTE_LANES), axis=1, keepdims=True)
        e21 = jnp.exp(m2 - m1)
        pk1 = 1.0 / (1.0 + e21)
        wt1, wt2 = p_g * pk1, p_g * (e21 * pk1)

        oh1 = (lane == i1 - N_GROUPS).astype(F32)
        oh2 = (lane == i2 - N_GROUPS).astype(F32)
        cnt = oh1 + oh2
        before = jnp.dot(tri_ref[...], cnt.astype(BF16), preferred_element_type=F32) + running
        rank1 = jnp.sum(oh1 * before, axis=1, keepdims=True)
        rank2 = jnp.sum(oh2 * before, axis=1, keepdims=True)
        running = running + jnp.sum(cnt, axis=0, keepdims=True)
        vals = ((i1 - N_GROUPS).astype(F32), (i2 - N_GROUPS).astype(F32), rank1, rank2, wt1, wt2)
        route = jnp.zeros((sub, ROUTE_LANES), F32)
        for k, v in enumerate(vals):
            route = jnp.where(lane == k, v, route)
        route_ref[rows, :] = route
    carry_s[...] = running
    cnt_ref[...] = running


def mix_route(x, h_f, h_b, p_rm, y_hy_t, rg_proj, hy_proj, w_out, g1, n2g, sh2, sc2, wr, br, *, tile_l, n_exp):
    b, l, d = x.shape
    c = h_f.shape[2]
    t = min(tile_l, l)
    nt = l // t
    n = b * l
    sub = min(MIX_SUB, t)
    tri = (jnp.arange(sub)[:, None] > jnp.arange(sub)[None, :]).astype(BF16)
    tok = lambda bi, i: (bi, i, 0)
    col = lambda k: (lambda bi, i: (bi, i, k))
    full2 = lambda a: pl.BlockSpec(a.shape, lambda bi, i: (0, 0))
    per_b = pl.BlockSpec((1, 1, d), lambda bi, i: (bi, 0, 0))
    row = lambda bi, i: (bi * nt + i, 0)
    return pl.pallas_call(
        functools.partial(_mix_route_body, t=t, sub=sub, n_exp=n_exp),
        grid=(b, nt),
        in_specs=[pl.BlockSpec((1, t, d), tok), pl.BlockSpec((1, t, c), tok), pl.BlockSpec((1, t, c), tok),
                  pl.BlockSpec((1, t, c), col(1)), pl.BlockSpec((1, t, c), col(2)), pl.BlockSpec((1, t, c), col(3)),
                  pl.BlockSpec((1, t // LANES, c, LANES), lambda bi, i: (bi, i, 0, 0)),
                  full2(rg_proj), full2(hy_proj), full2(w_out), per_b,
                  full2(n2g), per_b, per_b, full2(wr), full2(br), full2(tri)],
        out_specs=[pl.BlockSpec((1, t, d), tok),
                   pl.BlockSpec((t * SLAB, LANES), row),
                   pl.BlockSpec((t, ROUTE_LANES), row), pl.BlockSpec((1, ROUTE_LANES), lambda bi, i: (0, 0))],
        out_shape=[jax.ShapeDtypeStruct((b, l, d), F32), jax.ShapeDtypeStruct((n * SLAB, LANES), I32),
                   jax.ShapeDtypeStruct((n, ROUTE_LANES), F32), jax.ShapeDtypeStruct((1, ROUTE_LANES), F32)],
        scratch_shapes=[pltpu.VMEM((1, ROUTE_LANES), F32)],
        compiler_params=_cparams("arbitrary", "arbitrary"),
        name="mix_route",
    )(x, h_f, h_b, p_rm, p_rm, p_rm, y_hy_t, rg_proj, hy_proj, w_out, g1, n2g, sh2, sc2, wr, br, tri)


def _dest_body(route_ref, cnt_ref, ut_ref, dest_ref, blk_ref, *, t, n_exp, nb_pad):
    lane1 = lax.broadcasted_iota(I32, (1, ROUTE_LANES), 1)
    padded = jnp.floor((cnt_ref[...] + (MOE_BLOCK - 1.0)) * (1.0 / MOE_BLOCK)) * MOE_BLOCK
    padded = jnp.where(lane1 < n_exp, padded, 0.0)
    pend = jnp.dot(jnp.broadcast_to(padded, (SUBLANES, ROUTE_LANES)), ut_ref[...], precision=HIGHEST,
                   preferred_element_type=F32)[0:1]
    pstart = pend - padded
    route = route_ref[...]
    lane = lax.broadcasted_iota(I32, (t, ROUTE_LANES), 1)
    lf = lane.astype(F32)
    d1 = jnp.sum(jnp.where(lf == route[:, 0:1], pstart, 0.0), axis=1, keepdims=True) + route[:, 2:3]
    d2 = jnp.sum(jnp.where(lf == route[:, 1:2], pstart, 0.0), axis=1, keepdims=True) + route[:, 3:4]
    dmat = jnp.where(lane == 0, d1, jnp.where(lane == 1, d2, 0.0))
    dest_ref[...] = dmat.T[0:SUBLANES].astype(I32)
    first_row = lax.broadcasted_iota(I32, (nb_pad, ROUTE_LANES), 0).astype(F32) * float(MOE_BLOCK)
    lane_b = lax.broadcasted_iota(I32, (nb_pad, ROUTE_LANES), 1)
    nle = jnp.sum(jnp.where((lane_b < n_exp) & (pend <= first_row), 1.0, 0.0), axis=1, keepdims=True)
    blk_ref[...] = jnp.broadcast_to(jnp.minimum(nle, n_exp - 1.0), (nb_pad, ROUTE_LANES)).astype(I32)


def moe_dest(route, cnt, *, tile, n_exp, n_blocks):
    n = route.shape[0]
    t = min(tile, n)
    nb_pad = -(-n_blocks // SUBLANES) * SUBLANES
    ut = (jnp.arange(ROUTE_LANES)[:, None] <= jnp.arange(ROUTE_LANES)[None, :]).astype(F32)
    return pl.pallas_call(
        functools.partial(_dest_body, t=t, n_exp=n_exp, nb_pad=nb_pad),
        grid=(n // t,),
        in_specs=[pl.BlockSpec((t, ROUTE_LANES), lambda i: (i, 0)),
                  pl.BlockSpec((1, ROUTE_LANES), lambda i: (0, 0)),
                  pl.BlockSpec((ROUTE_LANES, ROUTE_LANES), lambda i: (0, 0))],
        out_specs=[pl.BlockSpec((SUBLANES, t), lambda i: (i, 0)),
                   pl.BlockSpec((nb_pad, ROUTE_LANES), lambda i: (0, 0))],
        out_shape=[jax.ShapeDtypeStruct((n // t * SUBLANES, t), I32),
                   jax.ShapeDtypeStruct((nb_pad, ROUTE_LANES), I32)],
        compiler_params=_cparams("arbitrary"),
        name="moe_dest",
    )(route, cnt, ut)


def _scatter_body(dest_ref, hx_ref, xb_in_ref, xb_ref, sem, *, t):
    del xb_in_ref

    def issue(r, _):
        for k in range(2):
            pltpu.make_async_copy(hx_ref.at[pl.ds(SLAB * r, SLAB)], xb_ref.at[pl.ds(SLAB * dest_ref[k, r], SLAB)],
                                  sem).start(priority=k)
        return 0

    lax.fori_loop(0, t, issue, 0, unroll=8)
    for k in range(2):
        pltpu.make_async_copy(hx_ref, xb_ref.at[pl.ds(0, SLAB * t)], sem).wait()


def moe_scatter(dest, hxp, n_rows, *, tile):
    n = hxp.shape[0] // SLAB
    t = min(tile, n)
    xb0 = jnp.zeros((n_rows * SLAB, LANES), I32)
    per_dest_tile = dest.shape[1] // t
    return pl.pallas_call(
        functools.partial(_scatter_body, t=t),
        grid=(n // t,),
        in_specs=[pl.BlockSpec((SUBLANES, t), lambda i: (i // per_dest_tile, i % per_dest_tile),
                               memory_space=pltpu.SMEM),
                  pl.BlockSpec((t * SLAB, LANES), lambda i: (i, 0)),
                  pl.BlockSpec(memory_space=pl.ANY)],
        out_specs=pl.BlockSpec(memory_space=pl.ANY),
        out_shape=jax.ShapeDtypeStruct((n_rows * SLAB, LANES), I32),
        scratch_shapes=[pltpu.SemaphoreType.DMA],
        input_output_aliases={2: 0},
        compiler_params=_cparams("arbitrary"),
        name="moe_scatter",
    )(dest, hxp, xb0)


def _expert_body(blk_ref, xb_ref, w1_ref, w3_ref, w2_ref, yb_ref, w1_s, w3_s, w2_s):
    i = pl.program_id(0)
    changed = (i == 0) | (blk_ref[i] != blk_ref[jnp.maximum(i - 1, 0)])

    @pl.when(changed)
    def _():
        w1_s[...] = w1_ref[0].astype(BF16)
        w3_s[...] = w3_ref[0].astype(BF16)
        w2_s[...] = w2_ref[0].astype(BF16)

    xblk = _unpack_bf16_pairs(_load_row_slabs(xb_ref)).astype(BF16)
    h1 = jnp.dot(xblk, w1_s[...], preferred_element_type=F32)
    h3 = jnp.dot(xblk, w3_s[...], preferred_element_type=F32)
    hid = (h1 * _sigmoid(h1) * h3).astype(BF16)
    _store_row_slabs(yb_ref, _pack_bf16_pairs(jnp.dot(hid, w2_s[...], preferred_element_type=F32)))


def moe_experts(blk_exp, xb, w1, w3, w2):
    p = xb.shape[0] // SLAB
    _, d, de = w1.shape
    nb = p // MOE_BLOCK
    grid_spec = pltpu.PrefetchScalarGridSpec(
        num_scalar_prefetch=1,
        grid=(nb,),
        in_specs=[pl.BlockSpec((MOE_BLOCK * SLAB, LANES), lambda i, blk: (i, 0)),
                  pl.BlockSpec((1, d, de), lambda i, blk: (blk[i], 0, 0)),
                  pl.BlockSpec((1, d, de), lambda i, blk: (blk[i], 0, 0)),
                  pl.BlockSpec((1, de, d), lambda i, blk: (blk[i], 0, 0))],
        out_specs=pl.BlockSpec((MOE_BLOCK * SLAB, LANES), lambda i, blk: (i, 0)),
        scratch_shapes=[pltpu.VMEM((d, de), BF16), pltpu.VMEM((d, de), BF16), pltpu.VMEM((de, d), BF16)],
    )
    return pl.pallas_call(
        _expert_body,
        grid_spec=grid_spec,
        out_shape=jax.ShapeDtypeStruct((p * SLAB, LANES), I32),
        compiler_params=_cparams("arbitrary"),
        name="moe_experts",
    )(blk_exp, xb, w1, w3, w2)


def _combine_body(dest_ref, x1_ref, route_ref, g2_ref, fg_ref, yb_ref, o_ref, y1_s, y2_s, sem, *, t):
    def issue(r, _):
        pltpu.make_async_copy(yb_ref.at[pl.ds(SLAB * dest_ref[0, r], SLAB)], y1_s.at[pl.ds(SLAB * r, SLAB)],
                              sem).start(priority=0)
        pltpu.make_async_copy(yb_ref.at[pl.ds(SLAB * dest_ref[1, r], SLAB)], y2_s.at[pl.ds(SLAB * r, SLAB)],
                              sem).start(priority=1)
        return 0

    lax.fori_loop(0, t, issue, 0, unroll=8)
    for y_s in (y1_s, y2_s):
        pltpu.make_async_copy(yb_ref.at[pl.ds(0, SLAB * t)], y_s, sem).wait()
    route = route_ref[...]
    moe = (route[:, 4:5] * _unpack_bf16_pairs(_load_row_slabs(y1_s))
           + route[:, 5:6] * _unpack_bf16_pairs(_load_row_slabs(y2_s)))
    x2 = x1_ref[0] + g2_ref[0] * moe
    ms = jnp.mean(x2 * x2, axis=-1, keepdims=True)
    o_ref[0] = x2 * lax.rsqrt(ms + EPS) * fg_ref[...]


def moe_combine(dest, x1, route, g2, final_g, yb, *, tile_l):
    b, l, d = x1.shape
    t = min(tile_l, l)
    nt = l // t
    slab = (t * SLAB, LANES)
    per_dest_tile = dest.shape[1] // t
    return pl.pallas_call(
        functools.partial(_combine_body, t=t),
        grid=(b, nt),
        in_specs=[pl.BlockSpec((SUBLANES, t),
                               lambda bi, i: ((bi * nt + i) // per_dest_tile, (bi * nt + i) % per_dest_tile),
                               memory_space=pltpu.SMEM),
                  pl.BlockSpec((1, t, d), lambda bi, i: (bi, i, 0)),
                  pl.BlockSpec((t, ROUTE_LANES), lambda bi, i: (bi * nt + i, 0)),
                  pl.BlockSpec((1, 1, d), lambda bi, i: (bi, 0, 0)),
                  pl.BlockSpec((1, d), lambda bi, i: (0, 0)),
                  pl.BlockSpec(memory_space=pl.ANY)],
        out_specs=pl.BlockSpec((1, t, d), lambda bi, i: (bi, i, 0)),
        out_shape=jax.ShapeDtypeStruct((b, l, d), F32),
        scratch_shapes=[pltpu.VMEM(slab, I32), pltpu.VMEM(slab, I32), pltpu.SemaphoreType.DMA],
        compiler_params=_cparams("arbitrary", "arbitrary"),
        name="moe_combine",
    )(dest, x1, route, g2, final_g, yb)


def kernel(x, c, ctx, c_ctx, ada_w, ada_b, norm1_g, norm2_g, final_g, w_in, rg_conv_w, rg_conv_b, rg_wa_f, rg_ba_f, rg_wx_f, rg_bx_f, rg_lam_f, rg_wa_b, rg_ba_b, rg_wx_b, rg_bx_b, rg_lam_b, rg_proj, hy_conv_w, hy_conv_b, hy_pos_w1, hy_pos_b1, hy_pos_w2, hy_pos_b2, hy_freq, hy_pos_w3, hy_skip, hy_proj, w_out, moe_wg, moe_bg, moe_we, moe_be, moe_w1, moe_w3, moe_w2):
    B, L, D = x.shape
    C = rg_conv_w.shape[-1]
    LC = ctx.shape[1]
    c8 = jnp.zeros((8, D), F32).at[:B].set(c).at[B].set(c_ctx)
    mods = ada_mods(c8, ada_w[0], ada_b)
    sh1, sc1, g1 = (mods[:B, None, k * D:(k + 1) * D] for k in range(3))
    sh2, sc2, g2 = (mods[:B, None, k * D:(k + 1) * D] for k in range(3, 6))
    csh1 = jnp.broadcast_to(mods[B:B + 1, None, 0:D], (B, 1, D))
    csc1 = jnp.broadcast_to(mods[B:B + 1, None, D:2 * D], (B, 1, D))

    w_in_b = w_in[0].astype(BF16)
    w_rm = jnp.concatenate([w_in_b[:, :2 * C], w_in_b[:, 5 * C:]], axis=1)
    wg_f = gate_blocks(rg_wa_f[0], rg_wx_f[0], C // 256)
    wg_b = gate_blocks(rg_wa_b[0], rg_wx_b[0], C // 256)
    rg_f = (rg_conv_w[0], rg_conv_b, wg_f, rg_ba_f, rg_bx_f, rg_lam_f)
    rg_b = (rg_conv_w[0], rg_conv_b, wg_b, rg_ba_b, rg_bx_b, rg_lam_b)

    pc = norm_mod_proj(ctx, norm1_g, csh1, csc1, w_rm[:, :C], tile_l=LC, chunk=C)
    zero = jnp.zeros((B, 1, C), F32)
    _, hcf = rg_scan(pc, 0, *rg_f, zero, reverse=False, tile_l=256)
    _, hcb = rg_scan(pc, 0, *rg_b, zero, reverse=True, tile_l=256)

    p_rm = norm_mod_proj(x, norm1_g, sh1, sc1, w_rm, tile_l=512, chunk=1024)
    hy_taps = jnp.concatenate([hy_conv_w[0], hy_conv_b], axis=0)
    u_hy, z0_hy = hyena_proj(x, norm1_g, sh1, sc1, w_in_b[:, 2 * C:5 * C], hy_taps, tile_l=512)
    h_f, _ = rg_scan(p_rm, 0, *rg_f, hcf, reverse=False, tile_l=512)
    h_b, _ = rg_scan(p_rm, 0, *rg_b, hcb, reverse=True, tile_l=512)

    tables = dft_tables(L)
    assert hy_pos_w1.shape[1] == len(HY_FEATURE_ORDER)
    w1t = jnp.zeros((HY_HID, HY_HID), F32).at[:, :hy_pos_w1.shape[1]].set(hy_pos_w1[0].T[:, jnp.array(HY_FEATURE_ORDER)])
    kt = hyena_filter_t(w1t, hy_pos_b1[0][:, None], hy_pos_w2[0].T, hy_pos_b2[0][:, None], hy_freq[0][:, None],
                        hy_pos_w3[0].T.reshape(2, C, HY_HID).astype(BF16), L, 256)
    spec = hyena_spectrum(kt.reshape(2, C, L // LANES, LANES), tables, 32)
    y_hy_t = hyena_fftconv(u_hy, z0_hy, hy_skip[0][:, None, None], spec, tables, tile_c=64, group=8)

    n_exp = moe_we.shape[-1]
    n_grp = moe_wg.shape[-1]
    assert n_grp == N_GROUPS and n_exp == N_GROUPS * EXPERTS_PER_GROUP
    wr = jnp.zeros((D, ROUTE_LANES), F32).at[:, :n_grp].set(moe_wg[0]).at[:, n_grp:n_grp + n_exp].set(moe_we[0])
    br = jnp.zeros((1, ROUTE_LANES), F32).at[:, :n_grp].set(moe_bg).at[:, n_grp:n_grp + n_exp].set(moe_be)
    wr_hi = wr.astype(BF16)
    wr_split = jnp.concatenate([wr_hi, (wr - wr_hi.astype(F32)).astype(BF16)], axis=1)
    x1, hxp, route, cnt = mix_route(x, h_f, h_b, p_rm, y_hy_t, rg_proj[0].astype(BF16), hy_proj[0].astype(BF16),
                                    w_out[0].astype(BF16), g1, norm2_g, sh2, sc2, wr_split, br, tile_l=512,
                                    n_exp=n_exp)

    n_blocks = (2 * B * L + n_exp * (MOE_BLOCK - 1)) // MOE_BLOCK
    dest, blk = moe_dest(route, cnt, tile=2048, n_exp=n_exp, n_blocks=n_blocks)
    xb = moe_scatter(dest, hxp, n_blocks * MOE_BLOCK, tile=512)
    yb = moe_experts(blk[:n_blocks, 0], xb, moe_w1[0], moe_w3[0], moe_w2[0])
    return moe_combine(dest, x1, route, g2, final_g[None], yb, tile_l=512)
```

```python
import functools
import math

import jax
import jax.numpy as jnp
from jax import lax
from jax.experimental import pallas as pl
from jax.experimental.pallas import tpu as pltpu

F32 = jnp.float32
BF16 = jnp.bfloat16
I32 = jnp.int32
HIGHEST = lax.Precision.HIGHEST

LANES = 128
SUBLANES = 8
EPS = 1e-6
RG_C = 8.0
RG_HEAD_DIM = 64
GRID_W = 64
HY_SEQ_BANDS = 16
HY_COL_BANDS = 8
HY_DECAY_TARGET = 1e-2
HY_FAST_DECAY = 0.3
HY_SLOW_DECAY = 1.5
N_GROUPS = 4
EXPERTS_PER_GROUP = 8
MOE_BLOCK = 512
VMEM_LIMIT = 56 * 1024 * 1024


def _cparams(*sem):
    return pltpu.CompilerParams(dimension_semantics=sem, vmem_limit_bytes=VMEM_LIMIT)


def _sigmoid(x):
    return 0.5 * (jnp.tanh(0.5 * x) + 1.0)


def _gelu_tanh(x):
    c = math.sqrt(2.0 / math.pi)
    return 0.5 * x * (1.0 + jnp.tanh(c * (x + 0.044715 * (x * x * x))))


def _ada_body(c_ref, w_ref, b_ref, o_ref):
    c = c_ref[...]
    s = c * _sigmoid(c)
    o_ref[...] = jnp.dot(s, w_ref[...], precision=HIGHEST, preferred_element_type=F32) + b_ref[...]


def ada_mods(c8, ada_w, ada_b):
    d, m = ada_w.shape
    tn = 1024 if m % 1024 == 0 else m
    return pl.pallas_call(
        _ada_body,
        grid=(m // tn,),
        in_specs=[pl.BlockSpec((c8.shape[0], d), lambda j: (0, 0)),
                  pl.BlockSpec((d, tn), lambda j: (0, j)),
                  pl.BlockSpec((1, tn), lambda j: (0, j))],
        out_specs=pl.BlockSpec((c8.shape[0], tn), lambda j: (0, j)),
        out_shape=jax.ShapeDtypeStruct((c8.shape[0], m), F32),
        compiler_params=_cparams("parallel"),
        name="ada_mods",
    )(c8, ada_w, ada_b)


def _norm_mod(x_ref, g_ref, sh_ref, sc_ref):
    x = x_ref[0]
    ms = jnp.mean(x * x, axis=-1, keepdims=True)
    y = x * lax.rsqrt(ms + EPS) * g_ref[...]
    return (y * (1.0 + sc_ref[0]) + sh_ref[0]).astype(BF16)


def _proj_body(x_ref, g_ref, sh_ref, sc_ref, w_ref, o_ref, *, chunk):
    hx = _norm_mod(x_ref, g_ref, sh_ref, sc_ref)
    for j in range(w_ref.shape[1] // chunk):
        cols = slice(j * chunk, (j + 1) * chunk)
        o_ref[0, :, cols] = jnp.dot(hx, w_ref[:, cols], preferred_element_type=F32).astype(o_ref.dtype)


def norm_mod_proj(x, g, shift, scale, w, *, tile_l, chunk):
    b, l, d = x.shape
    m = w.shape[1]
    tl = min(tile_l, l)
    return pl.pallas_call(
        functools.partial(_proj_body, chunk=min(chunk, m)),
        grid=(b, l // tl),
        in_specs=[pl.BlockSpec((1, tl, d), lambda bi, i: (bi, i, 0)),
                  pl.BlockSpec((1, d), lambda bi, i: (0, 0)),
                  pl.BlockSpec((1, 1, d), lambda bi, i: (bi, 0, 0)),
                  pl.BlockSpec((1, 1, d), lambda bi, i: (bi, 0, 0)),
                  pl.BlockSpec(w.shape, lambda bi, i: (0, 0))],
        out_specs=pl.BlockSpec((1, tl, m), lambda bi, i: (bi, i, 0)),
        out_shape=jax.ShapeDtypeStruct((b, l, m), BF16),
        compiler_params=_cparams("parallel", "parallel"),
        name="norm_mod_proj",
    )(x, g, shift, scale, w)


HALO = 16


def _scan_body(pc_ref, pp_ref, pn_ref, cw_ref, cb_ref, wg_ref, ba_ref, bx_ref, lam_ref, h0_ref,
               h_ref, hl_ref, ext_s, xc_s, g_s, a_s, b_s, hloc_s, pcum_s, carry_s,
               *, reverse, n_tiles, t, c, s_len, pitch):
    i = pl.program_id(1)
    ti = (n_tiles - 1 - i) if reverse else i
    n_slab = c // LANES
    n_blk = wg_ref.shape[0]
    blk = c // n_blk

    @pl.when(i == 0)
    def _():
        carry_s[...] = h0_ref[0]

    cur = pc_ref[0].astype(F32)
    has_prev = (ti > 0).astype(F32)
    has_next = (ti < n_tiles - 1).astype(F32)
    ext_s[0:SUBLANES, :] = pp_ref[0, HALO - SUBLANES:HALO, :].astype(F32) * has_prev
    ext_s[SUBLANES:SUBLANES + t, :] = cur
    ext_s[SUBLANES + t:2 * SUBLANES + t, :] = pn_ref[0, 0:SUBLANES, :].astype(F32) * has_next
    cw = cw_ref[...]
    xc = cb_ref[...] + cw[2:3] * cur
    for tap, off in ((0, -2), (1, -1), (3, 1)):
        xc = xc + cw[tap:tap + 1] * ext_s[SUBLANES + off:SUBLANES + off + t, :]
    xc_s[...] = xc

    for k in range(n_blk):
        xb = xc_s[:, k * blk:(k + 1) * blk].astype(BF16)
        g_s[:, k * 2 * blk:(k + 1) * 2 * blk] = jnp.dot(xb, wg_ref[k], preferred_element_type=F32)

    lam = lam_ref[...]
    softplus_neg_lam = jnp.maximum(-lam, 0.0) + jnp.log1p(jnp.exp(-jnp.abs(lam)))
    half_ca = (-0.5 * RG_C) * softplus_neg_lam
    half_ba, half_bx = 0.5 * ba_ref[...], 0.5 * bx_ref[...]
    slabs_per_blk = blk // LANES
    for j in range(SUBLANES):
        r0 = j * s_len
        for k in range(n_slab):
            kb, ks = k // slabs_per_blk, k % slabs_per_blk
            ga = g_s[r0:r0 + s_len, kb * 2 * blk + ks * LANES:kb * 2 * blk + (ks + 1) * LANES]
            gx = g_s[r0:r0 + s_len, kb * 2 * blk + blk + ks * LANES:kb * 2 * blk + blk + (ks + 1) * LANES]
            lane = slice(k * LANES, (k + 1) * LANES)
            half_x = 0.5 * xc_s[r0:r0 + s_len, lane]
            hca = half_ca[:, lane]
            log_a = hca * jnp.tanh(ga + half_ba[:, lane]) + hca
            gated_x = half_x * jnp.tanh(gx + half_bx[:, lane]) + half_x
            a = jnp.exp(log_a)
            a_s[k, j * pitch:j * pitch + s_len, :] = a
            gain2 = -jnp.tanh(log_a) * (a * a + 1.0)
            gain = jnp.where(gain2 > 0.0, gain2 * lax.rsqrt(gain2), 0.0)
            b_s[k, j * pitch:j * pitch + s_len, :] = gain * gated_x

    def step1(s, hp):
        hs, ps = hp
        srow = (s_len - 1 - s) if reverse else s
        hs2, ps2 = [], []
        for k in range(n_slab):
            av = a_s[k, pl.ds(srow, SUBLANES, stride=pitch), :]
            bv = b_s[k, pl.ds(srow, SUBLANES, stride=pitch), :]
            h = av * hs[k] + bv
            p = av * ps[k]
            hloc_s[k, pl.ds(srow, SUBLANES, stride=pitch), :] = h
            pcum_s[k, pl.ds(srow, SUBLANES, stride=pitch), :] = p
            hs2.append(h)
            ps2.append(p)
        return tuple(hs2), tuple(ps2)

    zeros = tuple(jnp.zeros((SUBLANES, LANES), F32) for _ in range(n_slab))
    ones = tuple(jnp.ones((SUBLANES, LANES), F32) for _ in range(n_slab))
    h_end, p_end = lax.fori_loop(0, s_len, step1, (zeros, ones))

    order = range(SUBLANES - 1, -1, -1) if reverse else range(SUBLANES)
    for k in range(n_slab):
        cst = carry_s[:, k * LANES:(k + 1) * LANES]
        for j in order:
            rows = slice(j * pitch, j * pitch + s_len)
            h_ref[0, j * s_len:(j + 1) * s_len, k * LANES:(k + 1) * LANES] = (
                hloc_s[k, rows, :] + pcum_s[k, rows, :] * cst).astype(h_ref.dtype)
            cst = p_end[k][j:j + 1] * cst + h_end[k][j:j + 1]
        carry_s[:, k * LANES:(k + 1) * LANES] = cst
    hl_ref[0] = carry_s[...]


def rg_scan(p, col_blk, conv_w, conv_b, wg, ba, bx, lam, h0, *, reverse, tile_l):
    b, l, _ = p.shape
    c = conv_w.shape[1]
    t = min(tile_l, l)
    n_tiles = l // t
    s_len = t // SUBLANES
    pitch = s_len + SUBLANES
    hb = t // HALO
    n_hblk = l // HALO

    def nat(i):
        return (n_tiles - 1 - i) if reverse else i

    body = functools.partial(_scan_body, reverse=reverse, n_tiles=n_tiles, t=t, c=c, s_len=s_len, pitch=pitch)
    vec = pl.BlockSpec((1, c), lambda bi, i: (0, 0))
    return pl.pallas_call(
        body,
        grid=(b, n_tiles),
        in_specs=[pl.BlockSpec((1, t, c), lambda bi, i: (bi, nat(i), col_blk)),
                  pl.BlockSpec((1, HALO, c), lambda bi, i: (bi, jnp.maximum(nat(i) * hb - 1, 0), col_blk)),
                  pl.BlockSpec((1, HALO, c), lambda bi, i: (bi, jnp.minimum((nat(i) + 1) * hb, n_hblk - 1), col_blk)),
                  pl.BlockSpec(conv_w.shape, lambda bi, i: (0, 0)),
                  vec,
                  pl.BlockSpec(wg.shape, lambda bi, i: (0, 0, 0)),
                  vec, vec, vec,
                  pl.BlockSpec((1, 1, c), lambda bi, i: (bi, 0, 0))],
        out_specs=[pl.BlockSpec((1, t, c), lambda bi, i: (bi, nat(i), 0)),
                   pl.BlockSpec((1, 1, c), lambda bi, i: (bi, 0, 0))],
        out_shape=[jax.ShapeDtypeStruct((b, l, c), BF16), jax.ShapeDtypeStruct((b, 1, c), F32)],
        scratch_shapes=[pltpu.VMEM((t + 2 * SUBLANES, c), F32), pltpu.VMEM((t, c), F32), pltpu.VMEM((t, 2 * c), F32)]
        + [pltpu.VMEM((c // LANES, SUBLANES * pitch, LANES), F32) for _ in range(4)]
        + [pltpu.VMEM((1, c), F32)],
        compiler_params=_cparams("parallel", "arbitrary"),
        name="rg_scan_bwd" if reverse else "rg_scan_fwd",
    )(p, p, p, conv_w, conv_b, wg, ba, bx, lam, h0)


def gate_blocks(wa, wx, n_blk):
    h, d, _ = wa.shape
    hp = h // n_blk
    eye = jnp.eye(hp, dtype=wa.dtype)

    def bd(w):
        w = w.reshape(n_blk, hp, d, d)
        return jnp.einsum('khde,hg->khdge', w, eye).reshape(n_blk, hp * d, hp * d)

    return (0.5 * jnp.concatenate([bd(wa), bd(wx)], axis=-1)).astype(BF16)


HY_HID = 64
HY_FEATURE_ORDER = (list(range(1, 1 + 2 * HY_SEQ_BANDS))
                    + list(range(2 + 2 * HY_SEQ_BANDS, 2 + 2 * HY_SEQ_BANDS + 2 * HY_COL_BANDS))
                    + [0, 1 + 2 * HY_SEQ_BANDS])


def _filter_body(w1t_ref, b1_ref, w2t_ref, b2_ref, fr_ref, w3t_ref, o_ref, z_s, *, l, c, ct, rows_grid):
    d = pl.program_id(0)
    j = pl.program_id(1)
    lane = lax.broadcasted_iota(I32, (1, l), 1)
    s_i = jnp.where(d == 0, lane, l - lane)
    sf = s_i.astype(F32)
    t_norm = sf / float(max(l - 1, 1))

    @pl.when(j == 0)
    def _():
        band_step = (HY_SEQ_BANDS - 1 - 1e-4) / (HY_SEQ_BANDS - 1)
        seq_band = 1e-4 + band_step * lax.broadcasted_iota(I32, (HY_SEQ_BANDS, 1), 0).astype(F32)
        col_band = 1.0 + lax.broadcasted_iota(I32, (HY_COL_BANDS, 1), 0).astype(F32)
        col_pos = (s_i & (GRID_W - 1)).astype(F32)
        row_lag = (s_i >> int(math.log2(GRID_W))).astype(F32) / float(rows_grid)
        ang_seq = ((2.0 * math.pi / l) * sf) * seq_band
        ang_col = ((2.0 * math.pi / GRID_W) * col_pos) * col_band
        n_trig = 2 * HY_SEQ_BANDS + 2 * HY_COL_BANDS
        trow = lax.broadcasted_iota(I32, (HY_HID - n_trig, 1), 0)
        tail = jnp.where(trow == 0, t_norm, jnp.where(trow == 1, row_lag, 0.0))
        feats = jnp.concatenate([jnp.cos(ang_seq), jnp.sin(ang_seq), jnp.cos(ang_col), jnp.sin(ang_col), tail], axis=0)
        fr = fr_ref[...]
        z = jnp.sin(fr * (jnp.dot(w1t_ref[...], feats, precision=HIGHEST, preferred_element_type=F32) + b1_ref[...]))
        z_s[...] = jnp.sin(fr * (jnp.dot(w2t_ref[...], z, precision=HIGHEST, preferred_element_type=F32) + b2_ref[...]))

    k = jnp.dot(w3t_ref[0], z_s[...].astype(BF16), preferred_element_type=F32)
    ch = (lax.broadcasted_iota(I32, (ct, 1), 0) + j * ct).astype(F32)
    max_decay = math.log(HY_DECAY_TARGET) / HY_FAST_DECAY
    min_decay = math.log(HY_DECAY_TARGET) / HY_SLOW_DECAY
    delta = jnp.abs(min_decay + ch * ((max_decay - min_decay) / (c - 1)))
    k = k * jnp.exp(-t_norm * delta)
    k = jnp.where((d == 1) & (lane == 0), 0.0, k)
    o_ref[0] = k.astype(o_ref.dtype)


def hyena_filter_t(w1t, b1, w2t, b2, fr, w3t, l, tile_c):
    assert GRID_W & (GRID_W - 1) == 0
    c = w3t.shape[1]
    ct = min(tile_c, c)
    body = functools.partial(_filter_body, l=l, c=c, ct=ct, rows_grid=l // GRID_W)
    small = lambda shape: pl.BlockSpec(shape, lambda d, j: (0,) * len(shape))
    return pl.pallas_call(
        body,
        grid=(2, c // ct),
        in_specs=[small(w1t.shape), small(b1.shape), small(w2t.shape), small(b2.shape), small(fr.shape),
                  pl.BlockSpec((1, ct, HY_HID), lambda d, j: (d, j, 0))],
        out_specs=pl.BlockSpec((1, ct, l), lambda d, j: (d, j, 0)),
        out_shape=jax.ShapeDtypeStruct((2, c, l), BF16),
        scratch_shapes=[pltpu.VMEM((HY_HID, l), F32)],
        compiler_params=_cparams("arbitrary", "arbitrary"),
        name="hyena_filter",
    )(w1t, b1, w2t, b2, fr, w3t)


def dft_tables(l):
    import numpy as np
    n = 2 * l
    r_in, nk = l // LANES, n // LANES
    ka = np.arange(nk)[:, None].astype(np.float64)
    r = np.arange(r_in)[None, :].astype(np.float64)
    a1 = 2.0 * np.pi * ka * r / nk
    f1 = np.concatenate([np.cos(a1), -np.sin(a1)], axis=0)
    lane = np.arange(LANES)[None, :].astype(np.float64)
    at = 2.0 * np.pi * ka * lane / n
    twr, twi = np.cos(at), -np.sin(at)
    a2 = 2.0 * np.pi * np.arange(LANES)[:, None] * np.arange(LANES)[None, :] / LANES
    cr, ci = np.cos(a2), -np.sin(a2)
    m2 = np.block([[cr, ci], [-ci, cr]])
    m2i = np.block([[cr, -ci], [ci, cr]])
    ai = 2.0 * np.pi * np.arange(r_in)[:, None] * np.arange(nk)[None, :] / nk
    gi = np.concatenate([np.cos(ai), -np.sin(ai)], axis=1) / n
    as_bf = lambda a: jnp.asarray(a, F32).astype(BF16)
    return as_bf(f1), jnp.asarray(twr, F32), jnp.asarray(twi, F32), as_bf(m2), as_bf(m2i), as_bf(gi)


def _fwd_rows_twiddle(x_a, x_b, f1, twr, twi, nk):
    a = jnp.dot(f1, jnp.concatenate([x_a, x_b], axis=1), preferred_element_type=F32)
    out = []
    for h in range(2):
        re, im = a[:nk, h * LANES:(h + 1) * LANES], a[nk:, h * LANES:(h + 1) * LANES]
        out.append((re * twr - im * twi, re * twi + im * twr))
    return out


def _spectrum_body(k_ref, f1_ref, twr_ref, twi_ref, m2_ref, o_ref, *, g, nk, r_in):
    f1, twr, twi = f1_ref[...], twr_ref[...], twi_ref[...]
    sign = jnp.where((lax.broadcasted_iota(I32, (nk, 1), 0) & 1) == 0, 1.0, -1.0)
    a2 = []
    for ci in range(g):
        (fre, fim), (bre, bim) = _fwd_rows_twiddle(k_ref[0, ci], k_ref[1, ci], f1, twr, twi, nk)
        a2.append(jnp.concatenate([fre + sign * bre, fim + sign * bim], axis=1).astype(BF16))
    spec = jnp.dot(jnp.concatenate(a2, axis=0), m2_ref[...], preferred_element_type=F32)
    o_ref[...] = spec.reshape(g, nk, 2 * LANES).astype(o_ref.dtype)


def hyena_spectrum(kt4, tables, group):
    _, c, r_in, _ = kt4.shape
    nk = 2 * r_in
    f1, twr, twi, m2, _, _ = tables
    g = min(group, c)
    full = lambda a: pl.BlockSpec(a.shape, lambda j: (0,) * a.ndim)
    return pl.pallas_call(
        functools.partial(_spectrum_body, g=g, nk=nk, r_in=r_in),
        grid=(c // g,),
        in_specs=[pl.BlockSpec((2, g, r_in, LANES), lambda j: (0, j, 0, 0)), full(f1), full(twr), full(twi), full(m2)],
        out_specs=pl.BlockSpec((g, nk, 2 * LANES), lambda j: (j, 0, 0)),
        out_shape=jax.ShapeDtypeStruct((c, nk, 2 * LANES), BF16),
        compiler_params=_cparams("parallel"),
        name="hyena_spectrum",
    )(kt4, f1, twr, twi, m2)


HY_CHUNK = 256


def _first_rows_body(x_ref, g_ref, sh_ref, sc_ref, w_ref, o_ref):
    nt, rows, d = x_ref.shape[1:]
    x = x_ref[0].reshape(nt * rows, d)
    ms = jnp.mean(x * x, axis=-1, keepdims=True)
    y = x * lax.rsqrt(ms + EPS) * g_ref[...]
    hx = (y * (1.0 + sc_ref[0]) + sh_ref[0]).astype(BF16)
    o_ref[0] = jnp.dot(hx, w_ref[...], preferred_element_type=F32).reshape(nt, rows, w_ref.shape[1])


def hyena_first_rows(x, g, shift, scale, w, *, tile_l):
    b, l, d = x.shape
    t = min(tile_l, l)
    nt = l // t
    m = w.shape[1]
    return pl.pallas_call(
        _first_rows_body,
        grid=(b,),
        in_specs=[pl.BlockSpec((1, nt, SUBLANES, d), lambda bi: (bi, 0, 0, 0)),
                  pl.BlockSpec((1, d), lambda bi: (0, 0)),
                  pl.BlockSpec((1, 1, d), lambda bi: (bi, 0, 0)),
                  pl.BlockSpec((1, 1, d), lambda bi: (bi, 0, 0)),
                  pl.BlockSpec(w.shape, lambda bi: (0, 0))],
        out_specs=pl.BlockSpec((1, nt, SUBLANES, m), lambda bi: (bi, 0, 0, 0)),
        out_shape=jax.ShapeDtypeStruct((b, nt, SUBLANES, m), F32),
        compiler_params=_cparams("parallel"),
        name="hyena_first_rows",
    )(x.reshape(b, nt, t, d), g, shift, scale, w)


def _hyena_proj_body(x_ref, g_ref, sh_ref, sc_ref, w_ref, taps_ref, nxt_ref, u_ref, z0_ref, last_s, *, n_tiles, t, c):
    i = pl.program_id(1)

    @pl.when(i == 0)
    def _():
        last_s[...] = jnp.zeros_like(last_s)

    hx = _norm_mod(x_ref, g_ref, sh_ref, sc_ref)
    row = lax.broadcasted_iota(I32, (t, 1), 0)
    has_next = (i < n_tiles - 1).astype(F32)
    cw = min(HY_CHUNK, c)
    for j in range(c // cw):
        zs = []
        for k in range(3):
            cols = slice(k * c + j * cw, k * c + (j + 1) * cw)
            p = jnp.dot(hx, w_ref[:, cols], preferred_element_type=F32)
            up = jnp.where(row == 0, last_s[:, cols], pltpu.roll(p, 1, 0))
            dn = jnp.where(row == t - 1, nxt_ref[0, 0, 0:1, cols] * has_next, pltpu.roll(p, t - 1, 0))
            tp = taps_ref[:, cols]
            zs.append(tp[3:4] + tp[0:1] * up + tp[1:2] * p + tp[2:3] * dn)
            last_s[:, cols] = p[t - 1:t, :]
        z0, z1, zv = zs
        u_t, z0_t = (zv * z1).T, z0.T
        for q in range(t // LANES):
            u_ref[0, q, j * cw:(j + 1) * cw, :] = u_t[:, q * LANES:(q + 1) * LANES].astype(u_ref.dtype)
            z0_ref[0, q, j * cw:(j + 1) * cw, :] = z0_t[:, q * LANES:(q + 1) * LANES].astype(z0_ref.dtype)


def hyena_proj(x, g, shift, scale, w, taps, *, tile_l):
    b, l, d = x.shape
    c = w.shape[1] // 3
    t = min(tile_l, l)
    n_tiles = l // t
    rq = t // LANES
    nxt = hyena_first_rows(x, g, shift, scale, w, tile_l=tile_l)
    o_spec = pl.BlockSpec((1, rq, c, LANES), lambda bi, i: (bi, i, 0, 0))
    o_shape = jax.ShapeDtypeStruct((b, l // LANES, c, LANES), BF16)
    return pl.pallas_call(
        functools.partial(_hyena_proj_body, n_tiles=n_tiles, t=t, c=c),
        grid=(b, n_tiles),
        in_specs=[pl.BlockSpec((1, t, d), lambda bi, i: (bi, i, 0)),
                  pl.BlockSpec((1, d), lambda bi, i: (0, 0)),
                  pl.BlockSpec((1, 1, d), lambda bi, i: (bi, 0, 0)),
                  pl.BlockSpec((1, 1, d), lambda bi, i: (bi, 0, 0)),
                  pl.BlockSpec(w.shape, lambda bi, i: (0, 0)),
                  pl.BlockSpec(taps.shape, lambda bi, i: (0, 0)),
                  pl.BlockSpec((1, 1, SUBLANES, 3 * c), lambda bi, i: (bi, jnp.minimum(i + 1, n_tiles - 1), 0, 0))],
        out_specs=[o_spec, o_spec],
        out_shape=[o_shape, o_shape],
        scratch_shapes=[pltpu.VMEM((1, 3 * c), F32)],
        compiler_params=_cparams("parallel", "arbitrary"),
        name="hyena_proj",
    )(x, g, shift, scale, w, taps, nxt)


def _fftconv_body(u_ref, z0_ref, skip_ref, k_ref, f1_ref, twr_ref, twi_ref, m2_ref, m2i_ref, gi_ref,
                  o_ref, u_s, z0_s, y_s, *, ct, g, nk, r_in, pitch):
    for r in range(r_in):
        u_s[r * pitch:r * pitch + ct, :] = u_ref[0, r].astype(F32)
        z0_s[r * pitch:r * pitch + ct, :] = z0_ref[0, r].astype(F32)
    f1, twr, twi = f1_ref[...], twr_ref[...], twi_ref[...]

    def chan(ref, ch):
        return ref[pl.ds(ch, r_in, stride=pitch), :]

    def rows_fwd(c0):
        a2 = []
        for ci in range(0, g, 2):
            pair = _fwd_rows_twiddle(chan(u_s, c0 + ci).astype(BF16), chan(u_s, c0 + ci + 1).astype(BF16),
                                     f1, twr, twi, nk)
            a2 += [jnp.concatenate([tre, tim], axis=1).astype(BF16) for tre, tim in pair]
        return jnp.concatenate(a2, axis=0)

    def lanes_fwd(a2):
        return jnp.dot(a2, m2_ref[...], preferred_element_type=F32)

    def times_filter(c0, spec):
        kf = k_ref[pl.ds(c0, g)].astype(F32).reshape(g * nk, 2 * LANES)
        sre, sim = spec[:, :LANES], spec[:, LANES:]
        kre, kim = kf[:, :LANES], kf[:, LANES:]
        return jnp.concatenate([sre * kre - sim * kim, sre * kim + sim * kre], axis=1).astype(BF16)

    def lanes_inv(prod):
        return jnp.dot(prod, m2i_ref[...], preferred_element_type=F32)

    def rows_inv(c0, cc):
        for ci in range(0, g, 2):
            st = []
            for h in range(2):
                blk = cc[(ci + h) * nk:(ci + h + 1) * nk]
                cre, cim = blk[:, :LANES], blk[:, LANES:]
                st.append(jnp.concatenate([cre * twr + cim * twi, cim * twr - cre * twi], axis=0).astype(BF16))
            y2 = jnp.dot(gi_ref[...], jnp.concatenate(st, axis=1), preferred_element_type=F32)
            for h in range(2):
                ch = c0 + ci + h
                y = y2[:, h * LANES:(h + 1) * LANES]
                y_s[pl.ds(ch, r_in, stride=pitch), :] = (y + chan(u_s, ch) * skip_ref[ch]) * chan(z0_s, ch)

    def two_groups(i, _):
        ca, cb = 2 * g * i, 2 * g * i + g
        a2_a = rows_fwd(ca)
        spec_a = lanes_fwd(a2_a)
        a2_b = rows_fwd(cb)
        prod_a = times_filter(ca, spec_a)
        spec_b = lanes_fwd(a2_b)
        cc_a = lanes_inv(prod_a)
        prod_b = times_filter(cb, spec_b)
        cc_b = lanes_inv(prod_b)
        rows_inv(ca, cc_a)
        rows_inv(cb, cc_b)
        return 0

    lax.fori_loop(0, ct // (2 * g), two_groups, 0)
    for r in range(r_in):
        o_ref[0, r] = y_s[r * pitch:r * pitch + ct, :].astype(o_ref.dtype)


def hyena_fftconv(u, z0, skip3, spec, tables, *, tile_c, group):
    b, r_in, c, _ = u.shape
    nk = 2 * r_in
    ct = min(tile_c, c)
    g = min(group, ct // 2)
    assert ct % (2 * g) == 0 and g % 2 == 0
    pitch = ct + SUBLANES
    f1, twr, twi, m2, m2i, gi = tables
    full = lambda a: pl.BlockSpec(a.shape, lambda j, bi: (0,) * a.ndim)
    io_spec = pl.BlockSpec((1, r_in, ct, LANES), lambda j, bi: (bi, 0, j, 0))
    return pl.pallas_call(
        functools.partial(_fftconv_body, ct=ct, g=g, nk=nk, r_in=r_in, pitch=pitch),
        grid=(c // ct, b),
        in_specs=[io_spec, io_spec,
                  pl.BlockSpec((ct, 1, 1), lambda j, bi: (j, 0, 0)),
                  pl.BlockSpec((ct, nk, 2 * LANES), lambda j, bi: (j, 0, 0)),
                  full(f1), full(twr), full(twi), full(m2), full(m2i), full(gi)],
        out_specs=io_spec,
        out_shape=jax.ShapeDtypeStruct((b, r_in, c, LANES), BF16),
        scratch_shapes=[pltpu.VMEM((r_in * pitch, LANES), F32) for _ in range(3)],
        compiler_params=_cparams("parallel", "arbitrary"),
        name="hyena_fftconv",
    )(u, z0, skip3, spec, f1, twr, twi, m2, m2i, gi)


ROUTE_LANES = LANES
NEG_BIG = -1e30
HALF_WORD = 16


def _pack_bf16_pairs(v):
    h = v.shape[1] // 2
    bits = pltpu.bitcast(v.astype(BF16).astype(F32), I32)
    return bits[:, :h] | lax.shift_right_logical(bits[:, h:], HALF_WORD)


def _unpack_bf16_pairs(w):
    hi = pltpu.bitcast(w & jnp.int32(-65536), F32)
    lo = pltpu.bitcast(lax.shift_left(w, HALF_WORD), F32)
    return jnp.concatenate([hi, lo], axis=1)


SLAB = 4


def _store_row_slabs(ref, words, row0=0):
    rows = words.shape[0]
    for j in range(SLAB):
        ref[pl.ds(SLAB * row0 + j, rows, stride=SLAB), :] = words[:, j * LANES:(j + 1) * LANES]


def _load_row_slabs(ref, rows=None):
    rows = ref.shape[0] // SLAB if rows is None else rows
    return jnp.concatenate([ref[pl.ds(j, rows, stride=SLAB), :] for j in range(SLAB)], axis=1)


MIX_SUB = 512


def _mix_route_body(x_ref, hf_ref, hb_ref, prg_ref, pga_ref, pgb_ref, yt_ref, rgp_ref, hyp_ref, wo_ref, g1_ref,
                    n2g_ref, sh2_ref, sc2_ref, wr_ref, br_ref, tri_ref,
                    x1_ref, hxp_ref, route_ref, cnt_ref, carry_s, *, t, sub, n_exp):
    @pl.when((pl.program_id(0) == 0) & (pl.program_id(1) == 0))
    def _():
        carry_s[...] = jnp.zeros_like(carry_s)

    running = carry_s[...]
    for r0 in range(0, t, sub):
        rows = slice(r0, r0 + sub)
        hsum = hf_ref[0, rows, :].astype(F32) + hb_ref[0, rows, :].astype(F32)
        y_rg = (hsum * _gelu_tanh(prg_ref[0, rows, :].astype(F32))).astype(BF16)
        t1 = jnp.dot(y_rg, rgp_ref[...], preferred_element_type=F32)
        t2 = jnp.concatenate([lax.dot_general(yt_ref[0, q], hyp_ref[...], (((0,), (0,)), ((), ())),
                                              preferred_element_type=F32)
                              for q in range(r0 // LANES, (r0 + sub) // LANES)], axis=0)
        merged = _sigmoid(pga_ref[0, rows, :].astype(F32)) * t1 + _sigmoid(pgb_ref[0, rows, :].astype(F32)) * t2
        out = jnp.dot(merged.astype(BF16), wo_ref[...], preferred_element_type=F32)
        x1 = x_ref[0, rows, :] + g1_ref[0] * out
        x1_ref[0, rows, :] = x1
        ms = jnp.mean(x1 * x1, axis=-1, keepdims=True)
        hx2 = (x1 * lax.rsqrt(ms + EPS) * n2g_ref[...]) * (1.0 + sc2_ref[0]) + sh2_ref[0]
        _store_row_slabs(hxp_ref, _pack_bf16_pairs(hx2), r0)

        hx_hi = hx2.astype(BF16)
        hx_lo = (hx2 - hx_hi.astype(F32)).astype(BF16)
        parts = (jnp.dot(hx_hi, wr_ref[...], preferred_element_type=F32)
                 + jnp.dot(hx_lo, wr_ref[...], preferred_element_type=F32))
        logits = parts[:, :ROUTE_LANES] + parts[:, ROUTE_LANES:] + br_ref[...]
        lane = lax.broadcasted_iota(I32, (sub, ROUTE_LANES), 1)
        is_g = lane < N_GROUPS
        glog = jnp.where(is_g, logits, NEG_BIG)
        gmax = jnp.max(glog, axis=1, keepdims=True)
        gidx = jnp.min(jnp.where(glog == gmax, lane, ROUTE_LANES), axis=1, keepdims=True)
        gsum = jnp.sum(jnp.where(is_g, jnp.exp(glog - gmax), 0.0), axis=1, keepdims=True)
        p_g = 1.0 / gsum
        e_lane = lane - N_GROUPS
        grp_of_lane = lax.shift_right_arithmetic(e_lane, int(math.log2(EXPERTS_PER_GROUP)))
        in_grp = (e_lane >= 0) & (e_lane < n_exp) & (grp_of_lane == gidx)
        elog = jnp.where(in_grp, logits, NEG_BIG)
        m1 = jnp.max(elog, axis=1, keepdims=True)
        i1 = jnp.min(jnp.where(elog == m1, lane, ROUTE_LANES), axis=1, keepdims=True)
        elog2 = jnp.where(lane == i1, NEG_BIG, elog)
        m2 = jnp.max(elog2, axis=1, keepdims=True)
        i2 = jnp.min(jnp.where(elog2 == m2, lane, ROUTE_LANES), axis=1, keepdims=True)
        e21 = jnp.exp(m2 - m1)
        pk1 = 1.0 / (1.0 + e21)
        wt1, wt2 = p_g * pk1, p_g * (e21 * pk1)

        oh1 = (lane == i1 - N_GROUPS).astype(F32)
        oh2 = (lane == i2 - N_GROUPS).astype(F32)
        cnt = oh1 + oh2
        before = jnp.dot(tri_ref[...], cnt.astype(BF16), preferred_element_type=F32) + running
        rank1 = jnp.sum(oh1 * before, axis=1, keepdims=True)
        rank2 = jnp.sum(oh2 * before, axis=1, keepdims=True)
        running = running + jnp.sum(cnt, axis=0, keepdims=True)
        vals = ((i1 - N_GROUPS).astype(F32), (i2 - N_GROUPS).astype(F32), rank1, rank2, wt1, wt2)
        route = jnp.zeros((sub, ROUTE_LANES), F32)
        for k, v in enumerate(vals):
            route = jnp.where(lane == k, v, route)
        route_ref[rows, :] = route
    carry_s[...] = running
    cnt_ref[...] = running


def mix_route(x, h_f, h_b, p_rm, y_hy_t, rg_proj, hy_proj, w_out, g1, n2g, sh2, sc2, wr, br, *, tile_l, n_exp):
    b, l, d = x.shape
    c = h_f.shape[2]
    t = min(tile_l, l)
    nt = l // t
    n = b * l
    sub = min(MIX_SUB, t)
    tri = (jnp.arange(sub)[:, None] > jnp.arange(sub)[None, :]).astype(BF16)
    tok = lambda bi, i: (bi, i, 0)
    col = lambda k: (lambda bi, i: (bi, i, k))
    full2 = lambda a: pl.BlockSpec(a.shape, lambda bi, i: (0, 0))
    per_b = pl.BlockSpec((1, 1, d), lambda bi, i: (bi, 0, 0))
    row = lambda bi, i: (bi * nt + i, 0)
    return pl.pallas_call(
        functools.partial(_mix_route_body, t=t, sub=sub, n_exp=n_exp),
        grid=(b, nt),
        in_specs=[pl.BlockSpec((1, t, d), tok), pl.BlockSpec((1, t, c), tok), pl.BlockSpec((1, t, c), tok),
                  pl.BlockSpec((1, t, c), col(1)), pl.BlockSpec((1, t, c), col(2)), pl.BlockSpec((1, t, c), col(3)),
                  pl.BlockSpec((1, t // LANES, c, LANES), lambda bi, i: (bi, i, 0, 0)),
                  full2(rg_proj), full2(hy_proj), full2(w_out), per_b,
                  full2(n2g), per_b, per_b, full2(wr), full2(br), full2(tri)],
        out_specs=[pl.BlockSpec((1, t, d), tok),
                   pl.BlockSpec((t * SLAB, LANES), row),
                   pl.BlockSpec((t, ROUTE_LANES), row), pl.BlockSpec((1, ROUTE_LANES), lambda bi, i: (0, 0))],
        out_shape=[jax.ShapeDtypeStruct((b, l, d), F32), jax.ShapeDtypeStruct((n * SLAB, LANES), I32),
                   jax.ShapeDtypeStruct((n, ROUTE_LANES), F32), jax.ShapeDtypeStruct((1, ROUTE_LANES), F32)],
        scratch_shapes=[pltpu.VMEM((1, ROUTE_LANES), F32)],
        compiler_params=_cparams("arbitrary", "arbitrary"),
        name="mix_route",
    )(x, h_f, h_b, p_rm, p_rm, p_rm, y_hy_t, rg_proj, hy_proj, w_out, g1, n2g, sh2, sc2, wr, br, tri)


def _dest_body(route_ref, cnt_ref, ut_ref, dest_ref, blk_ref, *, t, n_exp, nb_pad):
    lane1 = lax.broadcasted_iota(I32, (1, ROUTE_LANES), 1)
    padded = jnp.floor((cnt_ref[...] + (MOE_BLOCK - 1.0)) * (1.0 / MOE_BLOCK)) * MOE_BLOCK
    padded = jnp.where(lane1 < n_exp, padded, 0.0)
    pend = jnp.dot(jnp.broadcast_to(padded, (SUBLANES, ROUTE_LANES)), ut_ref[...], precision=HIGHEST,
                   preferred_element_type=F32)[0:1]
    pstart = pend - padded
    route = route_ref[...]
    lane = lax.broadcasted_iota(I32, (t, ROUTE_LANES), 1)
    lf = lane.astype(F32)
    d1 = jnp.sum(jnp.where(lf == route[:, 0:1], pstart, 0.0), axis=1, keepdims=True) + route[:, 2:3]
    d2 = jnp.sum(jnp.where(lf == route[:, 1:2], pstart, 0.0), axis=1, keepdims=True) + route[:, 3:4]
    dmat = jnp.where(lane == 0, d1, jnp.where(lane == 1, d2, 0.0))
    dest_ref[...] = dmat.T[0:SUBLANES].astype(I32)
    first_row = lax.broadcasted_iota(I32, (nb_pad, ROUTE_LANES), 0).astype(F32) * float(MOE_BLOCK)
    lane_b = lax.broadcasted_iota(I32, (nb_pad, ROUTE_LANES), 1)
    nle = jnp.sum(jnp.where((lane_b < n_exp) & (pend <= first_row), 1.0, 0.0), axis=1, keepdims=True)
    e_blk = jnp.minimum(nle, n_exp - 1.0)
    mine = lane_b.astype(F32) == e_blk
    cnt_e = jnp.sum(jnp.where(mine, cnt_ref[...], 0.0), axis=1, keepdims=True)
    start_e = jnp.sum(jnp.where(mine, pstart, 0.0), axis=1, keepdims=True)
    valid = jnp.clip(cnt_e - (first_row - start_e), 0.0, float(MOE_BLOCK))
    blk_ref[...] = jnp.where(lane_b == 0, e_blk, jnp.where(lane_b == 1, valid, 0.0)).astype(I32)


def moe_dest(route, cnt, *, tile, n_exp, n_blocks):
    n = route.shape[0]
    t = min(tile, n)
    nb_pad = -(-n_blocks // SUBLANES) * SUBLANES
    ut = (jnp.arange(ROUTE_LANES)[:, None] <= jnp.arange(ROUTE_LANES)[None, :]).astype(F32)
    return pl.pallas_call(
        functools.partial(_dest_body, t=t, n_exp=n_exp, nb_pad=nb_pad),
        grid=(n // t,),
        in_specs=[pl.BlockSpec((t, ROUTE_LANES), lambda i: (i, 0)),
                  pl.BlockSpec((1, ROUTE_LANES), lambda i: (0, 0)),
                  pl.BlockSpec((ROUTE_LANES, ROUTE_LANES), lambda i: (0, 0))],
        out_specs=[pl.BlockSpec((SUBLANES, t), lambda i: (i, 0)),
                   pl.BlockSpec((nb_pad, ROUTE_LANES), lambda i: (0, 0))],
        out_shape=[jax.ShapeDtypeStruct((n // t * SUBLANES, t), I32),
                   jax.ShapeDtypeStruct((nb_pad, ROUTE_LANES), I32)],
        compiler_params=_cparams("arbitrary"),
        name="moe_dest",
    )(route, cnt, ut)


def _scatter_body(dest_ref, hx_ref, xb_in_ref, xb_ref, sem, *, t):
    del xb_in_ref

    def issue(r, _):
        for k in range(2):
            pltpu.make_async_copy(hx_ref.at[pl.ds(SLAB * r, SLAB)], xb_ref.at[pl.ds(SLAB * dest_ref[k, r], SLAB)],
                                  sem).start(priority=k)
        return 0

    lax.fori_loop(0, t, issue, 0, unroll=8)
    for k in range(2):
        pltpu.make_async_copy(hx_ref, xb_ref.at[pl.ds(0, SLAB * t)], sem).wait()


def moe_scatter(dest, hxp, n_rows, *, tile):
    n = hxp.shape[0] // SLAB
    t = min(tile, n)
    xb0 = jnp.zeros((n_rows * SLAB, LANES), I32)
    per_dest_tile = dest.shape[1] // t
    return pl.pallas_call(
        functools.partial(_scatter_body, t=t),
        grid=(n // t,),
        in_specs=[pl.BlockSpec((SUBLANES, t), lambda i: (i // per_dest_tile, i % per_dest_tile),
                               memory_space=pltpu.SMEM),
                  pl.BlockSpec((t * SLAB, LANES), lambda i: (i, 0)),
                  pl.BlockSpec(memory_space=pl.ANY)],
        out_specs=pl.BlockSpec(memory_space=pl.ANY),
        out_shape=jax.ShapeDtypeStruct((n_rows * SLAB, LANES), I32),
        scratch_shapes=[pltpu.SemaphoreType.DMA],
        input_output_aliases={2: 0},
        compiler_params=_cparams("arbitrary"),
        name="moe_scatter",
    )(dest, hxp, xb0)


def _expert_body(blk_ref, valid_ref, xb_ref, w1_ref, w3_ref, w2_ref, yb_ref, w1_s, w3_s, w2_s):
    i = pl.program_id(0)
    valid = valid_ref[i]
    half = MOE_BLOCK // 2
    changed = (i == 0) | (blk_ref[i] != blk_ref[jnp.maximum(i - 1, 0)])

    @pl.when(changed & (valid > 0))
    def _():
        w1_s[...] = w1_ref[0].astype(BF16)
        w3_s[...] = w3_ref[0].astype(BF16)
        w2_s[...] = w2_ref[0].astype(BF16)

    def run(rows):
        xblk = _unpack_bf16_pairs(_load_row_slabs(xb_ref, rows)).astype(BF16)
        h1 = jnp.dot(xblk, w1_s[...], preferred_element_type=F32)
        h3 = jnp.dot(xblk, w3_s[...], preferred_element_type=F32)
        hid = (h1 * _sigmoid(h1) * h3).astype(BF16)
        _store_row_slabs(yb_ref, _pack_bf16_pairs(jnp.dot(hid, w2_s[...], preferred_element_type=F32)))

    @pl.when(valid > half)
    def _():
        run(MOE_BLOCK)

    @pl.when(valid <= half)
    def _():
        yb_ref[...] = jnp.zeros_like(yb_ref)

    @pl.when((valid > 0) & (valid <= half))
    def _():
        run(half)


def moe_experts(blk_exp, blk_valid, xb, w1, w3, w2):
    p = xb.shape[0] // SLAB
    _, d, de = w1.shape
    nb = p // MOE_BLOCK
    grid_spec = pltpu.PrefetchScalarGridSpec(
        num_scalar_prefetch=2,
        grid=(nb,),
        in_specs=[pl.BlockSpec((MOE_BLOCK * SLAB, LANES), lambda i, blk, valid: (i, 0)),
                  pl.BlockSpec((1, d, de), lambda i, blk, valid: (blk[i], 0, 0)),
                  pl.BlockSpec((1, d, de), lambda i, blk, valid: (blk[i], 0, 0)),
                  pl.BlockSpec((1, de, d), lambda i, blk, valid: (blk[i], 0, 0))],
        out_specs=pl.BlockSpec((MOE_BLOCK * SLAB, LANES), lambda i, blk, valid: (i, 0)),
        scratch_shapes=[pltpu.VMEM((d, de), BF16), pltpu.VMEM((d, de), BF16), pltpu.VMEM((de, d), BF16)],
    )
    return pl.pallas_call(
        _expert_body,
        grid_spec=grid_spec,
        out_shape=jax.ShapeDtypeStruct((p * SLAB, LANES), I32),
        compiler_params=_cparams("arbitrary"),
        name="moe_experts",
    )(blk_exp, blk_valid, xb, w1, w3, w2)


def _combine_body(dest_ref, x1_ref, route_ref, g2_ref, fg_ref, yb_ref, o_ref, y1_s, y2_s, sem, *, t):
    def issue(r, _):
        pltpu.make_async_copy(yb_ref.at[pl.ds(SLAB * dest_ref[0, r], SLAB)], y1_s.at[pl.ds(SLAB * r, SLAB)],
                              sem).start(priority=0)
        pltpu.make_async_copy(yb_ref.at[pl.ds(SLAB * dest_ref[1, r], SLAB)], y2_s.at[pl.ds(SLAB * r, SLAB)],
                              sem).start(priority=1)
        return 0

    lax.fori_loop(0, t, issue, 0, unroll=8)
    for y_s in (y1_s, y2_s):
        pltpu.make_async_copy(yb_ref.at[pl.ds(0, SLAB * t)], y_s, sem).wait()
    route = route_ref[...]
    moe = (route[:, 4:5] * _unpack_bf16_pairs(_load_row_slabs(y1_s))
           + route[:, 5:6] * _unpack_bf16_pairs(_load_row_slabs(y2_s)))
    x2 = x1_ref[0] + g2_ref[0] * moe
    ms = jnp.mean(x2 * x2, axis=-1, keepdims=True)
    o_ref[0] = x2 * lax.rsqrt(ms + EPS) * fg_ref[...]


def moe_combine(dest, x1, route, g2, final_g, yb, *, tile_l):
    b, l, d = x1.shape
    t = min(tile_l, l)
    nt = l // t
    slab = (t * SLAB, LANES)
    per_dest_tile = dest.shape[1] // t
    return pl.pallas_call(
        functools.partial(_combine_body, t=t),
        grid=(b, nt),
        in_specs=[pl.BlockSpec((SUBLANES, t),
                               lambda bi, i: ((bi * nt + i) // per_dest_tile, (bi * nt + i) % per_dest_tile),
                               memory_space=pltpu.SMEM),
                  pl.BlockSpec((1, t, d), lambda bi, i: (bi, i, 0)),
                  pl.BlockSpec((t, ROUTE_LANES), lambda bi, i: (bi * nt + i, 0)),
                  pl.BlockSpec((1, 1, d), lambda bi, i: (bi, 0, 0)),
                  pl.BlockSpec((1, d), lambda bi, i: (0, 0)),
                  pl.BlockSpec(memory_space=pl.ANY)],
        out_specs=pl.BlockSpec((1, t, d), lambda bi, i: (bi, i, 0)),
        out_shape=jax.ShapeDtypeStruct((b, l, d), F32),
        scratch_shapes=[pltpu.VMEM(slab, I32), pltpu.VMEM(slab, I32), pltpu.SemaphoreType.DMA],
        compiler_params=_cparams("arbitrary", "arbitrary"),
        name="moe_combine",
    )(dest, x1, route, g2, final_g, yb)


def kernel(x, c, ctx, c_ctx, ada_w, ada_b, norm1_g, norm2_g, final_g, w_in, rg_conv_w, rg_conv_b, rg_wa_f, rg_ba_f, rg_wx_f, rg_bx_f, rg_lam_f, rg_wa_b, rg_ba_b, rg_wx_b, rg_bx_b, rg_lam_b, rg_proj, hy_conv_w, hy_conv_b, hy_pos_w1, hy_pos_b1, hy_pos_w2, hy_pos_b2, hy_freq, hy_pos_w3, hy_skip, hy_proj, w_out, moe_wg, moe_bg, moe_we, moe_be, moe_w1, moe_w3, moe_w2):
    B, L, D = x.shape
    C = rg_conv_w.shape[-1]
    LC = ctx.shape[1]
    c8 = jnp.zeros((8, D), F32).at[:B].set(c).at[B].set(c_ctx)
    mods = ada_mods(c8, ada_w[0], ada_b)
    sh1, sc1, g1 = (mods[:B, None, k * D:(k + 1) * D] for k in range(3))
    sh2, sc2, g2 = (mods[:B, None, k * D:(k + 1) * D] for k in range(3, 6))
    csh1 = jnp.broadcast_to(mods[B:B + 1, None, 0:D], (B, 1, D))
    csc1 = jnp.broadcast_to(mods[B:B + 1, None, D:2 * D], (B, 1, D))

    w_in_b = w_in[0].astype(BF16)
    w_rm = jnp.concatenate([w_in_b[:, :2 * C], w_in_b[:, 5 * C:]], axis=1)
    wg_f = gate_blocks(rg_wa_f[0], rg_wx_f[0], C // 256)
    wg_b = gate_blocks(rg_wa_b[0], rg_wx_b[0], C // 256)
    rg_f = (rg_conv_w[0], rg_conv_b, wg_f, rg_ba_f, rg_bx_f, rg_lam_f)
    rg_b = (rg_conv_w[0], rg_conv_b, wg_b, rg_ba_b, rg_bx_b, rg_lam_b)

    pc = norm_mod_proj(ctx, norm1_g, csh1, csc1, w_rm[:, :C], tile_l=LC, chunk=C)
    zero = jnp.zeros((B, 1, C), F32)
    _, hcf = rg_scan(pc, 0, *rg_f, zero, reverse=False, tile_l=256)
    _, hcb = rg_scan(pc, 0, *rg_b, zero, reverse=True, tile_l=256)

    p_rm = norm_mod_proj(x, norm1_g, sh1, sc1, w_rm, tile_l=512, chunk=1024)
    hy_taps = jnp.concatenate([hy_conv_w[0], hy_conv_b], axis=0)
    u_hy, z0_hy = hyena_proj(x, norm1_g, sh1, sc1, w_in_b[:, 2 * C:5 * C], hy_taps, tile_l=512)
    h_f, _ = rg_scan(p_rm, 0, *rg_f, hcf, reverse=False, tile_l=512)
    h_b, _ = rg_scan(p_rm, 0, *rg_b, hcb, reverse=True, tile_l=512)

    tables = dft_tables(L)
    assert hy_pos_w1.shape[1] == len(HY_FEATURE_ORDER)
    w1t = jnp.zeros((HY_HID, HY_HID), F32).at[:, :hy_pos_w1.shape[1]].set(hy_pos_w1[0].T[:, jnp.array(HY_FEATURE_ORDER)])
    kt = hyena_filter_t(w1t, hy_pos_b1[0][:, None], hy_pos_w2[0].T, hy_pos_b2[0][:, None], hy_freq[0][:, None],
                        hy_pos_w3[0].T.reshape(2, C, HY_HID).astype(BF16), L, 256)
    spec = hyena_spectrum(kt.reshape(2, C, L // LANES, LANES), tables, 32)
    y_hy_t = hyena_fftconv(u_hy, z0_hy, hy_skip[0][:, None, None], spec, tables, tile_c=64, group=8)

    n_exp = moe_we.shape[-1]
    n_grp = moe_wg.shape[-1]
    assert n_grp == N_GROUPS and n_exp == N_GROUPS * EXPERTS_PER_GROUP
    wr = jnp.zeros((D, ROUTE_LANES), F32).at[:, :n_grp].set(moe_wg[0]).at[:, n_grp:n_grp + n_exp].set(moe_we[0])
    br = jnp.zeros((1, ROUTE_LANES), F32).at[:, :n_grp].set(moe_bg).at[:, n_grp:n_grp + n_exp].set(moe_be)
    wr_hi = wr.astype(BF16)
    wr_split = jnp.concatenate([wr_hi, (wr - wr_hi.astype(F32)).astype(BF16)], axis=1)
    x1, hxp, route, cnt = mix_route(x, h_f, h_b, p_rm, y_hy_t, rg_proj[0].astype(BF16), hy_proj[0].astype(BF16),
                                    w_out[0].astype(BF16), g1, norm2_g, sh2, sc2, wr_split, br, tile_l=512,
                                    n_exp=n_exp)

    n_blocks = (2 * B * L + n_exp * (MOE_BLOCK - 1)) // MOE_BLOCK
    dest, blk = moe_dest(route, cnt, tile=2048, n_exp=n_exp, n_blocks=n_blocks)
    xb = moe_scatter(dest, hxp, n_blocks * MOE_BLOCK, tile=512)
    yb = moe_experts(blk[:n_blocks, 0], blk[:n_blocks, 1], xb, moe_w1[0], moe_w3[0], moe_w2[0])
    return moe_combine(dest, x1, route, g2, final_g[None], yb, tile_l=512)
```

```python
import functools
import math

import jax
import jax.numpy as jnp
from jax import lax
from jax.experimental import pallas as pl
from jax.experimental.pallas import tpu as pltpu

F32 = jnp.float32
BF16 = jnp.bfloat16
I32 = jnp.int32
HIGHEST = lax.Precision.HIGHEST

LANES = 128
SUBLANES = 8
EPS = 1e-6
RG_C = 8.0
RG_HEAD_DIM = 64
GRID_W = 64
HY_SEQ_BANDS = 16
HY_COL_BANDS = 8
HY_DECAY_TARGET = 1e-2
HY_FAST_DECAY = 0.3
HY_SLOW_DECAY = 1.5
N_GROUPS = 4
EXPERTS_PER_GROUP = 8
MOE_BLOCK = 512
VMEM_LIMIT = 56 * 1024 * 1024

TILE_PROJ = 512
PROJ_CHUNK = 1024
TILE_SCAN = 512
TILE_MIX = 512
TILE_DEST = 2048
TILE_DISPATCH = 1024
FILTER_TILE_C = 256
SPECTRUM_GROUP = 32
FFT_TILE_C = 64
FFT_GROUP = 8


def _cparams(*sem):
    return pltpu.CompilerParams(dimension_semantics=sem, vmem_limit_bytes=VMEM_LIMIT)


def _sigmoid(x):
    return 0.5 * (jnp.tanh(0.5 * x) + 1.0)


def _gelu_tanh(x):
    c = math.sqrt(2.0 / math.pi)
    return 0.5 * x * (1.0 + jnp.tanh(c * (x + 0.044715 * (x * x * x))))


def _ada_body(c_ref, w_ref, b_ref, o_ref):
    c = c_ref[...]
    s = c * _sigmoid(c)
    o_ref[...] = jnp.dot(s, w_ref[...], precision=HIGHEST, preferred_element_type=F32) + b_ref[...]


def ada_mods(c8, ada_w, ada_b):
    d, m = ada_w.shape
    tn = 1024 if m % 1024 == 0 else m
    return pl.pallas_call(
        _ada_body,
        grid=(m // tn,),
        in_specs=[pl.BlockSpec((c8.shape[0], d), lambda j: (0, 0)),
                  pl.BlockSpec((d, tn), lambda j: (0, j)),
                  pl.BlockSpec((1, tn), lambda j: (0, j))],
        out_specs=pl.BlockSpec((c8.shape[0], tn), lambda j: (0, j)),
        out_shape=jax.ShapeDtypeStruct((c8.shape[0], m), F32),
        compiler_params=_cparams("parallel"),
        name="ada_mods",
    )(c8, ada_w, ada_b)


def _norm_mod(x_ref, g_ref, sh_ref, sc_ref):
    x = x_ref[0]
    ms = jnp.mean(x * x, axis=-1, keepdims=True)
    y = x * lax.rsqrt(ms + EPS) * g_ref[...]
    return (y * (1.0 + sc_ref[0]) + sh_ref[0]).astype(BF16)


def _proj_body(x_ref, g_ref, sh_ref, sc_ref, w_ref, o_ref, *, chunk):
    hx = _norm_mod(x_ref, g_ref, sh_ref, sc_ref)
    for j in range(w_ref.shape[1] // chunk):
        cols = slice(j * chunk, (j + 1) * chunk)
        o_ref[0, :, cols] = jnp.dot(hx, w_ref[:, cols], preferred_element_type=F32).astype(o_ref.dtype)


def norm_mod_proj(x, g, shift, scale, w, *, tile_l, chunk):
    b, l, d = x.shape
    m = w.shape[1]
    tl = min(tile_l, l)
    return pl.pallas_call(
        functools.partial(_proj_body, chunk=min(chunk, m)),
        grid=(b, l // tl),
        in_specs=[pl.BlockSpec((1, tl, d), lambda bi, i: (bi, i, 0)),
                  pl.BlockSpec((1, d), lambda bi, i: (0, 0)),
                  pl.BlockSpec((1, 1, d), lambda bi, i: (bi, 0, 0)),
                  pl.BlockSpec((1, 1, d), lambda bi, i: (bi, 0, 0)),
                  pl.BlockSpec(w.shape, lambda bi, i: (0, 0))],
        out_specs=pl.BlockSpec((1, tl, m), lambda bi, i: (bi, i, 0)),
        out_shape=jax.ShapeDtypeStruct((b, l, m), BF16),
        compiler_params=_cparams("parallel", "parallel"),
        name="norm_mod_proj",
    )(x, g, shift, scale, w)


HALO = 16


def _scan_body(pc_ref, pp_ref, pn_ref, cw_ref, cb_ref, wg_ref, ba_ref, bx_ref, lam_ref, h0_ref,
               h_ref, hl_ref, ext_s, xc_s, g_s, a_s, b_s, hloc_s, pcum_s, carry_s,
               *, reverse, n_tiles, t, c, s_len, pitch):
    i = pl.program_id(1)
    ti = (n_tiles - 1 - i) if reverse else i
    n_slab = c // LANES
    n_blk = wg_ref.shape[0]
    blk = c // n_blk

    @pl.when(i == 0)
    def _():
        carry_s[...] = h0_ref[0]

    cur = pc_ref[0].astype(F32)
    has_prev = (ti > 0).astype(F32)
    has_next = (ti < n_tiles - 1).astype(F32)
    ext_s[0:SUBLANES, :] = pp_ref[0, HALO - SUBLANES:HALO, :].astype(F32) * has_prev
    ext_s[SUBLANES:SUBLANES + t, :] = cur
    ext_s[SUBLANES + t:2 * SUBLANES + t, :] = pn_ref[0, 0:SUBLANES, :].astype(F32) * has_next
    cw = cw_ref[...]
    xc = cb_ref[...] + cw[2:3] * cur
    for tap, off in ((0, -2), (1, -1), (3, 1)):
        xc = xc + cw[tap:tap + 1] * ext_s[SUBLANES + off:SUBLANES + off + t, :]
    xc_s[...] = xc

    for k in range(n_blk):
        xb = xc_s[:, k * blk:(k + 1) * blk].astype(BF16)
        g_s[:, k * 2 * blk:(k + 1) * 2 * blk] = jnp.dot(xb, wg_ref[k], preferred_element_type=F32)

    lam = lam_ref[...]
    softplus_neg_lam = jnp.maximum(-lam, 0.0) + jnp.log1p(jnp.exp(-jnp.abs(lam)))
    half_ca = (-0.5 * RG_C) * softplus_neg_lam
    half_ba, half_bx = 0.5 * ba_ref[...], 0.5 * bx_ref[...]
    slabs_per_blk = blk // LANES
    for j in range(SUBLANES):
        r0 = j * s_len
        for k in range(n_slab):
            kb, ks = k // slabs_per_blk, k % slabs_per_blk
            ga = g_s[r0:r0 + s_len, kb * 2 * blk + ks * LANES:kb * 2 * blk + (ks + 1) * LANES]
            gx = g_s[r0:r0 + s_len, kb * 2 * blk + blk + ks * LANES:kb * 2 * blk + blk + (ks + 1) * LANES]
            lane = slice(k * LANES, (k + 1) * LANES)
            half_x = 0.5 * xc_s[r0:r0 + s_len, lane]
            hca = half_ca[:, lane]
            log_a = hca * jnp.tanh(ga + half_ba[:, lane]) + hca
            gated_x = half_x * jnp.tanh(gx + half_bx[:, lane]) + half_x
            a = jnp.exp(log_a)
            a_s[k, j * pitch:j * pitch + s_len, :] = a
            gain2 = -jnp.tanh(log_a) * (a * a + 1.0)
            gain = jnp.where(gain2 > 0.0, gain2 * lax.rsqrt(gain2), 0.0)
            b_s[k, j * pitch:j * pitch + s_len, :] = gain * gated_x

    def step1(s, hp):
        hs, ps = hp
        srow = (s_len - 1 - s) if reverse else s
        hs2, ps2 = [], []
        for k in range(n_slab):
            av = a_s[k, pl.ds(srow, SUBLANES, stride=pitch), :]
            bv = b_s[k, pl.ds(srow, SUBLANES, stride=pitch), :]
            h = av * hs[k] + bv
            p = av * ps[k]
            hloc_s[k, pl.ds(srow, SUBLANES, stride=pitch), :] = h
            pcum_s[k, pl.ds(srow, SUBLANES, stride=pitch), :] = p
            hs2.append(h)
            ps2.append(p)
        return tuple(hs2), tuple(ps2)

    zeros = tuple(jnp.zeros((SUBLANES, LANES), F32) for _ in range(n_slab))
    ones = tuple(jnp.ones((SUBLANES, LANES), F32) for _ in range(n_slab))
    h_end, p_end = lax.fori_loop(0, s_len, step1, (zeros, ones))

    order = range(SUBLANES - 1, -1, -1) if reverse else range(SUBLANES)
    for k in range(n_slab):
        cst = carry_s[:, k * LANES:(k + 1) * LANES]
        for j in order:
            rows = slice(j * pitch, j * pitch + s_len)
            h_ref[0, j * s_len:(j + 1) * s_len, k * LANES:(k + 1) * LANES] = (
                hloc_s[k, rows, :] + pcum_s[k, rows, :] * cst).astype(h_ref.dtype)
            cst = p_end[k][j:j + 1] * cst + h_end[k][j:j + 1]
        carry_s[:, k * LANES:(k + 1) * LANES] = cst
    hl_ref[0] = carry_s[...]


def rg_scan(p, col_blk, conv_w, conv_b, wg, ba, bx, lam, h0, *, reverse, tile_l):
    b, l, _ = p.shape
    c = conv_w.shape[1]
    t = min(tile_l, l)
    n_tiles = l // t
    s_len = t // SUBLANES
    pitch = s_len + SUBLANES
    hb = t // HALO
    n_hblk = l // HALO

    def nat(i):
        return (n_tiles - 1 - i) if reverse else i

    body = functools.partial(_scan_body, reverse=reverse, n_tiles=n_tiles, t=t, c=c, s_len=s_len, pitch=pitch)
    vec = pl.BlockSpec((1, c), lambda bi, i: (0, 0))
    return pl.pallas_call(
        body,
        grid=(b, n_tiles),
        in_specs=[pl.BlockSpec((1, t, c), lambda bi, i: (bi, nat(i), col_blk)),
                  pl.BlockSpec((1, HALO, c), lambda bi, i: (bi, jnp.maximum(nat(i) * hb - 1, 0), col_blk)),
                  pl.BlockSpec((1, HALO, c), lambda bi, i: (bi, jnp.minimum((nat(i) + 1) * hb, n_hblk - 1), col_blk)),
                  pl.BlockSpec(conv_w.shape, lambda bi, i: (0, 0)),
                  vec,
                  pl.BlockSpec(wg.shape, lambda bi, i: (0, 0, 0)),
                  vec, vec, vec,
                  pl.BlockSpec((1, 1, c), lambda bi, i: (bi, 0, 0))],
        out_specs=[pl.BlockSpec((1, t, c), lambda bi, i: (bi, nat(i), 0)),
                   pl.BlockSpec((1, 1, c), lambda bi, i: (bi, 0, 0))],
        out_shape=[jax.ShapeDtypeStruct((b, l, c), BF16), jax.ShapeDtypeStruct((b, 1, c), F32)],
        scratch_shapes=[pltpu.VMEM((t + 2 * SUBLANES, c), F32), pltpu.VMEM((t, c), F32), pltpu.VMEM((t, 2 * c), F32)]
        + [pltpu.VMEM((c // LANES, SUBLANES * pitch, LANES), F32) for _ in range(4)]
        + [pltpu.VMEM((1, c), F32)],
        compiler_params=_cparams("parallel", "arbitrary"),
        name="rg_scan_bwd" if reverse else "rg_scan_fwd",
    )(p, p, p, conv_w, conv_b, wg, ba, bx, lam, h0)


def gate_blocks(wa, wx, n_blk):
    h, d, _ = wa.shape
    hp = h // n_blk
    eye = jnp.eye(hp, dtype=wa.dtype)

    def bd(w):
        w = w.reshape(n_blk, hp, d, d)
        return jnp.einsum('khde,hg->khdge', w, eye).reshape(n_blk, hp * d, hp * d)

    return (0.5 * jnp.concatenate([bd(wa), bd(wx)], axis=-1)).astype(BF16)


HY_HID = 64
HY_FEATURE_ORDER = (list(range(1, 1 + 2 * HY_SEQ_BANDS))
                    + list(range(2 + 2 * HY_SEQ_BANDS, 2 + 2 * HY_SEQ_BANDS + 2 * HY_COL_BANDS))
                    + [0, 1 + 2 * HY_SEQ_BANDS])


def _filter_body(w1t_ref, b1_ref, w2t_ref, b2_ref, fr_ref, w3t_ref, o_ref, z_s, *, l, c, ct, rows_grid):
    d = pl.program_id(0)
    j = pl.program_id(1)
    lane = lax.broadcasted_iota(I32, (1, l), 1)
    s_i = jnp.where(d == 0, lane, l - lane)
    sf = s_i.astype(F32)
    t_norm = sf / float(max(l - 1, 1))

    @pl.when(j == 0)
    def _():
        band_step = (HY_SEQ_BANDS - 1 - 1e-4) / (HY_SEQ_BANDS - 1)
        seq_band = 1e-4 + band_step * lax.broadcasted_iota(I32, (HY_SEQ_BANDS, 1), 0).astype(F32)
        col_band = 1.0 + lax.broadcasted_iota(I32, (HY_COL_BANDS, 1), 0).astype(F32)
        col_pos = (s_i & (GRID_W - 1)).astype(F32)
        row_lag = (s_i >> int(math.log2(GRID_W))).astype(F32) / float(rows_grid)
        ang_seq = ((2.0 * math.pi / l) * sf) * seq_band
        ang_col = ((2.0 * math.pi / GRID_W) * col_pos) * col_band
        n_trig = 2 * HY_SEQ_BANDS + 2 * HY_COL_BANDS
        trow = lax.broadcasted_iota(I32, (HY_HID - n_trig, 1), 0)
        tail = jnp.where(trow == 0, t_norm, jnp.where(trow == 1, row_lag, 0.0))
        feats = jnp.concatenate([jnp.cos(ang_seq), jnp.sin(ang_seq), jnp.cos(ang_col), jnp.sin(ang_col), tail], axis=0)
        fr = fr_ref[...]
        z = jnp.sin(fr * (jnp.dot(w1t_ref[...], feats, precision=HIGHEST, preferred_element_type=F32) + b1_ref[...]))
        z_s[...] = jnp.sin(fr * (jnp.dot(w2t_ref[...], z, precision=HIGHEST, preferred_element_type=F32) + b2_ref[...]))

    k = jnp.dot(w3t_ref[0], z_s[...].astype(BF16), preferred_element_type=F32)
    ch = (lax.broadcasted_iota(I32, (ct, 1), 0) + j * ct).astype(F32)
    max_decay = math.log(HY_DECAY_TARGET) / HY_FAST_DECAY
    min_decay = math.log(HY_DECAY_TARGET) / HY_SLOW_DECAY
    delta = jnp.abs(min_decay + ch * ((max_decay - min_decay) / (c - 1)))
    k = k * jnp.exp(-t_norm * delta)
    k = jnp.where((d == 1) & (lane == 0), 0.0, k)
    o_ref[0] = k.astype(o_ref.dtype)


def hyena_filter_t(w1t, b1, w2t, b2, fr, w3t, l, tile_c):
    assert GRID_W & (GRID_W - 1) == 0
    c = w3t.shape[1]
    ct = min(tile_c, c)
    body = functools.partial(_filter_body, l=l, c=c, ct=ct, rows_grid=l // GRID_W)
    small = lambda shape: pl.BlockSpec(shape, lambda d, j: (0,) * len(shape))
    return pl.pallas_call(
        body,
        grid=(2, c // ct),
        in_specs=[small(w1t.shape), small(b1.shape), small(w2t.shape), small(b2.shape), small(fr.shape),
                  pl.BlockSpec((1, ct, HY_HID), lambda d, j: (d, j, 0))],
        out_specs=pl.BlockSpec((1, ct, l), lambda d, j: (d, j, 0)),
        out_shape=jax.ShapeDtypeStruct((2, c, l), BF16),
        scratch_shapes=[pltpu.VMEM((HY_HID, l), F32)],
        compiler_params=_cparams("arbitrary", "arbitrary"),
        name="hyena_filter",
    )(w1t, b1, w2t, b2, fr, w3t)


def dft_tables(l):
    import numpy as np
    n = 2 * l
    r_in, nk = l // LANES, n // LANES
    ka = np.arange(nk)[:, None].astype(np.float64)
    r = np.arange(r_in)[None, :].astype(np.float64)
    a1 = 2.0 * np.pi * ka * r / nk
    f1 = np.concatenate([np.cos(a1), -np.sin(a1)], axis=0)
    lane = np.arange(LANES)[None, :].astype(np.float64)
    at = 2.0 * np.pi * ka * lane / n
    twr, twi = np.cos(at), -np.sin(at)
    a2 = 2.0 * np.pi * np.arange(LANES)[:, None] * np.arange(LANES)[None, :] / LANES
    cr, ci = np.cos(a2), -np.sin(a2)
    m2 = np.block([[cr, ci], [-ci, cr]])
    m2i = np.block([[cr, -ci], [ci, cr]])
    ai = 2.0 * np.pi * np.arange(r_in)[:, None] * np.arange(nk)[None, :] / nk
    gi = np.concatenate([np.cos(ai), -np.sin(ai)], axis=1) / n
    as_bf = lambda a: jnp.asarray(a, F32).astype(BF16)
    return as_bf(f1), jnp.asarray(twr, F32), jnp.asarray(twi, F32), as_bf(m2), as_bf(m2i), as_bf(gi)


def _fwd_rows_twiddle(x_a, x_b, f1, twr, twi, nk):
    a = jnp.dot(f1, jnp.concatenate([x_a, x_b], axis=1), preferred_element_type=F32)
    out = []
    for h in range(2):
        re, im = a[:nk, h * LANES:(h + 1) * LANES], a[nk:, h * LANES:(h + 1) * LANES]
        out.append((re * twr - im * twi, re * twi + im * twr))
    return out


def _spectrum_body(k_ref, f1_ref, twr_ref, twi_ref, m2_ref, o_ref, *, g, nk, r_in):
    f1, twr, twi = f1_ref[...], twr_ref[...], twi_ref[...]
    sign = jnp.where((lax.broadcasted_iota(I32, (nk, 1), 0) & 1) == 0, 1.0, -1.0)
    a2 = []
    for ci in range(g):
        (fre, fim), (bre, bim) = _fwd_rows_twiddle(k_ref[0, ci], k_ref[1, ci], f1, twr, twi, nk)
        a2.append(jnp.concatenate([fre + sign * bre, fim + sign * bim], axis=1).astype(BF16))
    spec = jnp.dot(jnp.concatenate(a2, axis=0), m2_ref[...], preferred_element_type=F32)
    o_ref[...] = spec.reshape(g, nk, 2 * LANES).astype(o_ref.dtype)


def hyena_spectrum(kt4, tables, group):
    _, c, r_in, _ = kt4.shape
    nk = 2 * r_in
    f1, twr, twi, m2, _, _ = tables
    g = min(group, c)
    full = lambda a: pl.BlockSpec(a.shape, lambda j: (0,) * a.ndim)
    return pl.pallas_call(
        functools.partial(_spectrum_body, g=g, nk=nk, r_in=r_in),
        grid=(c // g,),
        in_specs=[pl.BlockSpec((2, g, r_in, LANES), lambda j: (0, j, 0, 0)), full(f1), full(twr), full(twi), full(m2)],
        out_specs=pl.BlockSpec((g, nk, 2 * LANES), lambda j: (j, 0, 0)),
        out_shape=jax.ShapeDtypeStruct((c, nk, 2 * LANES), BF16),
        compiler_params=_cparams("parallel"),
        name="hyena_spectrum",
    )(kt4, f1, twr, twi, m2)


HY_CHUNK = 256


def _first_rows_body(x_ref, g_ref, sh_ref, sc_ref, w_ref, o_ref):
    nt, rows, d = x_ref.shape[1:]
    x = x_ref[0].reshape(nt * rows, d)
    ms = jnp.mean(x * x, axis=-1, keepdims=True)
    y = x * lax.rsqrt(ms + EPS) * g_ref[...]
    hx = (y * (1.0 + sc_ref[0]) + sh_ref[0]).astype(BF16)
    o_ref[0] = jnp.dot(hx, w_ref[...], preferred_element_type=F32).reshape(nt, rows, w_ref.shape[1])


def hyena_first_rows(x, g, shift, scale, w, *, tile_l):
    b, l, d = x.shape
    t = min(tile_l, l)
    nt = l // t
    m = w.shape[1]
    return pl.pallas_call(
        _first_rows_body,
        grid=(b,),
        in_specs=[pl.BlockSpec((1, nt, SUBLANES, d), lambda bi: (bi, 0, 0, 0)),
                  pl.BlockSpec((1, d), lambda bi: (0, 0)),
                  pl.BlockSpec((1, 1, d), lambda bi: (bi, 0, 0)),
                  pl.BlockSpec((1, 1, d), lambda bi: (bi, 0, 0)),
                  pl.BlockSpec(w.shape, lambda bi: (0, 0))],
        out_specs=pl.BlockSpec((1, nt, SUBLANES, m), lambda bi: (bi, 0, 0, 0)),
        out_shape=jax.ShapeDtypeStruct((b, nt, SUBLANES, m), F32),
        compiler_params=_cparams("parallel"),
        name="hyena_first_rows",
    )(x.reshape(b, nt, t, d), g, shift, scale, w)


def _hyena_proj_body(x_ref, g_ref, sh_ref, sc_ref, w_ref, taps_ref, nxt_ref, u_ref, z0_ref, last_s, *, n_tiles, t, c):
    i = pl.program_id(1)

    @pl.when(i == 0)
    def _():
        last_s[...] = jnp.zeros_like(last_s)

    hx = _norm_mod(x_ref, g_ref, sh_ref, sc_ref)
    row = lax.broadcasted_iota(I32, (t, 1), 0)
    has_next = (i < n_tiles - 1).astype(F32)
    cw = min(HY_CHUNK, c)
    for j in range(c // cw):
        zs = []
        for k in range(3):
            cols = slice(k * c + j * cw, k * c + (j + 1) * cw)
            p = jnp.dot(hx, w_ref[:, cols], preferred_element_type=F32)
            up = jnp.where(row == 0, last_s[:, cols], pltpu.roll(p, 1, 0))
            dn = jnp.where(row == t - 1, nxt_ref[0, 0, 0:1, cols] * has_next, pltpu.roll(p, t - 1, 0))
            tp = taps_ref[:, cols]
            zs.append(tp[3:4] + tp[0:1] * up + tp[1:2] * p + tp[2:3] * dn)
            last_s[:, cols] = p[t - 1:t, :]
        z0, z1, zv = zs
        u_t, z0_t = (zv * z1).T, z0.T
        for q in range(t // LANES):
            u_ref[0, q, j * cw:(j + 1) * cw, :] = u_t[:, q * LANES:(q + 1) * LANES].astype(u_ref.dtype)
            z0_ref[0, q, j * cw:(j + 1) * cw, :] = z0_t[:, q * LANES:(q + 1) * LANES].astype(z0_ref.dtype)


def hyena_proj(x, g, shift, scale, w, taps, *, tile_l):
    b, l, d = x.shape
    c = w.shape[1] // 3
    t = min(tile_l, l)
    n_tiles = l // t
    rq = t // LANES
    nxt = hyena_first_rows(x, g, shift, scale, w, tile_l=tile_l)
    o_spec = pl.BlockSpec((1, rq, c, LANES), lambda bi, i: (bi, i, 0, 0))
    o_shape = jax.ShapeDtypeStruct((b, l // LANES, c, LANES), BF16)
    return pl.pallas_call(
        functools.partial(_hyena_proj_body, n_tiles=n_tiles, t=t, c=c),
        grid=(b, n_tiles),
        in_specs=[pl.BlockSpec((1, t, d), lambda bi, i: (bi, i, 0)),
                  pl.BlockSpec((1, d), lambda bi, i: (0, 0)),
                  pl.BlockSpec((1, 1, d), lambda bi, i: (bi, 0, 0)),
                  pl.BlockSpec((1, 1, d), lambda bi, i: (bi, 0, 0)),
                  pl.BlockSpec(w.shape, lambda bi, i: (0, 0)),
                  pl.BlockSpec(taps.shape, lambda bi, i: (0, 0)),
                  pl.BlockSpec((1, 1, SUBLANES, 3 * c), lambda bi, i: (bi, jnp.minimum(i + 1, n_tiles - 1), 0, 0))],
        out_specs=[o_spec, o_spec],
        out_shape=[o_shape, o_shape],
        scratch_shapes=[pltpu.VMEM((1, 3 * c), F32)],
        compiler_params=_cparams("parallel", "arbitrary"),
        name="hyena_proj",
    )(x, g, shift, scale, w, taps, nxt)


def _fftconv_body(u_ref, z0_ref, skip_ref, k_ref, f1_ref, twr_ref, twi_ref, m2_ref, m2i_ref, gi_ref,
                  o_ref, u_s, z0_s, y_s, *, ct, g, nk, r_in, pitch):
    for r in range(r_in):
        u_s[r * pitch:r * pitch + ct, :] = u_ref[0, r].astype(F32)
        z0_s[r * pitch:r * pitch + ct, :] = z0_ref[0, r].astype(F32)
    f1, twr, twi = f1_ref[...], twr_ref[...], twi_ref[...]

    def chan(ref, ch):
        return ref[pl.ds(ch, r_in, stride=pitch), :]

    def rows_fwd(c0):
        a2 = []
        for ci in range(0, g, 2):
            pair = _fwd_rows_twiddle(chan(u_s, c0 + ci).astype(BF16), chan(u_s, c0 + ci + 1).astype(BF16),
                                     f1, twr, twi, nk)
            a2 += [jnp.concatenate([tre, tim], axis=1).astype(BF16) for tre, tim in pair]
        return jnp.concatenate(a2, axis=0)

    def lanes_fwd(a2):
        return jnp.dot(a2, m2_ref[...], preferred_element_type=F32)

    def times_filter(c0, spec):
        kf = k_ref[pl.ds(c0, g)].astype(F32).reshape(g * nk, 2 * LANES)
        sre, sim = spec[:, :LANES], spec[:, LANES:]
        kre, kim = kf[:, :LANES], kf[:, LANES:]
        return jnp.concatenate([sre * kre - sim * kim, sre * kim + sim * kre], axis=1).astype(BF16)

    def lanes_inv(prod):
        return jnp.dot(prod, m2i_ref[...], preferred_element_type=F32)

    def rows_inv(c0, cc):
        for ci in range(0, g, 2):
            st = []
            for h in range(2):
                blk = cc[(ci + h) * nk:(ci + h + 1) * nk]
                cre, cim = blk[:, :LANES], blk[:, LANES:]
                st.append(jnp.concatenate([cre * twr + cim * twi, cim * twr - cre * twi], axis=0).astype(BF16))
            y2 = jnp.dot(gi_ref[...], jnp.concatenate(st, axis=1), preferred_element_type=F32)
            for h in range(2):
                ch = c0 + ci + h
                y = y2[:, h * LANES:(h + 1) * LANES]
                y_s[pl.ds(ch, r_in, stride=pitch), :] = (y + chan(u_s, ch) * skip_ref[ch]) * chan(z0_s, ch)

    def two_groups(i, _):
        ca, cb = 2 * g * i, 2 * g * i + g
        a2_a = rows_fwd(ca)
        spec_a = lanes_fwd(a2_a)
        a2_b = rows_fwd(cb)
        prod_a = times_filter(ca, spec_a)
        spec_b = lanes_fwd(a2_b)
        cc_a = lanes_inv(prod_a)
        prod_b = times_filter(cb, spec_b)
        cc_b = lanes_inv(prod_b)
        rows_inv(ca, cc_a)
        rows_inv(cb, cc_b)
        return 0

    lax.fori_loop(0, ct // (2 * g), two_groups, 0)
    for r in range(r_in):
        o_ref[0, r] = y_s[r * pitch:r * pitch + ct, :].astype(o_ref.dtype)


def hyena_fftconv(u, z0, skip3, spec, tables, *, tile_c, group):
    b, r_in, c, _ = u.shape
    nk = 2 * r_in
    ct = min(tile_c, c)
    g = min(group, ct // 2)
    assert ct % (2 * g) == 0 and g % 2 == 0
    pitch = ct + SUBLANES
    f1, twr, twi, m2, m2i, gi = tables
    full = lambda a: pl.BlockSpec(a.shape, lambda j, bi: (0,) * a.ndim)
    io_spec = pl.BlockSpec((1, r_in, ct, LANES), lambda j, bi: (bi, 0, j, 0))
    return pl.pallas_call(
        functools.partial(_fftconv_body, ct=ct, g=g, nk=nk, r_in=r_in, pitch=pitch),
        grid=(c // ct, b),
        in_specs=[io_spec, io_spec,
                  pl.BlockSpec((ct, 1, 1), lambda j, bi: (j, 0, 0)),
                  pl.BlockSpec((ct, nk, 2 * LANES), lambda j, bi: (j, 0, 0)),
                  full(f1), full(twr), full(twi), full(m2), full(m2i), full(gi)],
        out_specs=io_spec,
        out_shape=jax.ShapeDtypeStruct((b, r_in, c, LANES), BF16),
        scratch_shapes=[pltpu.VMEM((r_in * pitch, LANES), F32) for _ in range(3)],
        compiler_params=_cparams("parallel", "arbitrary"),
        name="hyena_fftconv",
    )(u, z0, skip3, spec, f1, twr, twi, m2, m2i, gi)


ROUTE_LANES = LANES
NEG_BIG = -1e30
HALF_WORD = 16


def _pack_bf16_pairs(v):
    h = v.shape[1] // 2
    bits = pltpu.bitcast(v.astype(BF16).astype(F32), I32)
    return bits[:, :h] | lax.shift_right_logical(bits[:, h:], HALF_WORD)


def _unpack_bf16_pairs(w):
    hi = pltpu.bitcast(w & jnp.int32(-65536), F32)
    lo = pltpu.bitcast(lax.shift_left(w, HALF_WORD), F32)
    return jnp.concatenate([hi, lo], axis=1)


SLAB = 4


def _store_row_slabs(ref, words, row0=0):
    rows = words.shape[0]
    for j in range(SLAB):
        ref[pl.ds(SLAB * row0 + j, rows, stride=SLAB), :] = words[:, j * LANES:(j + 1) * LANES]


def _load_row_slabs(ref, rows=None, row0=0):
    rows = ref.shape[0] // SLAB if rows is None else rows
    return jnp.concatenate([ref[pl.ds(SLAB * row0 + j, rows, stride=SLAB), :] for j in range(SLAB)], axis=1)


MIX_SUB = 512


def _mix_route_body(x_ref, hf_ref, hb_ref, prg_ref, pga_ref, pgb_ref, yt_ref, rgp_ref, hyp_ref, wo_ref, g1_ref,
                    n2g_ref, sh2_ref, sc2_ref, wr_ref, br_ref, tri_ref,
                    x1_ref, hxp_ref, route_ref, cnt_ref, carry_s, *, t, sub, n_exp):
    @pl.when((pl.program_id(0) == 0) & (pl.program_id(1) == 0))
    def _():
        carry_s[...] = jnp.zeros_like(carry_s)

    running = carry_s[...]
    for r0 in range(0, t, sub):
        rows = slice(r0, r0 + sub)
        hsum = hf_ref[0, rows, :].astype(F32) + hb_ref[0, rows, :].astype(F32)
        y_rg = (hsum * _gelu_tanh(prg_ref[0, rows, :].astype(F32))).astype(BF16)
        t1 = jnp.dot(y_rg, rgp_ref[...], preferred_element_type=F32)
        t2 = jnp.concatenate([lax.dot_general(yt_ref[0, q], hyp_ref[...], (((0,), (0,)), ((), ())),
                                              preferred_element_type=F32)
                              for q in range(r0 // LANES, (r0 + sub) // LANES)], axis=0)
        merged = _sigmoid(pga_ref[0, rows, :].astype(F32)) * t1 + _sigmoid(pgb_ref[0, rows, :].astype(F32)) * t2
        out = jnp.dot(merged.astype(BF16), wo_ref[...], preferred_element_type=F32)
        x1 = x_ref[0, rows, :] + g1_ref[0] * out
        x1_ref[0, rows, :] = x1
        ms = jnp.mean(x1 * x1, axis=-1, keepdims=True)
        hx2 = (x1 * lax.rsqrt(ms + EPS) * n2g_ref[...]) * (1.0 + sc2_ref[0]) + sh2_ref[0]
        _store_row_slabs(hxp_ref, _pack_bf16_pairs(hx2), r0)

        hx_hi = hx2.astype(BF16)
        hx_lo = (hx2 - hx_hi.astype(F32)).astype(BF16)
        parts = (jnp.dot(hx_hi, wr_ref[...], preferred_element_type=F32)
                 + jnp.dot(hx_lo, wr_ref[...], preferred_element_type=F32))
        logits = parts[:, :ROUTE_LANES] + parts[:, ROUTE_LANES:] + br_ref[...]
        lane = lax.broadcasted_iota(I32, (sub, ROUTE_LANES), 1)
        is_g = lane < N_GROUPS
        glog = jnp.where(is_g, logits, NEG_BIG)
        gmax = jnp.max(glog, axis=1, keepdims=True)
        gidx = jnp.min(jnp.where(glog == gmax, lane, ROUTE_LANES), axis=1, keepdims=True)
        gsum = jnp.sum(jnp.where(is_g, jnp.exp(glog - gmax), 0.0), axis=1, keepdims=True)
        p_g = 1.0 / gsum
        e_lane = lane - N_GROUPS
        grp_of_lane = lax.shift_right_arithmetic(e_lane, int(math.log2(EXPERTS_PER_GROUP)))
        in_grp = (e_lane >= 0) & (e_lane < n_exp) & (grp_of_lane == gidx)
        elog = jnp.where(in_grp, logits, NEG_BIG)
        m1 = jnp.max(elog, axis=1, keepdims=True)
        i1 = jnp.min(jnp.where(elog == m1, lane, ROUTE_LANES), axis=1, keepdims=True)
        elog2 = jnp.where(lane == i1, NEG_BIG, elog)
        m2 = jnp.max(elog2, axis=1, keepdims=True)
        i2 = jnp.min(jnp.where(elog2 == m2, lane, ROUTE_LANES), axis=1, keepdims=True)
        e21 = jnp.exp(m2 - m1)
        pk1 = 1.0 / (1.0 + e21)
        wt1, wt2 = p_g * pk1, p_g * (e21 * pk1)

        oh1 = (lane == i1 - N_GROUPS).astype(F32)
        oh2 = (lane == i2 - N_GROUPS).astype(F32)
        cnt = oh1 + oh2
        before = jnp.dot(tri_ref[...], cnt.astype(BF16), preferred_element_type=F32) + running
        rank1 = jnp.sum(oh1 * before, axis=1, keepdims=True)
        rank2 = jnp.sum(oh2 * before, axis=1, keepdims=True)
        running = running + jnp.sum(cnt, axis=0, keepdims=True)
        vals = ((i1 - N_GROUPS).astype(F32), (i2 - N_GROUPS).astype(F32), rank1, rank2, wt1, wt2)
        route = jnp.zeros((sub, ROUTE_LANES), F32)
        for k, v in enumerate(vals):
            route = jnp.where(lane == k, v, route)
        route_ref[rows, :] = route
    carry_s[...] = running
    cnt_ref[...] = running


def mix_route(x, h_f, h_b, p_rm, y_hy_t, rg_proj, hy_proj, w_out, g1, n2g, sh2, sc2, wr, br, *, tile_l, n_exp):
    b, l, d = x.shape
    c = h_f.shape[2]
    t = min(tile_l, l)
    nt = l // t
    n = b * l
    sub = min(MIX_SUB, t)
    tri = (jnp.arange(sub)[:, None] > jnp.arange(sub)[None, :]).astype(BF16)
    tok = lambda bi, i: (bi, i, 0)
    col = lambda k: (lambda bi, i: (bi, i, k))
    full2 = lambda a: pl.BlockSpec(a.shape, lambda bi, i: (0, 0))
    per_b = pl.BlockSpec((1, 1, d), lambda bi, i: (bi, 0, 0))
    row = lambda bi, i: (bi * nt + i, 0)
    return pl.pallas_call(
        functools.partial(_mix_route_body, t=t, sub=sub, n_exp=n_exp),
        grid=(b, nt),
        in_specs=[pl.BlockSpec((1, t, d), tok), pl.BlockSpec((1, t, c), tok), pl.BlockSpec((1, t, c), tok),
                  pl.BlockSpec((1, t, c), col(1)), pl.BlockSpec((1, t, c), col(2)), pl.BlockSpec((1, t, c), col(3)),
                  pl.BlockSpec((1, t // LANES, c, LANES), lambda bi, i: (bi, i, 0, 0)),
                  full2(rg_proj), full2(hy_proj), full2(w_out), per_b,
                  full2(n2g), per_b, per_b, full2(wr), full2(br), full2(tri)],
        out_specs=[pl.BlockSpec((1, t, d), tok),
                   pl.BlockSpec((t * SLAB, LANES), row),
                   pl.BlockSpec((t, ROUTE_LANES), row), pl.BlockSpec((1, ROUTE_LANES), lambda bi, i: (0, 0))],
        out_shape=[jax.ShapeDtypeStruct((b, l, d), F32), jax.ShapeDtypeStruct((n * SLAB, LANES), I32),
                   jax.ShapeDtypeStruct((n, ROUTE_LANES), F32), jax.ShapeDtypeStruct((1, ROUTE_LANES), F32)],
        scratch_shapes=[pltpu.VMEM((1, ROUTE_LANES), F32)],
        compiler_params=_cparams("arbitrary", "arbitrary"),
        name="mix_route",
    )(x, h_f, h_b, p_rm, p_rm, p_rm, y_hy_t, rg_proj, hy_proj, w_out, g1, n2g, sh2, sc2, wr, br, tri)


def _dest_body(route_ref, cnt_ref, ut_ref, dest_ref, blk_ref, *, t, n_exp, nb_pad):
    lane1 = lax.broadcasted_iota(I32, (1, ROUTE_LANES), 1)
    padded = jnp.floor((cnt_ref[...] + (MOE_BLOCK - 1.0)) * (1.0 / MOE_BLOCK)) * MOE_BLOCK
    padded = jnp.where(lane1 < n_exp, padded, 0.0)
    pend = jnp.dot(jnp.broadcast_to(padded, (SUBLANES, ROUTE_LANES)), ut_ref[...], precision=HIGHEST,
                   preferred_element_type=F32)[0:1]
    pstart = pend - padded
    route = route_ref[...]
    lane = lax.broadcasted_iota(I32, (t, ROUTE_LANES), 1)
    lf = lane.astype(F32)
    d1 = jnp.sum(jnp.where(lf == route[:, 0:1], pstart, 0.0), axis=1, keepdims=True) + route[:, 2:3]
    d2 = jnp.sum(jnp.where(lf == route[:, 1:2], pstart, 0.0), axis=1, keepdims=True) + route[:, 3:4]
    dmat = jnp.where(lane == 0, d1, jnp.where(lane == 1, d2, 0.0))
    dest_ref[...] = dmat.T[0:SUBLANES].astype(I32)
    first_row = lax.broadcasted_iota(I32, (nb_pad, ROUTE_LANES), 0).astype(F32) * float(MOE_BLOCK)
    lane_b = lax.broadcasted_iota(I32, (nb_pad, ROUTE_LANES), 1)
    nle = jnp.sum(jnp.where((lane_b < n_exp) & (pend <= first_row), 1.0, 0.0), axis=1, keepdims=True)
    e_blk = jnp.minimum(nle, n_exp - 1.0)
    mine = lane_b.astype(F32) == e_blk
    cnt_e = jnp.sum(jnp.where(mine, cnt_ref[...], 0.0), axis=1, keepdims=True)
    start_e = jnp.sum(jnp.where(mine, pstart, 0.0), axis=1, keepdims=True)
    valid = jnp.clip(cnt_e - (first_row - start_e), 0.0, float(MOE_BLOCK))
    blk_ref[...] = jnp.where(lane_b == 0, e_blk, jnp.where(lane_b == 1, valid, 0.0)).astype(I32)


def moe_dest(route, cnt, *, tile, n_exp, n_blocks):
    n = route.shape[0]
    t = min(tile, n)
    nb_pad = -(-n_blocks // SUBLANES) * SUBLANES
    ut = (jnp.arange(ROUTE_LANES)[:, None] <= jnp.arange(ROUTE_LANES)[None, :]).astype(F32)
    return pl.pallas_call(
        functools.partial(_dest_body, t=t, n_exp=n_exp, nb_pad=nb_pad),
        grid=(n // t,),
        in_specs=[pl.BlockSpec((t, ROUTE_LANES), lambda i: (i, 0)),
                  pl.BlockSpec((1, ROUTE_LANES), lambda i: (0, 0)),
                  pl.BlockSpec((ROUTE_LANES, ROUTE_LANES), lambda i: (0, 0))],
        out_specs=[pl.BlockSpec((SUBLANES, t), lambda i: (i, 0)),
                   pl.BlockSpec((nb_pad, ROUTE_LANES), lambda i: (0, 0))],
        out_shape=[jax.ShapeDtypeStruct((n // t * SUBLANES, t), I32),
                   jax.ShapeDtypeStruct((nb_pad, ROUTE_LANES), I32)],
        compiler_params=_cparams("arbitrary"),
        name="moe_dest",
    )(route, cnt, ut)


def _scatter_body(dest_ref, hx_ref, xb_in_ref, xb_ref, sem, *, t):
    del xb_in_ref

    def issue(r, _):
        for k in range(2):
            pltpu.make_async_copy(hx_ref.at[pl.ds(SLAB * r, SLAB)], xb_ref.at[pl.ds(SLAB * dest_ref[k, r], SLAB)],
                                  sem).start(priority=k)
        return 0

    lax.fori_loop(0, t, issue, 0, unroll=8)
    for k in range(2):
        pltpu.make_async_copy(hx_ref, xb_ref.at[pl.ds(0, SLAB * t)], sem).wait()


def moe_scatter(dest, hxp, n_rows, *, tile):
    n = hxp.shape[0] // SLAB
    t = min(tile, n)
    xb0 = jnp.zeros((n_rows * SLAB, LANES), I32)
    per_dest_tile = dest.shape[1] // t
    return pl.pallas_call(
        functools.partial(_scatter_body, t=t),
        grid=(n // t,),
        in_specs=[pl.BlockSpec((SUBLANES, t), lambda i: (i // per_dest_tile, i % per_dest_tile),
                               memory_space=pltpu.SMEM),
                  pl.BlockSpec((t * SLAB, LANES), lambda i: (i, 0)),
                  pl.BlockSpec(memory_space=pl.ANY)],
        out_specs=pl.BlockSpec(memory_space=pl.ANY),
        out_shape=jax.ShapeDtypeStruct((n_rows * SLAB, LANES), I32),
        scratch_shapes=[pltpu.SemaphoreType.DMA],
        input_output_aliases={2: 0},
        compiler_params=_cparams("arbitrary"),
        name="moe_scatter",
    )(dest, hxp, xb0)


def _expert_body(blk_ref, valid_ref, xb_ref, w1_ref, w3_ref, w2_ref, yb_ref, w1_s, w3_s, w2_s):
    i = pl.program_id(0)
    valid = valid_ref[i]
    half = MOE_BLOCK // 2
    changed = (i == 0) | (blk_ref[i] != blk_ref[jnp.maximum(i - 1, 0)])

    @pl.when(changed & (valid > 0))
    def _():
        w1_s[...] = w1_ref[0].astype(BF16)
        w3_s[...] = w3_ref[0].astype(BF16)
        w2_s[...] = w2_ref[0].astype(BF16)

    def run(rows):
        xblk = _unpack_bf16_pairs(_load_row_slabs(xb_ref, rows)).astype(BF16)
        h1 = jnp.dot(xblk, w1_s[...], preferred_element_type=F32)
        h3 = jnp.dot(xblk, w3_s[...], preferred_element_type=F32)
        hid = (h1 * _sigmoid(h1) * h3).astype(BF16)
        _store_row_slabs(yb_ref, _pack_bf16_pairs(jnp.dot(hid, w2_s[...], preferred_element_type=F32)))

    @pl.when(valid > half)
    def _():
        run(MOE_BLOCK)

    @pl.when(valid <= half)
    def _():
        yb_ref[...] = jnp.zeros_like(yb_ref)

    @pl.when((valid > 0) & (valid <= half))
    def _():
        run(half)


def moe_experts(blk_exp, blk_valid, xb, w1, w3, w2):
    p = xb.shape[0] // SLAB
    _, d, de = w1.shape
    nb = p // MOE_BLOCK
    grid_spec = pltpu.PrefetchScalarGridSpec(
        num_scalar_prefetch=2,
        grid=(nb,),
        in_specs=[pl.BlockSpec((MOE_BLOCK * SLAB, LANES), lambda i, blk, valid: (i, 0)),
                  pl.BlockSpec((1, d, de), lambda i, blk, valid: (blk[i], 0, 0)),
                  pl.BlockSpec((1, d, de), lambda i, blk, valid: (blk[i], 0, 0)),
                  pl.BlockSpec((1, de, d), lambda i, blk, valid: (blk[i], 0, 0))],
        out_specs=pl.BlockSpec((MOE_BLOCK * SLAB, LANES), lambda i, blk, valid: (i, 0)),
        scratch_shapes=[pltpu.VMEM((d, de), BF16), pltpu.VMEM((d, de), BF16), pltpu.VMEM((de, d), BF16)],
    )
    return pl.pallas_call(
        _expert_body,
        grid_spec=grid_spec,
        out_shape=jax.ShapeDtypeStruct((p * SLAB, LANES), I32),
        compiler_params=_cparams("arbitrary"),
        name="moe_experts",
    )(blk_exp, blk_valid, xb, w1, w3, w2)


COMBINE_PARTS = 4


def _combine_body(dest_ref, x1_ref, route_ref, g2_ref, fg_ref, yb_ref, o_ref, y1_s, y2_s, sems, *, t):
    tp = t // COMBINE_PARTS
    for part in range(COMBINE_PARTS):
        def issue(r, _, sem=sems.at[part]):
            pltpu.make_async_copy(yb_ref.at[pl.ds(SLAB * dest_ref[0, r], SLAB)], y1_s.at[pl.ds(SLAB * r, SLAB)],
                                  sem).start(priority=0)
            pltpu.make_async_copy(yb_ref.at[pl.ds(SLAB * dest_ref[1, r], SLAB)], y2_s.at[pl.ds(SLAB * r, SLAB)],
                                  sem).start(priority=1)
            return 0

        lax.fori_loop(part * tp, (part + 1) * tp, issue, 0, unroll=8)
    for part in range(COMBINE_PARTS):
        rows = slice(part * tp, (part + 1) * tp)
        lines = pl.ds(SLAB * part * tp, SLAB * tp)
        for y_s in (y1_s, y2_s):
            pltpu.make_async_copy(yb_ref.at[pl.ds(0, SLAB * tp)], y_s.at[lines], sems.at[part]).wait()
        route = route_ref[rows, :]
        moe = (route[:, 4:5] * _unpack_bf16_pairs(_load_row_slabs(y1_s, tp, part * tp))
               + route[:, 5:6] * _unpack_bf16_pairs(_load_row_slabs(y2_s, tp, part * tp)))
        x2 = x1_ref[0, rows, :] + g2_ref[0] * moe
        ms = jnp.mean(x2 * x2, axis=-1, keepdims=True)
        o_ref[0, rows, :] = x2 * lax.rsqrt(ms + EPS) * fg_ref[...]


def moe_combine(dest, x1, route, g2, final_g, yb, *, tile_l):
    b, l, d = x1.shape
    t = min(tile_l, l)
    nt = l // t
    slab = (t * SLAB, LANES)
    per_dest_tile = dest.shape[1] // t
    return pl.pallas_call(
        functools.partial(_combine_body, t=t),
        grid=(b, nt),
        in_specs=[pl.BlockSpec((SUBLANES, t),
                               lambda bi, i: ((bi * nt + i) // per_dest_tile, (bi * nt + i) % per_dest_tile),
                               memory_space=pltpu.SMEM),
                  pl.BlockSpec((1, t, d), lambda bi, i: (bi, i, 0)),
                  pl.BlockSpec((t, ROUTE_LANES), lambda bi, i: (bi * nt + i, 0)),
                  pl.BlockSpec((1, 1, d), lambda bi, i: (bi, 0, 0)),
                  pl.BlockSpec((1, d), lambda bi, i: (0, 0)),
                  pl.BlockSpec(memory_space=pl.ANY)],
        out_specs=pl.BlockSpec((1, t, d), lambda bi, i: (bi, i, 0)),
        out_shape=jax.ShapeDtypeStruct((b, l, d), F32),
        scratch_shapes=[pltpu.VMEM(slab, I32), pltpu.VMEM(slab, I32), pltpu.SemaphoreType.DMA((COMBINE_PARTS,))],
        compiler_params=_cparams("arbitrary", "arbitrary"),
        name="moe_combine",
    )(dest, x1, route, g2, final_g, yb)


def kernel(x, c, ctx, c_ctx, ada_w, ada_b, norm1_g, norm2_g, final_g, w_in, rg_conv_w, rg_conv_b, rg_wa_f, rg_ba_f, rg_wx_f, rg_bx_f, rg_lam_f, rg_wa_b, rg_ba_b, rg_wx_b, rg_bx_b, rg_lam_b, rg_proj, hy_conv_w, hy_conv_b, hy_pos_w1, hy_pos_b1, hy_pos_w2, hy_pos_b2, hy_freq, hy_pos_w3, hy_skip, hy_proj, w_out, moe_wg, moe_bg, moe_we, moe_be, moe_w1, moe_w3, moe_w2):
    B, L, D = x.shape
    C = rg_conv_w.shape[-1]
    LC = ctx.shape[1]
    c8 = jnp.zeros((8, D), F32).at[:B].set(c).at[B].set(c_ctx)
    mods = ada_mods(c8, ada_w[0], ada_b)
    sh1, sc1, g1 = (mods[:B, None, k * D:(k + 1) * D] for k in range(3))
    sh2, sc2, g2 = (mods[:B, None, k * D:(k + 1) * D] for k in range(3, 6))
    csh1 = jnp.broadcast_to(mods[B:B + 1, None, 0:D], (B, 1, D))
    csc1 = jnp.broadcast_to(mods[B:B + 1, None, D:2 * D], (B, 1, D))

    w_in_b = w_in[0].astype(BF16)
    w_rm = jnp.concatenate([w_in_b[:, :2 * C], w_in_b[:, 5 * C:]], axis=1)
    wg_f = gate_blocks(rg_wa_f[0], rg_wx_f[0], C // 256)
    wg_b = gate_blocks(rg_wa_b[0], rg_wx_b[0], C // 256)
    rg_f = (rg_conv_w[0], rg_conv_b, wg_f, rg_ba_f, rg_bx_f, rg_lam_f)
    rg_b = (rg_conv_w[0], rg_conv_b, wg_b, rg_ba_b, rg_bx_b, rg_lam_b)

    pc = norm_mod_proj(ctx, norm1_g, csh1, csc1, w_rm[:, :C], tile_l=LC, chunk=C)
    zero = jnp.zeros((B, 1, C), F32)
    _, hcf = rg_scan(pc, 0, *rg_f, zero, reverse=False, tile_l=TILE_SCAN)
    _, hcb = rg_scan(pc, 0, *rg_b, zero, reverse=True, tile_l=TILE_SCAN)

    p_rm = norm_mod_proj(x, norm1_g, sh1, sc1, w_rm, tile_l=TILE_PROJ, chunk=PROJ_CHUNK)
    hy_taps = jnp.concatenate([hy_conv_w[0], hy_conv_b], axis=0)
    u_hy, z0_hy = hyena_proj(x, norm1_g, sh1, sc1, w_in_b[:, 2 * C:5 * C], hy_taps, tile_l=TILE_PROJ)
    h_f, _ = rg_scan(p_rm, 0, *rg_f, hcf, reverse=False, tile_l=TILE_SCAN)
    h_b, _ = rg_scan(p_rm, 0, *rg_b, hcb, reverse=True, tile_l=TILE_SCAN)

    tables = dft_tables(L)
    assert hy_pos_w1.shape[1] == len(HY_FEATURE_ORDER)
    w1t = jnp.zeros((HY_HID, HY_HID), F32).at[:, :hy_pos_w1.shape[1]].set(hy_pos_w1[0].T[:, jnp.array(HY_FEATURE_ORDER)])
    kt = hyena_filter_t(w1t, hy_pos_b1[0][:, None], hy_pos_w2[0].T, hy_pos_b2[0][:, None], hy_freq[0][:, None],
                        hy_pos_w3[0].T.reshape(2, C, HY_HID).astype(BF16), L, FILTER_TILE_C)
    spec = hyena_spectrum(kt.reshape(2, C, L // LANES, LANES), tables, SPECTRUM_GROUP)
    y_hy_t = hyena_fftconv(u_hy, z0_hy, hy_skip[0][:, None, None], spec, tables, tile_c=FFT_TILE_C, group=FFT_GROUP)

    n_exp = moe_we.shape[-1]
    n_grp = moe_wg.shape[-1]
    assert n_grp == N_GROUPS and n_exp == N_GROUPS * EXPERTS_PER_GROUP
    wr = jnp.zeros((D, ROUTE_LANES), F32).at[:, :n_grp].set(moe_wg[0]).at[:, n_grp:n_grp + n_exp].set(moe_we[0])
    br = jnp.zeros((1, ROUTE_LANES), F32).at[:, :n_grp].set(moe_bg).at[:, n_grp:n_grp + n_exp].set(moe_be)
    wr_hi = wr.astype(BF16)
    wr_split = jnp.concatenate([wr_hi, (wr - wr_hi.astype(F32)).astype(BF16)], axis=1)
    x1, hxp, route, cnt = mix_route(x, h_f, h_b, p_rm, y_hy_t, rg_proj[0].astype(BF16), hy_proj[0].astype(BF16),
                                    w_out[0].astype(BF16), g1, norm2_g, sh2, sc2, wr_split, br, tile_l=TILE_MIX,
                                    n_exp=n_exp)

    n_blocks = (2 * B * L + n_exp * (MOE_BLOCK - 1)) // MOE_BLOCK
    dest, blk = moe_dest(route, cnt, tile=TILE_DEST, n_exp=n_exp, n_blocks=n_blocks)
    xb = moe_scatter(dest, hxp, n_blocks * MOE_BLOCK, tile=TILE_DISPATCH)
    yb = moe_experts(blk[:n_blocks, 0], blk[:n_blocks, 1], xb, moe_w1[0], moe_w3[0], moe_w2[0])
    return moe_combine(dest, x1, route, g2, final_g[None], yb, tile_l=TILE_DISPATCH)
```

```python
import functools
import math

import jax
import jax.numpy as jnp
from jax import lax
from jax.experimental import pallas as pl
from jax.experimental.pallas import tpu as pltpu

F32 = jnp.float32
BF16 = jnp.bfloat16
I32 = jnp.int32
HIGHEST = lax.Precision.HIGHEST

LANES = 128
SUBLANES = 8
EPS = 1e-6
RG_C = 8.0
RG_HEAD_DIM = 64
GRID_W = 64
HY_SEQ_BANDS = 16
HY_COL_BANDS = 8
HY_DECAY_TARGET = 1e-2
HY_FAST_DECAY = 0.3
HY_SLOW_DECAY = 1.5
N_GROUPS = 4
EXPERTS_PER_GROUP = 8
MOE_BLOCK = 512
VMEM_LIMIT = 56 * 1024 * 1024

TILE_PROJ = 512
PROJ_CHUNK = 1024
TILE_SCAN = 512
TILE_MIX = 512
TILE_DEST = 2048
TILE_DISPATCH = 1024
FILTER_TILE_C = 256
SPECTRUM_GROUP = 32
FFT_TILE_C = 64
FFT_GROUP = 8


def _cparams(*sem):
    return pltpu.CompilerParams(dimension_semantics=sem, vmem_limit_bytes=VMEM_LIMIT)


def _sigmoid(x):
    return 0.5 * (jnp.tanh(0.5 * x) + 1.0)


def _gelu_tanh(x):
    c = math.sqrt(2.0 / math.pi)
    return 0.5 * x * (1.0 + jnp.tanh(c * (x + 0.044715 * (x * x * x))))


def _ada_body(c_ref, w_ref, b_ref, o_ref):
    c = c_ref[...]
    s = c * _sigmoid(c)
    o_ref[...] = jnp.dot(s, w_ref[...], precision=HIGHEST, preferred_element_type=F32) + b_ref[...]


def ada_mods(c8, ada_w, ada_b):
    d, m = ada_w.shape
    tn = 1024 if m % 1024 == 0 else m
    return pl.pallas_call(
        _ada_body,
        grid=(m // tn,),
        in_specs=[pl.BlockSpec((c8.shape[0], d), lambda j: (0, 0)),
                  pl.BlockSpec((d, tn), lambda j: (0, j)),
                  pl.BlockSpec((1, tn), lambda j: (0, j))],
        out_specs=pl.BlockSpec((c8.shape[0], tn), lambda j: (0, j)),
        out_shape=jax.ShapeDtypeStruct((c8.shape[0], m), F32),
        compiler_params=_cparams("parallel"),
        name="ada_mods",
    )(c8, ada_w, ada_b)


def _norm_mod(x_ref, g_ref, sh_ref, sc_ref):
    x = x_ref[0]
    ms = jnp.mean(x * x, axis=-1, keepdims=True)
    y = x * lax.rsqrt(ms + EPS) * g_ref[...]
    return (y * (1.0 + sc_ref[0]) + sh_ref[0]).astype(BF16)


def _first_rows_body(x_ref, g_ref, sh_ref, sc_ref, w_ref, o_ref):
    nt, rows, d = x_ref.shape[1:]
    x = x_ref[0].reshape(nt * rows, d)
    ms = jnp.mean(x * x, axis=-1, keepdims=True)
    y = x * lax.rsqrt(ms + EPS) * g_ref[...]
    hx = (y * (1.0 + sc_ref[0]) + sh_ref[0]).astype(BF16)
    o_ref[0] = jnp.dot(hx, w_ref[...], preferred_element_type=F32).reshape(nt, rows, w_ref.shape[1])


def proj_first_rows(x, g, shift, scale, w, *, tile_l):
    b, l, d = x.shape
    t = min(tile_l, l)
    nt = l // t
    m = w.shape[1]
    return pl.pallas_call(
        _first_rows_body,
        grid=(b,),
        in_specs=[pl.BlockSpec((1, nt, SUBLANES, d), lambda bi: (bi, 0, 0, 0)),
                  pl.BlockSpec((1, d), lambda bi: (0, 0)),
                  pl.BlockSpec((1, 1, d), lambda bi: (bi, 0, 0)),
                  pl.BlockSpec((1, 1, d), lambda bi: (bi, 0, 0)),
                  pl.BlockSpec(w.shape, lambda bi: (0, 0))],
        out_specs=pl.BlockSpec((1, nt, SUBLANES, m), lambda bi: (bi, 0, 0, 0)),
        out_shape=jax.ShapeDtypeStruct((b, nt, SUBLANES, m), F32),
        compiler_params=_cparams("parallel"),
        name="proj_first_rows",
    )(x.reshape(b, nt, t, d), g, shift, scale, w)


def _proj_body(x_ref, g_ref, sh_ref, sc_ref, w_ref, cw_ref, cb_ref, nxt_ref, o_ref, last_s, *, chunk, n_tiles, t, c):
    i = pl.program_id(1)

    @pl.when(i == 0)
    def _():
        last_s[...] = jnp.zeros_like(last_s)

    hx = _norm_mod(x_ref, g_ref, sh_ref, sc_ref)
    p = jnp.dot(hx, w_ref[:, 0:c], preferred_element_type=F32)
    row = lax.broadcasted_iota(I32, (t, 1), 0)
    prev2, prev1 = last_s[SUBLANES - 2:SUBLANES - 1, :], last_s[SUBLANES - 1:SUBLANES, :]
    next0 = nxt_ref[0, 0, 0:1, :] * (i < n_tiles - 1).astype(F32)
    m1 = jnp.where(row == 0, prev1, pltpu.roll(p, 1, 0))
    m2 = jnp.where(row == 0, prev2, jnp.where(row == 1, prev1, pltpu.roll(p, 2, 0)))
    p1 = jnp.where(row == t - 1, next0, pltpu.roll(p, t - 1, 0))
    cw = cw_ref[...]
    xc = cb_ref[...] + cw[0:1] * m2 + cw[1:2] * m1 + cw[2:3] * p + cw[3:4] * p1
    o_ref[0, :, 0:c] = xc.astype(o_ref.dtype)
    last_s[...] = p[t - SUBLANES:t, :]
    m = w_ref.shape[1]
    for j in range((m - c) // chunk):
        cols = slice(c + j * chunk, c + (j + 1) * chunk)
        o_ref[0, :, cols] = jnp.dot(hx, w_ref[:, cols], preferred_element_type=F32).astype(o_ref.dtype)


def norm_mod_proj(x, g, shift, scale, w, conv_w, conv_b, *, tile_l, chunk):
    b, l, d = x.shape
    m = w.shape[1]
    c = conv_w.shape[1]
    tl = min(tile_l, l)
    n_tiles = l // tl
    chunk = min(chunk, max(m - c, 1))
    assert (m - c) % chunk == 0
    nxt = proj_first_rows(x, g, shift, scale, w[:, :c], tile_l=tile_l)
    return pl.pallas_call(
        functools.partial(_proj_body, chunk=chunk, n_tiles=n_tiles, t=tl, c=c),
        grid=(b, n_tiles),
        in_specs=[pl.BlockSpec((1, tl, d), lambda bi, i: (bi, i, 0)),
                  pl.BlockSpec((1, d), lambda bi, i: (0, 0)),
                  pl.BlockSpec((1, 1, d), lambda bi, i: (bi, 0, 0)),
                  pl.BlockSpec((1, 1, d), lambda bi, i: (bi, 0, 0)),
                  pl.BlockSpec(w.shape, lambda bi, i: (0, 0)),
                  pl.BlockSpec(conv_w.shape, lambda bi, i: (0, 0)),
                  pl.BlockSpec(conv_b.shape, lambda bi, i: (0, 0)),
                  pl.BlockSpec((1, 1, SUBLANES, c), lambda bi, i: (bi, jnp.minimum(i + 1, n_tiles - 1), 0, 0))],
        out_specs=pl.BlockSpec((1, tl, m), lambda bi, i: (bi, i, 0)),
        out_shape=jax.ShapeDtypeStruct((b, l, m), BF16),
        scratch_shapes=[pltpu.VMEM((SUBLANES, c), F32)],
        compiler_params=_cparams("parallel", "arbitrary"),
        name="norm_mod_proj",
    )(x, g, shift, scale, w, conv_w, conv_b, nxt)


def _scan_body(xc_ref, wg_ref, ba_ref, bx_ref, lam_ref, h0_ref,
               h_ref, hl_ref, xc_s, g_s, a_s, b_s, hloc_s, pcum_s, carry_s,
               *, reverse, t, c, s_len, pitch):
    i = pl.program_id(1)
    n_slab = c // LANES
    n_blk = wg_ref.shape[0]
    blk = c // n_blk

    @pl.when(i == 0)
    def _():
        carry_s[...] = h0_ref[0]

    xc_s[...] = xc_ref[0].astype(F32)

    for k in range(n_blk):
        g_s[:, k * 2 * blk:(k + 1) * 2 * blk] = jnp.dot(xc_ref[0, :, k * blk:(k + 1) * blk], wg_ref[k],
                                                          preferred_element_type=F32)

    lam = lam_ref[...]
    softplus_neg_lam = jnp.maximum(-lam, 0.0) + jnp.log1p(jnp.exp(-jnp.abs(lam)))
    half_ca = (-0.5 * RG_C) * softplus_neg_lam
    half_ba, half_bx = 0.5 * ba_ref[...], 0.5 * bx_ref[...]
    slabs_per_blk = blk // LANES
    for j in range(SUBLANES):
        r0 = j * s_len
        for k in range(n_slab):
            kb, ks = k // slabs_per_blk, k % slabs_per_blk
            ga = g_s[r0:r0 + s_len, kb * 2 * blk + ks * LANES:kb * 2 * blk + (ks + 1) * LANES]
            gx = g_s[r0:r0 + s_len, kb * 2 * blk + blk + ks * LANES:kb * 2 * blk + blk + (ks + 1) * LANES]
            lane = slice(k * LANES, (k + 1) * LANES)
            half_x = 0.5 * xc_s[r0:r0 + s_len, lane]
            hca = half_ca[:, lane]
            log_a = hca * jnp.tanh(ga + half_ba[:, lane]) + hca
            gated_x = half_x * jnp.tanh(gx + half_bx[:, lane]) + half_x
            a = jnp.exp(log_a)
            a_s[k, j * pitch:j * pitch + s_len, :] = a
            gain2 = -jnp.tanh(log_a) * (a * a + 1.0)
            gain = jnp.where(gain2 > 0.0, gain2 * lax.rsqrt(gain2), 0.0)
            b_s[k, j * pitch:j * pitch + s_len, :] = gain * gated_x

    def step1(s, hp):
        hs, ps = hp
        srow = (s_len - 1 - s) if reverse else s
        hs2, ps2 = [], []
        for k in range(n_slab):
            av = a_s[k, pl.ds(srow, SUBLANES, stride=pitch), :]
            bv = b_s[k, pl.ds(srow, SUBLANES, stride=pitch), :]
            h = av * hs[k] + bv
            p = av * ps[k]
            hloc_s[k, pl.ds(srow, SUBLANES, stride=pitch), :] = h
            pcum_s[k, pl.ds(srow, SUBLANES, stride=pitch), :] = p
            hs2.append(h)
            ps2.append(p)
        return tuple(hs2), tuple(ps2)

    zeros = tuple(jnp.zeros((SUBLANES, LANES), F32) for _ in range(n_slab))
    ones = tuple(jnp.ones((SUBLANES, LANES), F32) for _ in range(n_slab))
    h_end, p_end = lax.fori_loop(0, s_len, step1, (zeros, ones))

    order = range(SUBLANES - 1, -1, -1) if reverse else range(SUBLANES)
    for k in range(n_slab):
        cst = carry_s[:, k * LANES:(k + 1) * LANES]
        for j in order:
            rows = slice(j * pitch, j * pitch + s_len)
            h_ref[0, j * s_len:(j + 1) * s_len, k * LANES:(k + 1) * LANES] = (
                hloc_s[k, rows, :] + pcum_s[k, rows, :] * cst).astype(h_ref.dtype)
            cst = p_end[k][j:j + 1] * cst + h_end[k][j:j + 1]
        carry_s[:, k * LANES:(k + 1) * LANES] = cst
    hl_ref[0] = carry_s[...]


def rg_scan(p, col_blk, wg, ba, bx, lam, h0, *, reverse, tile_l):
    b, l, _ = p.shape
    c = ba.shape[1]
    t = min(tile_l, l)
    n_tiles = l // t
    s_len = t // SUBLANES
    pitch = s_len + SUBLANES

    def nat(i):
        return (n_tiles - 1 - i) if reverse else i

    body = functools.partial(_scan_body, reverse=reverse, t=t, c=c, s_len=s_len, pitch=pitch)
    vec = pl.BlockSpec((1, c), lambda bi, i: (0, 0))
    return pl.pallas_call(
        body,
        grid=(b, n_tiles),
        in_specs=[pl.BlockSpec((1, t, c), lambda bi, i: (bi, nat(i), col_blk)),
                  pl.BlockSpec(wg.shape, lambda bi, i: (0, 0, 0)),
                  vec, vec, vec,
                  pl.BlockSpec((1, 1, c), lambda bi, i: (bi, 0, 0))],
        out_specs=[pl.BlockSpec((1, t, c), lambda bi, i: (bi, nat(i), 0)),
                   pl.BlockSpec((1, 1, c), lambda bi, i: (bi, 0, 0))],
        out_shape=[jax.ShapeDtypeStruct((b, l, c), BF16), jax.ShapeDtypeStruct((b, 1, c), F32)],
        scratch_shapes=[pltpu.VMEM((t, c), F32), pltpu.VMEM((t, 2 * c), F32)]
        + [pltpu.VMEM((c // LANES, SUBLANES * pitch, LANES), F32) for _ in range(4)]
        + [pltpu.VMEM((1, c), F32)],
        compiler_params=_cparams("parallel", "arbitrary"),
        name="rg_scan_bwd" if reverse else "rg_scan_fwd",
    )(p, wg, ba, bx, lam, h0)


def gate_blocks(wa, wx, n_blk):
    h, d, _ = wa.shape
    hp = h // n_blk
    eye = jnp.eye(hp, dtype=wa.dtype)

    def bd(w):
        w = w.reshape(n_blk, hp, d, d)
        return jnp.einsum('khde,hg->khdge', w, eye).reshape(n_blk, hp * d, hp * d)

    return (0.5 * jnp.concatenate([bd(wa), bd(wx)], axis=-1)).astype(BF16)


HY_HID = 64
HY_FEATURE_ORDER = (list(range(1, 1 + 2 * HY_SEQ_BANDS))
                    + list(range(2 + 2 * HY_SEQ_BANDS, 2 + 2 * HY_SEQ_BANDS + 2 * HY_COL_BANDS))
                    + [0, 1 + 2 * HY_SEQ_BANDS])


def _filter_body(w1t_ref, b1_ref, w2t_ref, b2_ref, fr_ref, w3t_ref, o_ref, z_s, *, l, c, ct, rows_grid):
    d = pl.program_id(0)
    j = pl.program_id(1)
    lane = lax.broadcasted_iota(I32, (1, l), 1)
    s_i = jnp.where(d == 0, lane, l - lane)
    sf = s_i.astype(F32)
    t_norm = sf / float(max(l - 1, 1))

    @pl.when(j == 0)
    def _():
        band_step = (HY_SEQ_BANDS - 1 - 1e-4) / (HY_SEQ_BANDS - 1)
        seq_band = 1e-4 + band_step * lax.broadcasted_iota(I32, (HY_SEQ_BANDS, 1), 0).astype(F32)
        col_band = 1.0 + lax.broadcasted_iota(I32, (HY_COL_BANDS, 1), 0).astype(F32)
        col_pos = (s_i & (GRID_W - 1)).astype(F32)
        row_lag = (s_i >> int(math.log2(GRID_W))).astype(F32) / float(rows_grid)
        ang_seq = ((2.0 * math.pi / l) * sf) * seq_band
        ang_col = ((2.0 * math.pi / GRID_W) * col_pos) * col_band
        n_trig = 2 * HY_SEQ_BANDS + 2 * HY_COL_BANDS
        trow = lax.broadcasted_iota(I32, (HY_HID - n_trig, 1), 0)
        tail = jnp.where(trow == 0, t_norm, jnp.where(trow == 1, row_lag, 0.0))
        feats = jnp.concatenate([jnp.cos(ang_seq), jnp.sin(ang_seq), jnp.cos(ang_col), jnp.sin(ang_col), tail], axis=0)
        fr = fr_ref[...]
        z = jnp.sin(fr * (jnp.dot(w1t_ref[...], feats, precision=HIGHEST, preferred_element_type=F32) + b1_ref[...]))
        z_s[...] = jnp.sin(fr * (jnp.dot(w2t_ref[...], z, precision=HIGHEST, preferred_element_type=F32) + b2_ref[...]))

    k = jnp.dot(w3t_ref[0], z_s[...].astype(BF16), preferred_element_type=F32)
    ch = (lax.broadcasted_iota(I32, (ct, 1), 0) + j * ct).astype(F32)
    max_decay = math.log(HY_DECAY_TARGET) / HY_FAST_DECAY
    min_decay = math.log(HY_DECAY_TARGET) / HY_SLOW_DECAY
    delta = jnp.abs(min_decay + ch * ((max_decay - min_decay) / (c - 1)))
    k = k * jnp.exp(-t_norm * delta)
    k = jnp.where((d == 1) & (lane == 0), 0.0, k)
    o_ref[0] = k.astype(o_ref.dtype)


def hyena_filter_t(w1t, b1, w2t, b2, fr, w3t, l, tile_c):
    assert GRID_W & (GRID_W - 1) == 0
    c = w3t.shape[1]
    ct = min(tile_c, c)
    body = functools.partial(_filter_body, l=l, c=c, ct=ct, rows_grid=l // GRID_W)
    small = lambda shape: pl.BlockSpec(shape, lambda d, j: (0,) * len(shape))
    return pl.pallas_call(
        body,
        grid=(2, c // ct),
        in_specs=[small(w1t.shape), small(b1.shape), small(w2t.shape), small(b2.shape), small(fr.shape),
                  pl.BlockSpec((1, ct, HY_HID), lambda d, j: (d, j, 0))],
        out_specs=pl.BlockSpec((1, ct, l), lambda d, j: (d, j, 0)),
        out_shape=jax.ShapeDtypeStruct((2, c, l), BF16),
        scratch_shapes=[pltpu.VMEM((HY_HID, l), F32)],
        compiler_params=_cparams("arbitrary", "arbitrary"),
        name="hyena_filter",
    )(w1t, b1, w2t, b2, fr, w3t)


def dft_tables(l):
    import numpy as np
    n = 2 * l
    r_in, nk = l // LANES, n // LANES
    ka = np.arange(nk)[:, None].astype(np.float64)
    r = np.arange(r_in)[None, :].astype(np.float64)
    a1 = 2.0 * np.pi * ka * r / nk
    f1 = np.concatenate([np.cos(a1), -np.sin(a1)], axis=0)
    lane = np.arange(LANES)[None, :].astype(np.float64)
    at = 2.0 * np.pi * ka * lane / n
    twr, twi = np.cos(at), -np.sin(at)
    a2 = 2.0 * np.pi * np.arange(LANES)[:, None] * np.arange(LANES)[None, :] / LANES
    cr, ci = np.cos(a2), -np.sin(a2)
    m2 = np.block([[cr, ci], [-ci, cr]])
    m2i = np.block([[cr, -ci], [ci, cr]])
    ai = 2.0 * np.pi * np.arange(r_in)[:, None] * np.arange(nk)[None, :] / nk
    gi = np.concatenate([np.cos(ai), -np.sin(ai)], axis=1) / n
    as_bf = lambda a: jnp.asarray(a, F32).astype(BF16)
    return as_bf(f1), jnp.asarray(twr, F32), jnp.asarray(twi, F32), as_bf(m2), as_bf(m2i), as_bf(gi)


def _fwd_rows_twiddle(x_a, x_b, f1, twr, twi, nk):
    a = jnp.dot(f1, jnp.concatenate([x_a, x_b], axis=1), preferred_element_type=F32)
    out = []
    for h in range(2):
        re, im = a[:nk, h * LANES:(h + 1) * LANES], a[nk:, h * LANES:(h + 1) * LANES]
        out.append((re * twr - im * twi, re * twi + im * twr))
    return out


def _spectrum_body(k_ref, f1_ref, twr_ref, twi_ref, m2_ref, o_ref, *, g, nk, r_in):
    f1, twr, twi = f1_ref[...], twr_ref[...], twi_ref[...]
    sign = jnp.where((lax.broadcasted_iota(I32, (nk, 1), 0) & 1) == 0, 1.0, -1.0)
    a2 = []
    for ci in range(g):
        (fre, fim), (bre, bim) = _fwd_rows_twiddle(k_ref[0, ci], k_ref[1, ci], f1, twr, twi, nk)
        a2.append(jnp.concatenate([fre + sign * bre, fim + sign * bim], axis=1).astype(BF16))
    spec = jnp.dot(jnp.concatenate(a2, axis=0), m2_ref[...], preferred_element_type=F32)
    o_ref[...] = spec.reshape(g, nk, 2 * LANES).astype(o_ref.dtype)


def hyena_spectrum(kt4, tables, group):
    _, c, r_in, _ = kt4.shape
    nk = 2 * r_in
    f1, twr, twi, m2, _, _ = tables
    g = min(group, c)
    full = lambda a: pl.BlockSpec(a.shape, lambda j: (0,) * a.ndim)
    return pl.pallas_call(
        functools.partial(_spectrum_body, g=g, nk=nk, r_in=r_in),
        grid=(c // g,),
        in_specs=[pl.BlockSpec((2, g, r_in, LANES), lambda j: (0, j, 0, 0)), full(f1), full(twr), full(twi), full(m2)],
        out_specs=pl.BlockSpec((g, nk, 2 * LANES), lambda j: (j, 0, 0)),
        out_shape=jax.ShapeDtypeStruct((c, nk, 2 * LANES), BF16),
        compiler_params=_cparams("parallel"),
        name="hyena_spectrum",
    )(kt4, f1, twr, twi, m2)


HY_CHUNK = 256


def _hyena_proj_body(x_ref, g_ref, sh_ref, sc_ref, w_ref, taps_ref, nxt_ref, u_ref, z0_ref, last_s, *, n_tiles, t, c):
    i = pl.program_id(1)

    @pl.when(i == 0)
    def _():
        last_s[...] = jnp.zeros_like(last_s)

    hx = _norm_mod(x_ref, g_ref, sh_ref, sc_ref)
    row = lax.broadcasted_iota(I32, (t, 1), 0)
    has_next = (i < n_tiles - 1).astype(F32)
    cw = min(HY_CHUNK, c)
    for j in range(c // cw):
        zs = []
        for k in range(3):
            cols = slice(k * c + j * cw, k * c + (j + 1) * cw)
            p = jnp.dot(hx, w_ref[:, cols], preferred_element_type=F32)
            up = jnp.where(row == 0, last_s[:, cols], pltpu.roll(p, 1, 0))
            dn = jnp.where(row == t - 1, nxt_ref[0, 0, 0:1, cols] * has_next, pltpu.roll(p, t - 1, 0))
            tp = taps_ref[:, cols]
            zs.append(tp[3:4] + tp[0:1] * up + tp[1:2] * p + tp[2:3] * dn)
            last_s[:, cols] = p[t - 1:t, :]
        z0, z1, zv = zs
        u_t, z0_t = (zv * z1).T, z0.T
        for q in range(t // LANES):
            u_ref[0, q, j * cw:(j + 1) * cw, :] = u_t[:, q * LANES:(q + 1) * LANES].astype(u_ref.dtype)
            z0_ref[0, q, j * cw:(j + 1) * cw, :] = z0_t[:, q * LANES:(q + 1) * LANES].astype(z0_ref.dtype)


def hyena_proj(x, g, shift, scale, w, taps, *, tile_l):
    b, l, d = x.shape
    c = w.shape[1] // 3
    t = min(tile_l, l)
    n_tiles = l // t
    rq = t // LANES
    nxt = proj_first_rows(x, g, shift, scale, w, tile_l=tile_l)
    o_spec = pl.BlockSpec((1, rq, c, LANES), lambda bi, i: (bi, i, 0, 0))
    o_shape = jax.ShapeDtypeStruct((b, l // LANES, c, LANES), BF16)
    return pl.pallas_call(
        functools.partial(_hyena_proj_body, n_tiles=n_tiles, t=t, c=c),
        grid=(b, n_tiles),
        in_specs=[pl.BlockSpec((1, t, d), lambda bi, i: (bi, i, 0)),
                  pl.BlockSpec((1, d), lambda bi, i: (0, 0)),
                  pl.BlockSpec((1, 1, d), lambda bi, i: (bi, 0, 0)),
                  pl.BlockSpec((1, 1, d), lambda bi, i: (bi, 0, 0)),
                  pl.BlockSpec(w.shape, lambda bi, i: (0, 0)),
                  pl.BlockSpec(taps.shape, lambda bi, i: (0, 0)),
                  pl.BlockSpec((1, 1, SUBLANES, 3 * c), lambda bi, i: (bi, jnp.minimum(i + 1, n_tiles - 1), 0, 0))],
        out_specs=[o_spec, o_spec],
        out_shape=[o_shape, o_shape],
        scratch_shapes=[pltpu.VMEM((1, 3 * c), F32)],
        compiler_params=_cparams("parallel", "arbitrary"),
        name="hyena_proj",
    )(x, g, shift, scale, w, taps, nxt)


def _fftconv_body(u_ref, z0_ref, skip_ref, k_ref, f1_ref, twr_ref, twi_ref, m2_ref, m2i_ref, gi_ref,
                  o_ref, u_s, z0_s, y_s, *, ct, g, nk, r_in, pitch):
    for r in range(r_in):
        u_s[r * pitch:r * pitch + ct, :] = u_ref[0, r].astype(F32)
        z0_s[r * pitch:r * pitch + ct, :] = z0_ref[0, r].astype(F32)
    f1, twr, twi = f1_ref[...], twr_ref[...], twi_ref[...]

    def chan(ref, ch):
        return ref[pl.ds(ch, r_in, stride=pitch), :]

    def rows_fwd(c0):
        a2 = []
        for ci in range(0, g, 2):
            pair = _fwd_rows_twiddle(chan(u_s, c0 + ci).astype(BF16), chan(u_s, c0 + ci + 1).astype(BF16),
                                     f1, twr, twi, nk)
            a2 += [jnp.concatenate([tre, tim], axis=1).astype(BF16) for tre, tim in pair]
        return jnp.concatenate(a2, axis=0)

    def lanes_fwd(a2):
        return jnp.dot(a2, m2_ref[...], preferred_element_type=F32)

    def times_filter(c0, spec):
        kf = k_ref[pl.ds(c0, g)].astype(F32).reshape(g * nk, 2 * LANES)
        sre, sim = spec[:, :LANES], spec[:, LANES:]
        kre, kim = kf[:, :LANES], kf[:, LANES:]
        return jnp.concatenate([sre * kre - sim * kim, sre * kim + sim * kre], axis=1).astype(BF16)

    def lanes_inv(prod):
        return jnp.dot(prod, m2i_ref[...], preferred_element_type=F32)

    def rows_inv(c0, cc):
        for ci in range(0, g, 2):
            st = []
            for h in range(2):
                blk = cc[(ci + h) * nk:(ci + h + 1) * nk]
                cre, cim = blk[:, :LANES], blk[:, LANES:]
                st.append(jnp.concatenate([cre * twr + cim * twi, cim * twr - cre * twi], axis=0).astype(BF16))
            y2 = jnp.dot(gi_ref[...], jnp.concatenate(st, axis=1), preferred_element_type=F32)
            for h in range(2):
                ch = c0 + ci + h
                y = y2[:, h * LANES:(h + 1) * LANES]
                y_s[pl.ds(ch, r_in, stride=pitch), :] = (y + chan(u_s, ch) * skip_ref[ch]) * chan(z0_s, ch)

    def two_groups(i, _):
        ca, cb = 2 * g * i, 2 * g * i + g
        a2_a = rows_fwd(ca)
        spec_a = lanes_fwd(a2_a)
        a2_b = rows_fwd(cb)
        prod_a = times_filter(ca, spec_a)
        spec_b = lanes_fwd(a2_b)
        cc_a = lanes_inv(prod_a)
        prod_b = times_filter(cb, spec_b)
        cc_b = lanes_inv(prod_b)
        rows_inv(ca, cc_a)
        rows_inv(cb, cc_b)
        return 0

    lax.fori_loop(0, ct // (2 * g), two_groups, 0)
    for r in range(r_in):
        o_ref[0, r] = y_s[r * pitch:r * pitch + ct, :].astype(o_ref.dtype)


def hyena_fftconv(u, z0, skip3, spec, tables, *, tile_c, group):
    b, r_in, c, _ = u.shape
    nk = 2 * r_in
    ct = min(tile_c, c)
    g = min(group, ct // 2)
    assert ct % (2 * g) == 0 and g % 2 == 0
    pitch = ct + SUBLANES
    f1, twr, twi, m2, m2i, gi = tables
    full = lambda a: pl.BlockSpec(a.shape, lambda j, bi: (0,) * a.ndim)
    io_spec = pl.BlockSpec((1, r_in, ct, LANES), lambda j, bi: (bi, 0, j, 0))
    return pl.pallas_call(
        functools.partial(_fftconv_body, ct=ct, g=g, nk=nk, r_in=r_in, pitch=pitch),
        grid=(c // ct, b),
        in_specs=[io_spec, io_spec,
                  pl.BlockSpec((ct, 1, 1), lambda j, bi: (j, 0, 0)),
                  pl.BlockSpec((ct, nk, 2 * LANES), lambda j, bi: (j, 0, 0)),
                  full(f1), full(twr), full(twi), full(m2), full(m2i), full(gi)],
        out_specs=io_spec,
        out_shape=jax.ShapeDtypeStruct((b, r_in, c, LANES), BF16),
        scratch_shapes=[pltpu.VMEM((r_in * pitch, LANES), F32) for _ in range(3)],
        compiler_params=_cparams("parallel", "arbitrary"),
        name="hyena_fftconv",
    )(u, z0, skip3, spec, f1, twr, twi, m2, m2i, gi)


ROUTE_LANES = LANES
NEG_BIG = -1e30
HALF_WORD = 16


def _pack_bf16_pairs(v):
    h = v.shape[1] // 2
    bits = pltpu.bitcast(v.astype(BF16).astype(F32), I32)
    return bits[:, :h] | lax.shift_right_logical(bits[:, h:], HALF_WORD)


def _unpack_bf16_pairs(w):
    hi = pltpu.bitcast(w & jnp.int32(-65536), F32)
    lo = pltpu.bitcast(lax.shift_left(w, HALF_WORD), F32)
    return jnp.concatenate([hi, lo], axis=1)


SLAB = 4


def _store_row_slabs(ref, words, row0=0):
    rows = words.shape[0]
    for j in range(SLAB):
        ref[pl.ds(SLAB * row0 + j, rows, stride=SLAB), :] = words[:, j * LANES:(j + 1) * LANES]


def _load_row_slabs(ref, rows=None, row0=0):
    rows = ref.shape[0] // SLAB if rows is None else rows
    return jnp.concatenate([ref[pl.ds(SLAB * row0 + j, rows, stride=SLAB), :] for j in range(SLAB)], axis=1)


MIX_SUB = 512


def _mix_route_body(x_ref, hf_ref, hb_ref, prg_ref, pga_ref, pgb_ref, yt_ref, rgp_ref, hyp_ref, wo_ref, g1_ref,
                    n2g_ref, sh2_ref, sc2_ref, wr_ref, br_ref, tri_ref,
                    x1_ref, hxp_ref, route_ref, cnt_ref, carry_s, *, t, sub, n_exp):
    @pl.when((pl.program_id(0) == 0) & (pl.program_id(1) == 0))
    def _():
        carry_s[...] = jnp.zeros_like(carry_s)

    running = carry_s[...]
    for r0 in range(0, t, sub):
        rows = slice(r0, r0 + sub)
        hsum = hf_ref[0, rows, :].astype(F32) + hb_ref[0, rows, :].astype(F32)
        y_rg = (hsum * _gelu_tanh(prg_ref[0, rows, :].astype(F32))).astype(BF16)
        t1 = jnp.dot(y_rg, rgp_ref[...], preferred_element_type=F32)
        t2 = jnp.concatenate([lax.dot_general(yt_ref[0, q], hyp_ref[...], (((0,), (0,)), ((), ())),
                                              preferred_element_type=F32)
                              for q in range(r0 // LANES, (r0 + sub) // LANES)], axis=0)
        merged = _sigmoid(pga_ref[0, rows, :].astype(F32)) * t1 + _sigmoid(pgb_ref[0, rows, :].astype(F32)) * t2
        out = jnp.dot(merged.astype(BF16), wo_ref[...], preferred_element_type=F32)
        x1 = x_ref[0, rows, :] + g1_ref[0] * out
        x1_ref[0, rows, :] = x1
        ms = jnp.mean(x1 * x1, axis=-1, keepdims=True)
        hx2 = (x1 * lax.rsqrt(ms + EPS) * n2g_ref[...]) * (1.0 + sc2_ref[0]) + sh2_ref[0]
        _store_row_slabs(hxp_ref, _pack_bf16_pairs(hx2), r0)

        hx_hi = hx2.astype(BF16)
        hx_lo = (hx2 - hx_hi.astype(F32)).astype(BF16)
        parts = (jnp.dot(hx_hi, wr_ref[...], preferred_element_type=F32)
                 + jnp.dot(hx_lo, wr_ref[...], preferred_element_type=F32))
        logits = parts[:, :ROUTE_LANES] + parts[:, ROUTE_LANES:] + br_ref[...]
        lane = lax.broadcasted_iota(I32, (sub, ROUTE_LANES), 1)
        is_g = lane < N_GROUPS
        glog = jnp.where(is_g, logits, NEG_BIG)
        gmax = jnp.max(glog, axis=1, keepdims=True)
        gidx = jnp.min(jnp.where(glog == gmax, lane, ROUTE_LANES), axis=1, keepdims=True)
        gsum = jnp.sum(jnp.where(is_g, jnp.exp(glog - gmax), 0.0), axis=1, keepdims=True)
        p_g = 1.0 / gsum
        e_lane = lane - N_GROUPS
        grp_of_lane = lax.shift_right_arithmetic(e_lane, int(math.log2(EXPERTS_PER_GROUP)))
        in_grp = (e_lane >= 0) & (e_lane < n_exp) & (grp_of_lane == gidx)
        elog = jnp.where(in_grp, logits, NEG_BIG)
        m1 = jnp.max(elog, axis=1, keepdims=True)
        i1 = jnp.min(jnp.where(elog == m1, lane, ROUTE_LANES), axis=1, keepdims=True)
        elog2 = jnp.where(lane == i1, NEG_BIG, elog)
        m2 = jnp.max(elog2, axis=1, keepdims=True)
        i2 = jnp.min(jnp.where(elog2 == m2, lane, ROUTE_LANES), axis=1, keepdims=True)
        e21 = jnp.exp(m2 - m1)
        pk1 = 1.0 / (1.0 + e21)
        wt1, wt2 = p_g * pk1, p_g * (e21 * pk1)

        oh1 = (lane == i1 - N_GROUPS).astype(F32)
        oh2 = (lane == i2 - N_GROUPS).astype(F32)
        cnt = oh1 + oh2
        before = jnp.dot(tri_ref[...], cnt.astype(BF16), preferred_element_type=F32) + running
        rank1 = jnp.sum(oh1 * before, axis=1, keepdims=True)
        rank2 = jnp.sum(oh2 * before, axis=1, keepdims=True)
        running = running + jnp.sum(cnt, axis=0, keepdims=True)
        vals = ((i1 - N_GROUPS).astype(F32), (i2 - N_GROUPS).astype(F32), rank1, rank2, wt1, wt2)
        route = jnp.zeros((sub, ROUTE_LANES), F32)
        for k, v in enumerate(vals):
            route = jnp.where(lane == k, v, route)
        route_ref[rows, :] = route
    carry_s[...] = running
    cnt_ref[...] = running


def mix_route(x, h_f, h_b, p_rm, y_hy_t, rg_proj, hy_proj, w_out, g1, n2g, sh2, sc2, wr, br, *, tile_l, n_exp):
    b, l, d = x.shape
    c = h_f.shape[2]
    t = min(tile_l, l)
    nt = l // t
    n = b * l
    sub = min(MIX_SUB, t)
    tri = (jnp.arange(sub)[:, None] > jnp.arange(sub)[None, :]).astype(BF16)
    tok = lambda bi, i: (bi, i, 0)
    col = lambda k: (lambda bi, i: (bi, i, k))
    full2 = lambda a: pl.BlockSpec(a.shape, lambda bi, i: (0, 0))
    per_b = pl.BlockSpec((1, 1, d), lambda bi, i: (bi, 0, 0))
    row = lambda bi, i: (bi * nt + i, 0)
    return pl.pallas_call(
        functools.partial(_mix_route_body, t=t, sub=sub, n_exp=n_exp),
        grid=(b, nt),
        in_specs=[pl.BlockSpec((1, t, d), tok), pl.BlockSpec((1, t, c), tok), pl.BlockSpec((1, t, c), tok),
                  pl.BlockSpec((1, t, c), col(1)), pl.BlockSpec((1, t, c), col(2)), pl.BlockSpec((1, t, c), col(3)),
                  pl.BlockSpec((1, t // LANES, c, LANES), lambda bi, i: (bi, i, 0, 0)),
                  full2(rg_proj), full2(hy_proj), full2(w_out), per_b,
                  full2(n2g), per_b, per_b, full2(wr), full2(br), full2(tri)],
        out_specs=[pl.BlockSpec((1, t, d), tok),
                   pl.BlockSpec((t * SLAB, LANES), row),
                   pl.BlockSpec((t, ROUTE_LANES), row), pl.BlockSpec((1, ROUTE_LANES), lambda bi, i: (0, 0))],
        out_shape=[jax.ShapeDtypeStruct((b, l, d), F32), jax.ShapeDtypeStruct((n * SLAB, LANES), I32),
                   jax.ShapeDtypeStruct((n, ROUTE_LANES), F32), jax.ShapeDtypeStruct((1, ROUTE_LANES), F32)],
        scratch_shapes=[pltpu.VMEM((1, ROUTE_LANES), F32)],
        compiler_params=_cparams("arbitrary", "arbitrary"),
        name="mix_route",
    )(x, h_f, h_b, p_rm, p_rm, p_rm, y_hy_t, rg_proj, hy_proj, w_out, g1, n2g, sh2, sc2, wr, br, tri)


def _dest_body(route_ref, cnt_ref, ut_ref, dest_ref, blk_ref, *, t, n_exp, nb_pad):
    lane1 = lax.broadcasted_iota(I32, (1, ROUTE_LANES), 1)
    padded = jnp.floor((cnt_ref[...] + (MOE_BLOCK - 1.0)) * (1.0 / MOE_BLOCK)) * MOE_BLOCK
    padded = jnp.where(lane1 < n_exp, padded, 0.0)
    pend = jnp.dot(jnp.broadcast_to(padded, (SUBLANES, ROUTE_LANES)), ut_ref[...], precision=HIGHEST,
                   preferred_element_type=F32)[0:1]
    pstart = pend - padded
    route = route_ref[...]
    lane = lax.broadcasted_iota(I32, (t, ROUTE_LANES), 1)
    lf = lane.astype(F32)
    d1 = jnp.sum(jnp.where(lf == route[:, 0:1], pstart, 0.0), axis=1, keepdims=True) + route[:, 2:3]
    d2 = jnp.sum(jnp.where(lf == route[:, 1:2], pstart, 0.0), axis=1, keepdims=True) + route[:, 3:4]
    dmat = jnp.where(lane == 0, d1, jnp.where(lane == 1, d2, 0.0))
    dest_ref[...] = dmat.T[0:SUBLANES].astype(I32)
    first_row = lax.broadcasted_iota(I32, (nb_pad, ROUTE_LANES), 0).astype(F32) * float(MOE_BLOCK)
    lane_b = lax.broadcasted_iota(I32, (nb_pad, ROUTE_LANES), 1)
    nle = jnp.sum(jnp.where((lane_b < n_exp) & (pend <= first_row), 1.0, 0.0), axis=1, keepdims=True)
    e_blk = jnp.minimum(nle, n_exp - 1.0)
    mine = lane_b.astype(F32) == e_blk
    cnt_e = jnp.sum(jnp.where(mine, cnt_ref[...], 0.0), axis=1, keepdims=True)
    start_e = jnp.sum(jnp.where(mine, pstart, 0.0), axis=1, keepdims=True)
    valid = jnp.clip(cnt_e - (first_row - start_e), 0.0, float(MOE_BLOCK))
    blk_ref[...] = jnp.where(lane_b == 0, e_blk, jnp.where(lane_b == 1, valid, 0.0)).astype(I32)


def moe_dest(route, cnt, *, tile, n_exp, n_blocks):
    n = route.shape[0]
    t = min(tile, n)
    nb_pad = -(-n_blocks // SUBLANES) * SUBLANES
    ut = (jnp.arange(ROUTE_LANES)[:, None] <= jnp.arange(ROUTE_LANES)[None, :]).astype(F32)
    return pl.pallas_call(
        functools.partial(_dest_body, t=t, n_exp=n_exp, nb_pad=nb_pad),
        grid=(n // t,),
        in_specs=[pl.BlockSpec((t, ROUTE_LANES), lambda i: (i, 0)),
                  pl.BlockSpec((1, ROUTE_LANES), lambda i: (0, 0)),
                  pl.BlockSpec((ROUTE_LANES, ROUTE_LANES), lambda i: (0, 0))],
        out_specs=[pl.BlockSpec((SUBLANES, t), lambda i: (i, 0)),
                   pl.BlockSpec((nb_pad, ROUTE_LANES), lambda i: (0, 0))],
        out_shape=[jax.ShapeDtypeStruct((n // t * SUBLANES, t), I32),
                   jax.ShapeDtypeStruct((nb_pad, ROUTE_LANES), I32)],
        compiler_params=_cparams("arbitrary"),
        name="moe_dest",
    )(route, cnt, ut)


def _scatter_body(dest_ref, hx_ref, xb_in_ref, xb_ref, sem, *, t):
    del xb_in_ref

    def issue(r, _):
        for k in range(2):
            pltpu.make_async_copy(hx_ref.at[pl.ds(SLAB * r, SLAB)], xb_ref.at[pl.ds(SLAB * dest_ref[k, r], SLAB)],
                                  sem).start(priority=k)
        return 0

    lax.fori_loop(0, t, issue, 0, unroll=8)
    for k in range(2):
        pltpu.make_async_copy(hx_ref, xb_ref.at[pl.ds(0, SLAB * t)], sem).wait()


def moe_scatter(dest, hxp, n_rows, *, tile):
    n = hxp.shape[0] // SLAB
    t = min(tile, n)
    xb0 = jnp.zeros((n_rows * SLAB, LANES), I32)
    per_dest_tile = dest.shape[1] // t
    return pl.pallas_call(
        functools.partial(_scatter_body, t=t),
        grid=(n // t,),
        in_specs=[pl.BlockSpec((SUBLANES, t), lambda i: (i // per_dest_tile, i % per_dest_tile),
                               memory_space=pltpu.SMEM),
                  pl.BlockSpec((t * SLAB, LANES), lambda i: (i, 0)),
                  pl.BlockSpec(memory_space=pl.ANY)],
        out_specs=pl.BlockSpec(memory_space=pl.ANY),
        out_shape=jax.ShapeDtypeStruct((n_rows * SLAB, LANES), I32),
        scratch_shapes=[pltpu.SemaphoreType.DMA],
        input_output_aliases={2: 0},
        compiler_params=_cparams("arbitrary"),
        name="moe_scatter",
    )(dest, hxp, xb0)


def _expert_body(blk_ref, valid_ref, xb_ref, w1_ref, w3_ref, w2_ref, yb_ref, w1_s, w3_s, w2_s):
    i = pl.program_id(0)
    valid = valid_ref[i]
    half = MOE_BLOCK // 2
    changed = (i == 0) | (blk_ref[i] != blk_ref[jnp.maximum(i - 1, 0)])

    @pl.when(changed & (valid > 0))
    def _():
        w1_s[...] = w1_ref[0].astype(BF16)
        w3_s[...] = w3_ref[0].astype(BF16)
        w2_s[...] = w2_ref[0].astype(BF16)

    def run(rows):
        xblk = _unpack_bf16_pairs(_load_row_slabs(xb_ref, rows)).astype(BF16)
        h1 = jnp.dot(xblk, w1_s[...], preferred_element_type=F32)
        h3 = jnp.dot(xblk, w3_s[...], preferred_element_type=F32)
        hid = (h1 * _sigmoid(h1) * h3).astype(BF16)
        _store_row_slabs(yb_ref, _pack_bf16_pairs(jnp.dot(hid, w2_s[...], preferred_element_type=F32)))

    @pl.when(valid > half)
    def _():
        run(MOE_BLOCK)

    @pl.when(valid <= half)
    def _():
        yb_ref[...] = jnp.zeros_like(yb_ref)

    @pl.when((valid > 0) & (valid <= half))
    def _():
        run(half)


def moe_experts(blk_exp, blk_valid, xb, w1, w3, w2):
    p = xb.shape[0] // SLAB
    _, d, de = w1.shape
    nb = p // MOE_BLOCK
    grid_spec = pltpu.PrefetchScalarGridSpec(
        num_scalar_prefetch=2,
        grid=(nb,),
        in_specs=[pl.BlockSpec((MOE_BLOCK * SLAB, LANES), lambda i, blk, valid: (i, 0)),
                  pl.BlockSpec((1, d, de), lambda i, blk, valid: (blk[i], 0, 0)),
                  pl.BlockSpec((1, d, de), lambda i, blk, valid: (blk[i], 0, 0)),
                  pl.BlockSpec((1, de, d), lambda i, blk, valid: (blk[i], 0, 0))],
        out_specs=pl.BlockSpec((MOE_BLOCK * SLAB, LANES), lambda i, blk, valid: (i, 0)),
        scratch_shapes=[pltpu.VMEM((d, de), BF16), pltpu.VMEM((d, de), BF16), pltpu.VMEM((de, d), BF16)],
    )
    return pl.pallas_call(
        _expert_body,
        grid_spec=grid_spec,
        out_shape=jax.ShapeDtypeStruct((p * SLAB, LANES), I32),
        compiler_params=_cparams("arbitrary"),
        name="moe_experts",
    )(blk_exp, blk_valid, xb, w1, w3, w2)


COMBINE_PARTS = 4


def _combine_body(dest_ref, x1_ref, route_ref, g2_ref, fg_ref, yb_ref, o_ref, y1_s, y2_s, sems, *, t):
    tp = t // COMBINE_PARTS
    for part in range(COMBINE_PARTS):
        def issue(r, _, sem=sems.at[part]):
            pltpu.make_async_copy(yb_ref.at[pl.ds(SLAB * dest_ref[0, r], SLAB)], y1_s.at[pl.ds(SLAB * r, SLAB)],
                                  sem).start(priority=0)
            pltpu.make_async_copy(yb_ref.at[pl.ds(SLAB * dest_ref[1, r], SLAB)], y2_s.at[pl.ds(SLAB * r, SLAB)],
                                  sem).start(priority=1)
            return 0

        lax.fori_loop(part * tp, (part + 1) * tp, issue, 0, unroll=8)
    for part in range(COMBINE_PARTS):
        rows = slice(part * tp, (part + 1) * tp)
        lines = pl.ds(SLAB * part * tp, SLAB * tp)
        for y_s in (y1_s, y2_s):
            pltpu.make_async_copy(yb_ref.at[pl.ds(0, SLAB * tp)], y_s.at[lines], sems.at[part]).wait()
        route = route_ref[rows, :]
        moe = (route[:, 4:5] * _unpack_bf16_pairs(_load_row_slabs(y1_s, tp, part * tp))
               + route[:, 5:6] * _unpack_bf16_pairs(_load_row_slabs(y2_s, tp, part * tp)))
        x2 = x1_ref[0, rows, :] + g2_ref[0] * moe
        ms = jnp.mean(x2 * x2, axis=-1, keepdims=True)
        o_ref[0, rows, :] = x2 * lax.rsqrt(ms + EPS) * fg_ref[...]


def moe_combine(dest, x1, route, g2, final_g, yb, *, tile_l):
    b, l, d = x1.shape
    t = min(tile_l, l)
    nt = l // t
    slab = (t * SLAB, LANES)
    per_dest_tile = dest.shape[1] // t
    return pl.pallas_call(
        functools.partial(_combine_body, t=t),
        grid=(b, nt),
        in_specs=[pl.BlockSpec((SUBLANES, t),
                               lambda bi, i: ((bi * nt + i) // per_dest_tile, (bi * nt + i) % per_dest_tile),
                               memory_space=pltpu.SMEM),
                  pl.BlockSpec((1, t, d), lambda bi, i: (bi, i, 0)),
                  pl.BlockSpec((t, ROUTE_LANES), lambda bi, i: (bi * nt + i, 0)),
                  pl.BlockSpec((1, 1, d), lambda bi, i: (bi, 0, 0)),
                  pl.BlockSpec((1, d), lambda bi, i: (0, 0)),
                  pl.BlockSpec(memory_space=pl.ANY)],
        out_specs=pl.BlockSpec((1, t, d), lambda bi, i: (bi, i, 0)),
        out_shape=jax.ShapeDtypeStruct((b, l, d), F32),
        scratch_shapes=[pltpu.VMEM(slab, I32), pltpu.VMEM(slab, I32), pltpu.SemaphoreType.DMA((COMBINE_PARTS,))],
        compiler_params=_cparams("arbitrary", "arbitrary"),
        name="moe_combine",
    )(dest, x1, route, g2, final_g, yb)


def kernel(x, c, ctx, c_ctx, ada_w, ada_b, norm1_g, norm2_g, final_g, w_in, rg_conv_w, rg_conv_b, rg_wa_f, rg_ba_f, rg_wx_f, rg_bx_f, rg_lam_f, rg_wa_b, rg_ba_b, rg_wx_b, rg_bx_b, rg_lam_b, rg_proj, hy_conv_w, hy_conv_b, hy_pos_w1, hy_pos_b1, hy_pos_w2, hy_pos_b2, hy_freq, hy_pos_w3, hy_skip, hy_proj, w_out, moe_wg, moe_bg, moe_we, moe_be, moe_w1, moe_w3, moe_w2):
    B, L, D = x.shape
    C = rg_conv_w.shape[-1]
    LC = ctx.shape[1]
    c8 = jnp.zeros((8, D), F32).at[:B].set(c).at[B].set(c_ctx)
    mods = ada_mods(c8, ada_w[0], ada_b)
    sh1, sc1, g1 = (mods[:B, None, k * D:(k + 1) * D] for k in range(3))
    sh2, sc2, g2 = (mods[:B, None, k * D:(k + 1) * D] for k in range(3, 6))
    csh1 = jnp.broadcast_to(mods[B:B + 1, None, 0:D], (B, 1, D))
    csc1 = jnp.broadcast_to(mods[B:B + 1, None, D:2 * D], (B, 1, D))

    w_in_b = w_in[0].astype(BF16)
    w_rm = jnp.concatenate([w_in_b[:, :2 * C], w_in_b[:, 5 * C:]], axis=1)
    wg_f = gate_blocks(rg_wa_f[0], rg_wx_f[0], C // 256)
    wg_b = gate_blocks(rg_wa_b[0], rg_wx_b[0], C // 256)
    rg_f = (wg_f, rg_ba_f, rg_bx_f, rg_lam_f)
    rg_b = (wg_b, rg_ba_b, rg_bx_b, rg_lam_b)

    pc = norm_mod_proj(ctx, norm1_g, csh1, csc1, w_rm[:, :C], rg_conv_w[0], rg_conv_b, tile_l=LC, chunk=C)
    zero = jnp.zeros((B, 1, C), F32)
    _, hcf = rg_scan(pc, 0, *rg_f, zero, reverse=False, tile_l=TILE_SCAN)
    _, hcb = rg_scan(pc, 0, *rg_b, zero, reverse=True, tile_l=TILE_SCAN)

    p_rm = norm_mod_proj(x, norm1_g, sh1, sc1, w_rm, rg_conv_w[0], rg_conv_b, tile_l=TILE_PROJ, chunk=PROJ_CHUNK)
    hy_taps = jnp.concatenate([hy_conv_w[0], hy_conv_b], axis=0)
    u_hy, z0_hy = hyena_proj(x, norm1_g, sh1, sc1, w_in_b[:, 2 * C:5 * C], hy_taps, tile_l=TILE_PROJ)
    h_f, _ = rg_scan(p_rm, 0, *rg_f, hcf, reverse=False, tile_l=TILE_SCAN)
    h_b, _ = rg_scan(p_rm, 0, *rg_b, hcb, reverse=True, tile_l=TILE_SCAN)

    tables = dft_tables(L)
    assert hy_pos_w1.shape[1] == len(HY_FEATURE_ORDER)
    w1t = jnp.zeros((HY_HID, HY_HID), F32).at[:, :hy_pos_w1.shape[1]].set(hy_pos_w1[0].T[:, jnp.array(HY_FEATURE_ORDER)])
    kt = hyena_filter_t(w1t, hy_pos_b1[0][:, None], hy_pos_w2[0].T, hy_pos_b2[0][:, None], hy_freq[0][:, None],
                        hy_pos_w3[0].T.reshape(2, C, HY_HID).astype(BF16), L, FILTER_TILE_C)
    spec = hyena_spectrum(kt.reshape(2, C, L // LANES, LANES), tables, SPECTRUM_GROUP)
    y_hy_t = hyena_fftconv(u_hy, z0_hy, hy_skip[0][:, None, None], spec, tables, tile_c=FFT_TILE_C, group=FFT_GROUP)

    n_exp = moe_we.shape[-1]
    n_grp = moe_wg.shape[-1]
    assert n_grp == N_GROUPS and n_exp == N_GROUPS * EXPERTS_PER_GROUP
    wr = jnp.zeros((D, ROUTE_LANES), F32).at[:, :n_grp].set(moe_wg[0]).at[:, n_grp:n_grp + n_exp].set(moe_we[0])
    br = jnp.zeros((1, ROUTE_LANES), F32).at[:, :n_grp].set(moe_bg).at[:, n_grp:n_grp + n_exp].set(moe_be)
    wr_hi = wr.astype(BF16)
    wr_split = jnp.concatenate([wr_hi, (wr - wr_hi.astype(F32)).astype(BF16)], axis=1)
    x1, hxp, route, cnt = mix_route(x, h_f, h_b, p_rm, y_hy_t, rg_proj[0].astype(BF16), hy_proj[0].astype(BF16),
                                    w_out[0].astype(BF16), g1, norm2_g, sh2, sc2, wr_split, br, tile_l=TILE_MIX,
                                    n_exp=n_exp)

    n_blocks = (2 * B * L + n_exp * (MOE_BLOCK - 1)) // MOE_BLOCK
    dest, blk = moe_dest(route, cnt, tile=TILE_DEST, n_exp=n_exp, n_blocks=n_blocks)
    xb = moe_scatter(dest, hxp, n_blocks * MOE_BLOCK, tile=TILE_DISPATCH)
    yb = moe_experts(blk[:n_blocks, 0], blk[:n_blocks, 1], xb, moe_w1[0], moe_w3[0], moe_w2[0])
    return moe_combine(dest, x1, route, g2, final_g[None], yb, tile_l=TILE_DISPATCH)
```

```python
import functools
import math

import jax
import jax.numpy as jnp
from jax import lax
from jax.experimental import pallas as pl
from jax.experimental.pallas import tpu as pltpu

F32 = jnp.float32
BF16 = jnp.bfloat16
I32 = jnp.int32
HIGHEST = lax.Precision.HIGHEST

LANES = 128
SUBLANES = 8
EPS = 1e-6
RG_C = 8.0
RG_HEAD_DIM = 64
GRID_W = 64
HY_SEQ_BANDS = 16
HY_COL_BANDS = 8
HY_DECAY_TARGET = 1e-2
HY_FAST_DECAY = 0.3
HY_SLOW_DECAY = 1.5
N_GROUPS = 4
EXPERTS_PER_GROUP = 8
MOE_BLOCK = 512
VMEM_LIMIT = 56 * 1024 * 1024

TILE_PROJ = 512
PROJ_CHUNK = 1024
CONV_CHUNK = 256
TILE_SCAN = 512
TILE_MIX = 512
TILE_DEST = 2048
TILE_DISPATCH = 1024
FILTER_TILE_C = 256
SPECTRUM_GROUP = 32
FFT_TILE_C = 64
FFT_GROUP = 8


def _cparams(*sem):
    return pltpu.CompilerParams(dimension_semantics=sem, vmem_limit_bytes=VMEM_LIMIT)


def _sigmoid(x):
    return 0.5 * (jnp.tanh(0.5 * x) + 1.0)


def _gelu_tanh(x):
    c = math.sqrt(2.0 / math.pi)
    return 0.5 * x * (1.0 + jnp.tanh(c * (x + 0.044715 * (x * x * x))))


def _ada_body(c_ref, w_ref, b_ref, o_ref):
    c = c_ref[...]
    s = c * _sigmoid(c)
    o_ref[...] = jnp.dot(s, w_ref[...], precision=HIGHEST, preferred_element_type=F32) + b_ref[...]


def ada_mods(c8, ada_w, ada_b):
    d, m = ada_w.shape
    tn = 1024 if m % 1024 == 0 else m
    return pl.pallas_call(
        _ada_body,
        grid=(m // tn,),
        in_specs=[pl.BlockSpec((c8.shape[0], d), lambda j: (0, 0)),
                  pl.BlockSpec((d, tn), lambda j: (0, j)),
                  pl.BlockSpec((1, tn), lambda j: (0, j))],
        out_specs=pl.BlockSpec((c8.shape[0], tn), lambda j: (0, j)),
        out_shape=jax.ShapeDtypeStruct((c8.shape[0], m), F32),
        compiler_params=_cparams("parallel"),
        name="ada_mods",
    )(c8, ada_w, ada_b)


def _norm_mod(x_ref, g_ref, sh_ref, sc_ref):
    x = x_ref[0]
    ms = jnp.mean(x * x, axis=-1, keepdims=True)
    y = x * lax.rsqrt(ms + EPS) * g_ref[...]
    return (y * (1.0 + sc_ref[0]) + sh_ref[0]).astype(BF16)


def _first_rows_body(x_ref, g_ref, sh_ref, sc_ref, w_ref, o_ref):
    nt, rows, d = x_ref.shape[1:]
    x = x_ref[0].reshape(nt * rows, d)
    ms = jnp.mean(x * x, axis=-1, keepdims=True)
    y = x * lax.rsqrt(ms + EPS) * g_ref[...]
    hx = (y * (1.0 + sc_ref[0]) + sh_ref[0]).astype(BF16)
    o_ref[0] = jnp.dot(hx, w_ref[...], preferred_element_type=F32).reshape(nt, rows, w_ref.shape[1])


def proj_first_rows(x, g, shift, scale, w, *, tile_l):
    b, l, d = x.shape
    t = min(tile_l, l)
    nt = l // t
    m = w.shape[1]
    return pl.pallas_call(
        _first_rows_body,
        grid=(b,),
        in_specs=[pl.BlockSpec((1, nt, SUBLANES, d), lambda bi: (bi, 0, 0, 0)),
                  pl.BlockSpec((1, d), lambda bi: (0, 0)),
                  pl.BlockSpec((1, 1, d), lambda bi: (bi, 0, 0)),
                  pl.BlockSpec((1, 1, d), lambda bi: (bi, 0, 0)),
                  pl.BlockSpec(w.shape, lambda bi: (0, 0))],
        out_specs=pl.BlockSpec((1, nt, SUBLANES, m), lambda bi: (bi, 0, 0, 0)),
        out_shape=jax.ShapeDtypeStruct((b, nt, SUBLANES, m), F32),
        compiler_params=_cparams("parallel"),
        name="proj_first_rows",
    )(x.reshape(b, nt, t, d), g, shift, scale, w)


def _proj_body(x_ref, g_ref, sh_ref, sc_ref, w_ref, cw_ref, cb_ref, nxt_ref, o_ref, last_s, ext_s,
               *, chunk, n_tiles, t, c):
    i = pl.program_id(1)

    @pl.when(i == 0)
    def _():
        last_s[...] = jnp.zeros_like(last_s)

    hx = _norm_mod(x_ref, g_ref, sh_ref, sc_ref)
    has_next = (i < n_tiles - 1).astype(F32)
    cw = cw_ref[...]
    cc = ext_s.shape[1]
    for j in range(c // cc):
        cols = slice(j * cc, (j + 1) * cc)
        p = jnp.dot(hx, w_ref[:, cols], preferred_element_type=F32)
        ext_s[0:SUBLANES, :] = last_s[:, cols]
        ext_s[SUBLANES:SUBLANES + t, :] = p
        ext_s[SUBLANES + t:2 * SUBLANES + t, :] = nxt_ref[0, 0, :, cols] * has_next
        xc = cb_ref[:, cols] + cw[2:3, cols] * p
        for tap, off in ((0, -2), (1, -1), (3, 1)):
            xc = xc + cw[tap:tap + 1, cols] * ext_s[SUBLANES + off:SUBLANES + off + t, :]
        o_ref[0, :, cols] = xc.astype(o_ref.dtype)
        last_s[:, cols] = p[t - SUBLANES:t, :]
    m = w_ref.shape[1]
    for j in range((m - c) // chunk):
        cols = slice(c + j * chunk, c + (j + 1) * chunk)
        o_ref[0, :, cols] = jnp.dot(hx, w_ref[:, cols], preferred_element_type=F32).astype(o_ref.dtype)


def norm_mod_proj(x, g, shift, scale, w, conv_w, conv_b, *, tile_l, chunk):
    b, l, d = x.shape
    m = w.shape[1]
    c = conv_w.shape[1]
    tl = min(tile_l, l)
    n_tiles = l // tl
    chunk = min(chunk, max(m - c, 1))
    assert (m - c) % chunk == 0
    nxt = proj_first_rows(x, g, shift, scale, w[:, :c], tile_l=tile_l)
    return pl.pallas_call(
        functools.partial(_proj_body, chunk=chunk, n_tiles=n_tiles, t=tl, c=c),
        grid=(b, n_tiles),
        in_specs=[pl.BlockSpec((1, tl, d), lambda bi, i: (bi, i, 0)),
                  pl.BlockSpec((1, d), lambda bi, i: (0, 0)),
                  pl.BlockSpec((1, 1, d), lambda bi, i: (bi, 0, 0)),
                  pl.BlockSpec((1, 1, d), lambda bi, i: (bi, 0, 0)),
                  pl.BlockSpec(w.shape, lambda bi, i: (0, 0)),
                  pl.BlockSpec(conv_w.shape, lambda bi, i: (0, 0)),
                  pl.BlockSpec(conv_b.shape, lambda bi, i: (0, 0)),
                  pl.BlockSpec((1, 1, SUBLANES, c), lambda bi, i: (bi, jnp.minimum(i + 1, n_tiles - 1), 0, 0))],
        out_specs=pl.BlockSpec((1, tl, m), lambda bi, i: (bi, i, 0)),
        out_shape=jax.ShapeDtypeStruct((b, l, m), BF16),
        scratch_shapes=[pltpu.VMEM((SUBLANES, c), F32), pltpu.VMEM((tl + 2 * SUBLANES, min(CONV_CHUNK, c)), F32)],
        compiler_params=_cparams("parallel", "arbitrary"),
        name="norm_mod_proj",
    )(x, g, shift, scale, w, conv_w, conv_b, nxt)


def _scan_body(xc_ref, wg_ref, ba_ref, bx_ref, lam_ref, h0_ref,
               h_ref, hl_ref, xc_s, g_s, a_s, b_s, hloc_s, pcum_s, carry_s,
               *, reverse, t, c, s_len, pitch):
    i = pl.program_id(1)
    n_slab = c // LANES
    n_blk = wg_ref.shape[0]
    blk = c // n_blk

    @pl.when(i == 0)
    def _():
        carry_s[...] = h0_ref[0]

    xc_s[...] = xc_ref[0].astype(F32)

    for k in range(n_blk):
        g_s[:, k * 2 * blk:(k + 1) * 2 * blk] = jnp.dot(xc_ref[0, :, k * blk:(k + 1) * blk], wg_ref[k],
                                                          preferred_element_type=F32)

    lam = lam_ref[...]
    softplus_neg_lam = jnp.maximum(-lam, 0.0) + jnp.log1p(jnp.exp(-jnp.abs(lam)))
    half_ca = (-0.5 * RG_C) * softplus_neg_lam
    half_ba, half_bx = 0.5 * ba_ref[...], 0.5 * bx_ref[...]
    slabs_per_blk = blk // LANES
    for j in range(SUBLANES):
        r0 = j * s_len
        for k in range(n_slab):
            kb, ks = k // slabs_per_blk, k % slabs_per_blk
            ga = g_s[r0:r0 + s_len, kb * 2 * blk + ks * LANES:kb * 2 * blk + (ks + 1) * LANES]
            gx = g_s[r0:r0 + s_len, kb * 2 * blk + blk + ks * LANES:kb * 2 * blk + blk + (ks + 1) * LANES]
            lane = slice(k * LANES, (k + 1) * LANES)
            half_x = 0.5 * xc_s[r0:r0 + s_len, lane]
            hca = half_ca[:, lane]
            log_a = hca * jnp.tanh(ga + half_ba[:, lane]) + hca
            gated_x = half_x * jnp.tanh(gx + half_bx[:, lane]) + half_x
            a = jnp.exp(log_a)
            a_s[k, j * pitch:j * pitch + s_len, :] = a
            gain2 = -jnp.tanh(log_a) * (a * a + 1.0)
            gain = jnp.where(gain2 > 0.0, gain2 * lax.rsqrt(gain2), 0.0)
            b_s[k, j * pitch:j * pitch + s_len, :] = gain * gated_x

    def step1(s, hp):
        hs, ps = hp
        srow = (s_len - 1 - s) if reverse else s
        hs2, ps2 = [], []
        for k in range(n_slab):
            av = a_s[k, pl.ds(srow, SUBLANES, stride=pitch), :]
            bv = b_s[k, pl.ds(srow, SUBLANES, stride=pitch), :]
            h = av * hs[k] + bv
            p = av * ps[k]
            hloc_s[k, pl.ds(srow, SUBLANES, stride=pitch), :] = h
            pcum_s[k, pl.ds(srow, SUBLANES, stride=pitch), :] = p
            hs2.append(h)
            ps2.append(p)
        return tuple(hs2), tuple(ps2)

    zeros = tuple(jnp.zeros((SUBLANES, LANES), F32) for _ in range(n_slab))
    ones = tuple(jnp.ones((SUBLANES, LANES), F32) for _ in range(n_slab))
    h_end, p_end = lax.fori_loop(0, s_len, step1, (zeros, ones), unroll=4)

    order = range(SUBLANES - 1, -1, -1) if reverse else range(SUBLANES)
    for k in range(n_slab):
        cst = carry_s[:, k * LANES:(k + 1) * LANES]
        for j in order:
            rows = slice(j * pitch, j * pitch + s_len)
            h_ref[0, j * s_len:(j + 1) * s_len, k * LANES:(k + 1) * LANES] = (
                hloc_s[k, rows, :] + pcum_s[k, rows, :] * cst).astype(h_ref.dtype)
            cst = p_end[k][j:j + 1] * cst + h_end[k][j:j + 1]
        carry_s[:, k * LANES:(k + 1) * LANES] = cst
    hl_ref[0] = carry_s[...]


def rg_scan(p, col_blk, wg, ba, bx, lam, h0, *, reverse, tile_l):
    b, l, _ = p.shape
    c = ba.shape[1]
    t = min(tile_l, l)
    n_tiles = l // t
    s_len = t // SUBLANES
    pitch = s_len + SUBLANES

    def nat(i):
        return (n_tiles - 1 - i) if reverse else i

    body = functools.partial(_scan_body, reverse=reverse, t=t, c=c, s_len=s_len, pitch=pitch)
    vec = pl.BlockSpec((1, c), lambda bi, i: (0, 0))
    return pl.pallas_call(
        body,
        grid=(b, n_tiles),
        in_specs=[pl.BlockSpec((1, t, c), lambda bi, i: (bi, nat(i), col_blk)),
                  pl.BlockSpec(wg.shape, lambda bi, i: (0, 0, 0)),
                  vec, vec, vec,
                  pl.BlockSpec((1, 1, c), lambda bi, i: (bi, 0, 0))],
        out_specs=[pl.BlockSpec((1, t, c), lambda bi, i: (bi, nat(i), 0)),
                   pl.BlockSpec((1, 1, c), lambda bi, i: (bi, 0, 0))],
        out_shape=[jax.ShapeDtypeStruct((b, l, c), BF16), jax.ShapeDtypeStruct((b, 1, c), F32)],
        scratch_shapes=[pltpu.VMEM((t, c), F32), pltpu.VMEM((t, 2 * c), F32)]
        + [pltpu.VMEM((c // LANES, SUBLANES * pitch, LANES), F32) for _ in range(4)]
        + [pltpu.VMEM((1, c), F32)],
        compiler_params=_cparams("parallel", "arbitrary"),
        name="rg_scan_bwd" if reverse else "rg_scan_fwd",
    )(p, wg, ba, bx, lam, h0)


def gate_blocks(wa, wx, n_blk):
    h, d, _ = wa.shape
    hp = h // n_blk
    eye = jnp.eye(hp, dtype=wa.dtype)

    def bd(w):
        w = w.reshape(n_blk, hp, d, d)
        return jnp.einsum('khde,hg->khdge', w, eye).reshape(n_blk, hp * d, hp * d)

    return (0.5 * jnp.concatenate([bd(wa), bd(wx)], axis=-1)).astype(BF16)


HY_HID = 64
HY_FEATURE_ORDER = (list(range(1, 1 + 2 * HY_SEQ_BANDS))
                    + list(range(2 + 2 * HY_SEQ_BANDS, 2 + 2 * HY_SEQ_BANDS + 2 * HY_COL_BANDS))
                    + [0, 1 + 2 * HY_SEQ_BANDS])


def _filter_body(w1t_ref, b1_ref, w2t_ref, b2_ref, fr_ref, w3t_ref, o_ref, z_s, *, l, c, ct, rows_grid):
    d = pl.program_id(0)
    j = pl.program_id(1)
    lane = lax.broadcasted_iota(I32, (1, l), 1)
    s_i = jnp.where(d == 0, lane, l - lane)
    sf = s_i.astype(F32)
    t_norm = sf / float(max(l - 1, 1))

    @pl.when(j == 0)
    def _():
        band_step = (HY_SEQ_BANDS - 1 - 1e-4) / (HY_SEQ_BANDS - 1)
        seq_band = 1e-4 + band_step * lax.broadcasted_iota(I32, (HY_SEQ_BANDS, 1), 0).astype(F32)
        col_band = 1.0 + lax.broadcasted_iota(I32, (HY_COL_BANDS, 1), 0).astype(F32)
        col_pos = (s_i & (GRID_W - 1)).astype(F32)
        row_lag = (s_i >> int(math.log2(GRID_W))).astype(F32) / float(rows_grid)
        ang_seq = ((2.0 * math.pi / l) * sf) * seq_band
        ang_col = ((2.0 * math.pi / GRID_W) * col_pos) * col_band
        n_trig = 2 * HY_SEQ_BANDS + 2 * HY_COL_BANDS
        trow = lax.broadcasted_iota(I32, (HY_HID - n_trig, 1), 0)
        tail = jnp.where(trow == 0, t_norm, jnp.where(trow == 1, row_lag, 0.0))
        feats = jnp.concatenate([jnp.cos(ang_seq), jnp.sin(ang_seq), jnp.cos(ang_col), jnp.sin(ang_col), tail], axis=0)
        fr = fr_ref[...]
        z = jnp.sin(fr * (jnp.dot(w1t_ref[...], feats, precision=HIGHEST, preferred_element_type=F32) + b1_ref[...]))
        z_s[...] = jnp.sin(fr * (jnp.dot(w2t_ref[...], z, precision=HIGHEST, preferred_element_type=F32) + b2_ref[...]))

    k = jnp.dot(w3t_ref[0], z_s[...].astype(BF16), preferred_element_type=F32)
    ch = (lax.broadcasted_iota(I32, (ct, 1), 0) + j * ct).astype(F32)
    max_decay = math.log(HY_DECAY_TARGET) / HY_FAST_DECAY
    min_decay = math.log(HY_DECAY_TARGET) / HY_SLOW_DECAY
    delta = jnp.abs(min_decay + ch * ((max_decay - min_decay) / (c - 1)))
    k = k * jnp.exp(-t_norm * delta)
    k = jnp.where((d == 1) & (lane == 0), 0.0, k)
    o_ref[0] = k.astype(o_ref.dtype)


def hyena_filter_t(w1t, b1, w2t, b2, fr, w3t, l, tile_c):
    assert GRID_W & (GRID_W - 1) == 0
    c = w3t.shape[1]
    ct = min(tile_c, c)
    body = functools.partial(_filter_body, l=l, c=c, ct=ct, rows_grid=l // GRID_W)
    small = lambda shape: pl.BlockSpec(shape, lambda d, j: (0,) * len(shape))
    return pl.pallas_call(
        body,
        grid=(2, c // ct),
        in_specs=[small(w1t.shape), small(b1.shape), small(w2t.shape), small(b2.shape), small(fr.shape),
                  pl.BlockSpec((1, ct, HY_HID), lambda d, j: (d, j, 0))],
        out_specs=pl.BlockSpec((1, ct, l), lambda d, j: (d, j, 0)),
        out_shape=jax.ShapeDtypeStruct((2, c, l), BF16),
        scratch_shapes=[pltpu.VMEM((HY_HID, l), F32)],
        compiler_params=_cparams("arbitrary", "arbitrary"),
        name="hyena_filter",
    )(w1t, b1, w2t, b2, fr, w3t)


def dft_tables(l):
    import numpy as np
    n = 2 * l
    r_in, nk = l // LANES, n // LANES
    ka = np.arange(nk)[:, None].astype(np.float64)
    r = np.arange(r_in)[None, :].astype(np.float64)
    a1 = 2.0 * np.pi * ka * r / nk
    f1 = np.concatenate([np.cos(a1), -np.sin(a1)], axis=0)
    lane = np.arange(LANES)[None, :].astype(np.float64)
    at = 2.0 * np.pi * ka * lane / n
    twr, twi = np.cos(at), -np.sin(at)
    a2 = 2.0 * np.pi * np.arange(LANES)[:, None] * np.arange(LANES)[None, :] / LANES
    cr, ci = np.cos(a2), -np.sin(a2)
    m2 = np.block([[cr, ci], [-ci, cr]])
    m2i = np.block([[cr, -ci], [ci, cr]])
    ai = 2.0 * np.pi * np.arange(r_in)[:, None] * np.arange(nk)[None, :] / nk
    gi = np.concatenate([np.cos(ai), -np.sin(ai)], axis=1) / n
    as_bf = lambda a: jnp.asarray(a, F32).astype(BF16)
    return as_bf(f1), jnp.asarray(twr, F32), jnp.asarray(twi, F32), as_bf(m2), as_bf(m2i), as_bf(gi)


def _fwd_rows_twiddle(x_a, x_b, f1, twr, twi, nk):
    a = jnp.dot(f1, jnp.concatenate([x_a, x_b], axis=1), preferred_element_type=F32)
    out = []
    for h in range(2):
        re, im = a[:nk, h * LANES:(h + 1) * LANES], a[nk:, h * LANES:(h + 1) * LANES]
        out.append((re * twr - im * twi, re * twi + im * twr))
    return out


def _spectrum_body(k_ref, f1_ref, twr_ref, twi_ref, m2_ref, o_ref, *, g, nk, r_in):
    f1, twr, twi = f1_ref[...], twr_ref[...], twi_ref[...]
    sign = jnp.where((lax.broadcasted_iota(I32, (nk, 1), 0) & 1) == 0, 1.0, -1.0)
    a2 = []
    for ci in range(g):
        (fre, fim), (bre, bim) = _fwd_rows_twiddle(k_ref[0, ci], k_ref[1, ci], f1, twr, twi, nk)
        a2.append(jnp.concatenate([fre + sign * bre, fim + sign * bim], axis=1).astype(BF16))
    spec = jnp.dot(jnp.concatenate(a2, axis=0), m2_ref[...], preferred_element_type=F32)
    o_ref[...] = spec.reshape(g, nk, 2 * LANES).astype(o_ref.dtype)


def hyena_spectrum(kt4, tables, group):
    _, c, r_in, _ = kt4.shape
    nk = 2 * r_in
    f1, twr, twi, m2, _, _ = tables
    g = min(group, c)
    full = lambda a: pl.BlockSpec(a.shape, lambda j: (0,) * a.ndim)
    return pl.pallas_call(
        functools.partial(_spectrum_body, g=g, nk=nk, r_in=r_in),
        grid=(c // g,),
        in_specs=[pl.BlockSpec((2, g, r_in, LANES), lambda j: (0, j, 0, 0)), full(f1), full(twr), full(twi), full(m2)],
        out_specs=pl.BlockSpec((g, nk, 2 * LANES), lambda j: (j, 0, 0)),
        out_shape=jax.ShapeDtypeStruct((c, nk, 2 * LANES), BF16),
        compiler_params=_cparams("parallel"),
        name="hyena_spectrum",
    )(kt4, f1, twr, twi, m2)


HY_CHUNK = 256


def _hyena_proj_body(x_ref, g_ref, sh_ref, sc_ref, w_ref, taps_ref, nxt_ref, u_ref, z0_ref, last_s, *, n_tiles, t, c):
    i = pl.program_id(1)

    @pl.when(i == 0)
    def _():
        last_s[...] = jnp.zeros_like(last_s)

    hx = _norm_mod(x_ref, g_ref, sh_ref, sc_ref)
    row = lax.broadcasted_iota(I32, (t, 1), 0)
    has_next = (i < n_tiles - 1).astype(F32)
    cw = min(HY_CHUNK, c)
    for j in range(c // cw):
        zs = []
        for k in range(3):
            cols = slice(k * c + j * cw, k * c + (j + 1) * cw)
            p = jnp.dot(hx, w_ref[:, cols], preferred_element_type=F32)
            up = jnp.where(row == 0, last_s[:, cols], pltpu.roll(p, 1, 0))
            dn = jnp.where(row == t - 1, nxt_ref[0, 0, 0:1, cols] * has_next, pltpu.roll(p, t - 1, 0))
            tp = taps_ref[:, cols]
            zs.append(tp[3:4] + tp[0:1] * up + tp[1:2] * p + tp[2:3] * dn)
            last_s[:, cols] = p[t - 1:t, :]
        z0, z1, zv = zs
        u_t, z0_t = (zv * z1).T, z0.T
        for q in range(t // LANES):
            u_ref[0, q, j * cw:(j + 1) * cw, :] = u_t[:, q * LANES:(q + 1) * LANES].astype(u_ref.dtype)
            z0_ref[0, q, j * cw:(j + 1) * cw, :] = z0_t[:, q * LANES:(q + 1) * LANES].astype(z0_ref.dtype)


def hyena_proj(x, g, shift, scale, w, taps, *, tile_l):
    b, l, d = x.shape
    c = w.shape[1] // 3
    t = min(tile_l, l)
    n_tiles = l // t
    rq = t // LANES
    nxt = proj_first_rows(x, g, shift, scale, w, tile_l=tile_l)
    o_spec = pl.BlockSpec((1, rq, c, LANES), lambda bi, i: (bi, i, 0, 0))
    o_shape = jax.ShapeDtypeStruct((b, l // LANES, c, LANES), BF16)
    return pl.pallas_call(
        functools.partial(_hyena_proj_body, n_tiles=n_tiles, t=t, c=c),
        grid=(b, n_tiles),
        in_specs=[pl.BlockSpec((1, t, d), lambda bi, i: (bi, i, 0)),
                  pl.BlockSpec((1, d), lambda bi, i: (0, 0)),
                  pl.BlockSpec((1, 1, d), lambda bi, i: (bi, 0, 0)),
                  pl.BlockSpec((1, 1, d), lambda bi, i: (bi, 0, 0)),
                  pl.BlockSpec(w.shape, lambda bi, i: (0, 0)),
                  pl.BlockSpec(taps.shape, lambda bi, i: (0, 0)),
                  pl.BlockSpec((1, 1, SUBLANES, 3 * c), lambda bi, i: (bi, jnp.minimum(i + 1, n_tiles - 1), 0, 0))],
        out_specs=[o_spec, o_spec],
        out_shape=[o_shape, o_shape],
        scratch_shapes=[pltpu.VMEM((1, 3 * c), F32)],
        compiler_params=_cparams("parallel", "arbitrary"),
        name="hyena_proj",
    )(x, g, shift, scale, w, taps, nxt)


def _fftconv_body(u_ref, z0_ref, skip_ref, k_ref, f1_ref, twr_ref, twi_ref, m2_ref, m2i_ref, gi_ref,
                  o_ref, u_s, z0_s, y_s, *, ct, g, nk, r_in, pitch):
    for r in range(r_in):
        u_s[r * pitch:r * pitch + ct, :] = u_ref[0, r].astype(F32)
        z0_s[r * pitch:r * pitch + ct, :] = z0_ref[0, r].astype(F32)
    f1, twr, twi = f1_ref[...], twr_ref[...], twi_ref[...]

    def chan(ref, ch):
        return ref[pl.ds(ch, r_in, stride=pitch), :]

    def rows_fwd(c0):
        a2 = []
        for ci in range(0, g, 2):
            pair = _fwd_rows_twiddle(chan(u_s, c0 + ci).astype(BF16), chan(u_s, c0 + ci + 1).astype(BF16),
                                     f1, twr, twi, nk)
            a2 += [jnp.concatenate([tre, tim], axis=1).astype(BF16) for tre, tim in pair]
        return jnp.concatenate(a2, axis=0)

    def lanes_fwd(a2):
        return jnp.dot(a2, m2_ref[...], preferred_element_type=F32)

    def times_filter(c0, spec):
        kf = k_ref[pl.ds(c0, g)].astype(F32).reshape(g * nk, 2 * LANES)
        sre, sim = spec[:, :LANES], spec[:, LANES:]
        kre, kim = kf[:, :LANES], kf[:, LANES:]
        return jnp.concatenate([sre * kre - sim * kim, sre * kim + sim * kre], axis=1).astype(BF16)

    def lanes_inv(prod):
        return jnp.dot(prod, m2i_ref[...], preferred_element_type=F32)

    def rows_inv(c0, cc):
        for ci in range(0, g, 2):
            st = []
            for h in range(2):
                blk = cc[(ci + h) * nk:(ci + h + 1) * nk]
                cre, cim = blk[:, :LANES], blk[:, LANES:]
                st.append(jnp.concatenate([cre * twr + cim * twi, cim * twr - cre * twi], axis=0).astype(BF16))
            y2 = jnp.dot(gi_ref[...], jnp.concatenate(st, axis=1), preferred_element_type=F32)
            for h in range(2):
                ch = c0 + ci + h
                y = y2[:, h * LANES:(h + 1) * LANES]
                y_s[pl.ds(ch, r_in, stride=pitch), :] = (y + chan(u_s, ch) * skip_ref[ch]) * chan(z0_s, ch)

    def two_groups(i, _):
        ca, cb = 2 * g * i, 2 * g * i + g
        a2_a = rows_fwd(ca)
        spec_a = lanes_fwd(a2_a)
        a2_b = rows_fwd(cb)
        prod_a = times_filter(ca, spec_a)
        spec_b = lanes_fwd(a2_b)
        cc_a = lanes_inv(prod_a)
        prod_b = times_filter(cb, spec_b)
        cc_b = lanes_inv(prod_b)
        rows_inv(ca, cc_a)
        rows_inv(cb, cc_b)
        return 0

    lax.fori_loop(0, ct // (2 * g), two_groups, 0)
    for r in range(r_in):
        o_ref[0, r] = y_s[r * pitch:r * pitch + ct, :].astype(o_ref.dtype)


def hyena_fftconv(u, z0, skip3, spec, tables, *, tile_c, group):
    b, r_in, c, _ = u.shape
    nk = 2 * r_in
    ct = min(tile_c, c)
    g = min(group, ct // 2)
    assert ct % (2 * g) == 0 and g % 2 == 0
    pitch = ct + SUBLANES
    f1, twr, twi, m2, m2i, gi = tables
    full = lambda a: pl.BlockSpec(a.shape, lambda j, bi: (0,) * a.ndim)
    io_spec = pl.BlockSpec((1, r_in, ct, LANES), lambda j, bi: (bi, 0, j, 0))
    return pl.pallas_call(
        functools.partial(_fftconv_body, ct=ct, g=g, nk=nk, r_in=r_in, pitch=pitch),
        grid=(c // ct, b),
        in_specs=[io_spec, io_spec,
                  pl.BlockSpec((ct, 1, 1), lambda j, bi: (j, 0, 0)),
                  pl.BlockSpec((ct, nk, 2 * LANES), lambda j, bi: (j, 0, 0)),
                  full(f1), full(twr), full(twi), full(m2), full(m2i), full(gi)],
        out_specs=io_spec,
        out_shape=jax.ShapeDtypeStruct((b, r_in, c, LANES), BF16),
        scratch_shapes=[pltpu.VMEM((r_in * pitch, LANES), F32) for _ in range(3)],
        compiler_params=_cparams("parallel", "arbitrary"),
        name="hyena_fftconv",
    )(u, z0, skip3, spec, f1, twr, twi, m2, m2i, gi)


ROUTE_LANES = LANES
NEG_BIG = -1e30
HALF_WORD = 16


def _pack_bf16_pairs(v):
    h = v.shape[1] // 2
    bits = pltpu.bitcast(v.astype(BF16).astype(F32), I32)
    return bits[:, :h] | lax.shift_right_logical(bits[:, h:], HALF_WORD)


def _unpack_bf16_pairs(w):
    hi = pltpu.bitcast(w & jnp.int32(-65536), F32)
    lo = pltpu.bitcast(lax.shift_left(w, HALF_WORD), F32)
    return jnp.concatenate([hi, lo], axis=1)


SLAB = 4


def _store_row_slabs(ref, words, row0=0):
    rows = words.shape[0]
    for j in range(SLAB):
        ref[pl.ds(SLAB * row0 + j, rows, stride=SLAB), :] = words[:, j * LANES:(j + 1) * LANES]


def _load_row_slabs(ref, rows=None, row0=0):
    rows = ref.shape[0] // SLAB if rows is None else rows
    return jnp.concatenate([ref[pl.ds(SLAB * row0 + j, rows, stride=SLAB), :] for j in range(SLAB)], axis=1)


MIX_SUB = 512


def _mix_route_body(x_ref, hf_ref, hb_ref, prg_ref, pga_ref, pgb_ref, yt_ref, rgp_ref, hyp_ref, wo_ref, g1_ref,
                    n2g_ref, sh2_ref, sc2_ref, wr_ref, br_ref, tri_ref,
                    x1_ref, hxp_ref, route_ref, cnt_ref, carry_s, *, t, sub, n_exp):
    @pl.when((pl.program_id(0) == 0) & (pl.program_id(1) == 0))
    def _():
        carry_s[...] = jnp.zeros_like(carry_s)

    running = carry_s[...]
    for r0 in range(0, t, sub):
        rows = slice(r0, r0 + sub)
        hsum = hf_ref[0, rows, :].astype(F32) + hb_ref[0, rows, :].astype(F32)
        y_rg = (hsum * _gelu_tanh(prg_ref[0, rows, :].astype(F32))).astype(BF16)
        t1 = jnp.dot(y_rg, rgp_ref[...], preferred_element_type=F32)
        t2 = jnp.concatenate([lax.dot_general(yt_ref[0, q], hyp_ref[...], (((0,), (0,)), ((), ())),
                                              preferred_element_type=F32)
                              for q in range(r0 // LANES, (r0 + sub) // LANES)], axis=0)
        merged = _sigmoid(pga_ref[0, rows, :].astype(F32)) * t1 + _sigmoid(pgb_ref[0, rows, :].astype(F32)) * t2
        out = jnp.dot(merged.astype(BF16), wo_ref[...], preferred_element_type=F32)
        x1 = x_ref[0, rows, :] + g1_ref[0] * out
        x1_ref[0, rows, :] = x1
        ms = jnp.mean(x1 * x1, axis=-1, keepdims=True)
        hx2 = (x1 * lax.rsqrt(ms + EPS) * n2g_ref[...]) * (1.0 + sc2_ref[0]) + sh2_ref[0]
        _store_row_slabs(hxp_ref, _pack_bf16_pairs(hx2), r0)

        hx_hi = hx2.astype(BF16)
        hx_lo = (hx2 - hx_hi.astype(F32)).astype(BF16)
        parts = (jnp.dot(hx_hi, wr_ref[...], preferred_element_type=F32)
                 + jnp.dot(hx_lo, wr_ref[...], preferred_element_type=F32))
        logits = parts[:, :ROUTE_LANES] + parts[:, ROUTE_LANES:] + br_ref[...]
        lane = lax.broadcasted_iota(I32, (sub, ROUTE_LANES), 1)
        is_g = lane < N_GROUPS
        glog = jnp.where(is_g, logits, NEG_BIG)
        gmax = jnp.max(glog, axis=1, keepdims=True)
        gidx = jnp.min(jnp.where(glog == gmax, lane, ROUTE_LANES), axis=1, keepdims=True)
        gsum = jnp.sum(jnp.where(is_g, jnp.exp(glog - gmax), 0.0), axis=1, keepdims=True)
        p_g = 1.0 / gsum
        e_lane = lane - N_GROUPS
        grp_of_lane = lax.shift_right_arithmetic(e_lane, int(math.log2(EXPERTS_PER_GROUP)))
        in_grp = (e_lane >= 0) & (e_lane < n_exp) & (grp_of_lane == gidx)
        elog = jnp.where(in_grp, logits, NEG_BIG)
        m1 = jnp.max(elog, axis=1, keepdims=True)
        i1 = jnp.min(jnp.where(elog == m1, lane, ROUTE_LANES), axis=1, keepdims=True)
        elog2 = jnp.where(lane == i1, NEG_BIG, elog)
        m2 = jnp.max(elog2, axis=1, keepdims=True)
        i2 = jnp.min(jnp.where(elog2 == m2, lane, ROUTE_LANES), axis=1, keepdims=True)
        e21 = jnp.exp(m2 - m1)
        pk1 = 1.0 / (1.0 + e21)
        wt1, wt2 = p_g * pk1, p_g * (e21 * pk1)

        oh1 = (lane == i1 - N_GROUPS).astype(F32)
        oh2 = (lane == i2 - N_GROUPS).astype(F32)
        cnt = oh1 + oh2
        before = jnp.dot(tri_ref[...], cnt.astype(BF16), preferred_element_type=F32) + running
        rank1 = jnp.sum(oh1 * before, axis=1, keepdims=True)
        rank2 = jnp.sum(oh2 * before, axis=1, keepdims=True)
        running = running + jnp.sum(cnt, axis=0, keepdims=True)
        vals = ((i1 - N_GROUPS).astype(F32), (i2 - N_GROUPS).astype(F32), rank1, rank2, wt1, wt2)
        route = jnp.zeros((sub, ROUTE_LANES), F32)
        for k, v in enumerate(vals):
            route = jnp.where(lane == k, v, route)
        route_ref[rows, :] = route
    carry_s[...] = running
    cnt_ref[...] = running


def mix_route(x, h_f, h_b, p_rm, y_hy_t, rg_proj, hy_proj, w_out, g1, n2g, sh2, sc2, wr, br, *, tile_l, n_exp):
    b, l, d = x.shape
    c = h_f.shape[2]
    t = min(tile_l, l)
    nt = l // t
    n = b * l
    sub = min(MIX_SUB, t)
    tri = (jnp.arange(sub)[:, None] > jnp.arange(sub)[None, :]).astype(BF16)
    tok = lambda bi, i: (bi, i, 0)
    col = lambda k: (lambda bi, i: (bi, i, k))
    full2 = lambda a: pl.BlockSpec(a.shape, lambda bi, i: (0, 0))
    per_b = pl.BlockSpec((1, 1, d), lambda bi, i: (bi, 0, 0))
    row = lambda bi, i: (bi * nt + i, 0)
    return pl.pallas_call(
        functools.partial(_mix_route_body, t=t, sub=sub, n_exp=n_exp),
        grid=(b, nt),
        in_specs=[pl.BlockSpec((1, t, d), tok), pl.BlockSpec((1, t, c), tok), pl.BlockSpec((1, t, c), tok),
                  pl.BlockSpec((1, t, c), col(1)), pl.BlockSpec((1, t, c), col(2)), pl.BlockSpec((1, t, c), col(3)),
                  pl.BlockSpec((1, t // LANES, c, LANES), lambda bi, i: (bi, i, 0, 0)),
                  full2(rg_proj), full2(hy_proj), full2(w_out), per_b,
                  full2(n2g), per_b, per_b, full2(wr), full2(br), full2(tri)],
        out_specs=[pl.BlockSpec((1, t, d), tok),
                   pl.BlockSpec((t * SLAB, LANES), row),
                   pl.BlockSpec((t, ROUTE_LANES), row), pl.BlockSpec((1, ROUTE_LANES), lambda bi, i: (0, 0))],
        out_shape=[jax.ShapeDtypeStruct((b, l, d), F32), jax.ShapeDtypeStruct((n * SLAB, LANES), I32),
                   jax.ShapeDtypeStruct((n, ROUTE_LANES), F32), jax.ShapeDtypeStruct((1, ROUTE_LANES), F32)],
        scratch_shapes=[pltpu.VMEM((1, ROUTE_LANES), F32)],
        compiler_params=_cparams("arbitrary", "arbitrary"),
        name="mix_route",
    )(x, h_f, h_b, p_rm, p_rm, p_rm, y_hy_t, rg_proj, hy_proj, w_out, g1, n2g, sh2, sc2, wr, br, tri)


def _dest_body(route_ref, cnt_ref, ut_ref, dest_ref, blk_ref, zero_ref, *, t, n_exp, nb_pad, n_rows):
    lane1 = lax.broadcasted_iota(I32, (1, ROUTE_LANES), 1)
    padded = jnp.floor((cnt_ref[...] + (MOE_BLOCK - 1.0)) * (1.0 / MOE_BLOCK)) * MOE_BLOCK
    padded = jnp.where(lane1 < n_exp, padded, 0.0)
    pend = jnp.dot(jnp.broadcast_to(padded, (SUBLANES, ROUTE_LANES)), ut_ref[...], precision=HIGHEST,
                   preferred_element_type=F32)[0:1]
    pstart = pend - padded
    route = route_ref[...]
    lane = lax.broadcasted_iota(I32, (t, ROUTE_LANES), 1)
    lf = lane.astype(F32)
    d1 = jnp.sum(jnp.where(lf == route[:, 0:1], pstart, 0.0), axis=1, keepdims=True) + route[:, 2:3]
    d2 = jnp.sum(jnp.where(lf == route[:, 1:2], pstart, 0.0), axis=1, keepdims=True) + route[:, 3:4]
    dmat = jnp.where(lane == 0, d1, jnp.where(lane == 1, d2, 0.0))
    dest_ref[...] = dmat.T[0:SUBLANES].astype(I32)
    first_row = lax.broadcasted_iota(I32, (nb_pad, ROUTE_LANES), 0).astype(F32) * float(MOE_BLOCK)
    lane_b = lax.broadcasted_iota(I32, (nb_pad, ROUTE_LANES), 1)
    nle = jnp.sum(jnp.where((lane_b < n_exp) & (pend <= first_row), 1.0, 0.0), axis=1, keepdims=True)
    e_blk = jnp.minimum(nle, n_exp - 1.0)
    mine = lane_b.astype(F32) == e_blk
    cnt_e = jnp.sum(jnp.where(mine, cnt_ref[...], 0.0), axis=1, keepdims=True)
    start_e = jnp.sum(jnp.where(mine, pstart, 0.0), axis=1, keepdims=True)
    valid = jnp.clip(cnt_e - (first_row - start_e), 0.0, float(MOE_BLOCK))
    blk_ref[...] = jnp.where(lane_b == 0, e_blk, jnp.where(lane_b == 1, valid, 0.0)).astype(I32)
    used = jnp.sum(jnp.where(lane1 == n_exp - 1, pend, 0.0), axis=1, keepdims=True)
    last_block = jnp.where(padded > 0.0, pend - float(MOE_BLOCK), -1.0)
    spare = used + float(MOE_BLOCK) * lane1.astype(F32)
    spare = jnp.where((lane1 < n_exp) & (spare < float(n_rows)), spare, -1.0)
    sub = lax.broadcasted_iota(I32, (SUBLANES, ROUTE_LANES), 0)
    zero_ref[...] = jnp.where(sub == 0, last_block, jnp.where(sub == 1, spare, -1.0)).astype(I32)


def moe_dest(route, cnt, *, tile, n_exp, n_blocks):
    n = route.shape[0]
    t = min(tile, n)
    nb_pad = -(-n_blocks // SUBLANES) * SUBLANES
    ut = (jnp.arange(ROUTE_LANES)[:, None] <= jnp.arange(ROUTE_LANES)[None, :]).astype(F32)
    return pl.pallas_call(
        functools.partial(_dest_body, t=t, n_exp=n_exp, nb_pad=nb_pad, n_rows=n_blocks * MOE_BLOCK),
        grid=(n // t,),
        in_specs=[pl.BlockSpec((t, ROUTE_LANES), lambda i: (i, 0)),
                  pl.BlockSpec((1, ROUTE_LANES), lambda i: (0, 0)),
                  pl.BlockSpec((ROUTE_LANES, ROUTE_LANES), lambda i: (0, 0))],
        out_specs=[pl.BlockSpec((SUBLANES, t), lambda i: (i, 0)),
                   pl.BlockSpec((nb_pad, ROUTE_LANES), lambda i: (0, 0)),
                   pl.BlockSpec((SUBLANES, ROUTE_LANES), lambda i: (0, 0))],
        out_shape=[jax.ShapeDtypeStruct((n // t * SUBLANES, t), I32),
                   jax.ShapeDtypeStruct((nb_pad, ROUTE_LANES), I32),
                   jax.ShapeDtypeStruct((SUBLANES, ROUTE_LANES), I32)],
        compiler_params=_cparams("arbitrary"),
        name="moe_dest",
    )(route, cnt, ut)


def _scatter_body(dest_ref, zero_ref, hx_ref, xb_ref, zero_s, sem, *, t, n_exp):
    @pl.when(pl.program_id(0) == 0)
    def _():
        zero_s[...] = jnp.zeros_like(zero_s)
        for wait in (False, True):
            for k in range(2):
                for e in range(n_exp):
                    start = zero_ref[k, e]

                    @pl.when(start >= 0)
                    def _(start=start, k=k):
                        copy = pltpu.make_async_copy(
                            zero_s, xb_ref.at[pl.ds(SLAB * jnp.maximum(start, 0), SLAB * MOE_BLOCK)], sem)
                        if wait:
                            copy.wait()
                        else:
                            copy.start(priority=k)

    def issue(r, _):
        for k in range(2):
            pltpu.make_async_copy(hx_ref.at[pl.ds(SLAB * r, SLAB)], xb_ref.at[pl.ds(SLAB * dest_ref[k, r], SLAB)],
                                  sem).start(priority=k)
        return 0

    lax.fori_loop(0, t, issue, 0, unroll=8)
    for k in range(2):
        pltpu.make_async_copy(hx_ref, xb_ref.at[pl.ds(0, SLAB * t)], sem).wait()


def moe_scatter(dest, zero_starts, hxp, n_rows, *, tile, n_exp):
    n = hxp.shape[0] // SLAB
    t = min(tile, n)
    per_dest_tile = dest.shape[1] // t
    return pl.pallas_call(
        functools.partial(_scatter_body, t=t, n_exp=n_exp),
        grid=(n // t,),
        in_specs=[pl.BlockSpec((SUBLANES, t), lambda i: (i // per_dest_tile, i % per_dest_tile),
                               memory_space=pltpu.SMEM),
                  pl.BlockSpec(zero_starts.shape, lambda i: (0, 0), memory_space=pltpu.SMEM),
                  pl.BlockSpec((t * SLAB, LANES), lambda i: (i, 0))],
        out_specs=pl.BlockSpec(memory_space=pl.ANY),
        out_shape=jax.ShapeDtypeStruct((n_rows * SLAB, LANES), I32),
        scratch_shapes=[pltpu.VMEM((MOE_BLOCK * SLAB, LANES), I32), pltpu.SemaphoreType.DMA],
        compiler_params=_cparams("arbitrary"),
        name="moe_scatter",
    )(dest, zero_starts, hxp)


def _expert_body(blk_ref, valid_ref, xb_ref, w1_ref, w3_ref, w2_ref, yb_ref, w1_s, w3_s, w2_s):
    i = pl.program_id(0)
    valid = valid_ref[i]
    half = MOE_BLOCK // 2
    changed = (i == 0) | (blk_ref[i] != blk_ref[jnp.maximum(i - 1, 0)])

    @pl.when(changed & (valid > 0))
    def _():
        w1_s[...] = w1_ref[0].astype(BF16)
        w3_s[...] = w3_ref[0].astype(BF16)
        w2_s[...] = w2_ref[0].astype(BF16)

    def run(rows):
        xblk = _unpack_bf16_pairs(_load_row_slabs(xb_ref, rows)).astype(BF16)
        h1 = jnp.dot(xblk, w1_s[...], preferred_element_type=F32)
        h3 = jnp.dot(xblk, w3_s[...], preferred_element_type=F32)
        hid = (h1 * _sigmoid(h1) * h3).astype(BF16)
        _store_row_slabs(yb_ref, _pack_bf16_pairs(jnp.dot(hid, w2_s[...], preferred_element_type=F32)))

    @pl.when(valid > half)
    def _():
        run(MOE_BLOCK)

    @pl.when(valid <= half)
    def _():
        yb_ref[...] = jnp.zeros_like(yb_ref)

    @pl.when((valid > 0) & (valid <= half))
    def _():
        run(half)


def moe_experts(blk_exp, blk_valid, xb, w1, w3, w2):
    p = xb.shape[0] // SLAB
    _, d, de = w1.shape
    nb = p // MOE_BLOCK
    grid_spec = pltpu.PrefetchScalarGridSpec(
        num_scalar_prefetch=2,
        grid=(nb,),
        in_specs=[pl.BlockSpec((MOE_BLOCK * SLAB, LANES), lambda i, blk, valid: (i, 0)),
                  pl.BlockSpec((1, d, de), lambda i, blk, valid: (blk[i], 0, 0)),
                  pl.BlockSpec((1, d, de), lambda i, blk, valid: (blk[i], 0, 0)),
                  pl.BlockSpec((1, de, d), lambda i, blk, valid: (blk[i], 0, 0))],
        out_specs=pl.BlockSpec((MOE_BLOCK * SLAB, LANES), lambda i, blk, valid: (i, 0)),
        scratch_shapes=[pltpu.VMEM((d, de), BF16), pltpu.VMEM((d, de), BF16), pltpu.VMEM((de, d), BF16)],
    )
    return pl.pallas_call(
        _expert_body,
        grid_spec=grid_spec,
        out_shape=jax.ShapeDtypeStruct((p * SLAB, LANES), I32),
        compiler_params=_cparams("arbitrary"),
        name="moe_experts",
    )(blk_exp, blk_valid, xb, w1, w3, w2)


COMBINE_PARTS = 4


def _combine_body(dest_ref, x1_ref, route_ref, g2_ref, fg_ref, yb_ref, o_ref, y1_s, y2_s, sems, *, t):
    tp = t // COMBINE_PARTS
    for part in range(COMBINE_PARTS):
        def issue(r, _, sem=sems.at[part]):
            pltpu.make_async_copy(yb_ref.at[pl.ds(SLAB * dest_ref[0, r], SLAB)], y1_s.at[pl.ds(SLAB * r, SLAB)],
                                  sem).start(priority=0)
            pltpu.make_async_copy(yb_ref.at[pl.ds(SLAB * dest_ref[1, r], SLAB)], y2_s.at[pl.ds(SLAB * r, SLAB)],
                                  sem).start(priority=1)
            return 0

        lax.fori_loop(part * tp, (part + 1) * tp, issue, 0, unroll=8)
    for part in range(COMBINE_PARTS):
        rows = slice(part * tp, (part + 1) * tp)
        lines = pl.ds(SLAB * part * tp, SLAB * tp)
        for y_s in (y1_s, y2_s):
            pltpu.make_async_copy(yb_ref.at[pl.ds(0, SLAB * tp)], y_s.at[lines], sems.at[part]).wait()
        route = route_ref[rows, :]
        moe = (route[:, 4:5] * _unpack_bf16_pairs(_load_row_slabs(y1_s, tp, part * tp))
               + route[:, 5:6] * _unpack_bf16_pairs(_load_row_slabs(y2_s, tp, part * tp)))
        x2 = x1_ref[0, rows, :] + g2_ref[0] * moe
        ms = jnp.mean(x2 * x2, axis=-1, keepdims=True)
        o_ref[0, rows, :] = x2 * lax.rsqrt(ms + EPS) * fg_ref[...]


def moe_combine(dest, x1, route, g2, final_g, yb, *, tile_l):
    b, l, d = x1.shape
    t = min(tile_l, l)
    nt = l // t
    slab = (t * SLAB, LANES)
    per_dest_tile = dest.shape[1] // t
    return pl.pallas_call(
        functools.partial(_combine_body, t=t),
        grid=(b, nt),
        in_specs=[pl.BlockSpec((SUBLANES, t),
                               lambda bi, i: ((bi * nt + i) // per_dest_tile, (bi * nt + i) % per_dest_tile),
                               memory_space=pltpu.SMEM),
                  pl.BlockSpec((1, t, d), lambda bi, i: (bi, i, 0)),
                  pl.BlockSpec((t, ROUTE_LANES), lambda bi, i: (bi * nt + i, 0)),
                  pl.BlockSpec((1, 1, d), lambda bi, i: (bi, 0, 0)),
                  pl.BlockSpec((1, d), lambda bi, i: (0, 0)),
                  pl.BlockSpec(memory_space=pl.ANY)],
        out_specs=pl.BlockSpec((1, t, d), lambda bi, i: (bi, i, 0)),
        out_shape=jax.ShapeDtypeStruct((b, l, d), F32),
        scratch_shapes=[pltpu.VMEM(slab, I32), pltpu.VMEM(slab, I32), pltpu.SemaphoreType.DMA((COMBINE_PARTS,))],
        compiler_params=_cparams("arbitrary", "arbitrary"),
        name="moe_combine",
    )(dest, x1, route, g2, final_g, yb)


def kernel(x, c, ctx, c_ctx, ada_w, ada_b, norm1_g, norm2_g, final_g, w_in, rg_conv_w, rg_conv_b, rg_wa_f, rg_ba_f, rg_wx_f, rg_bx_f, rg_lam_f, rg_wa_b, rg_ba_b, rg_wx_b, rg_bx_b, rg_lam_b, rg_proj, hy_conv_w, hy_conv_b, hy_pos_w1, hy_pos_b1, hy_pos_w2, hy_pos_b2, hy_freq, hy_pos_w3, hy_skip, hy_proj, w_out, moe_wg, moe_bg, moe_we, moe_be, moe_w1, moe_w3, moe_w2):
    B, L, D = x.shape
    C = rg_conv_w.shape[-1]
    LC = ctx.shape[1]
    c8 = jnp.zeros((8, D), F32).at[:B].set(c).at[B].set(c_ctx)
    mods = ada_mods(c8, ada_w[0], ada_b)
    sh1, sc1, g1 = (mods[:B, None, k * D:(k + 1) * D] for k in range(3))
    sh2, sc2, g2 = (mods[:B, None, k * D:(k + 1) * D] for k in range(3, 6))
    csh1 = jnp.broadcast_to(mods[B:B + 1, None, 0:D], (B, 1, D))
    csc1 = jnp.broadcast_to(mods[B:B + 1, None, D:2 * D], (B, 1, D))

    w_in_b = w_in[0].astype(BF16)
    w_rm = jnp.concatenate([w_in_b[:, :2 * C], w_in_b[:, 5 * C:]], axis=1)
    wg_f = gate_blocks(rg_wa_f[0], rg_wx_f[0], C // 256)
    wg_b = gate_blocks(rg_wa_b[0], rg_wx_b[0], C // 256)
    rg_f = (wg_f, rg_ba_f, rg_bx_f, rg_lam_f)
    rg_b = (wg_b, rg_ba_b, rg_bx_b, rg_lam_b)

    pc = norm_mod_proj(ctx, norm1_g, csh1, csc1, w_rm[:, :C], rg_conv_w[0], rg_conv_b, tile_l=LC, chunk=C)
    zero = jnp.zeros((B, 1, C), F32)
    _, hcf = rg_scan(pc, 0, *rg_f, zero, reverse=False, tile_l=TILE_SCAN)
    _, hcb = rg_scan(pc, 0, *rg_b, zero, reverse=True, tile_l=TILE_SCAN)

    p_rm = norm_mod_proj(x, norm1_g, sh1, sc1, w_rm, rg_conv_w[0], rg_conv_b, tile_l=TILE_PROJ, chunk=PROJ_CHUNK)
    hy_taps = jnp.concatenate([hy_conv_w[0], hy_conv_b], axis=0)
    u_hy, z0_hy = hyena_proj(x, norm1_g, sh1, sc1, w_in_b[:, 2 * C:5 * C], hy_taps, tile_l=TILE_PROJ)
    h_f, _ = rg_scan(p_rm, 0, *rg_f, hcf, reverse=False, tile_l=TILE_SCAN)
    h_b, _ = rg_scan(p_rm, 0, *rg_b, hcb, reverse=True, tile_l=TILE_SCAN)

    tables = dft_tables(L)
    assert hy_pos_w1.shape[1] == len(HY_FEATURE_ORDER)
    w1t = jnp.zeros((HY_HID, HY_HID), F32).at[:, :hy_pos_w1.shape[1]].set(hy_pos_w1[0].T[:, jnp.array(HY_FEATURE_ORDER)])
    kt = hyena_filter_t(w1t, hy_pos_b1[0][:, None], hy_pos_w2[0].T, hy_pos_b2[0][:, None], hy_freq[0][:, None],
                        hy_pos_w3[0].T.reshape(2, C, HY_HID).astype(BF16), L, FILTER_TILE_C)
    spec = hyena_spectrum(kt.reshape(2, C, L // LANES, LANES), tables, SPECTRUM_GROUP)
    y_hy_t = hyena_fftconv(u_hy, z0_hy, hy_skip[0][:, None, None], spec, tables, tile_c=FFT_TILE_C, group=FFT_GROUP)

    n_exp = moe_we.shape[-1]
    n_grp = moe_wg.shape[-1]
    assert n_grp == N_GROUPS and n_exp == N_GROUPS * EXPERTS_PER_GROUP
    wr = jnp.zeros((D, ROUTE_LANES), F32).at[:, :n_grp].set(moe_wg[0]).at[:, n_grp:n_grp + n_exp].set(moe_we[0])
    br = jnp.zeros((1, ROUTE_LANES), F32).at[:, :n_grp].set(moe_bg).at[:, n_grp:n_grp + n_exp].set(moe_be)
    wr_hi = wr.astype(BF16)
    wr_split = jnp.concatenate([wr_hi, (wr - wr_hi.astype(F32)).astype(BF16)], axis=1)
    x1, hxp, route, cnt = mix_route(x, h_f, h_b, p_rm, y_hy_t, rg_proj[0].astype(BF16), hy_proj[0].astype(BF16),
                                    w_out[0].astype(BF16), g1, norm2_g, sh2, sc2, wr_split, br, tile_l=TILE_MIX,
                                    n_exp=n_exp)

    n_blocks = (2 * B * L + n_exp * (MOE_BLOCK - 1)) // MOE_BLOCK
    dest, blk, zero_starts = moe_dest(route, cnt, tile=TILE_DEST, n_exp=n_exp, n_blocks=n_blocks)
    xb = moe_scatter(dest, zero_starts, hxp, n_blocks * MOE_BLOCK, tile=TILE_DISPATCH, n_exp=n_exp)
    yb = moe_experts(blk[:n_blocks, 0], blk[:n_blocks, 1], xb, moe_w1[0], moe_w3[0], moe_w2[0])
    return moe_combine(dest, x1, route, g2, final_g[None], yb, tile_l=TILE_DISPATCH)
```

```python
import functools
import math

import jax
import jax.numpy as jnp
from jax import lax
from jax.experimental import pallas as pl
from jax.experimental.pallas import tpu as pltpu

F32 = jnp.float32
BF16 = jnp.bfloat16
I32 = jnp.int32
HIGHEST = lax.Precision.HIGHEST

LANES = 128
SUBLANES = 8
EPS = 1e-6
RG_C = 8.0
RG_HEAD_DIM = 64
GRID_W = 64
HY_SEQ_BANDS = 16
HY_COL_BANDS = 8
HY_DECAY_TARGET = 1e-2
HY_FAST_DECAY = 0.3
HY_SLOW_DECAY = 1.5
N_GROUPS = 4
EXPERTS_PER_GROUP = 8
MOE_BLOCK = 512
VMEM_LIMIT = 56 * 1024 * 1024

TILE_PROJ = 512
PROJ_CHUNK = 1024
CONV_CHUNK = 256
TILE_SCAN = 512
TILE_MIX = 512
TILE_DEST = 2048
TILE_DISPATCH = 1024
FILTER_TILE_C = 256
SPECTRUM_GROUP = 32
FFT_TILE_C = 64
FFT_GROUP = 8


def _cparams(*sem):
    return pltpu.CompilerParams(dimension_semantics=sem, vmem_limit_bytes=VMEM_LIMIT)


def _sigmoid(x):
    return 0.5 * (jnp.tanh(0.5 * x) + 1.0)


def _gelu_tanh(x):
    c = math.sqrt(2.0 / math.pi)
    h = 0.5 * x
    return h + h * jnp.tanh(x * (c + (0.044715 * c) * (x * x)))


def _ada_body(c_ref, w_ref, b_ref, o_ref):
    c = c_ref[...]
    s = c * _sigmoid(c)
    o_ref[...] = jnp.dot(s, w_ref[...], precision=HIGHEST, preferred_element_type=F32) + b_ref[...]


def ada_mods(c8, ada_w, ada_b):
    d, m = ada_w.shape
    tn = 1024 if m % 1024 == 0 else m
    return pl.pallas_call(
        _ada_body,
        grid=(m // tn,),
        in_specs=[pl.BlockSpec((c8.shape[0], d), lambda j: (0, 0)),
                  pl.BlockSpec((d, tn), lambda j: (0, j)),
                  pl.BlockSpec((1, tn), lambda j: (0, j))],
        out_specs=pl.BlockSpec((c8.shape[0], tn), lambda j: (0, j)),
        out_shape=jax.ShapeDtypeStruct((c8.shape[0], m), F32),
        compiler_params=_cparams("parallel"),
        name="ada_mods",
    )(c8, ada_w, ada_b)


def _norm_mod(x_ref, g_ref, sh_ref, sc_ref):
    x = x_ref[0]
    ms = jnp.mean(x * x, axis=-1, keepdims=True)
    y = x * lax.rsqrt(ms + EPS) * g_ref[...]
    return (y * (1.0 + sc_ref[0]) + sh_ref[0]).astype(BF16)


def _first_rows_body(x_ref, g_ref, sh_ref, sc_ref, w_ref, o_ref):
    nt, rows, d = x_ref.shape[1:]
    x = x_ref[0].reshape(nt * rows, d)
    ms = jnp.mean(x * x, axis=-1, keepdims=True)
    y = x * lax.rsqrt(ms + EPS) * g_ref[...]
    hx = (y * (1.0 + sc_ref[0]) + sh_ref[0]).astype(BF16)
    o_ref[0] = jnp.dot(hx, w_ref[...], preferred_element_type=F32).reshape(nt, rows, w_ref.shape[1])


def proj_first_rows(x, g, shift, scale, w, *, tile_l):
    b, l, d = x.shape
    t = min(tile_l, l)
    nt = l // t
    m = w.shape[1]
    return pl.pallas_call(
        _first_rows_body,
        grid=(b,),
        in_specs=[pl.BlockSpec((1, nt, SUBLANES, d), lambda bi: (bi, 0, 0, 0)),
                  pl.BlockSpec((1, d), lambda bi: (0, 0)),
                  pl.BlockSpec((1, 1, d), lambda bi: (bi, 0, 0)),
                  pl.BlockSpec((1, 1, d), lambda bi: (bi, 0, 0)),
                  pl.BlockSpec(w.shape, lambda bi: (0, 0))],
        out_specs=pl.BlockSpec((1, nt, SUBLANES, m), lambda bi: (bi, 0, 0, 0)),
        out_shape=jax.ShapeDtypeStruct((b, nt, SUBLANES, m), F32),
        compiler_params=_cparams("parallel"),
        name="proj_first_rows",
    )(x.reshape(b, nt, t, d), g, shift, scale, w)


def _proj_body(x_ref, g_ref, sh_ref, sc_ref, w_ref, cw_ref, cb_ref, nxt_ref, o_ref, last_s, ext_s,
               *, chunk, n_tiles, t, c):
    i = pl.program_id(1)

    @pl.when(i == 0)
    def _():
        last_s[...] = jnp.zeros_like(last_s)

    hx = _norm_mod(x_ref, g_ref, sh_ref, sc_ref)
    has_next = (i < n_tiles - 1).astype(F32)
    cw = cw_ref[...]
    cc = ext_s.shape[1]
    for j in range(c // cc):
        cols = slice(j * cc, (j + 1) * cc)
        p = jnp.dot(hx, w_ref[:, cols], preferred_element_type=F32)
        ext_s[0:SUBLANES, :] = last_s[:, cols]
        ext_s[SUBLANES:SUBLANES + t, :] = p
        ext_s[SUBLANES + t:2 * SUBLANES + t, :] = nxt_ref[0, 0, :, cols] * has_next
        xc = cb_ref[:, cols] + cw[2:3, cols] * p
        for tap, off in ((0, -2), (1, -1), (3, 1)):
            xc = xc + cw[tap:tap + 1, cols] * ext_s[SUBLANES + off:SUBLANES + off + t, :]
        o_ref[0, :, cols] = xc.astype(o_ref.dtype)
        last_s[:, cols] = p[t - SUBLANES:t, :]
    m = w_ref.shape[1]
    for j in range((m - c) // chunk):
        cols = slice(c + j * chunk, c + (j + 1) * chunk)
        o_ref[0, :, cols] = jnp.dot(hx, w_ref[:, cols], preferred_element_type=F32).astype(o_ref.dtype)


def norm_mod_proj(x, g, shift, scale, w, conv_w, conv_b, *, tile_l, chunk):
    b, l, d = x.shape
    m = w.shape[1]
    c = conv_w.shape[1]
    tl = min(tile_l, l)
    n_tiles = l // tl
    chunk = min(chunk, max(m - c, 1))
    assert (m - c) % chunk == 0
    nxt = proj_first_rows(x, g, shift, scale, w[:, :c], tile_l=tile_l)
    return pl.pallas_call(
        functools.partial(_proj_body, chunk=chunk, n_tiles=n_tiles, t=tl, c=c),
        grid=(b, n_tiles),
        in_specs=[pl.BlockSpec((1, tl, d), lambda bi, i: (bi, i, 0)),
                  pl.BlockSpec((1, d), lambda bi, i: (0, 0)),
                  pl.BlockSpec((1, 1, d), lambda bi, i: (bi, 0, 0)),
                  pl.BlockSpec((1, 1, d), lambda bi, i: (bi, 0, 0)),
                  pl.BlockSpec(w.shape, lambda bi, i: (0, 0)),
                  pl.BlockSpec(conv_w.shape, lambda bi, i: (0, 0)),
                  pl.BlockSpec(conv_b.shape, lambda bi, i: (0, 0)),
                  pl.BlockSpec((1, 1, SUBLANES, c), lambda bi, i: (bi, jnp.minimum(i + 1, n_tiles - 1), 0, 0))],
        out_specs=pl.BlockSpec((1, tl, m), lambda bi, i: (bi, i, 0)),
        out_shape=jax.ShapeDtypeStruct((b, l, m), BF16),
        scratch_shapes=[pltpu.VMEM((SUBLANES, c), F32), pltpu.VMEM((tl + 2 * SUBLANES, min(CONV_CHUNK, c)), F32)],
        compiler_params=_cparams("parallel", "arbitrary"),
        name="norm_mod_proj",
    )(x, g, shift, scale, w, conv_w, conv_b, nxt)


def _scan_body(xc_ref, wg_ref, ba_ref, bx_ref, lam_ref, h0_ref,
               h_ref, hl_ref, xc_s, g_s, a_s, b_s, hloc_s, pcum_s, carry_s,
               *, reverse, t, c, s_len, pitch):
    i = pl.program_id(1)
    n_slab = c // LANES
    n_blk = wg_ref.shape[0]
    blk = c // n_blk

    @pl.when(i == 0)
    def _():
        carry_s[...] = h0_ref[0]

    xc_s[...] = xc_ref[0].astype(F32)

    for k in range(n_blk):
        g_s[:, k * 2 * blk:(k + 1) * 2 * blk] = jnp.dot(xc_ref[0, :, k * blk:(k + 1) * blk], wg_ref[k],
                                                          preferred_element_type=F32)

    lam = lam_ref[...]
    softplus_neg_lam = jnp.maximum(-lam, 0.0) + jnp.log1p(jnp.exp(-jnp.abs(lam)))
    half_ca = (-0.5 * RG_C) * softplus_neg_lam
    half_ba, half_bx = 0.5 * ba_ref[...], 0.5 * bx_ref[...]
    slabs_per_blk = blk // LANES
    for j in range(SUBLANES):
        r0 = j * s_len
        for k in range(n_slab):
            kb, ks = k // slabs_per_blk, k % slabs_per_blk
            ga = g_s[r0:r0 + s_len, kb * 2 * blk + ks * LANES:kb * 2 * blk + (ks + 1) * LANES]
            gx = g_s[r0:r0 + s_len, kb * 2 * blk + blk + ks * LANES:kb * 2 * blk + blk + (ks + 1) * LANES]
            lane = slice(k * LANES, (k + 1) * LANES)
            half_x = 0.5 * xc_s[r0:r0 + s_len, lane]
            hca = half_ca[:, lane]
            log_a = hca * jnp.tanh(ga + half_ba[:, lane]) + hca
            gated_x = half_x * jnp.tanh(gx + half_bx[:, lane]) + half_x
            a = jnp.exp(log_a)
            a_s[k, j * pitch:j * pitch + s_len, :] = a
            gain2 = -jnp.tanh(log_a) * (a * a + 1.0)
            gain = jnp.where(gain2 > 0.0, gain2 * lax.rsqrt(gain2), 0.0)
            b_s[k, j * pitch:j * pitch + s_len, :] = gain * gated_x

    def step1(s, hp):
        hs, ps = hp
        srow = (s_len - 1 - s) if reverse else s
        hs2, ps2 = [], []
        for k in range(n_slab):
            av = a_s[k, pl.ds(srow, SUBLANES, stride=pitch), :]
            bv = b_s[k, pl.ds(srow, SUBLANES, stride=pitch), :]
            h = av * hs[k] + bv
            p = av * ps[k]
            hloc_s[k, pl.ds(srow, SUBLANES, stride=pitch), :] = h
            pcum_s[k, pl.ds(srow, SUBLANES, stride=pitch), :] = p
            hs2.append(h)
            ps2.append(p)
        return tuple(hs2), tuple(ps2)

    zeros = tuple(jnp.zeros((SUBLANES, LANES), F32) for _ in range(n_slab))
    ones = tuple(jnp.ones((SUBLANES, LANES), F32) for _ in range(n_slab))
    h_end, p_end = lax.fori_loop(0, s_len, step1, (zeros, ones), unroll=1 if reverse else 4)

    order = range(SUBLANES - 1, -1, -1) if reverse else range(SUBLANES)
    for k in range(n_slab):
        cst = carry_s[:, k * LANES:(k + 1) * LANES]
        for j in order:
            rows = slice(j * pitch, j * pitch + s_len)
            h_ref[0, j * s_len:(j + 1) * s_len, k * LANES:(k + 1) * LANES] = (
                hloc_s[k, rows, :] + pcum_s[k, rows, :] * cst).astype(h_ref.dtype)
            cst = p_end[k][j:j + 1] * cst + h_end[k][j:j + 1]
        carry_s[:, k * LANES:(k + 1) * LANES] = cst
    hl_ref[0] = carry_s[...]


def rg_scan(p, col_blk, wg, ba, bx, lam, h0, *, reverse, tile_l):
    b, l, _ = p.shape
    c = ba.shape[1]
    t = min(tile_l, l)
    n_tiles = l // t
    s_len = t // SUBLANES
    pitch = s_len + SUBLANES

    def nat(i):
        return (n_tiles - 1 - i) if reverse else i

    body = functools.partial(_scan_body, reverse=reverse, t=t, c=c, s_len=s_len, pitch=pitch)
    vec = pl.BlockSpec((1, c), lambda bi, i: (0, 0))
    return pl.pallas_call(
        body,
        grid=(b, n_tiles),
        in_specs=[pl.BlockSpec((1, t, c), lambda bi, i: (bi, nat(i), col_blk)),
                  pl.BlockSpec(wg.shape, lambda bi, i: (0, 0, 0)),
                  vec, vec, vec,
                  pl.BlockSpec((1, 1, c), lambda bi, i: (bi, 0, 0))],
        out_specs=[pl.BlockSpec((1, t, c), lambda bi, i: (bi, nat(i), 0)),
                   pl.BlockSpec((1, 1, c), lambda bi, i: (bi, 0, 0))],
        out_shape=[jax.ShapeDtypeStruct((b, l, c), BF16), jax.ShapeDtypeStruct((b, 1, c), F32)],
        scratch_shapes=[pltpu.VMEM((t, c), F32), pltpu.VMEM((t, 2 * c), F32)]
        + [pltpu.VMEM((c // LANES, SUBLANES * pitch, LANES), F32) for _ in range(4)]
        + [pltpu.VMEM((1, c), F32)],
        compiler_params=_cparams("parallel", "arbitrary"),
        name="rg_scan_bwd" if reverse else "rg_scan_fwd",
    )(p, wg, ba, bx, lam, h0)


def gate_blocks(wa, wx, n_blk):
    h, d, _ = wa.shape
    hp = h // n_blk
    eye = jnp.eye(hp, dtype=wa.dtype)

    def bd(w):
        w = w.reshape(n_blk, hp, d, d)
        return jnp.einsum('khde,hg->khdge', w, eye).reshape(n_blk, hp * d, hp * d)

    return (0.5 * jnp.concatenate([bd(wa), bd(wx)], axis=-1)).astype(BF16)


HY_HID = 64
HY_FEATURE_ORDER = (list(range(1, 1 + 2 * HY_SEQ_BANDS))
                    + list(range(2 + 2 * HY_SEQ_BANDS, 2 + 2 * HY_SEQ_BANDS + 2 * HY_COL_BANDS))
                    + [0, 1 + 2 * HY_SEQ_BANDS])


def _filter_body(w1t_ref, b1_ref, w2t_ref, b2_ref, fr_ref, w3t_ref, o_ref, z_s, *, l, c, ct, rows_grid):
    d = pl.program_id(0)
    j = pl.program_id(1)
    lane = lax.broadcasted_iota(I32, (1, l), 1)
    s_i = jnp.where(d == 0, lane, l - lane)
    sf = s_i.astype(F32)
    t_norm = sf / float(max(l - 1, 1))

    @pl.when(j == 0)
    def _():
        band_step = (HY_SEQ_BANDS - 1 - 1e-4) / (HY_SEQ_BANDS - 1)
        seq_band = 1e-4 + band_step * lax.broadcasted_iota(I32, (HY_SEQ_BANDS, 1), 0).astype(F32)
        col_band = 1.0 + lax.broadcasted_iota(I32, (HY_COL_BANDS, 1), 0).astype(F32)
        col_pos = (s_i & (GRID_W - 1)).astype(F32)
        row_lag = (s_i >> int(math.log2(GRID_W))).astype(F32) / float(rows_grid)
        ang_seq = ((2.0 * math.pi / l) * sf) * seq_band
        ang_col = ((2.0 * math.pi / GRID_W) * col_pos) * col_band
        n_trig = 2 * HY_SEQ_BANDS + 2 * HY_COL_BANDS
        trow = lax.broadcasted_iota(I32, (HY_HID - n_trig, 1), 0)
        tail = jnp.where(trow == 0, t_norm, jnp.where(trow == 1, row_lag, 0.0))
        feats = jnp.concatenate([jnp.cos(ang_seq), jnp.sin(ang_seq), jnp.cos(ang_col), jnp.sin(ang_col), tail], axis=0)
        fr = fr_ref[...]
        z = jnp.sin(fr * (jnp.dot(w1t_ref[...], feats, precision=HIGHEST, preferred_element_type=F32) + b1_ref[...]))
        z_s[...] = jnp.sin(fr * (jnp.dot(w2t_ref[...], z, precision=HIGHEST, preferred_element_type=F32) + b2_ref[...]))

    k = jnp.dot(w3t_ref[0], z_s[...].astype(BF16), preferred_element_type=F32)
    ch = (lax.broadcasted_iota(I32, (ct, 1), 0) + j * ct).astype(F32)
    max_decay = math.log(HY_DECAY_TARGET) / HY_FAST_DECAY
    min_decay = math.log(HY_DECAY_TARGET) / HY_SLOW_DECAY
    delta = jnp.abs(min_decay + ch * ((max_decay - min_decay) / (c - 1)))
    k = k * jnp.exp(-t_norm * delta)
    k = jnp.where((d == 1) & (lane == 0), 0.0, k)
    o_ref[0] = k.astype(o_ref.dtype)


def hyena_filter_t(w1t, b1, w2t, b2, fr, w3t, l, tile_c):
    assert GRID_W & (GRID_W - 1) == 0
    c = w3t.shape[1]
    ct = min(tile_c, c)
    body = functools.partial(_filter_body, l=l, c=c, ct=ct, rows_grid=l // GRID_W)
    small = lambda shape: pl.BlockSpec(shape, lambda d, j: (0,) * len(shape))
    return pl.pallas_call(
        body,
        grid=(2, c // ct),
        in_specs=[small(w1t.shape), small(b1.shape), small(w2t.shape), small(b2.shape), small(fr.shape),
                  pl.BlockSpec((1, ct, HY_HID), lambda d, j: (d, j, 0))],
        out_specs=pl.BlockSpec((1, ct, l), lambda d, j: (d, j, 0)),
        out_shape=jax.ShapeDtypeStruct((2, c, l), BF16),
        scratch_shapes=[pltpu.VMEM((HY_HID, l), F32)],
        compiler_params=_cparams("arbitrary", "arbitrary"),
        name="hyena_filter",
    )(w1t, b1, w2t, b2, fr, w3t)


def dft_tables(l):
    import numpy as np
    n = 2 * l
    r_in, nk = l // LANES, n // LANES
    ka = np.arange(nk)[:, None].astype(np.float64)
    r = np.arange(r_in)[None, :].astype(np.float64)
    a1 = 2.0 * np.pi * ka * r / nk
    f1 = np.concatenate([np.cos(a1), -np.sin(a1)], axis=0)
    lane = np.arange(LANES)[None, :].astype(np.float64)
    at = 2.0 * np.pi * ka * lane / n
    twr, twi = np.cos(at), -np.sin(at)
    a2 = 2.0 * np.pi * np.arange(LANES)[:, None] * np.arange(LANES)[None, :] / LANES
    cr, ci = np.cos(a2), -np.sin(a2)
    m2 = np.block([[cr, ci], [-ci, cr]])
    m2i = np.block([[cr, -ci], [ci, cr]])
    ai = 2.0 * np.pi * np.arange(r_in)[:, None] * np.arange(nk)[None, :] / nk
    gi = np.concatenate([np.cos(ai), -np.sin(ai)], axis=1) / n
    as_bf = lambda a: jnp.asarray(a, F32).astype(BF16)
    return as_bf(f1), jnp.asarray(twr, F32), jnp.asarray(twi, F32), as_bf(m2), as_bf(m2i), as_bf(gi)


def _fwd_rows_twiddle(x_a, x_b, f1, twr, twi, nk):
    a = jnp.dot(f1, jnp.concatenate([x_a, x_b], axis=1), preferred_element_type=F32)
    out = []
    for h in range(2):
        re, im = a[:nk, h * LANES:(h + 1) * LANES], a[nk:, h * LANES:(h + 1) * LANES]
        out.append((re * twr - im * twi, re * twi + im * twr))
    return out


def _spectrum_body(k_ref, f1_ref, twr_ref, twi_ref, m2_ref, o_ref, *, g, nk, r_in):
    f1, twr, twi = f1_ref[...], twr_ref[...], twi_ref[...]
    sign = jnp.where((lax.broadcasted_iota(I32, (nk, 1), 0) & 1) == 0, 1.0, -1.0)
    a2 = []
    for ci in range(g):
        (fre, fim), (bre, bim) = _fwd_rows_twiddle(k_ref[0, ci], k_ref[1, ci], f1, twr, twi, nk)
        a2.append(jnp.concatenate([fre + sign * bre, fim + sign * bim], axis=1).astype(BF16))
    spec = jnp.dot(jnp.concatenate(a2, axis=0), m2_ref[...], preferred_element_type=F32)
    o_ref[...] = spec.reshape(g, nk, 2 * LANES).astype(o_ref.dtype)


def hyena_spectrum(kt4, tables, group):
    _, c, r_in, _ = kt4.shape
    nk = 2 * r_in
    f1, twr, twi, m2, _, _ = tables
    g = min(group, c)
    full = lambda a: pl.BlockSpec(a.shape, lambda j: (0,) * a.ndim)
    return pl.pallas_call(
        functools.partial(_spectrum_body, g=g, nk=nk, r_in=r_in),
        grid=(c // g,),
        in_specs=[pl.BlockSpec((2, g, r_in, LANES), lambda j: (0, j, 0, 0)), full(f1), full(twr), full(twi), full(m2)],
        out_specs=pl.BlockSpec((g, nk, 2 * LANES), lambda j: (j, 0, 0)),
        out_shape=jax.ShapeDtypeStruct((c, nk, 2 * LANES), BF16),
        compiler_params=_cparams("parallel"),
        name="hyena_spectrum",
    )(kt4, f1, twr, twi, m2)


HY_CHUNK = 256


def _hyena_proj_body(x_ref, g_ref, sh_ref, sc_ref, w_ref, taps_ref, nxt_ref, u_ref, z0_ref, last_s, *, n_tiles, t, c):
    i = pl.program_id(1)

    @pl.when(i == 0)
    def _():
        last_s[...] = jnp.zeros_like(last_s)

    hx = _norm_mod(x_ref, g_ref, sh_ref, sc_ref)
    row = lax.broadcasted_iota(I32, (t, 1), 0)
    has_next = (i < n_tiles - 1).astype(F32)
    cw = min(HY_CHUNK, c)
    for j in range(c // cw):
        zs = []
        for k in range(3):
            cols = slice(k * c + j * cw, k * c + (j + 1) * cw)
            p = jnp.dot(hx, w_ref[:, cols], preferred_element_type=F32)
            up = jnp.where(row == 0, last_s[:, cols], pltpu.roll(p, 1, 0))
            dn = jnp.where(row == t - 1, nxt_ref[0, 0, 0:1, cols] * has_next, pltpu.roll(p, t - 1, 0))
            tp = taps_ref[:, cols]
            zs.append(tp[3:4] + tp[0:1] * up + tp[1:2] * p + tp[2:3] * dn)
            last_s[:, cols] = p[t - 1:t, :]
        z0, z1, zv = zs
        u_t, z0_t = (zv * z1).T, z0.T
        for q in range(t // LANES):
            u_ref[0, q, j * cw:(j + 1) * cw, :] = u_t[:, q * LANES:(q + 1) * LANES].astype(u_ref.dtype)
            z0_ref[0, q, j * cw:(j + 1) * cw, :] = z0_t[:, q * LANES:(q + 1) * LANES].astype(z0_ref.dtype)


def hyena_proj(x, g, shift, scale, w, taps, *, tile_l):
    b, l, d = x.shape
    c = w.shape[1] // 3
    t = min(tile_l, l)
    n_tiles = l // t
    rq = t // LANES
    nxt = proj_first_rows(x, g, shift, scale, w, tile_l=tile_l)
    o_spec = pl.BlockSpec((1, rq, c, LANES), lambda bi, i: (bi, i, 0, 0))
    o_shape = jax.ShapeDtypeStruct((b, l // LANES, c, LANES), BF16)
    return pl.pallas_call(
        functools.partial(_hyena_proj_body, n_tiles=n_tiles, t=t, c=c),
        grid=(b, n_tiles),
        in_specs=[pl.BlockSpec((1, t, d), lambda bi, i: (bi, i, 0)),
                  pl.BlockSpec((1, d), lambda bi, i: (0, 0)),
                  pl.BlockSpec((1, 1, d), lambda bi, i: (bi, 0, 0)),
                  pl.BlockSpec((1, 1, d), lambda bi, i: (bi, 0, 0)),
                  pl.BlockSpec(w.shape, lambda bi, i: (0, 0)),
                  pl.BlockSpec(taps.shape, lambda bi, i: (0, 0)),
                  pl.BlockSpec((1, 1, SUBLANES, 3 * c), lambda bi, i: (bi, jnp.minimum(i + 1, n_tiles - 1), 0, 0))],
        out_specs=[o_spec, o_spec],
        out_shape=[o_shape, o_shape],
        scratch_shapes=[pltpu.VMEM((1, 3 * c), F32)],
        compiler_params=_cparams("parallel", "arbitrary"),
        name="hyena_proj",
    )(x, g, shift, scale, w, taps, nxt)


def _fftconv_body(u_ref, z0_ref, skip_ref, k_ref, f1_ref, twr_ref, twi_ref, m2_ref, m2i_ref, gi_ref,
                  o_ref, u_s, z0_s, y_s, *, ct, g, nk, r_in, pitch):
    for r in range(r_in):
        u_s[r * pitch:r * pitch + ct, :] = u_ref[0, r].astype(F32)
        z0_s[r * pitch:r * pitch + ct, :] = z0_ref[0, r].astype(F32)
    f1, twr, twi = f1_ref[...], twr_ref[...], twi_ref[...]

    def chan(ref, ch):
        return ref[pl.ds(ch, r_in, stride=pitch), :]

    def rows_fwd(c0):
        a2 = []
        for ci in range(0, g, 2):
            pair = _fwd_rows_twiddle(chan(u_s, c0 + ci).astype(BF16), chan(u_s, c0 + ci + 1).astype(BF16),
                                     f1, twr, twi, nk)
            a2 += [jnp.concatenate([tre, tim], axis=1).astype(BF16) for tre, tim in pair]
        return jnp.concatenate(a2, axis=0)

    def lanes_fwd(a2):
        return jnp.dot(a2, m2_ref[...], preferred_element_type=F32)

    def times_filter(c0, spec):
        kf = k_ref[pl.ds(c0, g)].astype(F32).reshape(g * nk, 2 * LANES)
        sre, sim = spec[:, :LANES], spec[:, LANES:]
        kre, kim = kf[:, :LANES], kf[:, LANES:]
        return jnp.concatenate([sre * kre - sim * kim, sre * kim + sim * kre], axis=1).astype(BF16)

    def lanes_inv(prod):
        return jnp.dot(prod, m2i_ref[...], preferred_element_type=F32)

    def rows_inv(c0, cc):
        for ci in range(0, g, 2):
            st = []
            for h in range(2):
                blk = cc[(ci + h) * nk:(ci + h + 1) * nk]
                cre, cim = blk[:, :LANES], blk[:, LANES:]
                st.append(jnp.concatenate([cre * twr + cim * twi, cim * twr - cre * twi], axis=0).astype(BF16))
            y2 = jnp.dot(gi_ref[...], jnp.concatenate(st, axis=1), preferred_element_type=F32)
            for h in range(2):
                ch = c0 + ci + h
                y = y2[:, h * LANES:(h + 1) * LANES]
                y_s[pl.ds(ch, r_in, stride=pitch), :] = (y + chan(u_s, ch) * skip_ref[ch]) * chan(z0_s, ch)

    def two_groups(i, _):
        ca, cb = 2 * g * i, 2 * g * i + g
        a2_a = rows_fwd(ca)
        spec_a = lanes_fwd(a2_a)
        a2_b = rows_fwd(cb)
        prod_a = times_filter(ca, spec_a)
        spec_b = lanes_fwd(a2_b)
        cc_a = lanes_inv(prod_a)
        prod_b = times_filter(cb, spec_b)
        cc_b = lanes_inv(prod_b)
        rows_inv(ca, cc_a)
        rows_inv(cb, cc_b)
        return 0

    lax.fori_loop(0, ct // (2 * g), two_groups, 0)
    for r in range(r_in):
        o_ref[0, r] = y_s[r * pitch:r * pitch + ct, :].astype(o_ref.dtype)


def hyena_fftconv(u, z0, skip3, spec, tables, *, tile_c, group):
    b, r_in, c, _ = u.shape
    nk = 2 * r_in
    ct = min(tile_c, c)
    g = min(group, ct // 2)
    assert ct % (2 * g) == 0 and g % 2 == 0
    pitch = ct + SUBLANES
    f1, twr, twi, m2, m2i, gi = tables
    full = lambda a: pl.BlockSpec(a.shape, lambda j, bi: (0,) * a.ndim)
    io_spec = pl.BlockSpec((1, r_in, ct, LANES), lambda j, bi: (bi, 0, j, 0))
    return pl.pallas_call(
        functools.partial(_fftconv_body, ct=ct, g=g, nk=nk, r_in=r_in, pitch=pitch),
        grid=(c // ct, b),
        in_specs=[io_spec, io_spec,
                  pl.BlockSpec((ct, 1, 1), lambda j, bi: (j, 0, 0)),
                  pl.BlockSpec((ct, nk, 2 * LANES), lambda j, bi: (j, 0, 0)),
                  full(f1), full(twr), full(twi), full(m2), full(m2i), full(gi)],
        out_specs=io_spec,
        out_shape=jax.ShapeDtypeStruct((b, r_in, c, LANES), BF16),
        scratch_shapes=[pltpu.VMEM((r_in * pitch, LANES), F32) for _ in range(3)],
        compiler_params=_cparams("parallel", "arbitrary"),
        name="hyena_fftconv",
    )(u, z0, skip3, spec, f1, twr, twi, m2, m2i, gi)


ROUTE_LANES = LANES
NEG_BIG = -1e30
HALF_WORD = 16


def _pack_bf16_pairs(v):
    h = v.shape[1] // 2
    bits = pltpu.bitcast(v.astype(BF16).astype(F32), I32)
    return bits[:, :h] | lax.shift_right_logical(bits[:, h:], HALF_WORD)


def _unpack_bf16_pairs(w):
    hi = pltpu.bitcast(w & jnp.int32(-65536), F32)
    lo = pltpu.bitcast(lax.shift_left(w, HALF_WORD), F32)
    return jnp.concatenate([hi, lo], axis=1)


SLAB = 4


def _store_row_slabs(ref, words, row0=0):
    rows = words.shape[0]
    for j in range(SLAB):
        ref[pl.ds(SLAB * row0 + j, rows, stride=SLAB), :] = words[:, j * LANES:(j + 1) * LANES]


def _load_row_slabs(ref, rows=None, row0=0):
    rows = ref.shape[0] // SLAB if rows is None else rows
    return jnp.concatenate([ref[pl.ds(SLAB * row0 + j, rows, stride=SLAB), :] for j in range(SLAB)], axis=1)


MIX_SUB = 512


def _mix_route_body(x_ref, hf_ref, hb_ref, prg_ref, pga_ref, pgb_ref, yt_ref, rgp_ref, hyp_ref, wo_ref, g1_ref,
                    n2g_ref, sh2_ref, sc2_ref, wr_ref, br_ref, tri_ref,
                    x1_ref, hxp_ref, route_ref, cnt_ref, carry_s, *, t, sub, n_exp):
    @pl.when((pl.program_id(0) == 0) & (pl.program_id(1) == 0))
    def _():
        carry_s[...] = jnp.zeros_like(carry_s)

    running = carry_s[...]
    for r0 in range(0, t, sub):
        rows = slice(r0, r0 + sub)
        hsum = hf_ref[0, rows, :].astype(F32) + hb_ref[0, rows, :].astype(F32)
        y_rg = (hsum * _gelu_tanh(prg_ref[0, rows, :].astype(F32))).astype(BF16)
        t1 = jnp.dot(y_rg, rgp_ref[...], preferred_element_type=F32)
        t2 = jnp.concatenate([lax.dot_general(yt_ref[0, q], hyp_ref[...], (((0,), (0,)), ((), ())),
                                              preferred_element_type=F32)
                              for q in range(r0 // LANES, (r0 + sub) // LANES)], axis=0)
        merged = ((t1 + t2) + jnp.tanh(0.5 * pga_ref[0, rows, :].astype(F32)) * t1
                  + jnp.tanh(0.5 * pgb_ref[0, rows, :].astype(F32)) * t2)
        out = jnp.dot(merged.astype(BF16), wo_ref[...], preferred_element_type=F32)
        x1 = x_ref[0, rows, :] + g1_ref[0] * out
        x1_ref[0, rows, :] = x1
        ms = jnp.mean(x1 * x1, axis=-1, keepdims=True)
        hx2 = (x1 * lax.rsqrt(ms + EPS) * n2g_ref[...]) * (1.0 + sc2_ref[0]) + sh2_ref[0]
        _store_row_slabs(hxp_ref, _pack_bf16_pairs(hx2), r0)

        hx_hi = hx2.astype(BF16)
        hx_lo = (hx2 - hx_hi.astype(F32)).astype(BF16)
        parts = (jnp.dot(hx_hi, wr_ref[...], preferred_element_type=F32)
                 + jnp.dot(hx_lo, wr_ref[...], preferred_element_type=F32))
        logits = parts[:, :ROUTE_LANES] + parts[:, ROUTE_LANES:] + br_ref[...]
        lane = lax.broadcasted_iota(I32, (sub, ROUTE_LANES), 1)
        is_g = lane < N_GROUPS
        glog = jnp.where(is_g, logits, NEG_BIG)
        gmax = jnp.max(glog, axis=1, keepdims=True)
        gidx = jnp.min(jnp.where(glog == gmax, lane, ROUTE_LANES), axis=1, keepdims=True)
        gsum = jnp.sum(jnp.where(is_g, jnp.exp(glog - gmax), 0.0), axis=1, keepdims=True)
        p_g = 1.0 / gsum
        e_lane = lane - N_GROUPS
        grp_of_lane = lax.shift_right_arithmetic(e_lane, int(math.log2(EXPERTS_PER_GROUP)))
        in_grp = (e_lane >= 0) & (e_lane < n_exp) & (grp_of_lane == gidx)
        elog = jnp.where(in_grp, logits, NEG_BIG)
        m1 = jnp.max(elog, axis=1, keepdims=True)
        i1 = jnp.min(jnp.where(elog == m1, lane, ROUTE_LANES), axis=1, keepdims=True)
        elog2 = jnp.where(lane == i1, NEG_BIG, elog)
        m2 = jnp.max(elog2, axis=1, keepdims=True)
        i2 = jnp.min(jnp.where(elog2 == m2, lane, ROUTE_LANES), axis=1, keepdims=True)
        e21 = jnp.exp(m2 - m1)
        pk1 = 1.0 / (1.0 + e21)
        wt1, wt2 = p_g * pk1, p_g * (e21 * pk1)

        oh1 = (lane == i1 - N_GROUPS).astype(F32)
        oh2 = (lane == i2 - N_GROUPS).astype(F32)
        cnt = oh1 + oh2
        before = jnp.dot(tri_ref[...], cnt.astype(BF16), preferred_element_type=F32) + running
        rank1 = jnp.sum(oh1 * before, axis=1, keepdims=True)
        rank2 = jnp.sum(oh2 * before, axis=1, keepdims=True)
        running = running + jnp.sum(cnt, axis=0, keepdims=True)
        vals = ((i1 - N_GROUPS).astype(F32), (i2 - N_GROUPS).astype(F32), rank1, rank2, wt1, wt2)
        route = jnp.zeros((sub, ROUTE_LANES), F32)
        for k, v in enumerate(vals):
            route = jnp.where(lane == k, v, route)
        route_ref[rows, :] = route
    carry_s[...] = running
    cnt_ref[...] = running


def mix_route(x, h_f, h_b, p_rm, y_hy_t, rg_proj, hy_proj, w_out, g1, n2g, sh2, sc2, wr, br, *, tile_l, n_exp):
    b, l, d = x.shape
    c = h_f.shape[2]
    t = min(tile_l, l)
    nt = l // t
    n = b * l
    sub = min(MIX_SUB, t)
    tri = (jnp.arange(sub)[:, None] > jnp.arange(sub)[None, :]).astype(BF16)
    tok = lambda bi, i: (bi, i, 0)
    col = lambda k: (lambda bi, i: (bi, i, k))
    full2 = lambda a: pl.BlockSpec(a.shape, lambda bi, i: (0, 0))
    per_b = pl.BlockSpec((1, 1, d), lambda bi, i: (bi, 0, 0))
    row = lambda bi, i: (bi * nt + i, 0)
    return pl.pallas_call(
        functools.partial(_mix_route_body, t=t, sub=sub, n_exp=n_exp),
        grid=(b, nt),
        in_specs=[pl.BlockSpec((1, t, d), tok), pl.BlockSpec((1, t, c), tok), pl.BlockSpec((1, t, c), tok),
                  pl.BlockSpec((1, t, c), col(1)), pl.BlockSpec((1, t, c), col(2)), pl.BlockSpec((1, t, c), col(3)),
                  pl.BlockSpec((1, t // LANES, c, LANES), lambda bi, i: (bi, i, 0, 0)),
                  full2(rg_proj), full2(hy_proj), full2(w_out), per_b,
                  full2(n2g), per_b, per_b, full2(wr), full2(br), full2(tri)],
        out_specs=[pl.BlockSpec((1, t, d), tok),
                   pl.BlockSpec((t * SLAB, LANES), row),
                   pl.BlockSpec((t, ROUTE_LANES), row), pl.BlockSpec((1, ROUTE_LANES), lambda bi, i: (0, 0))],
        out_shape=[jax.ShapeDtypeStruct((b, l, d), F32), jax.ShapeDtypeStruct((n * SLAB, LANES), I32),
                   jax.ShapeDtypeStruct((n, ROUTE_LANES), F32), jax.ShapeDtypeStruct((1, ROUTE_LANES), F32)],
        scratch_shapes=[pltpu.VMEM((1, ROUTE_LANES), F32)],
        compiler_params=_cparams("arbitrary", "arbitrary"),
        name="mix_route",
    )(x, h_f, h_b, p_rm, p_rm, p_rm, y_hy_t, rg_proj, hy_proj, w_out, g1, n2g, sh2, sc2, wr, br, tri)


def _dest_body(route_ref, cnt_ref, ut_ref, dest_ref, blk_ref, zero_ref, *, t, n_exp, nb_pad, n_rows):
    lane1 = lax.broadcasted_iota(I32, (1, ROUTE_LANES), 1)
    padded = jnp.floor((cnt_ref[...] + (MOE_BLOCK - 1.0)) * (1.0 / MOE_BLOCK)) * MOE_BLOCK
    padded = jnp.where(lane1 < n_exp, padded, 0.0)
    pend = jnp.dot(jnp.broadcast_to(padded, (SUBLANES, ROUTE_LANES)), ut_ref[...], precision=HIGHEST,
                   preferred_element_type=F32)[0:1]
    pstart = pend - padded
    route = route_ref[...]
    lane = lax.broadcasted_iota(I32, (t, ROUTE_LANES), 1)
    lf = lane.astype(F32)
    d1 = jnp.sum(jnp.where(lf == route[:, 0:1], pstart, 0.0), axis=1, keepdims=True) + route[:, 2:3]
    d2 = jnp.sum(jnp.where(lf == route[:, 1:2], pstart, 0.0), axis=1, keepdims=True) + route[:, 3:4]
    dmat = jnp.where(lane == 0, d1, jnp.where(lane == 1, d2, 0.0))
    dest_ref[...] = dmat.T[0:SUBLANES].astype(I32)
    first_row = lax.broadcasted_iota(I32, (nb_pad, ROUTE_LANES), 0).astype(F32) * float(MOE_BLOCK)
    lane_b = lax.broadcasted_iota(I32, (nb_pad, ROUTE_LANES), 1)
    nle = jnp.sum(jnp.where((lane_b < n_exp) & (pend <= first_row), 1.0, 0.0), axis=1, keepdims=True)
    e_blk = jnp.minimum(nle, n_exp - 1.0)
    mine = lane_b.astype(F32) == e_blk
    cnt_e = jnp.sum(jnp.where(mine, cnt_ref[...], 0.0), axis=1, keepdims=True)
    start_e = jnp.sum(jnp.where(mine, pstart, 0.0), axis=1, keepdims=True)
    valid = jnp.clip(cnt_e - (first_row - start_e), 0.0, float(MOE_BLOCK))
    blk_ref[...] = jnp.where(lane_b == 0, e_blk, jnp.where(lane_b == 1, valid, 0.0)).astype(I32)
    used = jnp.sum(jnp.where(lane1 == n_exp - 1, pend, 0.0), axis=1, keepdims=True)
    last_block = jnp.where(padded > 0.0, pend - float(MOE_BLOCK), -1.0)
    spare = used + float(MOE_BLOCK) * lane1.astype(F32)
    spare = jnp.where((lane1 < n_exp) & (spare < float(n_rows)), spare, -1.0)
    sub = lax.broadcasted_iota(I32, (SUBLANES, ROUTE_LANES), 0)
    zero_ref[...] = jnp.where(sub == 0, last_block, jnp.where(sub == 1, spare, -1.0)).astype(I32)


def moe_dest(route, cnt, *, tile, n_exp, n_blocks):
    n = route.shape[0]
    t = min(tile, n)
    nb_pad = -(-n_blocks // SUBLANES) * SUBLANES
    ut = (jnp.arange(ROUTE_LANES)[:, None] <= jnp.arange(ROUTE_LANES)[None, :]).astype(F32)
    return pl.pallas_call(
        functools.partial(_dest_body, t=t, n_exp=n_exp, nb_pad=nb_pad, n_rows=n_blocks * MOE_BLOCK),
        grid=(n // t,),
        in_specs=[pl.BlockSpec((t, ROUTE_LANES), lambda i: (i, 0)),
                  pl.BlockSpec((1, ROUTE_LANES), lambda i: (0, 0)),
                  pl.BlockSpec((ROUTE_LANES, ROUTE_LANES), lambda i: (0, 0))],
        out_specs=[pl.BlockSpec((SUBLANES, t), lambda i: (i, 0)),
                   pl.BlockSpec((nb_pad, ROUTE_LANES), lambda i: (0, 0)),
                   pl.BlockSpec((SUBLANES, ROUTE_LANES), lambda i: (0, 0))],
        out_shape=[jax.ShapeDtypeStruct((n // t * SUBLANES, t), I32),
                   jax.ShapeDtypeStruct((nb_pad, ROUTE_LANES), I32),
                   jax.ShapeDtypeStruct((SUBLANES, ROUTE_LANES), I32)],
        compiler_params=_cparams("arbitrary"),
        name="moe_dest",
    )(route, cnt, ut)


def _scatter_body(dest_ref, zero_ref, hx_ref, xb_ref, zero_s, sem, *, t, n_exp):
    @pl.when(pl.program_id(0) == 0)
    def _():
        zero_s[...] = jnp.zeros_like(zero_s)
        for wait in (False, True):
            for k in range(2):
                for e in range(n_exp):
                    start = zero_ref[k, e]

                    @pl.when(start >= 0)
                    def _(start=start, k=k):
                        copy = pltpu.make_async_copy(
                            zero_s, xb_ref.at[pl.ds(SLAB * jnp.maximum(start, 0), SLAB * MOE_BLOCK)], sem)
                        if wait:
                            copy.wait()
                        else:
                            copy.start(priority=k)

    def issue(r, _):
        for k in range(2):
            pltpu.make_async_copy(hx_ref.at[pl.ds(SLAB * r, SLAB)], xb_ref.at[pl.ds(SLAB * dest_ref[k, r], SLAB)],
                                  sem).start(priority=k)
        return 0

    lax.fori_loop(0, t, issue, 0, unroll=8)
    for k in range(2):
        pltpu.make_async_copy(hx_ref, xb_ref.at[pl.ds(0, SLAB * t)], sem).wait()


def moe_scatter(dest, zero_starts, hxp, n_rows, *, tile, n_exp):
    n = hxp.shape[0] // SLAB
    t = min(tile, n)
    per_dest_tile = dest.shape[1] // t
    return pl.pallas_call(
        functools.partial(_scatter_body, t=t, n_exp=n_exp),
        grid=(n // t,),
        in_specs=[pl.BlockSpec((SUBLANES, t), lambda i: (i // per_dest_tile, i % per_dest_tile),
                               memory_space=pltpu.SMEM),
                  pl.BlockSpec(zero_starts.shape, lambda i: (0, 0), memory_space=pltpu.SMEM),
                  pl.BlockSpec((t * SLAB, LANES), lambda i: (i, 0))],
        out_specs=pl.BlockSpec(memory_space=pl.ANY),
        out_shape=jax.ShapeDtypeStruct((n_rows * SLAB, LANES), I32),
        scratch_shapes=[pltpu.VMEM((MOE_BLOCK * SLAB, LANES), I32), pltpu.SemaphoreType.DMA],
        compiler_params=_cparams("arbitrary"),
        name="moe_scatter",
    )(dest, zero_starts, hxp)


def _expert_body(blk_ref, valid_ref, xb_ref, w1_ref, w3_ref, w2_ref, yb_ref, w1_s, w3_s, w2_s):
    i = pl.program_id(0)
    valid = valid_ref[i]
    half = MOE_BLOCK // 2
    changed = (i == 0) | (blk_ref[i] != blk_ref[jnp.maximum(i - 1, 0)])

    @pl.when(changed & (valid > 0))
    def _():
        w1_s[...] = w1_ref[0].astype(BF16)
        w3_s[...] = w3_ref[0].astype(BF16)
        w2_s[...] = w2_ref[0].astype(BF16)

    def run(rows):
        xblk = _unpack_bf16_pairs(_load_row_slabs(xb_ref, rows)).astype(BF16)
        h1 = jnp.dot(xblk, w1_s[...], preferred_element_type=F32)
        h3 = jnp.dot(xblk, w3_s[...], preferred_element_type=F32)
        hid = (h1 * _sigmoid(h1) * h3).astype(BF16)
        _store_row_slabs(yb_ref, _pack_bf16_pairs(jnp.dot(hid, w2_s[...], preferred_element_type=F32)))

    @pl.when(valid > half)
    def _():
        run(MOE_BLOCK)

    @pl.when(valid <= half)
    def _():
        yb_ref[...] = jnp.zeros_like(yb_ref)

    @pl.when((valid > 0) & (valid <= half))
    def _():
        run(half)


def moe_experts(blk_exp, blk_valid, xb, w1, w3, w2):
    p = xb.shape[0] // SLAB
    _, d, de = w1.shape
    nb = p // MOE_BLOCK
    grid_spec = pltpu.PrefetchScalarGridSpec(
        num_scalar_prefetch=2,
        grid=(nb,),
        in_specs=[pl.BlockSpec((MOE_BLOCK * SLAB, LANES), lambda i, blk, valid: (i, 0)),
                  pl.BlockSpec((1, d, de), lambda i, blk, valid: (blk[i], 0, 0)),
                  pl.BlockSpec((1, d, de), lambda i, blk, valid: (blk[i], 0, 0)),
                  pl.BlockSpec((1, de, d), lambda i, blk, valid: (blk[i], 0, 0))],
        out_specs=pl.BlockSpec((MOE_BLOCK * SLAB, LANES), lambda i, blk, valid: (i, 0)),
        scratch_shapes=[pltpu.VMEM((d, de), BF16), pltpu.VMEM((d, de), BF16), pltpu.VMEM((de, d), BF16)],
    )
    return pl.pallas_call(
        _expert_body,
        grid_spec=grid_spec,
        out_shape=jax.ShapeDtypeStruct((p * SLAB, LANES), I32),
        compiler_params=_cparams("arbitrary"),
        name="moe_experts",
    )(blk_exp, blk_valid, xb, w1, w3, w2)


COMBINE_PARTS = 4


def _combine_body(dest_ref, x1_ref, route_ref, g2_ref, fg_ref, yb_ref, o_ref, y1_s, y2_s, sems, *, t):
    tp = t // COMBINE_PARTS
    for part in range(COMBINE_PARTS):
        def issue(r, _, sem=sems.at[part]):
            pltpu.make_async_copy(yb_ref.at[pl.ds(SLAB * dest_ref[0, r], SLAB)], y1_s.at[pl.ds(SLAB * r, SLAB)],
                                  sem).start(priority=0)
            pltpu.make_async_copy(yb_ref.at[pl.ds(SLAB * dest_ref[1, r], SLAB)], y2_s.at[pl.ds(SLAB * r, SLAB)],
                                  sem).start(priority=1)
            return 0

        lax.fori_loop(part * tp, (part + 1) * tp, issue, 0, unroll=8)
    for part in range(COMBINE_PARTS):
        rows = slice(part * tp, (part + 1) * tp)
        lines = pl.ds(SLAB * part * tp, SLAB * tp)
        for y_s in (y1_s, y2_s):
            pltpu.make_async_copy(yb_ref.at[pl.ds(0, SLAB * tp)], y_s.at[lines], sems.at[part]).wait()
        route = route_ref[rows, :]
        moe = (route[:, 4:5] * _unpack_bf16_pairs(_load_row_slabs(y1_s, tp, part * tp))
               + route[:, 5:6] * _unpack_bf16_pairs(_load_row_slabs(y2_s, tp, part * tp)))
        x2 = x1_ref[0, rows, :] + g2_ref[0] * moe
        ms = jnp.mean(x2 * x2, axis=-1, keepdims=True)
        o_ref[0, rows, :] = x2 * lax.rsqrt(ms + EPS) * fg_ref[...]


def moe_combine(dest, x1, route, g2, final_g, yb, *, tile_l):
    b, l, d = x1.shape
    t = min(tile_l, l)
    nt = l // t
    slab = (t * SLAB, LANES)
    per_dest_tile = dest.shape[1] // t
    return pl.pallas_call(
        functools.partial(_combine_body, t=t),
        grid=(b, nt),
        in_specs=[pl.BlockSpec((SUBLANES, t),
                               lambda bi, i: ((bi * nt + i) // per_dest_tile, (bi * nt + i) % per_dest_tile),
                               memory_space=pltpu.SMEM),
                  pl.BlockSpec((1, t, d), lambda bi, i: (bi, i, 0)),
                  pl.BlockSpec((t, ROUTE_LANES), lambda bi, i: (bi * nt + i, 0)),
                  pl.BlockSpec((1, 1, d), lambda bi, i: (bi, 0, 0)),
                  pl.BlockSpec((1, d), lambda bi, i: (0, 0)),
                  pl.BlockSpec(memory_space=pl.ANY)],
        out_specs=pl.BlockSpec((1, t, d), lambda bi, i: (bi, i, 0)),
        out_shape=jax.ShapeDtypeStruct((b, l, d), F32),
        scratch_shapes=[pltpu.VMEM(slab, I32), pltpu.VMEM(slab, I32), pltpu.SemaphoreType.DMA((COMBINE_PARTS,))],
        compiler_params=_cparams("arbitrary", "arbitrary"),
        name="moe_combine",
    )(dest, x1, route, g2, final_g, yb)


def kernel(x, c, ctx, c_ctx, ada_w, ada_b, norm1_g, norm2_g, final_g, w_in, rg_conv_w, rg_conv_b, rg_wa_f, rg_ba_f, rg_wx_f, rg_bx_f, rg_lam_f, rg_wa_b, rg_ba_b, rg_wx_b, rg_bx_b, rg_lam_b, rg_proj, hy_conv_w, hy_conv_b, hy_pos_w1, hy_pos_b1, hy_pos_w2, hy_pos_b2, hy_freq, hy_pos_w3, hy_skip, hy_proj, w_out, moe_wg, moe_bg, moe_we, moe_be, moe_w1, moe_w3, moe_w2):
    B, L, D = x.shape
    C = rg_conv_w.shape[-1]
    LC = ctx.shape[1]
    c8 = jnp.zeros((8, D), F32).at[:B].set(c).at[B].set(c_ctx)
    mods = ada_mods(c8, ada_w[0], ada_b)
    sh1, sc1, g1 = (mods[:B, None, k * D:(k + 1) * D] for k in range(3))
    sh2, sc2, g2 = (mods[:B, None, k * D:(k + 1) * D] for k in range(3, 6))
    csh1 = jnp.broadcast_to(mods[B:B + 1, None, 0:D], (B, 1, D))
    csc1 = jnp.broadcast_to(mods[B:B + 1, None, D:2 * D], (B, 1, D))

    w_in_b = w_in[0].astype(BF16)
    w_rm = jnp.concatenate([w_in_b[:, :2 * C], w_in_b[:, 5 * C:]], axis=1)
    wg_f = gate_blocks(rg_wa_f[0], rg_wx_f[0], C // 256)
    wg_b = gate_blocks(rg_wa_b[0], rg_wx_b[0], C // 256)
    rg_f = (wg_f, rg_ba_f, rg_bx_f, rg_lam_f)
    rg_b = (wg_b, rg_ba_b, rg_bx_b, rg_lam_b)

    pc = norm_mod_proj(ctx, norm1_g, csh1, csc1, w_rm[:, :C], rg_conv_w[0], rg_conv_b, tile_l=LC, chunk=C)
    zero = jnp.zeros((B, 1, C), F32)
    _, hcf = rg_scan(pc, 0, *rg_f, zero, reverse=False, tile_l=TILE_SCAN)
    _, hcb = rg_scan(pc, 0, *rg_b, zero, reverse=True, tile_l=TILE_SCAN)

    p_rm = norm_mod_proj(x, norm1_g, sh1, sc1, w_rm, rg_conv_w[0], rg_conv_b, tile_l=TILE_PROJ, chunk=PROJ_CHUNK)
    hy_taps = jnp.concatenate([hy_conv_w[0], hy_conv_b], axis=0)
    u_hy, z0_hy = hyena_proj(x, norm1_g, sh1, sc1, w_in_b[:, 2 * C:5 * C], hy_taps, tile_l=TILE_PROJ)
    h_f, _ = rg_scan(p_rm, 0, *rg_f, hcf, reverse=False, tile_l=TILE_SCAN)
    h_b, _ = rg_scan(p_rm, 0, *rg_b, hcb, reverse=True, tile_l=TILE_SCAN)

    tables = dft_tables(L)
    assert hy_pos_w1.shape[1] == len(HY_FEATURE_ORDER)
    w1t = jnp.zeros((HY_HID, HY_HID), F32).at[:, :hy_pos_w1.shape[1]].set(hy_pos_w1[0].T[:, jnp.array(HY_FEATURE_ORDER)])
    kt = hyena_filter_t(w1t, hy_pos_b1[0][:, None], hy_pos_w2[0].T, hy_pos_b2[0][:, None], hy_freq[0][:, None],
                        hy_pos_w3[0].T.reshape(2, C, HY_HID).astype(BF16), L, FILTER_TILE_C)
    spec = hyena_spectrum(kt.reshape(2, C, L // LANES, LANES), tables, SPECTRUM_GROUP)
    y_hy_t = hyena_fftconv(u_hy, z0_hy, hy_skip[0][:, None, None], spec, tables, tile_c=FFT_TILE_C, group=FFT_GROUP)

    n_exp = moe_we.shape[-1]
    n_grp = moe_wg.shape[-1]
    assert n_grp == N_GROUPS and n_exp == N_GROUPS * EXPERTS_PER_GROUP
    wr = jnp.zeros((D, ROUTE_LANES), F32).at[:, :n_grp].set(moe_wg[0]).at[:, n_grp:n_grp + n_exp].set(moe_we[0])
    br = jnp.zeros((1, ROUTE_LANES), F32).at[:, :n_grp].set(moe_bg).at[:, n_grp:n_grp + n_exp].set(moe_be)
    wr_hi = wr.astype(BF16)
    wr_split = jnp.concatenate([wr_hi, (wr - wr_hi.astype(F32)).astype(BF16)], axis=1)
    x1, hxp, route, cnt = mix_route(x, h_f, h_b, p_rm, y_hy_t, rg_proj[0].astype(BF16), hy_proj[0].astype(BF16),
                                    (0.5 * w_out[0]).astype(BF16), g1, norm2_g, sh2, sc2, wr_split, br, tile_l=TILE_MIX,
                                    n_exp=n_exp)

    n_blocks = (2 * B * L + n_exp * (MOE_BLOCK - 1)) // MOE_BLOCK
    dest, blk, zero_starts = moe_dest(route, cnt, tile=TILE_DEST, n_exp=n_exp, n_blocks=n_blocks)
    xb = moe_scatter(dest, zero_starts, hxp, n_blocks * MOE_BLOCK, tile=TILE_DISPATCH, n_exp=n_exp)
    yb = moe_experts(blk[:n_blocks, 0], blk[:n_blocks, 1], xb, moe_w1[0], moe_w3[0], moe_w2[0])
    return moe_combine(dest, x1, route, g2, final_g[None], yb, tile_l=TILE_DISPATCH)
```

```python
import functools
import math

import jax
import jax.numpy as jnp
from jax import lax
from jax.experimental import pallas as pl
from jax.experimental.pallas import tpu as pltpu

F32 = jnp.float32
BF16 = jnp.bfloat16
I32 = jnp.int32
HIGHEST = lax.Precision.HIGHEST

LANES = 128
SUBLANES = 8
EPS = 1e-6
RG_C = 8.0
GRID_W = 64
HY_SEQ_BANDS = 16
HY_COL_BANDS = 8
HY_DECAY_TARGET = 1e-2
HY_FAST_DECAY = 0.3
HY_SLOW_DECAY = 1.5
N_GROUPS = 4
EXPERTS_PER_GROUP = 8
MOE_BLOCK = 512
VMEM_LIMIT = 56 * 1024 * 1024

TILE_PROJ = 512
PROJ_CHUNK = 1024
CONV_CHUNK = 256
TILE_SCAN = 512
TILE_MIX = 512
TILE_DEST = 2048
TILE_DISPATCH = 1024
FILTER_TILE_C = 256
SPECTRUM_GROUP = 64
FFT_TILE_C = 128
FFT_GROUP = 8


def _cparams(*sem):
    return pltpu.CompilerParams(dimension_semantics=sem, vmem_limit_bytes=VMEM_LIMIT)


def _sigmoid(x):
    return 0.5 * (jnp.tanh(0.5 * x) + 1.0)


def _gelu_tanh(x):
    c = math.sqrt(2.0 / math.pi)
    h = 0.5 * x
    return h + h * jnp.tanh(x * (c + (0.044715 * c) * (x * x)))


def _ada_body(c_ref, w_ref, b_ref, o_ref):
    c = c_ref[...]
    s = c * _sigmoid(c)
    o_ref[...] = jnp.dot(s, w_ref[...], precision=HIGHEST, preferred_element_type=F32) + b_ref[...]


def ada_mods(c8, ada_w, ada_b):
    d, m = ada_w.shape
    tn = 1024 if m % 1024 == 0 else m
    return pl.pallas_call(
        _ada_body,
        grid=(m // tn,),
        in_specs=[pl.BlockSpec((c8.shape[0], d), lambda j: (0, 0)),
                  pl.BlockSpec((d, tn), lambda j: (0, j)),
                  pl.BlockSpec((1, tn), lambda j: (0, j))],
        out_specs=pl.BlockSpec((c8.shape[0], tn), lambda j: (0, j)),
        out_shape=jax.ShapeDtypeStruct((c8.shape[0], m), F32),
        compiler_params=_cparams("parallel"),
        name="ada_mods",
    )(c8, ada_w, ada_b)


def _norm_mod(x_ref, g_ref, sh_ref, sc_ref):
    x = x_ref[0]
    ms = jnp.mean(x * x, axis=-1, keepdims=True)
    y = x * lax.rsqrt(ms + EPS) * g_ref[...]
    return (y * (1.0 + sc_ref[0]) + sh_ref[0]).astype(BF16)


def _first_rows_body(x_ref, g_ref, sh_ref, sc_ref, w_ref, o_ref):
    nt, rows, d = x_ref.shape[1:]
    x = x_ref[0].reshape(nt * rows, d)
    ms = jnp.mean(x * x, axis=-1, keepdims=True)
    y = x * lax.rsqrt(ms + EPS) * g_ref[...]
    hx = (y * (1.0 + sc_ref[0]) + sh_ref[0]).astype(BF16)
    o_ref[0] = jnp.dot(hx, w_ref[...], preferred_element_type=F32).reshape(nt, rows, w_ref.shape[1])


def proj_first_rows(x, g, shift, scale, w, *, tile_l):
    b, l, d = x.shape
    t = min(tile_l, l)
    nt = l // t
    m = w.shape[1]
    return pl.pallas_call(
        _first_rows_body,
        grid=(b,),
        in_specs=[pl.BlockSpec((1, nt, SUBLANES, d), lambda bi: (bi, 0, 0, 0)),
                  pl.BlockSpec((1, d), lambda bi: (0, 0)),
                  pl.BlockSpec((1, 1, d), lambda bi: (bi, 0, 0)),
                  pl.BlockSpec((1, 1, d), lambda bi: (bi, 0, 0)),
                  pl.BlockSpec(w.shape, lambda bi: (0, 0))],
        out_specs=pl.BlockSpec((1, nt, SUBLANES, m), lambda bi: (bi, 0, 0, 0)),
        out_shape=jax.ShapeDtypeStruct((b, nt, SUBLANES, m), F32),
        compiler_params=_cparams("parallel"),
        name="proj_first_rows",
    )(x.reshape(b, nt, t, d), g, shift, scale, w)


def _proj_body(x_ref, g_ref, sh_ref, sc_ref, w_ref, cw_ref, cb_ref, nxt_ref, o_ref, last_s, ext_s,
               *, chunk, n_tiles, t, c):
    i = pl.program_id(1)

    @pl.when(i == 0)
    def _():
        last_s[...] = jnp.zeros_like(last_s)

    hx = _norm_mod(x_ref, g_ref, sh_ref, sc_ref)
    has_next = (i < n_tiles - 1).astype(F32)
    cw = cw_ref[...]
    cc = ext_s.shape[1]
    for j in range(c // cc):
        cols = slice(j * cc, (j + 1) * cc)
        p = jnp.dot(hx, w_ref[:, cols], preferred_element_type=F32)
        ext_s[0:SUBLANES, :] = last_s[:, cols]
        ext_s[SUBLANES:SUBLANES + t, :] = p
        ext_s[SUBLANES + t:2 * SUBLANES + t, :] = nxt_ref[0, 0, :, cols] * has_next
        xc = cb_ref[:, cols] + cw[2:3, cols] * p
        for tap, off in ((0, -2), (1, -1), (3, 1)):
            xc = xc + cw[tap:tap + 1, cols] * ext_s[SUBLANES + off:SUBLANES + off + t, :]
        o_ref[0, :, cols] = xc.astype(o_ref.dtype)
        last_s[:, cols] = p[t - SUBLANES:t, :]
    m = w_ref.shape[1]
    for j in range((m - c) // chunk):
        cols = slice(c + j * chunk, c + (j + 1) * chunk)
        o_ref[0, :, cols] = jnp.dot(hx, w_ref[:, cols], preferred_element_type=F32).astype(o_ref.dtype)


def norm_mod_proj(x, g, shift, scale, w, conv_w, conv_b, *, tile_l, chunk):
    b, l, d = x.shape
    m = w.shape[1]
    c = conv_w.shape[1]
    tl = min(tile_l, l)
    n_tiles = l // tl
    chunk = min(chunk, max(m - c, 1))
    assert (m - c) % chunk == 0
    nxt = proj_first_rows(x, g, shift, scale, w[:, :c], tile_l=tile_l)
    return pl.pallas_call(
        functools.partial(_proj_body, chunk=chunk, n_tiles=n_tiles, t=tl, c=c),
        grid=(b, n_tiles),
        in_specs=[pl.BlockSpec((1, tl, d), lambda bi, i: (bi, i, 0)),
                  pl.BlockSpec((1, d), lambda bi, i: (0, 0)),
                  pl.BlockSpec((1, 1, d), lambda bi, i: (bi, 0, 0)),
                  pl.BlockSpec((1, 1, d), lambda bi, i: (bi, 0, 0)),
                  pl.BlockSpec(w.shape, lambda bi, i: (0, 0)),
                  pl.BlockSpec(conv_w.shape, lambda bi, i: (0, 0)),
                  pl.BlockSpec(conv_b.shape, lambda bi, i: (0, 0)),
                  pl.BlockSpec((1, 1, SUBLANES, c), lambda bi, i: (bi, jnp.minimum(i + 1, n_tiles - 1), 0, 0))],
        out_specs=pl.BlockSpec((1, tl, m), lambda bi, i: (bi, i, 0)),
        out_shape=jax.ShapeDtypeStruct((b, l, m), BF16),
        scratch_shapes=[pltpu.VMEM((SUBLANES, c), F32), pltpu.VMEM((tl + 2 * SUBLANES, min(CONV_CHUNK, c)), F32)],
        compiler_params=_cparams("parallel", "arbitrary"),
        name="norm_mod_proj",
    )(x, g, shift, scale, w, conv_w, conv_b, nxt)


def _scan_body(xc_ref, wg_ref, ba_ref, bx_ref, lam_ref, h0_ref,
               h_ref, hl_ref, xc_s, g_s, a_s, b_s, hloc_s, pcum_s, carry_s,
               *, reverse, t, c, s_len, pitch):
    i = pl.program_id(1)
    n_slab = c // LANES
    n_blk = wg_ref.shape[0]
    blk = c // n_blk

    @pl.when(i == 0)
    def _():
        carry_s[...] = h0_ref[0]

    xc_s[...] = xc_ref[0].astype(F32)

    for k in range(n_blk):
        g_s[:, k * 2 * blk:(k + 1) * 2 * blk] = jnp.dot(xc_ref[0, :, k * blk:(k + 1) * blk], wg_ref[k],
                                                          preferred_element_type=F32)

    lam = lam_ref[...]
    softplus_neg_lam = jnp.maximum(-lam, 0.0) + jnp.log1p(jnp.exp(-jnp.abs(lam)))
    half_ca = (-0.5 * RG_C) * softplus_neg_lam
    half_ba, half_bx = 0.5 * ba_ref[...], 0.5 * bx_ref[...]
    slabs_per_blk = blk // LANES
    for j in range(SUBLANES):
        r0 = j * s_len
        for k in range(n_slab):
            kb, ks = k // slabs_per_blk, k % slabs_per_blk
            ga = g_s[r0:r0 + s_len, kb * 2 * blk + ks * LANES:kb * 2 * blk + (ks + 1) * LANES]
            gx = g_s[r0:r0 + s_len, kb * 2 * blk + blk + ks * LANES:kb * 2 * blk + blk + (ks + 1) * LANES]
            lane = slice(k * LANES, (k + 1) * LANES)
            half_x = 0.5 * xc_s[r0:r0 + s_len, lane]
            hca = half_ca[:, lane]
            log_a = hca * jnp.tanh(ga + half_ba[:, lane]) + hca
            gated_x = half_x * jnp.tanh(gx + half_bx[:, lane]) + half_x
            a = jnp.exp(log_a)
            a_s[k, j * pitch:j * pitch + s_len, :] = a
            gain2 = -jnp.tanh(log_a) * (a * a + 1.0)
            gain = jnp.where(gain2 > 0.0, gain2 * lax.rsqrt(gain2), 0.0)
            b_s[k, j * pitch:j * pitch + s_len, :] = gain * gated_x

    def step1(s, hp):
        hs, ps = hp
        srow = (s_len - 1 - s) if reverse else s
        hs2, ps2 = [], []
        for k in range(n_slab):
            av = a_s[k, pl.ds(srow, SUBLANES, stride=pitch), :]
            bv = b_s[k, pl.ds(srow, SUBLANES, stride=pitch), :]
            h = av * hs[k] + bv
            p = av * ps[k]
            hloc_s[k, pl.ds(srow, SUBLANES, stride=pitch), :] = h
            pcum_s[k, pl.ds(srow, SUBLANES, stride=pitch), :] = p
            hs2.append(h)
            ps2.append(p)
        return tuple(hs2), tuple(ps2)

    zeros = tuple(jnp.zeros((SUBLANES, LANES), F32) for _ in range(n_slab))
    ones = tuple(jnp.ones((SUBLANES, LANES), F32) for _ in range(n_slab))
    h_end, p_end = lax.fori_loop(0, s_len, step1, (zeros, ones), unroll=1 if reverse else 4)

    order = range(SUBLANES - 1, -1, -1) if reverse else range(SUBLANES)
    for k in range(n_slab):
        cst = carry_s[:, k * LANES:(k + 1) * LANES]
        for j in order:
            rows = slice(j * pitch, j * pitch + s_len)
            h_ref[0, j * s_len:(j + 1) * s_len, k * LANES:(k + 1) * LANES] = (
                hloc_s[k, rows, :] + pcum_s[k, rows, :] * cst).astype(h_ref.dtype)
            cst = p_end[k][j:j + 1] * cst + h_end[k][j:j + 1]
        carry_s[:, k * LANES:(k + 1) * LANES] = cst
    hl_ref[0] = carry_s[...]


def rg_scan(p, col_blk, wg, ba, bx, lam, h0, *, reverse, tile_l):
    b, l, _ = p.shape
    c = ba.shape[1]
    t = min(tile_l, l)
    n_tiles = l // t
    s_len = t // SUBLANES
    pitch = s_len + SUBLANES

    def nat(i):
        return (n_tiles - 1 - i) if reverse else i

    body = functools.partial(_scan_body, reverse=reverse, t=t, c=c, s_len=s_len, pitch=pitch)
    vec = pl.BlockSpec((1, c), lambda bi, i: (0, 0))
    return pl.pallas_call(
        body,
        grid=(b, n_tiles),
        in_specs=[pl.BlockSpec((1, t, c), lambda bi, i: (bi, nat(i), col_blk)),
                  pl.BlockSpec(wg.shape, lambda bi, i: (0, 0, 0)),
                  vec, vec, vec,
                  pl.BlockSpec((1, 1, c), lambda bi, i: (bi, 0, 0))],
        out_specs=[pl.BlockSpec((1, t, c), lambda bi, i: (bi, nat(i), 0)),
                   pl.BlockSpec((1, 1, c), lambda bi, i: (bi, 0, 0))],
        out_shape=[jax.ShapeDtypeStruct((b, l, c), BF16), jax.ShapeDtypeStruct((b, 1, c), F32)],
        scratch_shapes=[pltpu.VMEM((t, c), F32), pltpu.VMEM((t, 2 * c), F32)]
        + [pltpu.VMEM((c // LANES, SUBLANES * pitch, LANES), F32) for _ in range(4)]
        + [pltpu.VMEM((1, c), F32)],
        compiler_params=_cparams("parallel", "arbitrary"),
        name="rg_scan_bwd" if reverse else "rg_scan_fwd",
    )(p, wg, ba, bx, lam, h0)


def gate_blocks(wa, wx, n_blk):
    h, d, _ = wa.shape
    hp = h // n_blk
    eye = jnp.eye(hp, dtype=wa.dtype)

    def bd(w):
        w = w.reshape(n_blk, hp, d, d)
        return jnp.einsum('khde,hg->khdge', w, eye).reshape(n_blk, hp * d, hp * d)

    return (0.5 * jnp.concatenate([bd(wa), bd(wx)], axis=-1)).astype(BF16)


HY_HID = 64
HY_FEATURE_ORDER = (list(range(1, 1 + 2 * HY_SEQ_BANDS))
                    + list(range(2 + 2 * HY_SEQ_BANDS, 2 + 2 * HY_SEQ_BANDS + 2 * HY_COL_BANDS))
                    + [0, 1 + 2 * HY_SEQ_BANDS])


def _filter_body(w1t_ref, b1_ref, w2t_ref, b2_ref, fr_ref, w3t_ref, o_ref, z_s, *, l, c, ct, rows_grid):
    d = pl.program_id(0)
    j = pl.program_id(1)
    lane = lax.broadcasted_iota(I32, (1, l), 1)
    s_i = jnp.where(d == 0, lane, l - lane)
    sf = s_i.astype(F32)
    t_norm = sf / float(max(l - 1, 1))

    @pl.when(j == 0)
    def _():
        band_step = (HY_SEQ_BANDS - 1 - 1e-4) / (HY_SEQ_BANDS - 1)
        seq_band = 1e-4 + band_step * lax.broadcasted_iota(I32, (HY_SEQ_BANDS, 1), 0).astype(F32)
        col_band = 1.0 + lax.broadcasted_iota(I32, (HY_COL_BANDS, 1), 0).astype(F32)
        col_pos = (s_i & (GRID_W - 1)).astype(F32)
        row_lag = (s_i >> int(math.log2(GRID_W))).astype(F32) / float(rows_grid)
        ang_seq = ((2.0 * math.pi / l) * sf) * seq_band
        ang_col = ((2.0 * math.pi / GRID_W) * col_pos) * col_band
        n_trig = 2 * HY_SEQ_BANDS + 2 * HY_COL_BANDS
        trow = lax.broadcasted_iota(I32, (HY_HID - n_trig, 1), 0)
        tail = jnp.where(trow == 0, t_norm, jnp.where(trow == 1, row_lag, 0.0))
        feats = jnp.concatenate([jnp.cos(ang_seq), jnp.sin(ang_seq), jnp.cos(ang_col), jnp.sin(ang_col), tail], axis=0)
        fr = fr_ref[...]
        z = jnp.sin(fr * (jnp.dot(w1t_ref[...], feats, precision=HIGHEST, preferred_element_type=F32) + b1_ref[...]))
        z_s[...] = jnp.sin(fr * (jnp.dot(w2t_ref[...], z, precision=HIGHEST, preferred_element_type=F32) + b2_ref[...]))

    k = jnp.dot(w3t_ref[0], z_s[...].astype(BF16), preferred_element_type=F32)
    ch = (lax.broadcasted_iota(I32, (ct, 1), 0) + j * ct).astype(F32)
    max_decay = math.log(HY_DECAY_TARGET) / HY_FAST_DECAY
    min_decay = math.log(HY_DECAY_TARGET) / HY_SLOW_DECAY
    delta = jnp.abs(min_decay + ch * ((max_decay - min_decay) / (c - 1)))
    k = k * jnp.exp(-t_norm * delta)
    k = jnp.where((d == 1) & (lane == 0), 0.0, k)
    o_ref[0] = k.astype(o_ref.dtype)


def hyena_filter_t(w1t, b1, w2t, b2, fr, w3t, l, tile_c):
    assert GRID_W & (GRID_W - 1) == 0
    c = w3t.shape[1]
    ct = min(tile_c, c)
    body = functools.partial(_filter_body, l=l, c=c, ct=ct, rows_grid=l // GRID_W)
    small = lambda shape: pl.BlockSpec(shape, lambda d, j: (0,) * len(shape))
    return pl.pallas_call(
        body,
        grid=(2, c // ct),
        in_specs=[small(w1t.shape), small(b1.shape), small(w2t.shape), small(b2.shape), small(fr.shape),
                  pl.BlockSpec((1, ct, HY_HID), lambda d, j: (d, j, 0))],
        out_specs=pl.BlockSpec((1, ct, l), lambda d, j: (d, j, 0)),
        out_shape=jax.ShapeDtypeStruct((2, c, l), BF16),
        scratch_shapes=[pltpu.VMEM((HY_HID, l), F32)],
        compiler_params=_cparams("arbitrary", "arbitrary"),
        name="hyena_filter",
    )(w1t, b1, w2t, b2, fr, w3t)


def dft_tables(l):
    import numpy as np
    n = 2 * l
    r_in, nk = l // LANES, n // LANES
    ka = np.arange(nk)[:, None].astype(np.float64)
    r = np.arange(r_in)[None, :].astype(np.float64)
    a1 = 2.0 * np.pi * ka * r / nk
    f1 = np.concatenate([np.cos(a1), -np.sin(a1)], axis=0)
    lane = np.arange(LANES)[None, :].astype(np.float64)
    at = 2.0 * np.pi * ka * lane / n
    twr, twi = np.cos(at), -np.sin(at)
    a2 = 2.0 * np.pi * np.arange(LANES)[:, None] * np.arange(LANES)[None, :] / LANES
    cr, ci = np.cos(a2), -np.sin(a2)
    m2 = np.block([[cr, ci], [-ci, cr]])
    m2i = np.block([[cr, -ci], [ci, cr]])
    ai = 2.0 * np.pi * np.arange(r_in)[:, None] * np.arange(nk)[None, :] / nk
    gi = np.concatenate([np.cos(ai), -np.sin(ai)], axis=1) / n
    as_bf = lambda a: jnp.asarray(a, F32).astype(BF16)
    return as_bf(f1), jnp.asarray(twr, F32), jnp.asarray(twi, F32), as_bf(m2), as_bf(m2i), as_bf(gi)


def _fwd_rows_twiddle(x_a, x_b, f1, twr, twi, nk):
    a = jnp.dot(f1, jnp.concatenate([x_a, x_b], axis=1), preferred_element_type=F32)
    out = []
    for h in range(2):
        re, im = a[:nk, h * LANES:(h + 1) * LANES], a[nk:, h * LANES:(h + 1) * LANES]
        out.append((re * twr - im * twi, re * twi + im * twr))
    return out


def _spectrum_body(k_ref, f1_ref, twr_ref, twi_ref, m2_ref, o_ref, *, g, nk, r_in):
    f1, twr, twi = f1_ref[...], twr_ref[...], twi_ref[...]
    sign = jnp.where((lax.broadcasted_iota(I32, (nk, 1), 0) & 1) == 0, 1.0, -1.0)
    a2 = []
    for ci in range(g):
        (fre, fim), (bre, bim) = _fwd_rows_twiddle(k_ref[0, ci], k_ref[1, ci], f1, twr, twi, nk)
        a2.append(jnp.concatenate([fre + sign * bre, fim + sign * bim], axis=1).astype(BF16))
    spec = jnp.dot(jnp.concatenate(a2, axis=0), m2_ref[...], preferred_element_type=F32)
    o_ref[...] = spec.reshape(g, nk, 2 * LANES).astype(o_ref.dtype)


def hyena_spectrum(kt4, tables, group):
    _, c, r_in, _ = kt4.shape
    nk = 2 * r_in
    f1, twr, twi, m2, _, _ = tables
    g = min(group, c)
    full = lambda a: pl.BlockSpec(a.shape, lambda j: (0,) * a.ndim)
    return pl.pallas_call(
        functools.partial(_spectrum_body, g=g, nk=nk, r_in=r_in),
        grid=(c // g,),
        in_specs=[pl.BlockSpec((2, g, r_in, LANES), lambda j: (0, j, 0, 0)), full(f1), full(twr), full(twi), full(m2)],
        out_specs=pl.BlockSpec((g, nk, 2 * LANES), lambda j: (j, 0, 0)),
        out_shape=jax.ShapeDtypeStruct((c, nk, 2 * LANES), BF16),
        compiler_params=_cparams("parallel"),
        name="hyena_spectrum",
    )(kt4, f1, twr, twi, m2)


HY_CHUNK = 256


def _hyena_proj_body(x_ref, g_ref, sh_ref, sc_ref, w_ref, taps_ref, nxt_ref, u_ref, z0_ref, last_s, *, n_tiles, t, c):
    i = pl.program_id(1)

    @pl.when(i == 0)
    def _():
        last_s[...] = jnp.zeros_like(last_s)

    hx = _norm_mod(x_ref, g_ref, sh_ref, sc_ref)
    row = lax.broadcasted_iota(I32, (t, 1), 0)
    has_next = (i < n_tiles - 1).astype(F32)
    cw = min(HY_CHUNK, c)
    for j in range(c // cw):
        zs = []
        for k in range(3):
            cols = slice(k * c + j * cw, k * c + (j + 1) * cw)
            p = jnp.dot(hx, w_ref[:, cols], preferred_element_type=F32)
            up = jnp.where(row == 0, last_s[:, cols], pltpu.roll(p, 1, 0))
            dn = jnp.where(row == t - 1, nxt_ref[0, 0, 0:1, cols] * has_next, pltpu.roll(p, t - 1, 0))
            tp = taps_ref[:, cols]
            zs.append(tp[3:4] + tp[0:1] * up + tp[1:2] * p + tp[2:3] * dn)
            last_s[:, cols] = p[t - 1:t, :]
        z0, z1, zv = zs
        u_t, z0_t = (zv * z1).T, z0.T
        for q in range(t // LANES):
            u_ref[0, q, j * cw:(j + 1) * cw, :] = u_t[:, q * LANES:(q + 1) * LANES].astype(u_ref.dtype)
            z0_ref[0, q, j * cw:(j + 1) * cw, :] = z0_t[:, q * LANES:(q + 1) * LANES].astype(z0_ref.dtype)


def hyena_proj(x, g, shift, scale, w, taps, *, tile_l):
    b, l, d = x.shape
    c = w.shape[1] // 3
    t = min(tile_l, l)
    n_tiles = l // t
    rq = t // LANES
    nxt = proj_first_rows(x, g, shift, scale, w, tile_l=tile_l)
    o_spec = pl.BlockSpec((1, rq, c, LANES), lambda bi, i: (bi, i, 0, 0))
    o_shape = jax.ShapeDtypeStruct((b, l // LANES, c, LANES), BF16)
    return pl.pallas_call(
        functools.partial(_hyena_proj_body, n_tiles=n_tiles, t=t, c=c),
        grid=(b, n_tiles),
        in_specs=[pl.BlockSpec((1, t, d), lambda bi, i: (bi, i, 0)),
                  pl.BlockSpec((1, d), lambda bi, i: (0, 0)),
                  pl.BlockSpec((1, 1, d), lambda bi, i: (bi, 0, 0)),
                  pl.BlockSpec((1, 1, d), lambda bi, i: (bi, 0, 0)),
                  pl.BlockSpec(w.shape, lambda bi, i: (0, 0)),
                  pl.BlockSpec(taps.shape, lambda bi, i: (0, 0)),
                  pl.BlockSpec((1, 1, SUBLANES, 3 * c), lambda bi, i: (bi, jnp.minimum(i + 1, n_tiles - 1), 0, 0))],
        out_specs=[o_spec, o_spec],
        out_shape=[o_shape, o_shape],
        scratch_shapes=[pltpu.VMEM((1, 3 * c), F32)],
        compiler_params=_cparams("parallel", "arbitrary"),
        name="hyena_proj",
    )(x, g, shift, scale, w, taps, nxt)


def _fftconv_body(u_ref, z0_ref, skip_ref, k_ref, f1_ref, twr_ref, twi_ref, m2_ref, m2i_ref, gi_ref,
                  o_ref, u_s, z0_s, y_s, *, ct, g, nk, r_in, pitch):
    for r in range(r_in):
        u_s[r * pitch:r * pitch + ct, :] = u_ref[0, r].astype(F32)
        z0_s[r * pitch:r * pitch + ct, :] = z0_ref[0, r].astype(F32)
    f1, twr, twi = f1_ref[...], twr_ref[...], twi_ref[...]

    def chan(ref, ch):
        return ref[pl.ds(ch, r_in, stride=pitch), :]

    def rows_fwd(c0):
        a2 = []
        for ci in range(0, g, 2):
            pair = _fwd_rows_twiddle(chan(u_s, c0 + ci).astype(BF16), chan(u_s, c0 + ci + 1).astype(BF16),
                                     f1, twr, twi, nk)
            a2 += [jnp.concatenate([tre, tim], axis=1).astype(BF16) for tre, tim in pair]
        return jnp.concatenate(a2, axis=0)

    def lanes_fwd(a2):
        return jnp.dot(a2, m2_ref[...], preferred_element_type=F32)

    def times_filter(c0, spec):
        kf = k_ref[pl.ds(c0, g)].astype(F32).reshape(g * nk, 2 * LANES)
        sre, sim = spec[:, :LANES], spec[:, LANES:]
        kre, kim = kf[:, :LANES], kf[:, LANES:]
        return jnp.concatenate([sre * kre - sim * kim, sre * kim + sim * kre], axis=1).astype(BF16)

    def lanes_inv(prod):
        return jnp.dot(prod, m2i_ref[...], preferred_element_type=F32)

    def rows_inv(c0, cc):
        for ci in range(0, g, 2):
            st = []
            for h in range(2):
                blk = cc[(ci + h) * nk:(ci + h + 1) * nk]
                cre, cim = blk[:, :LANES], blk[:, LANES:]
                st.append(jnp.concatenate([cre * twr + cim * twi, cim * twr - cre * twi], axis=0).astype(BF16))
            y2 = jnp.dot(gi_ref[...], jnp.concatenate(st, axis=1), preferred_element_type=F32)
            for h in range(2):
                ch = c0 + ci + h
                y = y2[:, h * LANES:(h + 1) * LANES]
                y_s[pl.ds(ch, r_in, stride=pitch), :] = (y + chan(u_s, ch) * skip_ref[ch]) * chan(z0_s, ch)

    def two_groups(i, _):
        ca, cb = 2 * g * i, 2 * g * i + g
        a2_a = rows_fwd(ca)
        spec_a = lanes_fwd(a2_a)
        a2_b = rows_fwd(cb)
        prod_a = times_filter(ca, spec_a)
        spec_b = lanes_fwd(a2_b)
        cc_a = lanes_inv(prod_a)
        prod_b = times_filter(cb, spec_b)
        cc_b = lanes_inv(prod_b)
        rows_inv(ca, cc_a)
        rows_inv(cb, cc_b)
        return 0

    lax.fori_loop(0, ct // (2 * g), two_groups, 0)
    for r in range(r_in):
        o_ref[0, r] = y_s[r * pitch:r * pitch + ct, :].astype(o_ref.dtype)


def hyena_fftconv(u, z0, skip3, spec, tables, *, tile_c, group):
    b, r_in, c, _ = u.shape
    nk = 2 * r_in
    ct = min(tile_c, c)
    g = min(group, ct // 2)
    assert ct % (2 * g) == 0 and g % 2 == 0
    pitch = ct + SUBLANES
    f1, twr, twi, m2, m2i, gi = tables
    full = lambda a: pl.BlockSpec(a.shape, lambda j, bi: (0,) * a.ndim)
    io_spec = pl.BlockSpec((1, r_in, ct, LANES), lambda j, bi: (bi, 0, j, 0))
    return pl.pallas_call(
        functools.partial(_fftconv_body, ct=ct, g=g, nk=nk, r_in=r_in, pitch=pitch),
        grid=(c // ct, b),
        in_specs=[io_spec, io_spec,
                  pl.BlockSpec((ct, 1, 1), lambda j, bi: (j, 0, 0)),
                  pl.BlockSpec((ct, nk, 2 * LANES), lambda j, bi: (j, 0, 0)),
                  full(f1), full(twr), full(twi), full(m2), full(m2i), full(gi)],
        out_specs=io_spec,
        out_shape=jax.ShapeDtypeStruct((b, r_in, c, LANES), BF16),
        scratch_shapes=[pltpu.VMEM((r_in * pitch, LANES), F32) for _ in range(3)],
        compiler_params=_cparams("parallel", "arbitrary"),
        name="hyena_fftconv",
    )(u, z0, skip3, spec, f1, twr, twi, m2, m2i, gi)


ROUTE_LANES = LANES
NEG_BIG = -1e30
HALF_WORD = 16


def _pack_bf16_pairs(v):
    h = v.shape[1] // 2
    bits = pltpu.bitcast(v.astype(BF16).astype(F32), I32)
    return bits[:, :h] | lax.shift_right_logical(bits[:, h:], HALF_WORD)


def _unpack_bf16_pairs(w):
    hi = pltpu.bitcast(w & jnp.int32(-65536), F32)
    lo = pltpu.bitcast(lax.shift_left(w, HALF_WORD), F32)
    return jnp.concatenate([hi, lo], axis=1)


SLAB = 4


def _store_row_slabs(ref, words, row0=0):
    rows = words.shape[0]
    for j in range(SLAB):
        ref[pl.ds(SLAB * row0 + j, rows, stride=SLAB), :] = words[:, j * LANES:(j + 1) * LANES]


def _load_row_slabs(ref, rows=None, row0=0):
    rows = ref.shape[0] // SLAB if rows is None else rows
    return jnp.concatenate([ref[pl.ds(SLAB * row0 + j, rows, stride=SLAB), :] for j in range(SLAB)], axis=1)


MIX_SUB = 512


def _mix_route_body(x_ref, hf_ref, hb_ref, prg_ref, pga_ref, pgb_ref, yt_ref, rgp_ref, hyp_ref, wo_ref, g1_ref,
                    n2g_ref, sh2_ref, sc2_ref, wr_ref, br_ref, tri_ref,
                    x1_ref, hxp_ref, route_ref, cnt_ref, carry_s, *, t, sub, n_exp):
    @pl.when((pl.program_id(0) == 0) & (pl.program_id(1) == 0))
    def _():
        carry_s[...] = jnp.zeros_like(carry_s)

    running = carry_s[...]
    for r0 in range(0, t, sub):
        rows = slice(r0, r0 + sub)
        hsum = hf_ref[0, rows, :].astype(F32) + hb_ref[0, rows, :].astype(F32)
        y_rg = (hsum * _gelu_tanh(prg_ref[0, rows, :].astype(F32))).astype(BF16)
        t1 = jnp.dot(y_rg, rgp_ref[...], preferred_element_type=F32)
        t2 = jnp.concatenate([lax.dot_general(yt_ref[0, q], hyp_ref[...], (((0,), (0,)), ((), ())),
                                              preferred_element_type=F32)
                              for q in range(r0 // LANES, (r0 + sub) // LANES)], axis=0)
        merged = ((t1 + t2) + jnp.tanh(0.5 * pga_ref[0, rows, :].astype(F32)) * t1
                  + jnp.tanh(0.5 * pgb_ref[0, rows, :].astype(F32)) * t2)
        out = jnp.dot(merged.astype(BF16), wo_ref[...], preferred_element_type=F32)
        x1 = x_ref[0, rows, :] + g1_ref[0] * out
        x1_ref[0, rows, :] = x1
        ms = jnp.mean(x1 * x1, axis=-1, keepdims=True)
        hx2 = (x1 * lax.rsqrt(ms + EPS) * n2g_ref[...]) * (1.0 + sc2_ref[0]) + sh2_ref[0]
        _store_row_slabs(hxp_ref, _pack_bf16_pairs(hx2), r0)

        hx_hi = hx2.astype(BF16)
        hx_lo = (hx2 - hx_hi.astype(F32)).astype(BF16)
        parts = (jnp.dot(hx_hi, wr_ref[...], preferred_element_type=F32)
                 + jnp.dot(hx_lo, wr_ref[...], preferred_element_type=F32))
        logits = parts[:, :ROUTE_LANES] + parts[:, ROUTE_LANES:] + br_ref[...]
        lane = lax.broadcasted_iota(I32, (sub, ROUTE_LANES), 1)
        is_g = lane < N_GROUPS
        glog = jnp.where(is_g, logits, NEG_BIG)
        gmax = jnp.max(glog, axis=1, keepdims=True)
        gidx = jnp.min(jnp.where(glog == gmax, lane, ROUTE_LANES), axis=1, keepdims=True)
        gsum = jnp.sum(jnp.where(is_g, jnp.exp(glog - gmax), 0.0), axis=1, keepdims=True)
        p_g = 1.0 / gsum
        e_lane = lane - N_GROUPS
        grp_of_lane = lax.shift_right_arithmetic(e_lane, int(math.log2(EXPERTS_PER_GROUP)))
        in_grp = (e_lane >= 0) & (e_lane < n_exp) & (grp_of_lane == gidx)
        elog = jnp.where(in_grp, logits, NEG_BIG)
        m1 = jnp.max(elog, axis=1, keepdims=True)
        i1 = jnp.min(jnp.where(elog == m1, lane, ROUTE_LANES), axis=1, keepdims=True)
        elog2 = jnp.where(lane == i1, NEG_BIG, elog)
        m2 = jnp.max(elog2, axis=1, keepdims=True)
        i2 = jnp.min(jnp.where(elog2 == m2, lane, ROUTE_LANES), axis=1, keepdims=True)
        e21 = jnp.exp(m2 - m1)
        pk1 = 1.0 / (1.0 + e21)
        wt1, wt2 = p_g * pk1, p_g * (e21 * pk1)

        oh1 = (lane == i1 - N_GROUPS).astype(F32)
        oh2 = (lane == i2 - N_GROUPS).astype(F32)
        cnt = oh1 + oh2
        before = jnp.dot(tri_ref[...], cnt.astype(BF16), preferred_element_type=F32) + running
        rank1 = jnp.sum(oh1 * before, axis=1, keepdims=True)
        rank2 = jnp.sum(oh2 * before, axis=1, keepdims=True)
        running = running + jnp.sum(cnt, axis=0, keepdims=True)
        vals = ((i1 - N_GROUPS).astype(F32), (i2 - N_GROUPS).astype(F32), rank1, rank2, wt1, wt2)
        route = jnp.zeros((sub, ROUTE_LANES), F32)
        for k, v in enumerate(vals):
            route = jnp.where(lane == k, v, route)
        route_ref[rows, :] = route
    carry_s[...] = running
    cnt_ref[...] = running


def mix_route(x, h_f, h_b, p_rm, y_hy_t, rg_proj, hy_proj, w_out, g1, n2g, sh2, sc2, wr, br, *, tile_l, n_exp):
    b, l, d = x.shape
    c = h_f.shape[2]
    t = min(tile_l, l)
    nt = l // t
    n = b * l
    sub = min(MIX_SUB, t)
    tri = (jnp.arange(sub)[:, None] > jnp.arange(sub)[None, :]).astype(BF16)
    tok = lambda bi, i: (bi, i, 0)
    col = lambda k: (lambda bi, i: (bi, i, k))
    full2 = lambda a: pl.BlockSpec(a.shape, lambda bi, i: (0, 0))
    per_b = pl.BlockSpec((1, 1, d), lambda bi, i: (bi, 0, 0))
    row = lambda bi, i: (bi * nt + i, 0)
    return pl.pallas_call(
        functools.partial(_mix_route_body, t=t, sub=sub, n_exp=n_exp),
        grid=(b, nt),
        in_specs=[pl.BlockSpec((1, t, d), tok), pl.BlockSpec((1, t, c), tok), pl.BlockSpec((1, t, c), tok),
                  pl.BlockSpec((1, t, c), col(1)), pl.BlockSpec((1, t, c), col(2)), pl.BlockSpec((1, t, c), col(3)),
                  pl.BlockSpec((1, t // LANES, c, LANES), lambda bi, i: (bi, i, 0, 0)),
                  full2(rg_proj), full2(hy_proj), full2(w_out), per_b,
                  full2(n2g), per_b, per_b, full2(wr), full2(br), full2(tri)],
        out_specs=[pl.BlockSpec((1, t, d), tok),
                   pl.BlockSpec((t * SLAB, LANES), row),
                   pl.BlockSpec((t, ROUTE_LANES), row), pl.BlockSpec((1, ROUTE_LANES), lambda bi, i: (0, 0))],
        out_shape=[jax.ShapeDtypeStruct((b, l, d), F32), jax.ShapeDtypeStruct((n * SLAB, LANES), I32),
                   jax.ShapeDtypeStruct((n, ROUTE_LANES), F32), jax.ShapeDtypeStruct((1, ROUTE_LANES), F32)],
        scratch_shapes=[pltpu.VMEM((1, ROUTE_LANES), F32)],
        compiler_params=_cparams("arbitrary", "arbitrary"),
        name="mix_route",
    )(x, h_f, h_b, p_rm, p_rm, p_rm, y_hy_t, rg_proj, hy_proj, w_out, g1, n2g, sh2, sc2, wr, br, tri)


def _dest_body(route_ref, cnt_ref, ut_ref, dest_ref, blk_ref, zero_ref, *, t, n_exp, nb_pad, n_rows):
    lane1 = lax.broadcasted_iota(I32, (1, ROUTE_LANES), 1)
    padded = jnp.floor((cnt_ref[...] + (MOE_BLOCK - 1.0)) * (1.0 / MOE_BLOCK)) * MOE_BLOCK
    padded = jnp.where(lane1 < n_exp, padded, 0.0)
    pend = jnp.dot(jnp.broadcast_to(padded, (SUBLANES, ROUTE_LANES)), ut_ref[...], precision=HIGHEST,
                   preferred_element_type=F32)[0:1]
    pstart = pend - padded
    route = route_ref[...]
    lane = lax.broadcasted_iota(I32, (t, ROUTE_LANES), 1)
    lf = lane.astype(F32)
    d1 = jnp.sum(jnp.where(lf == route[:, 0:1], pstart, 0.0), axis=1, keepdims=True) + route[:, 2:3]
    d2 = jnp.sum(jnp.where(lf == route[:, 1:2], pstart, 0.0), axis=1, keepdims=True) + route[:, 3:4]
    dmat = jnp.where(lane == 0, d1, jnp.where(lane == 1, d2, 0.0))
    dest_ref[...] = dmat.T[0:SUBLANES].astype(I32)
    first_row = lax.broadcasted_iota(I32, (nb_pad, ROUTE_LANES), 0).astype(F32) * float(MOE_BLOCK)
    lane_b = lax.broadcasted_iota(I32, (nb_pad, ROUTE_LANES), 1)
    nle = jnp.sum(jnp.where((lane_b < n_exp) & (pend <= first_row), 1.0, 0.0), axis=1, keepdims=True)
    e_blk = jnp.minimum(nle, n_exp - 1.0)
    mine = lane_b.astype(F32) == e_blk
    cnt_e = jnp.sum(jnp.where(mine, cnt_ref[...], 0.0), axis=1, keepdims=True)
    start_e = jnp.sum(jnp.where(mine, pstart, 0.0), axis=1, keepdims=True)
    valid = jnp.clip(cnt_e - (first_row - start_e), 0.0, float(MOE_BLOCK))
    blk_ref[...] = jnp.where(lane_b == 0, e_blk, jnp.where(lane_b == 1, valid, 0.0)).astype(I32)
    used = jnp.sum(jnp.where(lane1 == n_exp - 1, pend, 0.0), axis=1, keepdims=True)
    last_block = jnp.where(padded > 0.0, pend - float(MOE_BLOCK), -1.0)
    spare = used + float(MOE_BLOCK) * lane1.astype(F32)
    spare = jnp.where((lane1 < n_exp) & (spare < float(n_rows)), spare, -1.0)
    sub = lax.broadcasted_iota(I32, (SUBLANES, ROUTE_LANES), 0)
    zero_ref[...] = jnp.where(sub == 0, last_block, jnp.where(sub == 1, spare, -1.0)).astype(I32)


def moe_dest(route, cnt, *, tile, n_exp, n_blocks):
    n = route.shape[0]
    t = min(tile, n)
    nb_pad = -(-n_blocks // SUBLANES) * SUBLANES
    ut = (jnp.arange(ROUTE_LANES)[:, None] <= jnp.arange(ROUTE_LANES)[None, :]).astype(F32)
    return pl.pallas_call(
        functools.partial(_dest_body, t=t, n_exp=n_exp, nb_pad=nb_pad, n_rows=n_blocks * MOE_BLOCK),
        grid=(n // t,),
        in_specs=[pl.BlockSpec((t, ROUTE_LANES), lambda i: (i, 0)),
                  pl.BlockSpec((1, ROUTE_LANES), lambda i: (0, 0)),
                  pl.BlockSpec((ROUTE_LANES, ROUTE_LANES), lambda i: (0, 0))],
        out_specs=[pl.BlockSpec((SUBLANES, t), lambda i: (i, 0)),
                   pl.BlockSpec((nb_pad, ROUTE_LANES), lambda i: (0, 0)),
                   pl.BlockSpec((SUBLANES, ROUTE_LANES), lambda i: (0, 0))],
        out_shape=[jax.ShapeDtypeStruct((n // t * SUBLANES, t), I32),
                   jax.ShapeDtypeStruct((nb_pad, ROUTE_LANES), I32),
                   jax.ShapeDtypeStruct((SUBLANES, ROUTE_LANES), I32)],
        compiler_params=_cparams("arbitrary"),
        name="moe_dest",
    )(route, cnt, ut)


def _scatter_body(dest_ref, zero_ref, hx_ref, xb_ref, zero_s, sem, *, t, n_exp):
    @pl.when(pl.program_id(0) == 0)
    def _():
        zero_s[...] = jnp.zeros_like(zero_s)
        for wait in (False, True):
            for k in range(2):
                for e in range(n_exp):
                    start = zero_ref[k, e]

                    @pl.when(start >= 0)
                    def _(start=start, k=k):
                        copy = pltpu.make_async_copy(
                            zero_s, xb_ref.at[pl.ds(SLAB * jnp.maximum(start, 0), SLAB * MOE_BLOCK)], sem)
                        if wait:
                            copy.wait()
                        else:
                            copy.start(priority=k)

    def issue(r, _):
        for k in range(2):
            pltpu.make_async_copy(hx_ref.at[pl.ds(SLAB * r, SLAB)], xb_ref.at[pl.ds(SLAB * dest_ref[k, r], SLAB)],
                                  sem).start(priority=k)
        return 0

    lax.fori_loop(0, t, issue, 0, unroll=8)
    for k in range(2):
        pltpu.make_async_copy(hx_ref, xb_ref.at[pl.ds(0, SLAB * t)], sem).wait()


def moe_scatter(dest, zero_starts, hxp, n_rows, *, tile, n_exp):
    n = hxp.shape[0] // SLAB
    t = min(tile, n)
    per_dest_tile = dest.shape[1] // t
    return pl.pallas_call(
        functools.partial(_scatter_body, t=t, n_exp=n_exp),
        grid=(n // t,),
        in_specs=[pl.BlockSpec((SUBLANES, t), lambda i: (i // per_dest_tile, i % per_dest_tile),
                               memory_space=pltpu.SMEM),
                  pl.BlockSpec(zero_starts.shape, lambda i: (0, 0), memory_space=pltpu.SMEM),
                  pl.BlockSpec((t * SLAB, LANES), lambda i: (i, 0))],
        out_specs=pl.BlockSpec(memory_space=pl.ANY),
        out_shape=jax.ShapeDtypeStruct((n_rows * SLAB, LANES), I32),
        scratch_shapes=[pltpu.VMEM((MOE_BLOCK * SLAB, LANES), I32), pltpu.SemaphoreType.DMA],
        compiler_params=_cparams("arbitrary"),
        name="moe_scatter",
    )(dest, zero_starts, hxp)


def _expert_body(blk_ref, valid_ref, xb_ref, w1_ref, w3_ref, w2_ref, yb_ref, w1_s, w3_s, w2_s):
    i = pl.program_id(0)
    valid = valid_ref[i]
    half = MOE_BLOCK // 2
    changed = (i == 0) | (blk_ref[i] != blk_ref[jnp.maximum(i - 1, 0)])

    @pl.when(changed & (valid > 0))
    def _():
        w1_s[...] = w1_ref[0].astype(BF16)
        w3_s[...] = w3_ref[0].astype(BF16)
        w2_s[...] = w2_ref[0].astype(BF16)

    def run(rows):
        xblk = _unpack_bf16_pairs(_load_row_slabs(xb_ref, rows)).astype(BF16)
        h1 = jnp.dot(xblk, w1_s[...], preferred_element_type=F32)
        h3 = jnp.dot(xblk, w3_s[...], preferred_element_type=F32)
        hid = (h1 * _sigmoid(h1) * h3).astype(BF16)
        _store_row_slabs(yb_ref, _pack_bf16_pairs(jnp.dot(hid, w2_s[...], preferred_element_type=F32)))

    @pl.when(valid > half)
    def _():
        run(MOE_BLOCK)

    @pl.when(valid <= half)
    def _():
        yb_ref[...] = jnp.zeros_like(yb_ref)

    @pl.when((valid > 0) & (valid <= half))
    def _():
        run(half)


def moe_experts(blk_exp, blk_valid, xb, w1, w3, w2):
    p = xb.shape[0] // SLAB
    _, d, de = w1.shape
    nb = p // MOE_BLOCK
    grid_spec = pltpu.PrefetchScalarGridSpec(
        num_scalar_prefetch=2,
        grid=(nb,),
        in_specs=[pl.BlockSpec((MOE_BLOCK * SLAB, LANES), lambda i, blk, valid: (i, 0)),
                  pl.BlockSpec((1, d, de), lambda i, blk, valid: (blk[i], 0, 0)),
                  pl.BlockSpec((1, d, de), lambda i, blk, valid: (blk[i], 0, 0)),
                  pl.BlockSpec((1, de, d), lambda i, blk, valid: (blk[i], 0, 0))],
        out_specs=pl.BlockSpec((MOE_BLOCK * SLAB, LANES), lambda i, blk, valid: (i, 0)),
        scratch_shapes=[pltpu.VMEM((d, de), BF16), pltpu.VMEM((d, de), BF16), pltpu.VMEM((de, d), BF16)],
    )
    return pl.pallas_call(
        _expert_body,
        grid_spec=grid_spec,
        out_shape=jax.ShapeDtypeStruct((p * SLAB, LANES), I32),
        compiler_params=_cparams("arbitrary"),
        name="moe_experts",
    )(blk_exp, blk_valid, xb, w1, w3, w2)


COMBINE_PARTS = 4


def _combine_body(dest_ref, x1_ref, route_ref, g2_ref, fg_ref, yb_ref, o_ref, y1_s, y2_s, sems, *, t):
    tp = t // COMBINE_PARTS
    for part in range(COMBINE_PARTS):
        def issue(r, _, sem=sems.at[part]):
            pltpu.make_async_copy(yb_ref.at[pl.ds(SLAB * dest_ref[0, r], SLAB)], y1_s.at[pl.ds(SLAB * r, SLAB)],
                                  sem).start(priority=0)
            pltpu.make_async_copy(yb_ref.at[pl.ds(SLAB * dest_ref[1, r], SLAB)], y2_s.at[pl.ds(SLAB * r, SLAB)],
                                  sem).start(priority=1)
            return 0

        lax.fori_loop(part * tp, (part + 1) * tp, issue, 0, unroll=8)
    for part in range(COMBINE_PARTS):
        rows = slice(part * tp, (part + 1) * tp)
        lines = pl.ds(SLAB * part * tp, SLAB * tp)
        for y_s in (y1_s, y2_s):
            pltpu.make_async_copy(yb_ref.at[pl.ds(0, SLAB * tp)], y_s.at[lines], sems.at[part]).wait()
        route = route_ref[rows, :]
        moe = (route[:, 4:5] * _unpack_bf16_pairs(_load_row_slabs(y1_s, tp, part * tp))
               + route[:, 5:6] * _unpack_bf16_pairs(_load_row_slabs(y2_s, tp, part * tp)))
        x2 = x1_ref[0, rows, :] + g2_ref[0] * moe
        ms = jnp.mean(x2 * x2, axis=-1, keepdims=True)
        o_ref[0, rows, :] = x2 * lax.rsqrt(ms + EPS) * fg_ref[...]


def moe_combine(dest, x1, route, g2, final_g, yb, *, tile_l):
    b, l, d = x1.shape
    t = min(tile_l, l)
    nt = l // t
    slab = (t * SLAB, LANES)
    per_dest_tile = dest.shape[1] // t
    return pl.pallas_call(
        functools.partial(_combine_body, t=t),
        grid=(b, nt),
        in_specs=[pl.BlockSpec((SUBLANES, t),
                               lambda bi, i: ((bi * nt + i) // per_dest_tile, (bi * nt + i) % per_dest_tile),
                               memory_space=pltpu.SMEM),
                  pl.BlockSpec((1, t, d), lambda bi, i: (bi, i, 0)),
                  pl.BlockSpec((t, ROUTE_LANES), lambda bi, i: (bi * nt + i, 0)),
                  pl.BlockSpec((1, 1, d), lambda bi, i: (bi, 0, 0)),
                  pl.BlockSpec((1, d), lambda bi, i: (0, 0)),
                  pl.BlockSpec(memory_space=pl.ANY)],
        out_specs=pl.BlockSpec((1, t, d), lambda bi, i: (bi, i, 0)),
        out_shape=jax.ShapeDtypeStruct((b, l, d), F32),
        scratch_shapes=[pltpu.VMEM(slab, I32), pltpu.VMEM(slab, I32), pltpu.SemaphoreType.DMA((COMBINE_PARTS,))],
        compiler_params=_cparams("arbitrary", "arbitrary"),
        name="moe_combine",
    )(dest, x1, route, g2, final_g, yb)


def kernel(x, c, ctx, c_ctx, ada_w, ada_b, norm1_g, norm2_g, final_g, w_in, rg_conv_w, rg_conv_b, rg_wa_f, rg_ba_f, rg_wx_f, rg_bx_f, rg_lam_f, rg_wa_b, rg_ba_b, rg_wx_b, rg_bx_b, rg_lam_b, rg_proj, hy_conv_w, hy_conv_b, hy_pos_w1, hy_pos_b1, hy_pos_w2, hy_pos_b2, hy_freq, hy_pos_w3, hy_skip, hy_proj, w_out, moe_wg, moe_bg, moe_we, moe_be, moe_w1, moe_w3, moe_w2):
    B, L, D = x.shape
    C = rg_conv_w.shape[-1]
    LC = ctx.shape[1]
    c8 = jnp.zeros((8, D), F32).at[:B].set(c).at[B].set(c_ctx)
    mods = ada_mods(c8, ada_w[0], ada_b)
    sh1, sc1, g1 = (mods[:B, None, k * D:(k + 1) * D] for k in range(3))
    sh2, sc2, g2 = (mods[:B, None, k * D:(k + 1) * D] for k in range(3, 6))
    csh1 = jnp.broadcast_to(mods[B:B + 1, None, 0:D], (B, 1, D))
    csc1 = jnp.broadcast_to(mods[B:B + 1, None, D:2 * D], (B, 1, D))

    w_in_b = w_in[0].astype(BF16)
    w_rm = jnp.concatenate([w_in_b[:, :2 * C], w_in_b[:, 5 * C:]], axis=1)
    wg_f = gate_blocks(rg_wa_f[0], rg_wx_f[0], C // 256)
    wg_b = gate_blocks(rg_wa_b[0], rg_wx_b[0], C // 256)
    rg_f = (wg_f, rg_ba_f, rg_bx_f, rg_lam_f)
    rg_b = (wg_b, rg_ba_b, rg_bx_b, rg_lam_b)

    pc = norm_mod_proj(ctx, norm1_g, csh1, csc1, w_rm[:, :C], rg_conv_w[0], rg_conv_b, tile_l=LC, chunk=C)
    zero = jnp.zeros((B, 1, C), F32)
    _, hcf = rg_scan(pc, 0, *rg_f, zero, reverse=False, tile_l=TILE_SCAN)
    _, hcb = rg_scan(pc, 0, *rg_b, zero, reverse=True, tile_l=TILE_SCAN)

    p_rm = norm_mod_proj(x, norm1_g, sh1, sc1, w_rm, rg_conv_w[0], rg_conv_b, tile_l=TILE_PROJ, chunk=PROJ_CHUNK)
    hy_taps = jnp.concatenate([hy_conv_w[0], hy_conv_b], axis=0)
    u_hy, z0_hy = hyena_proj(x, norm1_g, sh1, sc1, w_in_b[:, 2 * C:5 * C], hy_taps, tile_l=TILE_PROJ)
    h_f, _ = rg_scan(p_rm, 0, *rg_f, hcf, reverse=False, tile_l=TILE_SCAN)
    h_b, _ = rg_scan(p_rm, 0, *rg_b, hcb, reverse=True, tile_l=TILE_SCAN)

    tables = dft_tables(L)
    assert hy_pos_w1.shape[1] == len(HY_FEATURE_ORDER)
    w1t = jnp.zeros((HY_HID, HY_HID), F32).at[:, :hy_pos_w1.shape[1]].set(hy_pos_w1[0].T[:, jnp.array(HY_FEATURE_ORDER)])
    kt = hyena_filter_t(w1t, hy_pos_b1[0][:, None], hy_pos_w2[0].T, hy_pos_b2[0][:, None], hy_freq[0][:, None],
                        hy_pos_w3[0].T.reshape(2, C, HY_HID).astype(BF16), L, FILTER_TILE_C)
    spec = hyena_spectrum(kt.reshape(2, C, L // LANES, LANES), tables, SPECTRUM_GROUP)
    y_hy_t = hyena_fftconv(u_hy, z0_hy, hy_skip[0][:, None, None], spec, tables, tile_c=FFT_TILE_C, group=FFT_GROUP)

    n_exp = moe_we.shape[-1]
    n_grp = moe_wg.shape[-1]
    assert n_grp == N_GROUPS and n_exp == N_GROUPS * EXPERTS_PER_GROUP
    wr = jnp.zeros((D, ROUTE_LANES), F32).at[:, :n_grp].set(moe_wg[0]).at[:, n_grp:n_grp + n_exp].set(moe_we[0])
    br = jnp.zeros((1, ROUTE_LANES), F32).at[:, :n_grp].set(moe_bg).at[:, n_grp:n_grp + n_exp].set(moe_be)
    wr_hi = wr.astype(BF16)
    wr_split = jnp.concatenate([wr_hi, (wr - wr_hi.astype(F32)).astype(BF16)], axis=1)
    x1, hxp, route, cnt = mix_route(x, h_f, h_b, p_rm, y_hy_t, rg_proj[0].astype(BF16), hy_proj[0].astype(BF16),
                                    (0.5 * w_out[0]).astype(BF16), g1, norm2_g, sh2, sc2, wr_split, br, tile_l=TILE_MIX,
                                    n_exp=n_exp)

    n_blocks = (2 * B * L + n_exp * (MOE_BLOCK - 1)) // MOE_BLOCK
    dest, blk, zero_starts = moe_dest(route, cnt, tile=TILE_DEST, n_exp=n_exp, n_blocks=n_blocks)
    xb = moe_scatter(dest, zero_starts, hxp, n_blocks * MOE_BLOCK, tile=TILE_DISPATCH, n_exp=n_exp)
    yb = moe_experts(blk[:n_blocks, 0], blk[:n_blocks, 1], xb, moe_w1[0], moe_w3[0], moe_w2[0])
    return moe_combine(dest, x1, route, g2, final_g[None], yb, tile_l=TILE_DISPATCH)
```

```python
import functools
import math

import jax
import jax.numpy as jnp
from jax import lax
from jax.experimental import pallas as pl
from jax.experimental.pallas import tpu as pltpu

F32 = jnp.float32
BF16 = jnp.bfloat16
I32 = jnp.int32
HIGHEST = lax.Precision.HIGHEST

LANES = 128
SUBLANES = 8
EPS = 1e-6
RG_C = 8.0
GRID_W = 64
HY_SEQ_BANDS = 16
HY_COL_BANDS = 8
HY_DECAY_TARGET = 1e-2
HY_FAST_DECAY = 0.3
HY_SLOW_DECAY = 1.5
N_GROUPS = 4
EXPERTS_PER_GROUP = 8
MOE_BLOCK = 512
VMEM_LIMIT = 56 * 1024 * 1024

TILE_PROJ = 1024
PROJ_CHUNK = 1024
CONV_CHUNK = 256
TILE_SCAN = 512
TILE_MIX = 512
TILE_DEST = 2048
TILE_DISPATCH = 1024
FILTER_TILE_C = 256
SPECTRUM_GROUP = 64
FFT_TILE_C = 128
FFT_GROUP = 8


def _cparams(*sem):
    return pltpu.CompilerParams(dimension_semantics=sem, vmem_limit_bytes=VMEM_LIMIT)


def _sigmoid(x):
    return 0.5 * (jnp.tanh(0.5 * x) + 1.0)


def _gelu_tanh(x):
    c = math.sqrt(2.0 / math.pi)
    h = 0.5 * x
    return h + h * jnp.tanh(x * (c + (0.044715 * c) * (x * x)))


def _ada_body(c_ref, w_ref, b_ref, o_ref):
    c = c_ref[...]
    s = c * _sigmoid(c)
    o_ref[...] = jnp.dot(s, w_ref[...], precision=HIGHEST, preferred_element_type=F32) + b_ref[...]


def ada_mods(c8, ada_w, ada_b):
    d, m = ada_w.shape
    tn = 1024 if m % 1024 == 0 else m
    return pl.pallas_call(
        _ada_body,
        grid=(m // tn,),
        in_specs=[pl.BlockSpec((c8.shape[0], d), lambda j: (0, 0)),
                  pl.BlockSpec((d, tn), lambda j: (0, j)),
                  pl.BlockSpec((1, tn), lambda j: (0, j))],
        out_specs=pl.BlockSpec((c8.shape[0], tn), lambda j: (0, j)),
        out_shape=jax.ShapeDtypeStruct((c8.shape[0], m), F32),
        compiler_params=_cparams("parallel"),
        name="ada_mods",
    )(c8, ada_w, ada_b)


def _norm_mod(x_ref, g_ref, sh_ref, sc_ref):
    x = x_ref[0]
    ms = jnp.mean(x * x, axis=-1, keepdims=True)
    y = x * lax.rsqrt(ms + EPS) * g_ref[...]
    return (y * (1.0 + sc_ref[0]) + sh_ref[0]).astype(BF16)


def _first_rows_body(x_ref, g_ref, sh_ref, sc_ref, w_ref, o_ref):
    nt, rows, d = x_ref.shape[1:]
    x = x_ref[0].reshape(nt * rows, d)
    ms = jnp.mean(x * x, axis=-1, keepdims=True)
    y = x * lax.rsqrt(ms + EPS) * g_ref[...]
    hx = (y * (1.0 + sc_ref[0]) + sh_ref[0]).astype(BF16)
    o_ref[0] = jnp.dot(hx, w_ref[...], preferred_element_type=F32).reshape(nt, rows, w_ref.shape[1])


def proj_first_rows(x, g, shift, scale, w, *, tile_l):
    b, l, d = x.shape
    t = min(tile_l, l)
    nt = l // t
    m = w.shape[1]
    return pl.pallas_call(
        _first_rows_body,
        grid=(b,),
        in_specs=[pl.BlockSpec((1, nt, SUBLANES, d), lambda bi: (bi, 0, 0, 0)),
                  pl.BlockSpec((1, d), lambda bi: (0, 0)),
                  pl.BlockSpec((1, 1, d), lambda bi: (bi, 0, 0)),
                  pl.BlockSpec((1, 1, d), lambda bi: (bi, 0, 0)),
                  pl.BlockSpec(w.shape, lambda bi: (0, 0))],
        out_specs=pl.BlockSpec((1, nt, SUBLANES, m), lambda bi: (bi, 0, 0, 0)),
        out_shape=jax.ShapeDtypeStruct((b, nt, SUBLANES, m), F32),
        compiler_params=_cparams("parallel"),
        name="proj_first_rows",
    )(x.reshape(b, nt, t, d), g, shift, scale, w)


def _proj_body(x_ref, g_ref, sh_ref, sc_ref, w_ref, cw_ref, cb_ref, nxt_ref, o_ref, last_s, ext_s,
               *, chunk, n_tiles, t, c):
    i = pl.program_id(1)

    @pl.when(i == 0)
    def _():
        last_s[...] = jnp.zeros_like(last_s)

    hx = _norm_mod(x_ref, g_ref, sh_ref, sc_ref)
    has_next = (i < n_tiles - 1).astype(F32)
    cw = cw_ref[...]
    cc = ext_s.shape[1]
    for j in range(c // cc):
        cols = slice(j * cc, (j + 1) * cc)
        p = jnp.dot(hx, w_ref[:, cols], preferred_element_type=F32)
        ext_s[0:SUBLANES, :] = last_s[:, cols]
        ext_s[SUBLANES:SUBLANES + t, :] = p
        ext_s[SUBLANES + t:2 * SUBLANES + t, :] = nxt_ref[0, 0, :, cols] * has_next
        xc = cb_ref[:, cols] + cw[2:3, cols] * p
        for tap, off in ((0, -2), (1, -1), (3, 1)):
            xc = xc + cw[tap:tap + 1, cols] * ext_s[SUBLANES + off:SUBLANES + off + t, :]
        o_ref[0, :, cols] = xc.astype(o_ref.dtype)
        last_s[:, cols] = p[t - SUBLANES:t, :]
    m = w_ref.shape[1]
    for j in range((m - c) // chunk):
        cols = slice(c + j * chunk, c + (j + 1) * chunk)
        o_ref[0, :, cols] = jnp.dot(hx, w_ref[:, cols], preferred_element_type=F32).astype(o_ref.dtype)


def norm_mod_proj(x, g, shift, scale, w, conv_w, conv_b, *, tile_l, chunk):
    b, l, d = x.shape
    m = w.shape[1]
    c = conv_w.shape[1]
    tl = min(tile_l, l)
    n_tiles = l // tl
    chunk = min(chunk, max(m - c, 1))
    assert (m - c) % chunk == 0
    nxt = proj_first_rows(x, g, shift, scale, w[:, :c], tile_l=tile_l)
    return pl.pallas_call(
        functools.partial(_proj_body, chunk=chunk, n_tiles=n_tiles, t=tl, c=c),
        grid=(b, n_tiles),
        in_specs=[pl.BlockSpec((1, tl, d), lambda bi, i: (bi, i, 0)),
                  pl.BlockSpec((1, d), lambda bi, i: (0, 0)),
                  pl.BlockSpec((1, 1, d), lambda bi, i: (bi, 0, 0)),
                  pl.BlockSpec((1, 1, d), lambda bi, i: (bi, 0, 0)),
                  pl.BlockSpec(w.shape, lambda bi, i: (0, 0)),
                  pl.BlockSpec(conv_w.shape, lambda bi, i: (0, 0)),
                  pl.BlockSpec(conv_b.shape, lambda bi, i: (0, 0)),
                  pl.BlockSpec((1, 1, SUBLANES, c), lambda bi, i: (bi, jnp.minimum(i + 1, n_tiles - 1), 0, 0))],
        out_specs=pl.BlockSpec((1, tl, m), lambda bi, i: (bi, i, 0)),
        out_shape=jax.ShapeDtypeStruct((b, l, m), BF16),
        scratch_shapes=[pltpu.VMEM((SUBLANES, c), F32), pltpu.VMEM((tl + 2 * SUBLANES, min(CONV_CHUNK, c)), F32)],
        compiler_params=_cparams("parallel", "arbitrary"),
        name="norm_mod_proj",
    )(x, g, shift, scale, w, conv_w, conv_b, nxt)


def _scan_body(xc_ref, wg_ref, ba_ref, bx_ref, lam_ref, h0_ref,
               h_ref, hl_ref, xc_s, g_s, a_s, b_s, hloc_s, pcum_s, carry_s,
               *, reverse, t, c, s_len, pitch):
    i = pl.program_id(1)
    n_slab = c // LANES
    n_blk = wg_ref.shape[0]
    blk = c // n_blk

    @pl.when(i == 0)
    def _():
        carry_s[...] = h0_ref[0]

    xc_s[...] = xc_ref[0].astype(F32)

    for k in range(n_blk):
        g_s[:, k * 2 * blk:(k + 1) * 2 * blk] = jnp.dot(xc_ref[0, :, k * blk:(k + 1) * blk], wg_ref[k],
                                                          preferred_element_type=F32)

    lam = lam_ref[...]
    softplus_neg_lam = jnp.maximum(-lam, 0.0) + jnp.log1p(jnp.exp(-jnp.abs(lam)))
    half_ca = (-0.5 * RG_C) * softplus_neg_lam
    half_ba, half_bx = 0.5 * ba_ref[...], 0.5 * bx_ref[...]
    slabs_per_blk = blk // LANES
    for j in range(SUBLANES):
        r0 = j * s_len
        for k in range(n_slab):
            kb, ks = k // slabs_per_blk, k % slabs_per_blk
            ga = g_s[r0:r0 + s_len, kb * 2 * blk + ks * LANES:kb * 2 * blk + (ks + 1) * LANES]
            gx = g_s[r0:r0 + s_len, kb * 2 * blk + blk + ks * LANES:kb * 2 * blk + blk + (ks + 1) * LANES]
            lane = slice(k * LANES, (k + 1) * LANES)
            half_x = 0.5 * xc_s[r0:r0 + s_len, lane]
            hca = half_ca[:, lane]
            log_a = hca * jnp.tanh(ga + half_ba[:, lane]) + hca
            gated_x = half_x * jnp.tanh(gx + half_bx[:, lane]) + half_x
            a = jnp.exp(log_a)
            a_s[k, j * pitch:j * pitch + s_len, :] = a
            gain2 = -jnp.tanh(log_a) * (a * a + 1.0)
            gain = jnp.where(gain2 > 0.0, gain2 * lax.rsqrt(gain2), 0.0)
            b_s[k, j * pitch:j * pitch + s_len, :] = gain * gated_x

    def step1(s, hp):
        hs, ps = hp
        srow = (s_len - 1 - s) if reverse else s
        hs2, ps2 = [], []
        for k in range(n_slab):
            av = a_s[k, pl.ds(srow, SUBLANES, stride=pitch), :]
            bv = b_s[k, pl.ds(srow, SUBLANES, stride=pitch), :]
            h = av * hs[k] + bv
            p = av * ps[k]
            hloc_s[k, pl.ds(srow, SUBLANES, stride=pitch), :] = h
            pcum_s[k, pl.ds(srow, SUBLANES, stride=pitch), :] = p
            hs2.append(h)
            ps2.append(p)
        return tuple(hs2), tuple(ps2)

    zeros = tuple(jnp.zeros((SUBLANES, LANES), F32) for _ in range(n_slab))
    ones = tuple(jnp.ones((SUBLANES, LANES), F32) for _ in range(n_slab))
    h_end, p_end = lax.fori_loop(0, s_len, step1, (zeros, ones), unroll=1 if reverse else 4)

    order = range(SUBLANES - 1, -1, -1) if reverse else range(SUBLANES)
    for k in range(n_slab):
        cst = carry_s[:, k * LANES:(k + 1) * LANES]
        for j in order:
            rows = slice(j * pitch, j * pitch + s_len)
            h_ref[0, j * s_len:(j + 1) * s_len, k * LANES:(k + 1) * LANES] = (
                hloc_s[k, rows, :] + pcum_s[k, rows, :] * cst).astype(h_ref.dtype)
            cst = p_end[k][j:j + 1] * cst + h_end[k][j:j + 1]
        carry_s[:, k * LANES:(k + 1) * LANES] = cst
    hl_ref[0] = carry_s[...]


def rg_scan(p, col_blk, wg, ba, bx, lam, h0, *, reverse, tile_l):
    b, l, _ = p.shape
    c = ba.shape[1]
    t = min(tile_l, l)
    n_tiles = l // t
    s_len = t // SUBLANES
    pitch = s_len + SUBLANES

    def nat(i):
        return (n_tiles - 1 - i) if reverse else i

    body = functools.partial(_scan_body, reverse=reverse, t=t, c=c, s_len=s_len, pitch=pitch)
    vec = pl.BlockSpec((1, c), lambda bi, i: (0, 0))
    return pl.pallas_call(
        body,
        grid=(b, n_tiles),
        in_specs=[pl.BlockSpec((1, t, c), lambda bi, i: (bi, nat(i), col_blk)),
                  pl.BlockSpec(wg.shape, lambda bi, i: (0, 0, 0)),
                  vec, vec, vec,
                  pl.BlockSpec((1, 1, c), lambda bi, i: (bi, 0, 0))],
        out_specs=[pl.BlockSpec((1, t, c), lambda bi, i: (bi, nat(i), 0)),
                   pl.BlockSpec((1, 1, c), lambda bi, i: (bi, 0, 0))],
        out_shape=[jax.ShapeDtypeStruct((b, l, c), BF16), jax.ShapeDtypeStruct((b, 1, c), F32)],
        scratch_shapes=[pltpu.VMEM((t, c), F32), pltpu.VMEM((t, 2 * c), F32)]
        + [pltpu.VMEM((c // LANES, SUBLANES * pitch, LANES), F32) for _ in range(4)]
        + [pltpu.VMEM((1, c), F32)],
        compiler_params=_cparams("parallel", "arbitrary"),
        name="rg_scan_bwd" if reverse else "rg_scan_fwd",
    )(p, wg, ba, bx, lam, h0)


def gate_blocks(wa, wx, n_blk):
    h, d, _ = wa.shape
    hp = h // n_blk
    eye = jnp.eye(hp, dtype=wa.dtype)

    def bd(w):
        w = w.reshape(n_blk, hp, d, d)
        return jnp.einsum('khde,hg->khdge', w, eye).reshape(n_blk, hp * d, hp * d)

    return (0.5 * jnp.concatenate([bd(wa), bd(wx)], axis=-1)).astype(BF16)


HY_HID = 64
HY_FEATURE_ORDER = (list(range(1, 1 + 2 * HY_SEQ_BANDS))
                    + list(range(2 + 2 * HY_SEQ_BANDS, 2 + 2 * HY_SEQ_BANDS + 2 * HY_COL_BANDS))
                    + [0, 1 + 2 * HY_SEQ_BANDS])


def _filter_body(w1t_ref, b1_ref, w2t_ref, b2_ref, fr_ref, w3t_ref, o_ref, z_s, *, l, c, ct, rows_grid):
    d = pl.program_id(0)
    j = pl.program_id(1)
    lane = lax.broadcasted_iota(I32, (1, l), 1)
    s_i = jnp.where(d == 0, lane, l - lane)
    sf = s_i.astype(F32)
    t_norm = sf / float(max(l - 1, 1))

    @pl.when(j == 0)
    def _():
        band_step = (HY_SEQ_BANDS - 1 - 1e-4) / (HY_SEQ_BANDS - 1)
        seq_band = 1e-4 + band_step * lax.broadcasted_iota(I32, (HY_SEQ_BANDS, 1), 0).astype(F32)
        col_band = 1.0 + lax.broadcasted_iota(I32, (HY_COL_BANDS, 1), 0).astype(F32)
        col_pos = (s_i & (GRID_W - 1)).astype(F32)
        row_lag = (s_i >> int(math.log2(GRID_W))).astype(F32) / float(rows_grid)
        ang_seq = ((2.0 * math.pi / l) * sf) * seq_band
        ang_col = ((2.0 * math.pi / GRID_W) * col_pos) * col_band
        n_trig = 2 * HY_SEQ_BANDS + 2 * HY_COL_BANDS
        trow = lax.broadcasted_iota(I32, (HY_HID - n_trig, 1), 0)
        tail = jnp.where(trow == 0, t_norm, jnp.where(trow == 1, row_lag, 0.0))
        feats = jnp.concatenate([jnp.cos(ang_seq), jnp.sin(ang_seq), jnp.cos(ang_col), jnp.sin(ang_col), tail], axis=0)
        fr = fr_ref[...]
        z = jnp.sin(fr * (jnp.dot(w1t_ref[...], feats, precision=HIGHEST, preferred_element_type=F32) + b1_ref[...]))
        z_s[...] = jnp.sin(fr * (jnp.dot(w2t_ref[...], z, precision=HIGHEST, preferred_element_type=F32) + b2_ref[...]))

    k = jnp.dot(w3t_ref[0], z_s[...].astype(BF16), preferred_element_type=F32)
    ch = (lax.broadcasted_iota(I32, (ct, 1), 0) + j * ct).astype(F32)
    max_decay = math.log(HY_DECAY_TARGET) / HY_FAST_DECAY
    min_decay = math.log(HY_DECAY_TARGET) / HY_SLOW_DECAY
    delta = jnp.abs(min_decay + ch * ((max_decay - min_decay) / (c - 1)))
    k = k * jnp.exp(-t_norm * delta)
    k = jnp.where((d == 1) & (lane == 0), 0.0, k)
    o_ref[0] = k.astype(o_ref.dtype)


def hyena_filter_t(w1t, b1, w2t, b2, fr, w3t, l, tile_c):
    assert GRID_W & (GRID_W - 1) == 0
    c = w3t.shape[1]
    ct = min(tile_c, c)
    body = functools.partial(_filter_body, l=l, c=c, ct=ct, rows_grid=l // GRID_W)
    small = lambda shape: pl.BlockSpec(shape, lambda d, j: (0,) * len(shape))
    return pl.pallas_call(
        body,
        grid=(2, c // ct),
        in_specs=[small(w1t.shape), small(b1.shape), small(w2t.shape), small(b2.shape), small(fr.shape),
                  pl.BlockSpec((1, ct, HY_HID), lambda d, j: (d, j, 0))],
        out_specs=pl.BlockSpec((1, ct, l), lambda d, j: (d, j, 0)),
        out_shape=jax.ShapeDtypeStruct((2, c, l), BF16),
        scratch_shapes=[pltpu.VMEM((HY_HID, l), F32)],
        compiler_params=_cparams("arbitrary", "arbitrary"),
        name="hyena_filter",
    )(w1t, b1, w2t, b2, fr, w3t)


def dft_tables(l):
    import numpy as np
    n = 2 * l
    r_in, nk = l // LANES, n // LANES
    ka = np.arange(nk)[:, None].astype(np.float64)
    r = np.arange(r_in)[None, :].astype(np.float64)
    a1 = 2.0 * np.pi * ka * r / nk
    f1 = np.concatenate([np.cos(a1), -np.sin(a1)], axis=0)
    lane = np.arange(LANES)[None, :].astype(np.float64)
    at = 2.0 * np.pi * ka * lane / n
    twr, twi = np.cos(at), -np.sin(at)
    a2 = 2.0 * np.pi * np.arange(LANES)[:, None] * np.arange(LANES)[None, :] / LANES
    cr, ci = np.cos(a2), -np.sin(a2)
    m2 = np.block([[cr, ci], [-ci, cr]])
    m2i = np.block([[cr, -ci], [ci, cr]])
    ai = 2.0 * np.pi * np.arange(r_in)[:, None] * np.arange(nk)[None, :] / nk
    gi = np.concatenate([np.cos(ai), -np.sin(ai)], axis=1) / n
    as_bf = lambda a: jnp.asarray(a, F32).astype(BF16)
    return as_bf(f1), jnp.asarray(twr, F32), jnp.asarray(twi, F32), as_bf(m2), as_bf(m2i), as_bf(gi)


def _fwd_rows_twiddle(x_a, x_b, f1, twr, twi, nk):
    a = jnp.dot(f1, jnp.concatenate([x_a, x_b], axis=1), preferred_element_type=F32)
    out = []
    for h in range(2):
        re, im = a[:nk, h * LANES:(h + 1) * LANES], a[nk:, h * LANES:(h + 1) * LANES]
        out.append((re * twr - im * twi, re * twi + im * twr))
    return out


def _spectrum_body(k_ref, f1_ref, twr_ref, twi_ref, m2_ref, o_ref, *, g, nk, r_in):
    f1, twr, twi = f1_ref[...], twr_ref[...], twi_ref[...]
    sign = jnp.where((lax.broadcasted_iota(I32, (nk, 1), 0) & 1) == 0, 1.0, -1.0)
    a2 = []
    for ci in range(g):
        (fre, fim), (bre, bim) = _fwd_rows_twiddle(k_ref[0, ci], k_ref[1, ci], f1, twr, twi, nk)
        a2.append(jnp.concatenate([fre + sign * bre, fim + sign * bim], axis=1).astype(BF16))
    spec = jnp.dot(jnp.concatenate(a2, axis=0), m2_ref[...], preferred_element_type=F32)
    o_ref[...] = spec.reshape(g, nk, 2 * LANES).astype(o_ref.dtype)


def hyena_spectrum(kt4, tables, group):
    _, c, r_in, _ = kt4.shape
    nk = 2 * r_in
    f1, twr, twi, m2, _, _ = tables
    g = min(group, c)
    full = lambda a: pl.BlockSpec(a.shape, lambda j: (0,) * a.ndim)
    return pl.pallas_call(
        functools.partial(_spectrum_body, g=g, nk=nk, r_in=r_in),
        grid=(c // g,),
        in_specs=[pl.BlockSpec((2, g, r_in, LANES), lambda j: (0, j, 0, 0)), full(f1), full(twr), full(twi), full(m2)],
        out_specs=pl.BlockSpec((g, nk, 2 * LANES), lambda j: (j, 0, 0)),
        out_shape=jax.ShapeDtypeStruct((c, nk, 2 * LANES), BF16),
        compiler_params=_cparams("parallel"),
        name="hyena_spectrum",
    )(kt4, f1, twr, twi, m2)


HY_CHUNK = 256


def _hyena_proj_body(x_ref, g_ref, sh_ref, sc_ref, w_ref, taps_ref, nxt_ref, u_ref, z0_ref, last_s, *, n_tiles, t, c):
    i = pl.program_id(1)

    @pl.when(i == 0)
    def _():
        last_s[...] = jnp.zeros_like(last_s)

    hx = _norm_mod(x_ref, g_ref, sh_ref, sc_ref)
    row = lax.broadcasted_iota(I32, (t, 1), 0)
    has_next = (i < n_tiles - 1).astype(F32)
    cw = min(HY_CHUNK, c)
    for j in range(c // cw):
        zs = []
        for k in range(3):
            cols = slice(k * c + j * cw, k * c + (j + 1) * cw)
            p = jnp.dot(hx, w_ref[:, cols], preferred_element_type=F32)
            up = jnp.where(row == 0, last_s[:, cols], pltpu.roll(p, 1, 0))
            dn = jnp.where(row == t - 1, nxt_ref[0, 0, 0:1, cols] * has_next, pltpu.roll(p, t - 1, 0))
            tp = taps_ref[:, cols]
            zs.append(tp[3:4] + tp[0:1] * up + tp[1:2] * p + tp[2:3] * dn)
            last_s[:, cols] = p[t - 1:t, :]
        z0, z1, zv = zs
        u_t, z0_t = (zv * z1).T, z0.T
        for q in range(t // LANES):
            u_ref[0, q, j * cw:(j + 1) * cw, :] = u_t[:, q * LANES:(q + 1) * LANES].astype(u_ref.dtype)
            z0_ref[0, q, j * cw:(j + 1) * cw, :] = z0_t[:, q * LANES:(q + 1) * LANES].astype(z0_ref.dtype)


def hyena_proj(x, g, shift, scale, w, taps, *, tile_l):
    b, l, d = x.shape
    c = w.shape[1] // 3
    t = min(tile_l, l)
    n_tiles = l // t
    rq = t // LANES
    nxt = proj_first_rows(x, g, shift, scale, w, tile_l=tile_l)
    o_spec = pl.BlockSpec((1, rq, c, LANES), lambda bi, i: (bi, i, 0, 0))
    o_shape = jax.ShapeDtypeStruct((b, l // LANES, c, LANES), BF16)
    return pl.pallas_call(
        functools.partial(_hyena_proj_body, n_tiles=n_tiles, t=t, c=c),
        grid=(b, n_tiles),
        in_specs=[pl.BlockSpec((1, t, d), lambda bi, i: (bi, i, 0)),
                  pl.BlockSpec((1, d), lambda bi, i: (0, 0)),
                  pl.BlockSpec((1, 1, d), lambda bi, i: (bi, 0, 0)),
                  pl.BlockSpec((1, 1, d), lambda bi, i: (bi, 0, 0)),
                  pl.BlockSpec(w.shape, lambda bi, i: (0, 0)),
                  pl.BlockSpec(taps.shape, lambda bi, i: (0, 0)),
                  pl.BlockSpec((1, 1, SUBLANES, 3 * c), lambda bi, i: (bi, jnp.minimum(i + 1, n_tiles - 1), 0, 0))],
        out_specs=[o_spec, o_spec],
        out_shape=[o_shape, o_shape],
        scratch_shapes=[pltpu.VMEM((1, 3 * c), F32)],
        compiler_params=_cparams("parallel", "arbitrary"),
        name="hyena_proj",
    )(x, g, shift, scale, w, taps, nxt)


def _fftconv_body(u_ref, z0_ref, skip_ref, k_ref, f1_ref, twr_ref, twi_ref, m2_ref, m2i_ref, gi_ref,
                  o_ref, u_s, z0_s, y_s, *, ct, g, nk, r_in, pitch):
    for r in range(r_in):
        u_s[r * pitch:r * pitch + ct, :] = u_ref[0, r].astype(F32)
        z0_s[r * pitch:r * pitch + ct, :] = z0_ref[0, r].astype(F32)
    f1, twr, twi = f1_ref[...], twr_ref[...], twi_ref[...]

    def chan(ref, ch):
        return ref[pl.ds(ch, r_in, stride=pitch), :]

    def rows_fwd(c0):
        a2 = []
        for ci in range(0, g, 2):
            pair = _fwd_rows_twiddle(chan(u_s, c0 + ci).astype(BF16), chan(u_s, c0 + ci + 1).astype(BF16),
                                     f1, twr, twi, nk)
            a2 += [jnp.concatenate([tre, tim], axis=1).astype(BF16) for tre, tim in pair]
        return jnp.concatenate(a2, axis=0)

    def lanes_fwd(a2):
        return jnp.dot(a2, m2_ref[...], preferred_element_type=F32)

    def times_filter(c0, spec):
        kf = k_ref[pl.ds(c0, g)].astype(F32).reshape(g * nk, 2 * LANES)
        sre, sim = spec[:, :LANES], spec[:, LANES:]
        kre, kim = kf[:, :LANES], kf[:, LANES:]
        return jnp.concatenate([sre * kre - sim * kim, sre * kim + sim * kre], axis=1).astype(BF16)

    def lanes_inv(prod):
        return jnp.dot(prod, m2i_ref[...], preferred_element_type=F32)

    def rows_inv(c0, cc):
        for ci in range(0, g, 2):
            st = []
            for h in range(2):
                blk = cc[(ci + h) * nk:(ci + h + 1) * nk]
                cre, cim = blk[:, :LANES], blk[:, LANES:]
                st.append(jnp.concatenate([cre * twr + cim * twi, cim * twr - cre * twi], axis=0).astype(BF16))
            y2 = jnp.dot(gi_ref[...], jnp.concatenate(st, axis=1), preferred_element_type=F32)
            for h in range(2):
                ch = c0 + ci + h
                y = y2[:, h * LANES:(h + 1) * LANES]
                y_s[pl.ds(ch, r_in, stride=pitch), :] = (y + chan(u_s, ch) * skip_ref[ch]) * chan(z0_s, ch)

    def two_groups(i, _):
        ca, cb = 2 * g * i, 2 * g * i + g
        a2_a = rows_fwd(ca)
        spec_a = lanes_fwd(a2_a)
        a2_b = rows_fwd(cb)
        prod_a = times_filter(ca, spec_a)
        spec_b = lanes_fwd(a2_b)
        cc_a = lanes_inv(prod_a)
        prod_b = times_filter(cb, spec_b)
        cc_b = lanes_inv(prod_b)
        rows_inv(ca, cc_a)
        rows_inv(cb, cc_b)
        return 0

    lax.fori_loop(0, ct // (2 * g), two_groups, 0)
    for r in range(r_in):
        o_ref[0, r] = y_s[r * pitch:r * pitch + ct, :].astype(o_ref.dtype)


def hyena_fftconv(u, z0, skip3, spec, tables, *, tile_c, group):
    b, r_in, c, _ = u.shape
    nk = 2 * r_in
    ct = min(tile_c, c)
    g = min(group, ct // 2)
    assert ct % (2 * g) == 0 and g % 2 == 0
    pitch = ct + SUBLANES
    f1, twr, twi, m2, m2i, gi = tables
    full = lambda a: pl.BlockSpec(a.shape, lambda j, bi: (0,) * a.ndim)
    io_spec = pl.BlockSpec((1, r_in, ct, LANES), lambda j, bi: (bi, 0, j, 0))
    return pl.pallas_call(
        functools.partial(_fftconv_body, ct=ct, g=g, nk=nk, r_in=r_in, pitch=pitch),
        grid=(c // ct, b),
        in_specs=[io_spec, io_spec,
                  pl.BlockSpec((ct, 1, 1), lambda j, bi: (j, 0, 0)),
                  pl.BlockSpec((ct, nk, 2 * LANES), lambda j, bi: (j, 0, 0)),
                  full(f1), full(twr), full(twi), full(m2), full(m2i), full(gi)],
        out_specs=io_spec,
        out_shape=jax.ShapeDtypeStruct((b, r_in, c, LANES), BF16),
        scratch_shapes=[pltpu.VMEM((r_in * pitch, LANES), F32) for _ in range(3)],
        compiler_params=_cparams("parallel", "arbitrary"),
        name="hyena_fftconv",
    )(u, z0, skip3, spec, f1, twr, twi, m2, m2i, gi)


ROUTE_LANES = LANES
NEG_BIG = -1e30
HALF_WORD = 16


def _pack_bf16_pairs(v):
    h = v.shape[1] // 2
    bits = pltpu.bitcast(v.astype(BF16).astype(F32), I32)
    return bits[:, :h] | lax.shift_right_logical(bits[:, h:], HALF_WORD)


def _unpack_bf16_pairs(w):
    hi = pltpu.bitcast(w & jnp.int32(-65536), F32)
    lo = pltpu.bitcast(lax.shift_left(w, HALF_WORD), F32)
    return jnp.concatenate([hi, lo], axis=1)


SLAB = 4


def _store_row_slabs(ref, words, row0=0):
    rows = words.shape[0]
    for j in range(SLAB):
        ref[pl.ds(SLAB * row0 + j, rows, stride=SLAB), :] = words[:, j * LANES:(j + 1) * LANES]


def _load_row_slabs(ref, rows=None, row0=0):
    rows = ref.shape[0] // SLAB if rows is None else rows
    return jnp.concatenate([ref[pl.ds(SLAB * row0 + j, rows, stride=SLAB), :] for j in range(SLAB)], axis=1)


MIX_SUB = 512


def _mix_route_body(x_ref, hf_ref, hb_ref, prg_ref, pga_ref, pgb_ref, yt_ref, rgp_ref, hyp_ref, wo_ref, g1_ref,
                    n2g_ref, sh2_ref, sc2_ref, wr_ref, br_ref, tri_ref,
                    x1_ref, hxp_ref, route_ref, cnt_ref, carry_s, *, t, sub, n_exp):
    @pl.when((pl.program_id(0) == 0) & (pl.program_id(1) == 0))
    def _():
        carry_s[...] = jnp.zeros_like(carry_s)

    running = carry_s[...]
    for r0 in range(0, t, sub):
        rows = slice(r0, r0 + sub)
        hsum = hf_ref[0, rows, :].astype(F32) + hb_ref[0, rows, :].astype(F32)
        y_rg = (hsum * _gelu_tanh(prg_ref[0, rows, :].astype(F32))).astype(BF16)
        t1 = jnp.dot(y_rg, rgp_ref[...], preferred_element_type=F32)
        t2 = jnp.concatenate([lax.dot_general(yt_ref[0, q], hyp_ref[...], (((0,), (0,)), ((), ())),
                                              preferred_element_type=F32)
                              for q in range(r0 // LANES, (r0 + sub) // LANES)], axis=0)
        merged = ((t1 + t2) + jnp.tanh(0.5 * pga_ref[0, rows, :].astype(F32)) * t1
                  + jnp.tanh(0.5 * pgb_ref[0, rows, :].astype(F32)) * t2)
        out = jnp.dot(merged.astype(BF16), wo_ref[...], preferred_element_type=F32)
        x1 = x_ref[0, rows, :] + g1_ref[0] * out
        x1_ref[0, rows, :] = x1
        ms = jnp.mean(x1 * x1, axis=-1, keepdims=True)
        hx2 = (x1 * lax.rsqrt(ms + EPS) * n2g_ref[...]) * (1.0 + sc2_ref[0]) + sh2_ref[0]
        _store_row_slabs(hxp_ref, _pack_bf16_pairs(hx2), r0)

        hx_hi = hx2.astype(BF16)
        hx_lo = (hx2 - hx_hi.astype(F32)).astype(BF16)
        parts = (jnp.dot(hx_hi, wr_ref[...], preferred_element_type=F32)
                 + jnp.dot(hx_lo, wr_ref[...], preferred_element_type=F32))
        logits = parts[:, :ROUTE_LANES] + parts[:, ROUTE_LANES:] + br_ref[...]
        lane = lax.broadcasted_iota(I32, (sub, ROUTE_LANES), 1)
        is_g = lane < N_GROUPS
        glog = jnp.where(is_g, logits, NEG_BIG)
        gmax = jnp.max(glog, axis=1, keepdims=True)
        gidx = jnp.min(jnp.where(glog == gmax, lane, ROUTE_LANES), axis=1, keepdims=True)
        gsum = jnp.sum(jnp.where(is_g, jnp.exp(glog - gmax), 0.0), axis=1, keepdims=True)
        p_g = 1.0 / gsum
        e_lane = lane - N_GROUPS
        grp_of_lane = lax.shift_right_arithmetic(e_lane, int(math.log2(EXPERTS_PER_GROUP)))
        in_grp = (e_lane >= 0) & (e_lane < n_exp) & (grp_of_lane == gidx)
        elog = jnp.where(in_grp, logits, NEG_BIG)
        m1 = jnp.max(elog, axis=1, keepdims=True)
        i1 = jnp.min(jnp.where(elog == m1, lane, ROUTE_LANES), axis=1, keepdims=True)
        elog2 = jnp.where(lane == i1, NEG_BIG, elog)
        m2 = jnp.max(elog2, axis=1, keepdims=True)
        i2 = jnp.min(jnp.where(elog2 == m2, lane, ROUTE_LANES), axis=1, keepdims=True)
        e21 = jnp.exp(m2 - m1)
        pk1 = 1.0 / (1.0 + e21)
        wt1, wt2 = p_g * pk1, p_g * (e21 * pk1)

        oh1 = (lane == i1 - N_GROUPS).astype(F32)
        oh2 = (lane == i2 - N_GROUPS).astype(F32)
        cnt = oh1 + oh2
        before = jnp.dot(tri_ref[...], cnt.astype(BF16), preferred_element_type=F32) + running
        rank1 = jnp.sum(oh1 * before, axis=1, keepdims=True)
        rank2 = jnp.sum(oh2 * before, axis=1, keepdims=True)
        running = running + jnp.sum(cnt, axis=0, keepdims=True)
        vals = ((i1 - N_GROUPS).astype(F32), (i2 - N_GROUPS).astype(F32), rank1, rank2, wt1, wt2)
        route = jnp.zeros((sub, ROUTE_LANES), F32)
        for k, v in enumerate(vals):
            route = jnp.where(lane == k, v, route)
        route_ref[rows, :] = route
    carry_s[...] = running
    cnt_ref[...] = running


def mix_route(x, h_f, h_b, p_rm, y_hy_t, rg_proj, hy_proj, w_out, g1, n2g, sh2, sc2, wr, br, *, tile_l, n_exp):
    b, l, d = x.shape
    c = h_f.shape[2]
    t = min(tile_l, l)
    nt = l // t
    n = b * l
    sub = min(MIX_SUB, t)
    tri = (jnp.arange(sub)[:, None] > jnp.arange(sub)[None, :]).astype(BF16)
    tok = lambda bi, i: (bi, i, 0)
    col = lambda k: (lambda bi, i: (bi, i, k))
    full2 = lambda a: pl.BlockSpec(a.shape, lambda bi, i: (0, 0))
    per_b = pl.BlockSpec((1, 1, d), lambda bi, i: (bi, 0, 0))
    row = lambda bi, i: (bi * nt + i, 0)
    return pl.pallas_call(
        functools.partial(_mix_route_body, t=t, sub=sub, n_exp=n_exp),
        grid=(b, nt),
        in_specs=[pl.BlockSpec((1, t, d), tok), pl.BlockSpec((1, t, c), tok), pl.BlockSpec((1, t, c), tok),
                  pl.BlockSpec((1, t, c), col(1)), pl.BlockSpec((1, t, c), col(2)), pl.BlockSpec((1, t, c), col(3)),
                  pl.BlockSpec((1, t // LANES, c, LANES), lambda bi, i: (bi, i, 0, 0)),
                  full2(rg_proj), full2(hy_proj), full2(w_out), per_b,
                  full2(n2g), per_b, per_b, full2(wr), full2(br), full2(tri)],
        out_specs=[pl.BlockSpec((1, t, d), tok),
                   pl.BlockSpec((t * SLAB, LANES), row),
                   pl.BlockSpec((t, ROUTE_LANES), row), pl.BlockSpec((1, ROUTE_LANES), lambda bi, i: (0, 0))],
        out_shape=[jax.ShapeDtypeStruct((b, l, d), F32), jax.ShapeDtypeStruct((n * SLAB, LANES), I32),
                   jax.ShapeDtypeStruct((n, ROUTE_LANES), F32), jax.ShapeDtypeStruct((1, ROUTE_LANES), F32)],
        scratch_shapes=[pltpu.VMEM((1, ROUTE_LANES), F32)],
        compiler_params=_cparams("arbitrary", "arbitrary"),
        name="mix_route",
    )(x, h_f, h_b, p_rm, p_rm, p_rm, y_hy_t, rg_proj, hy_proj, w_out, g1, n2g, sh2, sc2, wr, br, tri)


def _dest_body(route_ref, cnt_ref, ut_ref, dest_ref, blk_ref, zero_ref, *, t, n_exp, nb_pad, n_rows):
    lane1 = lax.broadcasted_iota(I32, (1, ROUTE_LANES), 1)
    padded = jnp.floor((cnt_ref[...] + (MOE_BLOCK - 1.0)) * (1.0 / MOE_BLOCK)) * MOE_BLOCK
    padded = jnp.where(lane1 < n_exp, padded, 0.0)
    pend = jnp.dot(jnp.broadcast_to(padded, (SUBLANES, ROUTE_LANES)), ut_ref[...], precision=HIGHEST,
                   preferred_element_type=F32)[0:1]
    pstart = pend - padded
    route = route_ref[...]
    lane = lax.broadcasted_iota(I32, (t, ROUTE_LANES), 1)
    lf = lane.astype(F32)
    d1 = jnp.sum(jnp.where(lf == route[:, 0:1], pstart, 0.0), axis=1, keepdims=True) + route[:, 2:3]
    d2 = jnp.sum(jnp.where(lf == route[:, 1:2], pstart, 0.0), axis=1, keepdims=True) + route[:, 3:4]
    dmat = jnp.where(lane == 0, d1, jnp.where(lane == 1, d2, 0.0))
    dest_ref[...] = dmat.T[0:SUBLANES].astype(I32)
    first_row = lax.broadcasted_iota(I32, (nb_pad, ROUTE_LANES), 0).astype(F32) * float(MOE_BLOCK)
    lane_b = lax.broadcasted_iota(I32, (nb_pad, ROUTE_LANES), 1)
    nle = jnp.sum(jnp.where((lane_b < n_exp) & (pend <= first_row), 1.0, 0.0), axis=1, keepdims=True)
    e_blk = jnp.minimum(nle, n_exp - 1.0)
    mine = lane_b.astype(F32) == e_blk
    cnt_e = jnp.sum(jnp.where(mine, cnt_ref[...], 0.0), axis=1, keepdims=True)
    start_e = jnp.sum(jnp.where(mine, pstart, 0.0), axis=1, keepdims=True)
    valid = jnp.clip(cnt_e - (first_row - start_e), 0.0, float(MOE_BLOCK))
    blk_ref[...] = jnp.where(lane_b == 0, e_blk, jnp.where(lane_b == 1, valid, 0.0)).astype(I32)
    used = jnp.sum(jnp.where(lane1 == n_exp - 1, pend, 0.0), axis=1, keepdims=True)
    last_block = jnp.where(padded > 0.0, pend - float(MOE_BLOCK), -1.0)
    spare = used + float(MOE_BLOCK) * lane1.astype(F32)
    spare = jnp.where((lane1 < n_exp) & (spare < float(n_rows)), spare, -1.0)
    sub = lax.broadcasted_iota(I32, (SUBLANES, ROUTE_LANES), 0)
    zero_ref[...] = jnp.where(sub == 0, last_block, jnp.where(sub == 1, spare, -1.0)).astype(I32)


def moe_dest(route, cnt, *, tile, n_exp, n_blocks):
    n = route.shape[0]
    t = min(tile, n)
    nb_pad = -(-n_blocks // SUBLANES) * SUBLANES
    ut = (jnp.arange(ROUTE_LANES)[:, None] <= jnp.arange(ROUTE_LANES)[None, :]).astype(F32)
    return pl.pallas_call(
        functools.partial(_dest_body, t=t, n_exp=n_exp, nb_pad=nb_pad, n_rows=n_blocks * MOE_BLOCK),
        grid=(n // t,),
        in_specs=[pl.BlockSpec((t, ROUTE_LANES), lambda i: (i, 0)),
                  pl.BlockSpec((1, ROUTE_LANES), lambda i: (0, 0)),
                  pl.BlockSpec((ROUTE_LANES, ROUTE_LANES), lambda i: (0, 0))],
        out_specs=[pl.BlockSpec((SUBLANES, t), lambda i: (i, 0)),
                   pl.BlockSpec((nb_pad, ROUTE_LANES), lambda i: (0, 0)),
                   pl.BlockSpec((SUBLANES, ROUTE_LANES), lambda i: (0, 0))],
        out_shape=[jax.ShapeDtypeStruct((n // t * SUBLANES, t), I32),
                   jax.ShapeDtypeStruct((nb_pad, ROUTE_LANES), I32),
                   jax.ShapeDtypeStruct((SUBLANES, ROUTE_LANES), I32)],
        compiler_params=_cparams("arbitrary"),
        name="moe_dest",
    )(route, cnt, ut)


def _scatter_body(dest_ref, zero_ref, hx_ref, xb_ref, zero_s, sem, *, t, n_exp):
    @pl.when(pl.program_id(0) == 0)
    def _():
        zero_s[...] = jnp.zeros_like(zero_s)
        for wait in (False, True):
            for k in range(2):
                for e in range(n_exp):
                    start = zero_ref[k, e]

                    @pl.when(start >= 0)
                    def _(start=start, k=k):
                        copy = pltpu.make_async_copy(
                            zero_s, xb_ref.at[pl.ds(SLAB * jnp.maximum(start, 0), SLAB * MOE_BLOCK)], sem)
                        if wait:
                            copy.wait()
                        else:
                            copy.start(priority=k)

    def issue(r, _):
        for k in range(2):
            pltpu.make_async_copy(hx_ref.at[pl.ds(SLAB * r, SLAB)], xb_ref.at[pl.ds(SLAB * dest_ref[k, r], SLAB)],
                                  sem).start(priority=k)
        return 0

    lax.fori_loop(0, t, issue, 0, unroll=8)
    for k in range(2):
        pltpu.make_async_copy(hx_ref, xb_ref.at[pl.ds(0, SLAB * t)], sem).wait()


def moe_scatter(dest, zero_starts, hxp, n_rows, *, tile, n_exp):
    n = hxp.shape[0] // SLAB
    t = min(tile, n)
    per_dest_tile = dest.shape[1] // t
    return pl.pallas_call(
        functools.partial(_scatter_body, t=t, n_exp=n_exp),
        grid=(n // t,),
        in_specs=[pl.BlockSpec((SUBLANES, t), lambda i: (i // per_dest_tile, i % per_dest_tile),
                               memory_space=pltpu.SMEM),
                  pl.BlockSpec(zero_starts.shape, lambda i: (0, 0), memory_space=pltpu.SMEM),
                  pl.BlockSpec((t * SLAB, LANES), lambda i: (i, 0))],
        out_specs=pl.BlockSpec(memory_space=pl.ANY),
        out_shape=jax.ShapeDtypeStruct((n_rows * SLAB, LANES), I32),
        scratch_shapes=[pltpu.VMEM((MOE_BLOCK * SLAB, LANES), I32), pltpu.SemaphoreType.DMA],
        compiler_params=_cparams("arbitrary"),
        name="moe_scatter",
    )(dest, zero_starts, hxp)


def _expert_body(blk_ref, valid_ref, xb_ref, w1_ref, w3_ref, w2_ref, yb_ref, w1_s, w3_s, w2_s):
    i = pl.program_id(0)
    valid = valid_ref[i]
    half = MOE_BLOCK // 2
    changed = (i == 0) | (blk_ref[i] != blk_ref[jnp.maximum(i - 1, 0)])

    @pl.when(changed & (valid > 0))
    def _():
        w1_s[...] = w1_ref[0].astype(BF16)
        w3_s[...] = w3_ref[0].astype(BF16)
        w2_s[...] = w2_ref[0].astype(BF16)

    def run(rows):
        xblk = _unpack_bf16_pairs(_load_row_slabs(xb_ref, rows)).astype(BF16)
        h1 = jnp.dot(xblk, w1_s[...], preferred_element_type=F32)
        h3 = jnp.dot(xblk, w3_s[...], preferred_element_type=F32)
        hid = (h1 * _sigmoid(h1) * h3).astype(BF16)
        _store_row_slabs(yb_ref, _pack_bf16_pairs(jnp.dot(hid, w2_s[...], preferred_element_type=F32)))

    @pl.when(valid > half)
    def _():
        run(MOE_BLOCK)

    @pl.when(valid <= half)
    def _():
        yb_ref[...] = jnp.zeros_like(yb_ref)

    @pl.when((valid > 0) & (valid <= half))
    def _():
        run(half)


def moe_experts(blk_exp, blk_valid, xb, w1, w3, w2):
    p = xb.shape[0] // SLAB
    _, d, de = w1.shape
    nb = p // MOE_BLOCK
    grid_spec = pltpu.PrefetchScalarGridSpec(
        num_scalar_prefetch=2,
        grid=(nb,),
        in_specs=[pl.BlockSpec((MOE_BLOCK * SLAB, LANES), lambda i, blk, valid: (i, 0)),
                  pl.BlockSpec((1, d, de), lambda i, blk, valid: (blk[i], 0, 0)),
                  pl.BlockSpec((1, d, de), lambda i, blk, valid: (blk[i], 0, 0)),
                  pl.BlockSpec((1, de, d), lambda i, blk, valid: (blk[i], 0, 0))],
        out_specs=pl.BlockSpec((MOE_BLOCK * SLAB, LANES), lambda i, blk, valid: (i, 0)),
        scratch_shapes=[pltpu.VMEM((d, de), BF16), pltpu.VMEM((d, de), BF16), pltpu.VMEM((de, d), BF16)],
    )
    return pl.pallas_call(
        _expert_body,
        grid_spec=grid_spec,
        out_shape=jax.ShapeDtypeStruct((p * SLAB, LANES), I32),
        compiler_params=_cparams("arbitrary"),
        name="moe_experts",
    )(blk_exp, blk_valid, xb, w1, w3, w2)


COMBINE_PARTS = 4


def _combine_body(dest_ref, x1_ref, route_ref, g2_ref, fg_ref, yb_ref, o_ref, y1_s, y2_s, sems, *, t):
    tp = t // COMBINE_PARTS
    for part in range(COMBINE_PARTS):
        def issue(r, _, sem=sems.at[part]):
            pltpu.make_async_copy(yb_ref.at[pl.ds(SLAB * dest_ref[0, r], SLAB)], y1_s.at[pl.ds(SLAB * r, SLAB)],
                                  sem).start(priority=0)
            pltpu.make_async_copy(yb_ref.at[pl.ds(SLAB * dest_ref[1, r], SLAB)], y2_s.at[pl.ds(SLAB * r, SLAB)],
                                  sem).start(priority=1)
            return 0

        lax.fori_loop(part * tp, (part + 1) * tp, issue, 0, unroll=8)
    for part in range(COMBINE_PARTS):
        rows = slice(part * tp, (part + 1) * tp)
        lines = pl.ds(SLAB * part * tp, SLAB * tp)
        for y_s in (y1_s, y2_s):
            pltpu.make_async_copy(yb_ref.at[pl.ds(0, SLAB * tp)], y_s.at[lines], sems.at[part]).wait()
        route = route_ref[rows, :]
        moe = (route[:, 4:5] * _unpack_bf16_pairs(_load_row_slabs(y1_s, tp, part * tp))
               + route[:, 5:6] * _unpack_bf16_pairs(_load_row_slabs(y2_s, tp, part * tp)))
        x2 = x1_ref[0, rows, :] + g2_ref[0] * moe
        ms = jnp.mean(x2 * x2, axis=-1, keepdims=True)
        o_ref[0, rows, :] = x2 * lax.rsqrt(ms + EPS) * fg_ref[...]


def moe_combine(dest, x1, route, g2, final_g, yb, *, tile_l):
    b, l, d = x1.shape
    t = min(tile_l, l)
    nt = l // t
    slab = (t * SLAB, LANES)
    per_dest_tile = dest.shape[1] // t
    return pl.pallas_call(
        functools.partial(_combine_body, t=t),
        grid=(b, nt),
        in_specs=[pl.BlockSpec((SUBLANES, t),
                               lambda bi, i: ((bi * nt + i) // per_dest_tile, (bi * nt + i) % per_dest_tile),
                               memory_space=pltpu.SMEM),
                  pl.BlockSpec((1, t, d), lambda bi, i: (bi, i, 0)),
                  pl.BlockSpec((t, ROUTE_LANES), lambda bi, i: (bi * nt + i, 0)),
                  pl.BlockSpec((1, 1, d), lambda bi, i: (bi, 0, 0)),
                  pl.BlockSpec((1, d), lambda bi, i: (0, 0)),
                  pl.BlockSpec(memory_space=pl.ANY)],
        out_specs=pl.BlockSpec((1, t, d), lambda bi, i: (bi, i, 0)),
        out_shape=jax.ShapeDtypeStruct((b, l, d), F32),
        scratch_shapes=[pltpu.VMEM(slab, I32), pltpu.VMEM(slab, I32), pltpu.SemaphoreType.DMA((COMBINE_PARTS,))],
        compiler_params=_cparams("arbitrary", "arbitrary"),
        name="moe_combine",
    )(dest, x1, route, g2, final_g, yb)


def kernel(x, c, ctx, c_ctx, ada_w, ada_b, norm1_g, norm2_g, final_g, w_in, rg_conv_w, rg_conv_b, rg_wa_f, rg_ba_f, rg_wx_f, rg_bx_f, rg_lam_f, rg_wa_b, rg_ba_b, rg_wx_b, rg_bx_b, rg_lam_b, rg_proj, hy_conv_w, hy_conv_b, hy_pos_w1, hy_pos_b1, hy_pos_w2, hy_pos_b2, hy_freq, hy_pos_w3, hy_skip, hy_proj, w_out, moe_wg, moe_bg, moe_we, moe_be, moe_w1, moe_w3, moe_w2):
    B, L, D = x.shape
    C = rg_conv_w.shape[-1]
    LC = ctx.shape[1]
    c8 = jnp.zeros((8, D), F32).at[:B].set(c).at[B].set(c_ctx)
    mods = ada_mods(c8, ada_w[0], ada_b)
    sh1, sc1, g1 = (mods[:B, None, k * D:(k + 1) * D] for k in range(3))
    sh2, sc2, g2 = (mods[:B, None, k * D:(k + 1) * D] for k in range(3, 6))
    csh1 = jnp.broadcast_to(mods[B:B + 1, None, 0:D], (B, 1, D))
    csc1 = jnp.broadcast_to(mods[B:B + 1, None, D:2 * D], (B, 1, D))

    w_in_b = w_in[0].astype(BF16)
    w_rm = jnp.concatenate([w_in_b[:, :2 * C], w_in_b[:, 5 * C:]], axis=1)
    wg_f = gate_blocks(rg_wa_f[0], rg_wx_f[0], C // 256)
    wg_b = gate_blocks(rg_wa_b[0], rg_wx_b[0], C // 256)
    rg_f = (wg_f, rg_ba_f, rg_bx_f, rg_lam_f)
    rg_b = (wg_b, rg_ba_b, rg_bx_b, rg_lam_b)

    pc = norm_mod_proj(ctx, norm1_g, csh1, csc1, w_rm[:, :C], rg_conv_w[0], rg_conv_b, tile_l=LC, chunk=C)
    zero = jnp.zeros((B, 1, C), F32)
    _, hcf = rg_scan(pc, 0, *rg_f, zero, reverse=False, tile_l=TILE_SCAN)
    _, hcb = rg_scan(pc, 0, *rg_b, zero, reverse=True, tile_l=TILE_SCAN)

    p_rm = norm_mod_proj(x, norm1_g, sh1, sc1, w_rm, rg_conv_w[0], rg_conv_b, tile_l=TILE_PROJ, chunk=PROJ_CHUNK)
    hy_taps = jnp.concatenate([hy_conv_w[0], hy_conv_b], axis=0)
    u_hy, z0_hy = hyena_proj(x, norm1_g, sh1, sc1, w_in_b[:, 2 * C:5 * C], hy_taps, tile_l=TILE_PROJ)
    h_f, _ = rg_scan(p_rm, 0, *rg_f, hcf, reverse=False, tile_l=TILE_SCAN)
    h_b, _ = rg_scan(p_rm, 0, *rg_b, hcb, reverse=True, tile_l=TILE_SCAN)

    tables = dft_tables(L)
    assert hy_pos_w1.shape[1] == len(HY_FEATURE_ORDER)
    w1t = jnp.zeros((HY_HID, HY_HID), F32).at[:, :hy_pos_w1.shape[1]].set(hy_pos_w1[0].T[:, jnp.array(HY_FEATURE_ORDER)])
    kt = hyena_filter_t(w1t, hy_pos_b1[0][:, None], hy_pos_w2[0].T, hy_pos_b2[0][:, None], hy_freq[0][:, None],
                        hy_pos_w3[0].T.reshape(2, C, HY_HID).astype(BF16), L, FILTER_TILE_C)
    spec = hyena_spectrum(kt.reshape(2, C, L // LANES, LANES), tables, SPECTRUM_GROUP)
    y_hy_t = hyena_fftconv(u_hy, z0_hy, hy_skip[0][:, None, None], spec, tables, tile_c=FFT_TILE_C, group=FFT_GROUP)

    n_exp = moe_we.shape[-1]
    n_grp = moe_wg.shape[-1]
    assert n_grp == N_GROUPS and n_exp == N_GROUPS * EXPERTS_PER_GROUP
    wr = jnp.zeros((D, ROUTE_LANES), F32).at[:, :n_grp].set(moe_wg[0]).at[:, n_grp:n_grp + n_exp].set(moe_we[0])
    br = jnp.zeros((1, ROUTE_LANES), F32).at[:, :n_grp].set(moe_bg).at[:, n_grp:n_grp + n_exp].set(moe_be)
    wr_hi = wr.astype(BF16)
    wr_split = jnp.concatenate([wr_hi, (wr - wr_hi.astype(F32)).astype(BF16)], axis=1)
    x1, hxp, route, cnt = mix_route(x, h_f, h_b, p_rm, y_hy_t, rg_proj[0].astype(BF16), hy_proj[0].astype(BF16),
                                    (0.5 * w_out[0]).astype(BF16), g1, norm2_g, sh2, sc2, wr_split, br, tile_l=TILE_MIX,
                                    n_exp=n_exp)

    n_blocks = (2 * B * L + n_exp * (MOE_BLOCK - 1)) // MOE_BLOCK
    dest, blk, zero_starts = moe_dest(route, cnt, tile=TILE_DEST, n_exp=n_exp, n_blocks=n_blocks)
    xb = moe_scatter(dest, zero_starts, hxp, n_blocks * MOE_BLOCK, tile=TILE_DISPATCH, n_exp=n_exp)
    yb = moe_experts(blk[:n_blocks, 0], blk[:n_blocks, 1], xb, moe_w1[0], moe_w3[0], moe_w2[0])
    return moe_combine(dest, x1, route, g2, final_g[None], yb, tile_l=TILE_DISPATCH)
```

```python
import functools
import math

import jax
import jax.numpy as jnp
from jax import lax
from jax.experimental import pallas as pl
from jax.experimental.pallas import tpu as pltpu

F32 = jnp.float32
BF16 = jnp.bfloat16
I32 = jnp.int32
HIGHEST = lax.Precision.HIGHEST

LANES = 128
SUBLANES = 8
EPS = 1e-6
RG_C = 8.0
GRID_W = 64
HY_SEQ_BANDS = 16
HY_COL_BANDS = 8
HY_DECAY_TARGET = 1e-2
HY_FAST_DECAY = 0.3
HY_SLOW_DECAY = 1.5
N_GROUPS = 4
EXPERTS_PER_GROUP = 8
MOE_BLOCK = 512
VMEM_LIMIT = 56 * 1024 * 1024

TILE_PROJ = 1024
PROJ_CHUNK = 1024
CONV_CHUNK = 256
TILE_SCAN = 1024
TILE_MIX = 512
TILE_DEST = 2048
TILE_DISPATCH = 1024
FILTER_TILE_C = 256
SPECTRUM_GROUP = 64
FFT_TILE_C = 128
FFT_GROUP = 8


def _cparams(*sem):
    return pltpu.CompilerParams(dimension_semantics=sem, vmem_limit_bytes=VMEM_LIMIT)


def _sigmoid(x):
    return 0.5 * (jnp.tanh(0.5 * x) + 1.0)


def _gelu_tanh(x):
    c = math.sqrt(2.0 / math.pi)
    h = 0.5 * x
    return h + h * jnp.tanh(x * (c + (0.044715 * c) * (x * x)))


def _ada_body(c_ref, w_ref, b_ref, o_ref):
    c = c_ref[...]
    s = c * _sigmoid(c)
    o_ref[...] = jnp.dot(s, w_ref[...], precision=HIGHEST, preferred_element_type=F32) + b_ref[...]


def ada_mods(c8, ada_w, ada_b):
    d, m = ada_w.shape
    tn = 1024 if m % 1024 == 0 else m
    return pl.pallas_call(
        _ada_body,
        grid=(m // tn,),
        in_specs=[pl.BlockSpec((c8.shape[0], d), lambda j: (0, 0)),
                  pl.BlockSpec((d, tn), lambda j: (0, j)),
                  pl.BlockSpec((1, tn), lambda j: (0, j))],
        out_specs=pl.BlockSpec((c8.shape[0], tn), lambda j: (0, j)),
        out_shape=jax.ShapeDtypeStruct((c8.shape[0], m), F32),
        compiler_params=_cparams("parallel"),
        name="ada_mods",
    )(c8, ada_w, ada_b)


def _norm_mod(x_ref, g_ref, sh_ref, sc_ref):
    x = x_ref[0]
    ms = jnp.mean(x * x, axis=-1, keepdims=True)
    y = x * lax.rsqrt(ms + EPS) * g_ref[...]
    return (y * (1.0 + sc_ref[0]) + sh_ref[0]).astype(BF16)


def _first_rows_body(x_ref, g_ref, sh_ref, sc_ref, w_ref, o_ref):
    nt, rows, d = x_ref.shape[1:]
    x = x_ref[0].reshape(nt * rows, d)
    ms = jnp.mean(x * x, axis=-1, keepdims=True)
    y = x * lax.rsqrt(ms + EPS) * g_ref[...]
    hx = (y * (1.0 + sc_ref[0]) + sh_ref[0]).astype(BF16)
    o_ref[0] = jnp.dot(hx, w_ref[...], preferred_element_type=F32).reshape(nt, rows, w_ref.shape[1])


def proj_first_rows(x, g, shift, scale, w, *, tile_l):
    b, l, d = x.shape
    t = min(tile_l, l)
    nt = l // t
    m = w.shape[1]
    return pl.pallas_call(
        _first_rows_body,
        grid=(b,),
        in_specs=[pl.BlockSpec((1, nt, SUBLANES, d), lambda bi: (bi, 0, 0, 0)),
                  pl.BlockSpec((1, d), lambda bi: (0, 0)),
                  pl.BlockSpec((1, 1, d), lambda bi: (bi, 0, 0)),
                  pl.BlockSpec((1, 1, d), lambda bi: (bi, 0, 0)),
                  pl.BlockSpec(w.shape, lambda bi: (0, 0))],
        out_specs=pl.BlockSpec((1, nt, SUBLANES, m), lambda bi: (bi, 0, 0, 0)),
        out_shape=jax.ShapeDtypeStruct((b, nt, SUBLANES, m), F32),
        compiler_params=_cparams("parallel"),
        name="proj_first_rows",
    )(x.reshape(b, nt, t, d), g, shift, scale, w)


def _proj_body(x_ref, g_ref, sh_ref, sc_ref, w_ref, cw_ref, cb_ref, nxt_ref, o_ref, last_s, ext_s,
               *, chunk, n_tiles, t, c):
    i = pl.program_id(1)

    @pl.when(i == 0)
    def _():
        last_s[...] = jnp.zeros_like(last_s)

    hx = _norm_mod(x_ref, g_ref, sh_ref, sc_ref)
    has_next = (i < n_tiles - 1).astype(F32)
    cw = cw_ref[...]
    cc = ext_s.shape[1]
    for j in range(c // cc):
        cols = slice(j * cc, (j + 1) * cc)
        p = jnp.dot(hx, w_ref[:, cols], preferred_element_type=F32)
        ext_s[0:SUBLANES, :] = last_s[:, cols]
        ext_s[SUBLANES:SUBLANES + t, :] = p
        ext_s[SUBLANES + t:2 * SUBLANES + t, :] = nxt_ref[0, 0, :, cols] * has_next
        xc = cb_ref[:, cols] + cw[2:3, cols] * p
        for tap, off in ((0, -2), (1, -1), (3, 1)):
            xc = xc + cw[tap:tap + 1, cols] * ext_s[SUBLANES + off:SUBLANES + off + t, :]
        o_ref[0, :, cols] = xc.astype(o_ref.dtype)
        last_s[:, cols] = p[t - SUBLANES:t, :]
    m = w_ref.shape[1]
    for j in range((m - c) // chunk):
        cols = slice(c + j * chunk, c + (j + 1) * chunk)
        o_ref[0, :, cols] = jnp.dot(hx, w_ref[:, cols], preferred_element_type=F32).astype(o_ref.dtype)


def norm_mod_proj(x, g, shift, scale, w, conv_w, conv_b, *, tile_l, chunk):
    b, l, d = x.shape
    m = w.shape[1]
    c = conv_w.shape[1]
    tl = min(tile_l, l)
    n_tiles = l // tl
    chunk = min(chunk, max(m - c, 1))
    assert (m - c) % chunk == 0
    nxt = proj_first_rows(x, g, shift, scale, w[:, :c], tile_l=tile_l)
    return pl.pallas_call(
        functools.partial(_proj_body, chunk=chunk, n_tiles=n_tiles, t=tl, c=c),
        grid=(b, n_tiles),
        in_specs=[pl.BlockSpec((1, tl, d), lambda bi, i: (bi, i, 0)),
                  pl.BlockSpec((1, d), lambda bi, i: (0, 0)),
                  pl.BlockSpec((1, 1, d), lambda bi, i: (bi, 0, 0)),
                  pl.BlockSpec((1, 1, d), lambda bi, i: (bi, 0, 0)),
                  pl.BlockSpec(w.shape, lambda bi, i: (0, 0)),
                  pl.BlockSpec(conv_w.shape, lambda bi, i: (0, 0)),
                  pl.BlockSpec(conv_b.shape, lambda bi, i: (0, 0)),
                  pl.BlockSpec((1, 1, SUBLANES, c), lambda bi, i: (bi, jnp.minimum(i + 1, n_tiles - 1), 0, 0))],
        out_specs=pl.BlockSpec((1, tl, m), lambda bi, i: (bi, i, 0)),
        out_shape=jax.ShapeDtypeStruct((b, l, m), BF16),
        scratch_shapes=[pltpu.VMEM((SUBLANES, c), F32), pltpu.VMEM((tl + 2 * SUBLANES, min(CONV_CHUNK, c)), F32)],
        compiler_params=_cparams("parallel", "arbitrary"),
        name="norm_mod_proj",
    )(x, g, shift, scale, w, conv_w, conv_b, nxt)


def _scan_body(xc_ref, wg_ref, ba_ref, bx_ref, lam_ref, h0_ref,
               h_ref, hl_ref, xc_s, g_s, a_s, b_s, hloc_s, pcum_s, carry_s,
               *, reverse, t, c, s_len, pitch):
    i = pl.program_id(1)
    n_slab = c // LANES
    n_blk = wg_ref.shape[0]
    blk = c // n_blk

    @pl.when(i == 0)
    def _():
        carry_s[...] = h0_ref[0]

    xc_s[...] = xc_ref[0].astype(F32)

    for k in range(n_blk):
        g_s[:, k * 2 * blk:(k + 1) * 2 * blk] = jnp.dot(xc_ref[0, :, k * blk:(k + 1) * blk], wg_ref[k],
                                                          preferred_element_type=F32)

    lam = lam_ref[...]
    softplus_neg_lam = jnp.maximum(-lam, 0.0) + jnp.log1p(jnp.exp(-jnp.abs(lam)))
    half_ca = (-0.5 * RG_C) * softplus_neg_lam
    half_ba, half_bx = 0.5 * ba_ref[...], 0.5 * bx_ref[...]
    slabs_per_blk = blk // LANES
    for j in range(SUBLANES):
        r0 = j * s_len
        for k in range(n_slab):
            kb, ks = k // slabs_per_blk, k % slabs_per_blk
            ga = g_s[r0:r0 + s_len, kb * 2 * blk + ks * LANES:kb * 2 * blk + (ks + 1) * LANES]
            gx = g_s[r0:r0 + s_len, kb * 2 * blk + blk + ks * LANES:kb * 2 * blk + blk + (ks + 1) * LANES]
            lane = slice(k * LANES, (k + 1) * LANES)
            half_x = 0.5 * xc_s[r0:r0 + s_len, lane]
            hca = half_ca[:, lane]
            log_a = hca * jnp.tanh(ga + half_ba[:, lane]) + hca
            gated_x = half_x * jnp.tanh(gx + half_bx[:, lane]) + half_x
            a = jnp.exp(log_a)
            a_s[k, j * pitch:j * pitch + s_len, :] = a
            gain2 = -jnp.tanh(log_a) * (a * a + 1.0)
            gain = jnp.where(gain2 > 0.0, gain2 * lax.rsqrt(gain2), 0.0)
            b_s[k, j * pitch:j * pitch + s_len, :] = gain * gated_x

    def step1(s, hp):
        hs, ps = hp
        srow = (s_len - 1 - s) if reverse else s
        hs2, ps2 = [], []
        for k in range(n_slab):
            av = a_s[k, pl.ds(srow, SUBLANES, stride=pitch), :]
            bv = b_s[k, pl.ds(srow, SUBLANES, stride=pitch), :]
            h = av * hs[k] + bv
            p = av * ps[k]
            hloc_s[k, pl.ds(srow, SUBLANES, stride=pitch), :] = h
            pcum_s[k, pl.ds(srow, SUBLANES, stride=pitch), :] = p
            hs2.append(h)
            ps2.append(p)
        return tuple(hs2), tuple(ps2)

    zeros = tuple(jnp.zeros((SUBLANES, LANES), F32) for _ in range(n_slab))
    ones = tuple(jnp.ones((SUBLANES, LANES), F32) for _ in range(n_slab))
    h_end, p_end = lax.fori_loop(0, s_len, step1, (zeros, ones), unroll=1 if reverse else 4)

    order = range(SUBLANES - 1, -1, -1) if reverse else range(SUBLANES)
    for k in range(n_slab):
        cst = carry_s[:, k * LANES:(k + 1) * LANES]
        for j in order:
            rows = slice(j * pitch, j * pitch + s_len)
            h_ref[0, j * s_len:(j + 1) * s_len, k * LANES:(k + 1) * LANES] = (
                hloc_s[k, rows, :] + pcum_s[k, rows, :] * cst).astype(h_ref.dtype)
            cst = p_end[k][j:j + 1] * cst + h_end[k][j:j + 1]
        carry_s[:, k * LANES:(k + 1) * LANES] = cst
    hl_ref[0] = carry_s[...]


def rg_scan(p, col_blk, wg, ba, bx, lam, h0, *, reverse, tile_l):
    b, l, _ = p.shape
    c = ba.shape[1]
    t = min(tile_l, l)
    n_tiles = l // t
    s_len = t // SUBLANES
    pitch = s_len + SUBLANES

    def nat(i):
        return (n_tiles - 1 - i) if reverse else i

    body = functools.partial(_scan_body, reverse=reverse, t=t, c=c, s_len=s_len, pitch=pitch)
    vec = pl.BlockSpec((1, c), lambda bi, i: (0, 0))
    return pl.pallas_call(
        body,
        grid=(b, n_tiles),
        in_specs=[pl.BlockSpec((1, t, c), lambda bi, i: (bi, nat(i), col_blk)),
                  pl.BlockSpec(wg.shape, lambda bi, i: (0, 0, 0)),
                  vec, vec, vec,
                  pl.BlockSpec((1, 1, c), lambda bi, i: (bi, 0, 0))],
        out_specs=[pl.BlockSpec((1, t, c), lambda bi, i: (bi, nat(i), 0)),
                   pl.BlockSpec((1, 1, c), lambda bi, i: (bi, 0, 0))],
        out_shape=[jax.ShapeDtypeStruct((b, l, c), BF16), jax.ShapeDtypeStruct((b, 1, c), F32)],
        scratch_shapes=[pltpu.VMEM((t, c), F32), pltpu.VMEM((t, 2 * c), F32)]
        + [pltpu.VMEM((c // LANES, SUBLANES * pitch, LANES), F32) for _ in range(4)]
        + [pltpu.VMEM((1, c), F32)],
        compiler_params=_cparams("parallel", "arbitrary"),
        name="rg_scan_bwd" if reverse else "rg_scan_fwd",
    )(p, wg, ba, bx, lam, h0)


def gate_blocks(wa, wx, n_blk):
    h, d, _ = wa.shape
    hp = h // n_blk
    eye = jnp.eye(hp, dtype=wa.dtype)

    def bd(w):
        w = w.reshape(n_blk, hp, d, d)
        return jnp.einsum('khde,hg->khdge', w, eye).reshape(n_blk, hp * d, hp * d)

    return (0.5 * jnp.concatenate([bd(wa), bd(wx)], axis=-1)).astype(BF16)


HY_HID = 64
HY_FEATURE_ORDER = (list(range(1, 1 + 2 * HY_SEQ_BANDS))
                    + list(range(2 + 2 * HY_SEQ_BANDS, 2 + 2 * HY_SEQ_BANDS + 2 * HY_COL_BANDS))
                    + [0, 1 + 2 * HY_SEQ_BANDS])


def _filter_body(w1t_ref, b1_ref, w2t_ref, b2_ref, fr_ref, w3t_ref, o_ref, z_s, *, l, c, ct, rows_grid):
    d = pl.program_id(0)
    j = pl.program_id(1)
    lane = lax.broadcasted_iota(I32, (1, l), 1)
    s_i = jnp.where(d == 0, lane, l - lane)
    sf = s_i.astype(F32)
    t_norm = sf / float(max(l - 1, 1))

    @pl.when(j == 0)
    def _():
        band_step = (HY_SEQ_BANDS - 1 - 1e-4) / (HY_SEQ_BANDS - 1)
        seq_band = 1e-4 + band_step * lax.broadcasted_iota(I32, (HY_SEQ_BANDS, 1), 0).astype(F32)
        col_band = 1.0 + lax.broadcasted_iota(I32, (HY_COL_BANDS, 1), 0).astype(F32)
        col_pos = (s_i & (GRID_W - 1)).astype(F32)
        row_lag = (s_i >> int(math.log2(GRID_W))).astype(F32) / float(rows_grid)
        ang_seq = ((2.0 * math.pi / l) * sf) * seq_band
        ang_col = ((2.0 * math.pi / GRID_W) * col_pos) * col_band
        n_trig = 2 * HY_SEQ_BANDS + 2 * HY_COL_BANDS
        trow = lax.broadcasted_iota(I32, (HY_HID - n_trig, 1), 0)
        tail = jnp.where(trow == 0, t_norm, jnp.where(trow == 1, row_lag, 0.0))
        feats = jnp.concatenate([jnp.cos(ang_seq), jnp.sin(ang_seq), jnp.cos(ang_col), jnp.sin(ang_col), tail], axis=0)
        fr = fr_ref[...]
        z = jnp.sin(fr * (jnp.dot(w1t_ref[...], feats, precision=HIGHEST, preferred_element_type=F32) + b1_ref[...]))
        z_s[...] = jnp.sin(fr * (jnp.dot(w2t_ref[...], z, precision=HIGHEST, preferred_element_type=F32) + b2_ref[...]))

    k = jnp.dot(w3t_ref[0], z_s[...].astype(BF16), preferred_element_type=F32)
    ch = (lax.broadcasted_iota(I32, (ct, 1), 0) + j * ct).astype(F32)
    max_decay = math.log(HY_DECAY_TARGET) / HY_FAST_DECAY
    min_decay = math.log(HY_DECAY_TARGET) / HY_SLOW_DECAY
    delta = jnp.abs(min_decay + ch * ((max_decay - min_decay) / (c - 1)))
    k = k * jnp.exp(-t_norm * delta)
    k = jnp.where((d == 1) & (lane == 0), 0.0, k)
    o_ref[0] = k.astype(o_ref.dtype)


def hyena_filter_t(w1t, b1, w2t, b2, fr, w3t, l, tile_c):
    assert GRID_W & (GRID_W - 1) == 0
    c = w3t.shape[1]
    ct = min(tile_c, c)
    body = functools.partial(_filter_body, l=l, c=c, ct=ct, rows_grid=l // GRID_W)
    small = lambda shape: pl.BlockSpec(shape, lambda d, j: (0,) * len(shape))
    return pl.pallas_call(
        body,
        grid=(2, c // ct),
        in_specs=[small(w1t.shape), small(b1.shape), small(w2t.shape), small(b2.shape), small(fr.shape),
                  pl.BlockSpec((1, ct, HY_HID), lambda d, j: (d, j, 0))],
        out_specs=pl.BlockSpec((1, ct, l), lambda d, j: (d, j, 0)),
        out_shape=jax.ShapeDtypeStruct((2, c, l), BF16),
        scratch_shapes=[pltpu.VMEM((HY_HID, l), F32)],
        compiler_params=_cparams("arbitrary", "arbitrary"),
        name="hyena_filter",
    )(w1t, b1, w2t, b2, fr, w3t)


def dft_tables(l):
    import numpy as np
    n = 2 * l
    r_in, nk = l // LANES, n // LANES
    ka = np.arange(nk)[:, None].astype(np.float64)
    r = np.arange(r_in)[None, :].astype(np.float64)
    a1 = 2.0 * np.pi * ka * r / nk
    f1 = np.concatenate([np.cos(a1), -np.sin(a1)], axis=0)
    lane = np.arange(LANES)[None, :].astype(np.float64)
    at = 2.0 * np.pi * ka * lane / n
    twr, twi = np.cos(at), -np.sin(at)
    a2 = 2.0 * np.pi * np.arange(LANES)[:, None] * np.arange(LANES)[None, :] / LANES
    cr, ci = np.cos(a2), -np.sin(a2)
    m2 = np.block([[cr, ci], [-ci, cr]])
    m2i = np.block([[cr, -ci], [ci, cr]])
    ai = 2.0 * np.pi * np.arange(r_in)[:, None] * np.arange(nk)[None, :] / nk
    gi = np.concatenate([np.cos(ai), -np.sin(ai)], axis=1) / n
    as_bf = lambda a: jnp.asarray(a, F32).astype(BF16)
    return as_bf(f1), jnp.asarray(twr, F32), jnp.asarray(twi, F32), as_bf(m2), as_bf(m2i), as_bf(gi)


def _fwd_rows_twiddle(x_a, x_b, f1, twr, twi, nk):
    a = jnp.dot(f1, jnp.concatenate([x_a, x_b], axis=1), preferred_element_type=F32)
    out = []
    for h in range(2):
        re, im = a[:nk, h * LANES:(h + 1) * LANES], a[nk:, h * LANES:(h + 1) * LANES]
        out.append((re * twr - im * twi, re * twi + im * twr))
    return out


def _spectrum_body(k_ref, f1_ref, twr_ref, twi_ref, m2_ref, o_ref, *, g, nk, r_in):
    f1, twr, twi = f1_ref[...], twr_ref[...], twi_ref[...]
    sign = jnp.where((lax.broadcasted_iota(I32, (nk, 1), 0) & 1) == 0, 1.0, -1.0)
    a2 = []
    for ci in range(g):
        (fre, fim), (bre, bim) = _fwd_rows_twiddle(k_ref[0, ci], k_ref[1, ci], f1, twr, twi, nk)
        a2.append(jnp.concatenate([fre + sign * bre, fim + sign * bim], axis=1).astype(BF16))
    spec = jnp.dot(jnp.concatenate(a2, axis=0), m2_ref[...], preferred_element_type=F32)
    o_ref[...] = spec.reshape(g, nk, 2 * LANES).astype(o_ref.dtype)


def hyena_spectrum(kt4, tables, group):
    _, c, r_in, _ = kt4.shape
    nk = 2 * r_in
    f1, twr, twi, m2, _, _ = tables
    g = min(group, c)
    full = lambda a: pl.BlockSpec(a.shape, lambda j: (0,) * a.ndim)
    return pl.pallas_call(
        functools.partial(_spectrum_body, g=g, nk=nk, r_in=r_in),
        grid=(c // g,),
        in_specs=[pl.BlockSpec((2, g, r_in, LANES), lambda j: (0, j, 0, 0)), full(f1), full(twr), full(twi), full(m2)],
        out_specs=pl.BlockSpec((g, nk, 2 * LANES), lambda j: (j, 0, 0)),
        out_shape=jax.ShapeDtypeStruct((c, nk, 2 * LANES), BF16),
        compiler_params=_cparams("parallel"),
        name="hyena_spectrum",
    )(kt4, f1, twr, twi, m2)


HY_CHUNK = 256


def _hyena_proj_body(x_ref, g_ref, sh_ref, sc_ref, w_ref, taps_ref, nxt_ref, u_ref, z0_ref, last_s, *, n_tiles, t, c):
    i = pl.program_id(1)

    @pl.when(i == 0)
    def _():
        last_s[...] = jnp.zeros_like(last_s)

    hx = _norm_mod(x_ref, g_ref, sh_ref, sc_ref)
    row = lax.broadcasted_iota(I32, (t, 1), 0)
    has_next = (i < n_tiles - 1).astype(F32)
    cw = min(HY_CHUNK, c)
    for j in range(c // cw):
        zs = []
        for k in range(3):
            cols = slice(k * c + j * cw, k * c + (j + 1) * cw)
            p = jnp.dot(hx, w_ref[:, cols], preferred_element_type=F32)
            up = jnp.where(row == 0, last_s[:, cols], pltpu.roll(p, 1, 0))
            dn = jnp.where(row == t - 1, nxt_ref[0, 0, 0:1, cols] * has_next, pltpu.roll(p, t - 1, 0))
            tp = taps_ref[:, cols]
            zs.append(tp[3:4] + tp[0:1] * up + tp[1:2] * p + tp[2:3] * dn)
            last_s[:, cols] = p[t - 1:t, :]
        z0, z1, zv = zs
        u_t, z0_t = (zv * z1).T, z0.T
        for q in range(t // LANES):
            u_ref[0, q, j * cw:(j + 1) * cw, :] = u_t[:, q * LANES:(q + 1) * LANES].astype(u_ref.dtype)
            z0_ref[0, q, j * cw:(j + 1) * cw, :] = z0_t[:, q * LANES:(q + 1) * LANES].astype(z0_ref.dtype)


def hyena_proj(x, g, shift, scale, w, taps, *, tile_l):
    b, l, d = x.shape
    c = w.shape[1] // 3
    t = min(tile_l, l)
    n_tiles = l // t
    rq = t // LANES
    nxt = proj_first_rows(x, g, shift, scale, w, tile_l=tile_l)
    o_spec = pl.BlockSpec((1, rq, c, LANES), lambda bi, i: (bi, i, 0, 0))
    o_shape = jax.ShapeDtypeStruct((b, l // LANES, c, LANES), BF16)
    return pl.pallas_call(
        functools.partial(_hyena_proj_body, n_tiles=n_tiles, t=t, c=c),
        grid=(b, n_tiles),
        in_specs=[pl.BlockSpec((1, t, d), lambda bi, i: (bi, i, 0)),
                  pl.BlockSpec((1, d), lambda bi, i: (0, 0)),
                  pl.BlockSpec((1, 1, d), lambda bi, i: (bi, 0, 0)),
                  pl.BlockSpec((1, 1, d), lambda bi, i: (bi, 0, 0)),
                  pl.BlockSpec(w.shape, lambda bi, i: (0, 0)),
                  pl.BlockSpec(taps.shape, lambda bi, i: (0, 0)),
                  pl.BlockSpec((1, 1, SUBLANES, 3 * c), lambda bi, i: (bi, jnp.minimum(i + 1, n_tiles - 1), 0, 0))],
        out_specs=[o_spec, o_spec],
        out_shape=[o_shape, o_shape],
        scratch_shapes=[pltpu.VMEM((1, 3 * c), F32)],
        compiler_params=_cparams("parallel", "arbitrary"),
        name="hyena_proj",
    )(x, g, shift, scale, w, taps, nxt)


def _fftconv_body(u_ref, z0_ref, skip_ref, k_ref, f1_ref, twr_ref, twi_ref, m2_ref, m2i_ref, gi_ref,
                  o_ref, u_s, z0_s, y_s, *, ct, g, nk, r_in, pitch):
    for r in range(r_in):
        u_s[r * pitch:r * pitch + ct, :] = u_ref[0, r].astype(F32)
        z0_s[r * pitch:r * pitch + ct, :] = z0_ref[0, r].astype(F32)
    f1, twr, twi = f1_ref[...], twr_ref[...], twi_ref[...]

    def chan(ref, ch):
        return ref[pl.ds(ch, r_in, stride=pitch), :]

    def rows_fwd(c0):
        a2 = []
        for ci in range(0, g, 2):
            pair = _fwd_rows_twiddle(chan(u_s, c0 + ci).astype(BF16), chan(u_s, c0 + ci + 1).astype(BF16),
                                     f1, twr, twi, nk)
            a2 += [jnp.concatenate([tre, tim], axis=1).astype(BF16) for tre, tim in pair]
        return jnp.concatenate(a2, axis=0)

    def lanes_fwd(a2):
        return jnp.dot(a2, m2_ref[...], preferred_element_type=F32)

    def times_filter(c0, spec):
        kf = k_ref[pl.ds(c0, g)].astype(F32).reshape(g * nk, 2 * LANES)
        sre, sim = spec[:, :LANES], spec[:, LANES:]
        kre, kim = kf[:, :LANES], kf[:, LANES:]
        return jnp.concatenate([sre * kre - sim * kim, sre * kim + sim * kre], axis=1).astype(BF16)

    def lanes_inv(prod):
        return jnp.dot(prod, m2i_ref[...], preferred_element_type=F32)

    def rows_inv(c0, cc):
        for ci in range(0, g, 2):
            st = []
            for h in range(2):
                blk = cc[(ci + h) * nk:(ci + h + 1) * nk]
                cre, cim = blk[:, :LANES], blk[:, LANES:]
                st.append(jnp.concatenate([cre * twr + cim * twi, cim * twr - cre * twi], axis=0).astype(BF16))
            y2 = jnp.dot(gi_ref[...], jnp.concatenate(st, axis=1), preferred_element_type=F32)
            for h in range(2):
                ch = c0 + ci + h
                y = y2[:, h * LANES:(h + 1) * LANES]
                y_s[pl.ds(ch, r_in, stride=pitch), :] = (y + chan(u_s, ch) * skip_ref[ch]) * chan(z0_s, ch)

    def two_groups(i, _):
        ca, cb = 2 * g * i, 2 * g * i + g
        a2_a = rows_fwd(ca)
        spec_a = lanes_fwd(a2_a)
        a2_b = rows_fwd(cb)
        prod_a = times_filter(ca, spec_a)
        spec_b = lanes_fwd(a2_b)
        cc_a = lanes_inv(prod_a)
        prod_b = times_filter(cb, spec_b)
        cc_b = lanes_inv(prod_b)
        rows_inv(ca, cc_a)
        rows_inv(cb, cc_b)
        return 0

    lax.fori_loop(0, ct // (2 * g), two_groups, 0)
    for r in range(r_in):
        o_ref[0, r] = y_s[r * pitch:r * pitch + ct, :].astype(o_ref.dtype)


def hyena_fftconv(u, z0, skip3, spec, tables, *, tile_c, group):
    b, r_in, c, _ = u.shape
    nk = 2 * r_in
    ct = min(tile_c, c)
    g = min(group, ct // 2)
    assert ct % (2 * g) == 0 and g % 2 == 0
    pitch = ct + SUBLANES
    f1, twr, twi, m2, m2i, gi = tables
    full = lambda a: pl.BlockSpec(a.shape, lambda j, bi: (0,) * a.ndim)
    io_spec = pl.BlockSpec((1, r_in, ct, LANES), lambda j, bi: (bi, 0, j, 0))
    return pl.pallas_call(
        functools.partial(_fftconv_body, ct=ct, g=g, nk=nk, r_in=r_in, pitch=pitch),
        grid=(c // ct, b),
        in_specs=[io_spec, io_spec,
                  pl.BlockSpec((ct, 1, 1), lambda j, bi: (j, 0, 0)),
                  pl.BlockSpec((ct, nk, 2 * LANES), lambda j, bi: (j, 0, 0)),
                  full(f1), full(twr), full(twi), full(m2), full(m2i), full(gi)],
        out_specs=io_spec,
        out_shape=jax.ShapeDtypeStruct((b, r_in, c, LANES), BF16),
        scratch_shapes=[pltpu.VMEM((r_in * pitch, LANES), F32) for _ in range(3)],
        compiler_params=_cparams("parallel", "arbitrary"),
        name="hyena_fftconv",
    )(u, z0, skip3, spec, f1, twr, twi, m2, m2i, gi)


ROUTE_LANES = LANES
NEG_BIG = -1e30
HALF_WORD = 16


def _pack_bf16_pairs(v):
    h = v.shape[1] // 2
    bits = pltpu.bitcast(v.astype(BF16).astype(F32), I32)
    return bits[:, :h] | lax.shift_right_logical(bits[:, h:], HALF_WORD)


def _unpack_bf16_pairs(w):
    hi = pltpu.bitcast(w & jnp.int32(-65536), F32)
    lo = pltpu.bitcast(lax.shift_left(w, HALF_WORD), F32)
    return jnp.concatenate([hi, lo], axis=1)


SLAB = 4


def _store_row_slabs(ref, words, row0=0):
    rows = words.shape[0]
    for j in range(SLAB):
        ref[pl.ds(SLAB * row0 + j, rows, stride=SLAB), :] = words[:, j * LANES:(j + 1) * LANES]


def _load_row_slabs(ref, rows=None, row0=0):
    rows = ref.shape[0] // SLAB if rows is None else rows
    return jnp.concatenate([ref[pl.ds(SLAB * row0 + j, rows, stride=SLAB), :] for j in range(SLAB)], axis=1)


MIX_SUB = 512


def _mix_route_body(x_ref, hf_ref, hb_ref, prg_ref, pga_ref, pgb_ref, yt_ref, rgp_ref, hyp_ref, wo_ref, g1_ref,
                    n2g_ref, sh2_ref, sc2_ref, wr_ref, br_ref, tri_ref,
                    x1_ref, hxp_ref, route_ref, cnt_ref, carry_s, *, t, sub, n_exp):
    @pl.when((pl.program_id(0) == 0) & (pl.program_id(1) == 0))
    def _():
        carry_s[...] = jnp.zeros_like(carry_s)

    running = carry_s[...]
    for r0 in range(0, t, sub):
        rows = slice(r0, r0 + sub)
        hsum = hf_ref[0, rows, :].astype(F32) + hb_ref[0, rows, :].astype(F32)
        y_rg = (hsum * _gelu_tanh(prg_ref[0, rows, :].astype(F32))).astype(BF16)
        t1 = jnp.dot(y_rg, rgp_ref[...], preferred_element_type=F32)
        t2 = jnp.concatenate([lax.dot_general(yt_ref[0, q], hyp_ref[...], (((0,), (0,)), ((), ())),
                                              preferred_element_type=F32)
                              for q in range(r0 // LANES, (r0 + sub) // LANES)], axis=0)
        merged = ((t1 + t2) + jnp.tanh(0.5 * pga_ref[0, rows, :].astype(F32)) * t1
                  + jnp.tanh(0.5 * pgb_ref[0, rows, :].astype(F32)) * t2)
        out = jnp.dot(merged.astype(BF16), wo_ref[...], preferred_element_type=F32)
        x1 = x_ref[0, rows, :] + g1_ref[0] * out
        x1_ref[0, rows, :] = x1
        ms = jnp.mean(x1 * x1, axis=-1, keepdims=True)
        hx2 = (x1 * lax.rsqrt(ms + EPS) * n2g_ref[...]) * (1.0 + sc2_ref[0]) + sh2_ref[0]
        _store_row_slabs(hxp_ref, _pack_bf16_pairs(hx2), r0)

        hx_hi = hx2.astype(BF16)
        hx_lo = (hx2 - hx_hi.astype(F32)).astype(BF16)
        parts = (jnp.dot(hx_hi, wr_ref[...], preferred_element_type=F32)
                 + jnp.dot(hx_lo, wr_ref[...], preferred_element_type=F32))
        logits = parts[:, :ROUTE_LANES] + parts[:, ROUTE_LANES:] + br_ref[...]
        lane = lax.broadcasted_iota(I32, (sub, ROUTE_LANES), 1)
        is_g = lane < N_GROUPS
        glog = jnp.where(is_g, logits, NEG_BIG)
        gmax = jnp.max(glog, axis=1, keepdims=True)
        gidx = jnp.min(jnp.where(glog == gmax, lane, ROUTE_LANES), axis=1, keepdims=True)
        gsum = jnp.sum(jnp.where(is_g, jnp.exp(glog - gmax), 0.0), axis=1, keepdims=True)
        p_g = 1.0 / gsum
        e_lane = lane - N_GROUPS
        grp_of_lane = lax.shift_right_arithmetic(e_lane, int(math.log2(EXPERTS_PER_GROUP)))
        in_grp = (e_lane >= 0) & (e_lane < n_exp) & (grp_of_lane == gidx)
        elog = jnp.where(in_grp, logits, NEG_BIG)
        m1 = jnp.max(elog, axis=1, keepdims=True)
        i1 = jnp.min(jnp.where(elog == m1, lane, ROUTE_LANES), axis=1, keepdims=True)
        elog2 = jnp.where(lane == i1, NEG_BIG, elog)
        m2 = jnp.max(elog2, axis=1, keepdims=True)
        i2 = jnp.min(jnp.where(elog2 == m2, lane, ROUTE_LANES), axis=1, keepdims=True)
        e21 = jnp.exp(m2 - m1)
        pk1 = 1.0 / (1.0 + e21)
        wt1, wt2 = p_g * pk1, p_g * (e21 * pk1)

        oh1 = (lane == i1 - N_GROUPS).astype(F32)
        oh2 = (lane == i2 - N_GROUPS).astype(F32)
        cnt = oh1 + oh2
        before = jnp.dot(tri_ref[...], cnt.astype(BF16), preferred_element_type=F32) + running
        rank1 = jnp.sum(oh1 * before, axis=1, keepdims=True)
        rank2 = jnp.sum(oh2 * before, axis=1, keepdims=True)
        running = running + jnp.sum(cnt, axis=0, keepdims=True)
        vals = ((i1 - N_GROUPS).astype(F32), (i2 - N_GROUPS).astype(F32), rank1, rank2, wt1, wt2)
        route = jnp.zeros((sub, ROUTE_LANES), F32)
        for k, v in enumerate(vals):
            route = jnp.where(lane == k, v, route)
        route_ref[rows, :] = route
    carry_s[...] = running
    cnt_ref[...] = running


def mix_route(x, h_f, h_b, p_rm, y_hy_t, rg_proj, hy_proj, w_out, g1, n2g, sh2, sc2, wr, br, *, tile_l, n_exp):
    b, l, d = x.shape
    c = h_f.shape[2]
    t = min(tile_l, l)
    nt = l // t
    n = b * l
    sub = min(MIX_SUB, t)
    tri = (jnp.arange(sub)[:, None] > jnp.arange(sub)[None, :]).astype(BF16)
    tok = lambda bi, i: (bi, i, 0)
    col = lambda k: (lambda bi, i: (bi, i, k))
    full2 = lambda a: pl.BlockSpec(a.shape, lambda bi, i: (0, 0))
    per_b = pl.BlockSpec((1, 1, d), lambda bi, i: (bi, 0, 0))
    row = lambda bi, i: (bi * nt + i, 0)
    return pl.pallas_call(
        functools.partial(_mix_route_body, t=t, sub=sub, n_exp=n_exp),
        grid=(b, nt),
        in_specs=[pl.BlockSpec((1, t, d), tok), pl.BlockSpec((1, t, c), tok), pl.BlockSpec((1, t, c), tok),
                  pl.BlockSpec((1, t, c), col(1)), pl.BlockSpec((1, t, c), col(2)), pl.BlockSpec((1, t, c), col(3)),
                  pl.BlockSpec((1, t // LANES, c, LANES), lambda bi, i: (bi, i, 0, 0)),
                  full2(rg_proj), full2(hy_proj), full2(w_out), per_b,
                  full2(n2g), per_b, per_b, full2(wr), full2(br), full2(tri)],
        out_specs=[pl.BlockSpec((1, t, d), tok),
                   pl.BlockSpec((t * SLAB, LANES), row),
                   pl.BlockSpec((t, ROUTE_LANES), row), pl.BlockSpec((1, ROUTE_LANES), lambda bi, i: (0, 0))],
        out_shape=[jax.ShapeDtypeStruct((b, l, d), F32), jax.ShapeDtypeStruct((n * SLAB, LANES), I32),
                   jax.ShapeDtypeStruct((n, ROUTE_LANES), F32), jax.ShapeDtypeStruct((1, ROUTE_LANES), F32)],
        scratch_shapes=[pltpu.VMEM((1, ROUTE_LANES), F32)],
        compiler_params=_cparams("arbitrary", "arbitrary"),
        name="mix_route",
    )(x, h_f, h_b, p_rm, p_rm, p_rm, y_hy_t, rg_proj, hy_proj, w_out, g1, n2g, sh2, sc2, wr, br, tri)


def _dest_body(route_ref, cnt_ref, ut_ref, dest_ref, blk_ref, zero_ref, *, t, n_exp, nb_pad, n_rows):
    lane1 = lax.broadcasted_iota(I32, (1, ROUTE_LANES), 1)
    padded = jnp.floor((cnt_ref[...] + (MOE_BLOCK - 1.0)) * (1.0 / MOE_BLOCK)) * MOE_BLOCK
    padded = jnp.where(lane1 < n_exp, padded, 0.0)
    pend = jnp.dot(jnp.broadcast_to(padded, (SUBLANES, ROUTE_LANES)), ut_ref[...], precision=HIGHEST,
                   preferred_element_type=F32)[0:1]
    pstart = pend - padded
    route = route_ref[...]
    lane = lax.broadcasted_iota(I32, (t, ROUTE_LANES), 1)
    lf = lane.astype(F32)
    d1 = jnp.sum(jnp.where(lf == route[:, 0:1], pstart, 0.0), axis=1, keepdims=True) + route[:, 2:3]
    d2 = jnp.sum(jnp.where(lf == route[:, 1:2], pstart, 0.0), axis=1, keepdims=True) + route[:, 3:4]
    dmat = jnp.where(lane == 0, d1, jnp.where(lane == 1, d2, 0.0))
    dest_ref[...] = dmat.T[0:SUBLANES].astype(I32)
    first_row = lax.broadcasted_iota(I32, (nb_pad, ROUTE_LANES), 0).astype(F32) * float(MOE_BLOCK)
    lane_b = lax.broadcasted_iota(I32, (nb_pad, ROUTE_LANES), 1)
    nle = jnp.sum(jnp.where((lane_b < n_exp) & (pend <= first_row), 1.0, 0.0), axis=1, keepdims=True)
    e_blk = jnp.minimum(nle, n_exp - 1.0)
    mine = lane_b.astype(F32) == e_blk
    cnt_e = jnp.sum(jnp.where(mine, cnt_ref[...], 0.0), axis=1, keepdims=True)
    start_e = jnp.sum(jnp.where(mine, pstart, 0.0), axis=1, keepdims=True)
    valid = jnp.clip(cnt_e - (first_row - start_e), 0.0, float(MOE_BLOCK))
    blk_ref[...] = jnp.where(lane_b == 0, e_blk, jnp.where(lane_b == 1, valid, 0.0)).astype(I32)
    used = jnp.sum(jnp.where(lane1 == n_exp - 1, pend, 0.0), axis=1, keepdims=True)
    last_block = jnp.where(padded > 0.0, pend - float(MOE_BLOCK), -1.0)
    spare = used + float(MOE_BLOCK) * lane1.astype(F32)
    spare = jnp.where((lane1 < n_exp) & (spare < float(n_rows)), spare, -1.0)
    sub = lax.broadcasted_iota(I32, (SUBLANES, ROUTE_LANES), 0)
    zero_ref[...] = jnp.where(sub == 0, last_block, jnp.where(sub == 1, spare, -1.0)).astype(I32)


def moe_dest(route, cnt, *, tile, n_exp, n_blocks):
    n = route.shape[0]
    t = min(tile, n)
    nb_pad = -(-n_blocks // SUBLANES) * SUBLANES
    ut = (jnp.arange(ROUTE_LANES)[:, None] <= jnp.arange(ROUTE_LANES)[None, :]).astype(F32)
    return pl.pallas_call(
        functools.partial(_dest_body, t=t, n_exp=n_exp, nb_pad=nb_pad, n_rows=n_blocks * MOE_BLOCK),
        grid=(n // t,),
        in_specs=[pl.BlockSpec((t, ROUTE_LANES), lambda i: (i, 0)),
                  pl.BlockSpec((1, ROUTE_LANES), lambda i: (0, 0)),
                  pl.BlockSpec((ROUTE_LANES, ROUTE_LANES), lambda i: (0, 0))],
        out_specs=[pl.BlockSpec((SUBLANES, t), lambda i: (i, 0)),
                   pl.BlockSpec((nb_pad, ROUTE_LANES), lambda i: (0, 0)),
                   pl.BlockSpec((SUBLANES, ROUTE_LANES), lambda i: (0, 0))],
        out_shape=[jax.ShapeDtypeStruct((n // t * SUBLANES, t), I32),
                   jax.ShapeDtypeStruct((nb_pad, ROUTE_LANES), I32),
                   jax.ShapeDtypeStruct((SUBLANES, ROUTE_LANES), I32)],
        compiler_params=_cparams("arbitrary"),
        name="moe_dest",
    )(route, cnt, ut)


def _scatter_body(dest_ref, zero_ref, hx_ref, xb_ref, zero_s, sem, *, t, n_exp):
    @pl.when(pl.program_id(0) == 0)
    def _():
        zero_s[...] = jnp.zeros_like(zero_s)
        for wait in (False, True):
            for k in range(2):
                for e in range(n_exp):
                    start = zero_ref[k, e]

                    @pl.when(start >= 0)
                    def _(start=start, k=k):
                        copy = pltpu.make_async_copy(
                            zero_s, xb_ref.at[pl.ds(SLAB * jnp.maximum(start, 0), SLAB * MOE_BLOCK)], sem)
                        if wait:
                            copy.wait()
                        else:
                            copy.start(priority=k)

    def issue(r, _):
        for k in range(2):
            pltpu.make_async_copy(hx_ref.at[pl.ds(SLAB * r, SLAB)], xb_ref.at[pl.ds(SLAB * dest_ref[k, r], SLAB)],
                                  sem).start(priority=k)
        return 0

    lax.fori_loop(0, t, issue, 0, unroll=8)
    for k in range(2):
        pltpu.make_async_copy(hx_ref, xb_ref.at[pl.ds(0, SLAB * t)], sem).wait()


def moe_scatter(dest, zero_starts, hxp, n_rows, *, tile, n_exp):
    n = hxp.shape[0] // SLAB
    t = min(tile, n)
    per_dest_tile = dest.shape[1] // t
    return pl.pallas_call(
        functools.partial(_scatter_body, t=t, n_exp=n_exp),
        grid=(n // t,),
        in_specs=[pl.BlockSpec((SUBLANES, t), lambda i: (i // per_dest_tile, i % per_dest_tile),
                               memory_space=pltpu.SMEM),
                  pl.BlockSpec(zero_starts.shape, lambda i: (0, 0), memory_space=pltpu.SMEM),
                  pl.BlockSpec((t * SLAB, LANES), lambda i: (i, 0))],
        out_specs=pl.BlockSpec(memory_space=pl.ANY),
        out_shape=jax.ShapeDtypeStruct((n_rows * SLAB, LANES), I32),
        scratch_shapes=[pltpu.VMEM((MOE_BLOCK * SLAB, LANES), I32), pltpu.SemaphoreType.DMA],
        compiler_params=_cparams("arbitrary"),
        name="moe_scatter",
    )(dest, zero_starts, hxp)


def _expert_body(blk_ref, valid_ref, xb_ref, w1_ref, w3_ref, w2_ref, yb_ref, w1_s, w3_s, w2_s):
    i = pl.program_id(0)
    valid = valid_ref[i]
    half = MOE_BLOCK // 2
    changed = (i == 0) | (blk_ref[i] != blk_ref[jnp.maximum(i - 1, 0)])

    @pl.when(changed & (valid > 0))
    def _():
        w1_s[...] = w1_ref[0].astype(BF16)
        w3_s[...] = w3_ref[0].astype(BF16)
        w2_s[...] = w2_ref[0].astype(BF16)

    def run(rows):
        xblk = _unpack_bf16_pairs(_load_row_slabs(xb_ref, rows)).astype(BF16)
        h1 = jnp.dot(xblk, w1_s[...], preferred_element_type=F32)
        h3 = jnp.dot(xblk, w3_s[...], preferred_element_type=F32)
        hid = (h1 * _sigmoid(h1) * h3).astype(BF16)
        _store_row_slabs(yb_ref, _pack_bf16_pairs(jnp.dot(hid, w2_s[...], preferred_element_type=F32)))

    @pl.when(valid > half)
    def _():
        run(MOE_BLOCK)

    @pl.when(valid <= half)
    def _():
        yb_ref[...] = jnp.zeros_like(yb_ref)

    @pl.when((valid > 0) & (valid <= half))
    def _():
        run(half)


def moe_experts(blk_exp, blk_valid, xb, w1, w3, w2):
    p = xb.shape[0] // SLAB
    _, d, de = w1.shape
    nb = p // MOE_BLOCK
    grid_spec = pltpu.PrefetchScalarGridSpec(
        num_scalar_prefetch=2,
        grid=(nb,),
        in_specs=[pl.BlockSpec((MOE_BLOCK * SLAB, LANES), lambda i, blk, valid: (i, 0)),
                  pl.BlockSpec((1, d, de), lambda i, blk, valid: (blk[i], 0, 0)),
                  pl.BlockSpec((1, d, de), lambda i, blk, valid: (blk[i], 0, 0)),
                  pl.BlockSpec((1, de, d), lambda i, blk, valid: (blk[i], 0, 0))],
        out_specs=pl.BlockSpec((MOE_BLOCK * SLAB, LANES), lambda i, blk, valid: (i, 0)),
        scratch_shapes=[pltpu.VMEM((d, de), BF16), pltpu.VMEM((d, de), BF16), pltpu.VMEM((de, d), BF16)],
    )
    return pl.pallas_call(
        _expert_body,
        grid_spec=grid_spec,
        out_shape=jax.ShapeDtypeStruct((p * SLAB, LANES), I32),
        compiler_params=_cparams("arbitrary"),
        name="moe_experts",
    )(blk_exp, blk_valid, xb, w1, w3, w2)


COMBINE_PARTS = 4


def _combine_body(dest_ref, x1_ref, route_ref, g2_ref, fg_ref, yb_ref, o_ref, y1_s, y2_s, sems, *, t):
    tp = t // COMBINE_PARTS
    for part in range(COMBINE_PARTS):
        def issue(r, _, sem=sems.at[part]):
            pltpu.make_async_copy(yb_ref.at[pl.ds(SLAB * dest_ref[0, r], SLAB)], y1_s.at[pl.ds(SLAB * r, SLAB)],
                                  sem).start(priority=0)
            pltpu.make_async_copy(yb_ref.at[pl.ds(SLAB * dest_ref[1, r], SLAB)], y2_s.at[pl.ds(SLAB * r, SLAB)],
                                  sem).start(priority=1)
            return 0

        lax.fori_loop(part * tp, (part + 1) * tp, issue, 0, unroll=8)
    for part in range(COMBINE_PARTS):
        rows = slice(part * tp, (part + 1) * tp)
        lines = pl.ds(SLAB * part * tp, SLAB * tp)
        for y_s in (y1_s, y2_s):
            pltpu.make_async_copy(yb_ref.at[pl.ds(0, SLAB * tp)], y_s.at[lines], sems.at[part]).wait()
        route = route_ref[rows, :]
        moe = (route[:, 4:5] * _unpack_bf16_pairs(_load_row_slabs(y1_s, tp, part * tp))
               + route[:, 5:6] * _unpack_bf16_pairs(_load_row_slabs(y2_s, tp, part * tp)))
        x2 = x1_ref[0, rows, :] + g2_ref[0] * moe
        ms = jnp.mean(x2 * x2, axis=-1, keepdims=True)
        o_ref[0, rows, :] = x2 * lax.rsqrt(ms + EPS) * fg_ref[...]


def moe_combine(dest, x1, route, g2, final_g, yb, *, tile_l):
    b, l, d = x1.shape
    t = min(tile_l, l)
    nt = l // t
    slab = (t * SLAB, LANES)
    per_dest_tile = dest.shape[1] // t
    return pl.pallas_call(
        functools.partial(_combine_body, t=t),
        grid=(b, nt),
        in_specs=[pl.BlockSpec((SUBLANES, t),
                               lambda bi, i: ((bi * nt + i) // per_dest_tile, (bi * nt + i) % per_dest_tile),
                               memory_space=pltpu.SMEM),
                  pl.BlockSpec((1, t, d), lambda bi, i: (bi, i, 0)),
                  pl.BlockSpec((t, ROUTE_LANES), lambda bi, i: (bi * nt + i, 0)),
                  pl.BlockSpec((1, 1, d), lambda bi, i: (bi, 0, 0)),
                  pl.BlockSpec((1, d), lambda bi, i: (0, 0)),
                  pl.BlockSpec(memory_space=pl.ANY)],
        out_specs=pl.BlockSpec((1, t, d), lambda bi, i: (bi, i, 0)),
        out_shape=jax.ShapeDtypeStruct((b, l, d), F32),
        scratch_shapes=[pltpu.VMEM(slab, I32), pltpu.VMEM(slab, I32), pltpu.SemaphoreType.DMA((COMBINE_PARTS,))],
        compiler_params=_cparams("arbitrary", "arbitrary"),
        name="moe_combine",
    )(dest, x1, route, g2, final_g, yb)


def kernel(x, c, ctx, c_ctx, ada_w, ada_b, norm1_g, norm2_g, final_g, w_in, rg_conv_w, rg_conv_b, rg_wa_f, rg_ba_f, rg_wx_f, rg_bx_f, rg_lam_f, rg_wa_b, rg_ba_b, rg_wx_b, rg_bx_b, rg_lam_b, rg_proj, hy_conv_w, hy_conv_b, hy_pos_w1, hy_pos_b1, hy_pos_w2, hy_pos_b2, hy_freq, hy_pos_w3, hy_skip, hy_proj, w_out, moe_wg, moe_bg, moe_we, moe_be, moe_w1, moe_w3, moe_w2):
    B, L, D = x.shape
    C = rg_conv_w.shape[-1]
    LC = ctx.shape[1]
    c8 = jnp.zeros((8, D), F32).at[:B].set(c).at[B].set(c_ctx)
    mods = ada_mods(c8, ada_w[0], ada_b)
    sh1, sc1, g1 = (mods[:B, None, k * D:(k + 1) * D] for k in range(3))
    sh2, sc2, g2 = (mods[:B, None, k * D:(k + 1) * D] for k in range(3, 6))
    csh1 = jnp.broadcast_to(mods[B:B + 1, None, 0:D], (B, 1, D))
    csc1 = jnp.broadcast_to(mods[B:B + 1, None, D:2 * D], (B, 1, D))

    w_in_b = w_in[0].astype(BF16)
    w_rm = jnp.concatenate([w_in_b[:, :2 * C], w_in_b[:, 5 * C:]], axis=1)
    wg_f = gate_blocks(rg_wa_f[0], rg_wx_f[0], C // 256)
    wg_b = gate_blocks(rg_wa_b[0], rg_wx_b[0], C // 256)
    rg_f = (wg_f, rg_ba_f, rg_bx_f, rg_lam_f)
    rg_b = (wg_b, rg_ba_b, rg_bx_b, rg_lam_b)

    pc = norm_mod_proj(ctx, norm1_g, csh1, csc1, w_rm[:, :C], rg_conv_w[0], rg_conv_b, tile_l=LC, chunk=C)
    zero = jnp.zeros((B, 1, C), F32)
    _, hcf = rg_scan(pc, 0, *rg_f, zero, reverse=False, tile_l=TILE_SCAN)
    _, hcb = rg_scan(pc, 0, *rg_b, zero, reverse=True, tile_l=TILE_SCAN)

    p_rm = norm_mod_proj(x, norm1_g, sh1, sc1, w_rm, rg_conv_w[0], rg_conv_b, tile_l=TILE_PROJ, chunk=PROJ_CHUNK)
    hy_taps = jnp.concatenate([hy_conv_w[0], hy_conv_b], axis=0)
    u_hy, z0_hy = hyena_proj(x, norm1_g, sh1, sc1, w_in_b[:, 2 * C:5 * C], hy_taps, tile_l=TILE_PROJ)
    h_f, _ = rg_scan(p_rm, 0, *rg_f, hcf, reverse=False, tile_l=TILE_SCAN)
    h_b, _ = rg_scan(p_rm, 0, *rg_b, hcb, reverse=True, tile_l=TILE_SCAN)

    tables = dft_tables(L)
    assert hy_pos_w1.shape[1] == len(HY_FEATURE_ORDER)
    w1t = jnp.zeros((HY_HID, HY_HID), F32).at[:, :hy_pos_w1.shape[1]].set(hy_pos_w1[0].T[:, jnp.array(HY_FEATURE_ORDER)])
    kt = hyena_filter_t(w1t, hy_pos_b1[0][:, None], hy_pos_w2[0].T, hy_pos_b2[0][:, None], hy_freq[0][:, None],
                        hy_pos_w3[0].T.reshape(2, C, HY_HID).astype(BF16), L, FILTER_TILE_C)
    spec = hyena_spectrum(kt.reshape(2, C, L // LANES, LANES), tables, SPECTRUM_GROUP)
    y_hy_t = hyena_fftconv(u_hy, z0_hy, hy_skip[0][:, None, None], spec, tables, tile_c=FFT_TILE_C, group=FFT_GROUP)

    n_exp = moe_we.shape[-1]
    n_grp = moe_wg.shape[-1]
    assert n_grp == N_GROUPS and n_exp == N_GROUPS * EXPERTS_PER_GROUP
    wr = jnp.zeros((D, ROUTE_LANES), F32).at[:, :n_grp].set(moe_wg[0]).at[:, n_grp:n_grp + n_exp].set(moe_we[0])
    br = jnp.zeros((1, ROUTE_LANES), F32).at[:, :n_grp].set(moe_bg).at[:, n_grp:n_grp + n_exp].set(moe_be)
    wr_hi = wr.astype(BF16)
    wr_split = jnp.concatenate([wr_hi, (wr - wr_hi.astype(F32)).astype(BF16)], axis=1)
    x1, hxp, route, cnt = mix_route(x, h_f, h_b, p_rm, y_hy_t, rg_proj[0].astype(BF16), hy_proj[0].astype(BF16),
                                    (0.5 * w_out[0]).astype(BF16), g1, norm2_g, sh2, sc2, wr_split, br, tile_l=TILE_MIX,
                                    n_exp=n_exp)

    n_blocks = (2 * B * L + n_exp * (MOE_BLOCK - 1)) // MOE_BLOCK
    dest, blk, zero_starts = moe_dest(route, cnt, tile=TILE_DEST, n_exp=n_exp, n_blocks=n_blocks)
    xb = moe_scatter(dest, zero_starts, hxp, n_blocks * MOE_BLOCK, tile=TILE_DISPATCH, n_exp=n_exp)
    yb = moe_experts(blk[:n_blocks, 0], blk[:n_blocks, 1], xb, moe_w1[0], moe_w3[0], moe_w2[0])
    return moe_combine(dest, x1, route, g2, final_g[None], yb, tile_l=TILE_DISPATCH)
```
